```python
import math
import jax, jax.numpy as jnp
from jax import lax
import numpy as np

D_MODEL = 2048
BATCH = 1
SEQ = 16384
DEPTH = 2

CHUNK = 64
CONV_W = D_MODEL // 4
CONV_K = 3
RET_HEADS = 6
RET_DK = 64
RET_DV = 128
RET_QK = RET_HEADS * RET_DK
RET_V = RET_HEADS * RET_DV
FOX_HEADS = 6
FOX_DH = 128
FOX_W = FOX_HEADS * FOX_DH
Q_BLOCK = 128
D_MIX = CONV_W + RET_V + FOX_W
IN_SPLITS = (CONV_W, CONV_W, CONV_W, RET_QK, RET_QK, RET_V, RET_V, FOX_W, FOX_W, FOX_W, FOX_HEADS)
N_IN = 3 * CONV_W + 2 * RET_QK + 2 * RET_V + 3 * FOX_W + FOX_HEADS
N_EXPERTS = 64
N_GROUPS = 8
GROUP_SIZE = N_EXPERTS // N_GROUPS
TOP_K = 2
D_EXPERT = D_MODEL // 4
EXPERT_BLOCK = 128
ROPE_BASE = 10000.0
EPS = 1e-6

kernel_name = "hybrid_conv_retention_fox_moe_adaln"

F32 = jnp.float32


def rms_norm(x, g):
    xf = x.astype(F32)
    y = xf * lax.rsqrt(jnp.mean(xf * xf, axis=-1, keepdims=True) + EPS)
    return (y * g.astype(F32)).astype(x.dtype)


def modulate(x, g, shift, scale):
    return rms_norm(x, g) * (1.0 + scale[:, None, :]) + shift[:, None, :]


def rotary(t, pos):
    half = t.shape[-1] // 2
    inv = ROPE_BASE ** (-jnp.arange(half, dtype=F32) / half)
    ang = pos[:, None] * inv[None, :]
    cos = jnp.cos(ang)[None, :, None, :]
    sin = jnp.sin(ang)[None, :, None, :]
    tf = t.astype(F32)
    t1, t2 = tf[..., :half], tf[..., half:]
    return jnp.concatenate([t1 * cos - t2 * sin, t2 * cos + t1 * sin], axis=-1).astype(t.dtype)


def short_conv(b_gate, c_gate, u, w):
    z = c_gate * u
    y = lax.conv_general_dilated(
        z, w[:, None, :].astype(z.dtype), window_strides=(1,), padding=[(CONV_K - 1, 0)],
        dimension_numbers=("NWC", "WIO", "NWC"), feature_group_count=z.shape[-1])
    return b_gate * y


def retention(q, k, v, gate, gn_g):
    Bsz, S, H, dk = q.shape
    dv = v.shape[-1]
    nc = S // CHUNK

    def chunks(t):
        return t.reshape(Bsz, nc, CHUNK, H, t.shape[-1]).transpose(1, 0, 3, 2, 4)

    qc, kc, vc = chunks(q), chunks(k) * (dk ** -0.5), chunks(v)
    log_g = jnp.log(1.0 - 2.0 ** (-5.0 - jnp.arange(H, dtype=F32)))
    n = jnp.arange(CHUNK, dtype=F32)
    intra = jnp.exp(log_g[:, None, None] * jnp.abs(n[:, None] - n[None, :]))
    q_dec = jnp.exp(log_g[:, None] * (n + 1.0))
    k_dec = jnp.exp(log_g[:, None] * (CHUNK - 1.0 - n))
    c_dec = jnp.exp(log_g * CHUNK)

    scores = jnp.einsum('cbhnd,cbhmd->cbhnm', qc, kc) * intra
    o_intra = jnp.einsum('cbhnm,cbhmv->cbhnv', scores, vc.astype(scores.dtype))
    kv = jnp.einsum('cbhmd,cbhmv->cbhdv', (kc * k_dec[:, :, None]).astype(F32), vc.astype(F32))

    def step(state, kv_i):
        return state * c_dec[:, None, None] + kv_i, state

    _, prev = lax.scan(step, jnp.zeros((Bsz, H, dk, dv), F32), kv)
    o_cross = jnp.einsum('cbhnd,cbhdv->cbhnv', qc.astype(F32), prev) * q_dec[:, :, None]
    o = (o_intra.astype(F32) + o_cross).transpose(1, 0, 3, 2, 4).reshape(Bsz, S, H, dv)
    mu = jnp.mean(o, axis=-1, keepdims=True)
    var = jnp.mean((o - mu) ** 2, axis=-1, keepdims=True)
    o = ((o - mu) * lax.rsqrt(var + EPS)).reshape(Bsz, S, H * dv) * gn_g.astype(F32)
    return (jax.nn.silu(gate.astype(F32)) * o).astype(gate.dtype)


def forgetting_attention(q, k, v, f_logit):
    Bsz, S, H, d = q.shape
    nqb = S // Q_BLOCK
    cum = jnp.cumsum(jax.nn.log_sigmoid(f_logit.astype(F32)), axis=1).transpose(0, 2, 1)
    kh = k.transpose(0, 2, 1, 3)
    vh = v.transpose(0, 2, 1, 3)
    qb = q.transpose(0, 2, 1, 3).reshape(Bsz, H, nqb, Q_BLOCK, d).transpose(2, 0, 1, 3, 4)
    cq = cum.reshape(Bsz, H, nqb, Q_BLOCK).transpose(2, 0, 1, 3)
    kpos = jnp.arange(S)
    scale = d ** -0.5

    def block(args):
        i, q_i, c_i = args
        s = jnp.einsum('bhqd,bhkd->bhqk', q_i, kh).astype(F32) * scale
        s = s + c_i[..., None] - cum[:, :, None, :]
        qpos = i * Q_BLOCK + jnp.arange(Q_BLOCK)
        s = jnp.where(kpos[None, :] <= qpos[:, None], s, -jnp.inf)
        p = jax.nn.softmax(s, axis=-1)
        return jnp.einsum('bhqk,bhkd->bhqd', p.astype(vh.dtype), vh)

    out = lax.map(block, (jnp.arange(nqb), qb, cq))
    return out.transpose(1, 0, 3, 2, 4).reshape(Bsz, S, H * d)


def moe_ffn(h, router_w, router_b, w_gate, w_up, w_down):
    Bsz, S, D = h.shape
    T = Bsz * S
    ht = h.reshape(T, D)
    aff = jax.nn.sigmoid((ht @ router_w).astype(F32))
    sel = aff + router_b.astype(F32)
    grp_score = lax.top_k(sel.reshape(T, N_GROUPS, GROUP_SIZE), 2)[0].sum(-1)
    g_idx = jnp.argmax(grp_score, axis=-1)
    in_group = (jnp.arange(N_EXPERTS) // GROUP_SIZE)[None, :] == g_idx[:, None]
    _, top_idx = lax.top_k(jnp.where(in_group, sel, -jnp.inf), TOP_K)
    top_aff = jnp.take_along_axis(aff, top_idx, axis=-1)
    top_w = top_aff / jnp.sum(top_aff, axis=-1, keepdims=True)

    A = T * TOP_K
    P = ((A + N_EXPERTS * (EXPERT_BLOCK - 1)) + EXPERT_BLOCK - 1) // EXPERT_BLOCK * EXPERT_BLOCK
    nb = P // EXPERT_BLOCK
    flat_e = top_idx.reshape(A)
    order = jnp.argsort(flat_e)
    sorted_e = flat_e[order]
    counts = jnp.bincount(flat_e, length=N_EXPERTS)
    padded = (counts + EXPERT_BLOCK - 1) // EXPERT_BLOCK * EXPERT_BLOCK
    pad_end = jnp.cumsum(padded)
    pad_start = pad_end - padded
    start = jnp.cumsum(counts) - counts
    slot = pad_start[sorted_e] + jnp.arange(A) - start[sorted_e]
    slot_token = jnp.full((P,), T, jnp.int32).at[slot].set((order // TOP_K).astype(jnp.int32))
    xs = jnp.concatenate([ht, jnp.zeros((1, D), ht.dtype)], axis=0)[slot_token].reshape(nb, EXPERT_BLOCK, D)
    blk_e = jnp.minimum(jnp.searchsorted(pad_end, jnp.arange(nb) * EXPERT_BLOCK, side='right'), N_EXPERTS - 1)

    def expert_block(args):
        xb, e = args
        return (jax.nn.silu(xb @ w_gate[e]) * (xb @ w_up[e])) @ w_down[e]

    ys = lax.map(expert_block, (xs, blk_e)).reshape(P, D)
    slot_of_assign = jnp.zeros((A,), slot.dtype).at[order].set(slot)
    y = jnp.sum(ys[slot_of_assign].reshape(T, TOP_K, D) * top_w[..., None].astype(ys.dtype), axis=1)
    return y.reshape(Bsz, S, D)


def setup_inputs(seed: int = 0) -> dict:
    key = jax.random.key(seed)
    ks = jax.random.split(key, 20)

    def nrm(k, shape, scale):
        return jax.random.normal(k, shape, F32) * scale

    L, D = DEPTH, D_MODEL
    return {
        "x": nrm(ks[0], (BATCH, SEQ, D), 1.0),
        "c": nrm(ks[1], (BATCH, D), 1.0),
        "ada_w": nrm(ks[2], (L, D, 6 * D), 0.5 * D ** -0.5),
        "ada_b": nrm(ks[3], (L, 6 * D), 0.02),
        "norm1_g": 1.0 + nrm(ks[4], (L, D), 0.02),
        "norm2_g": 1.0 + nrm(ks[5], (L, D), 0.02),
        "w_in": nrm(ks[6], (L, D, N_IN), D ** -0.5),
        "conv_w": nrm(ks[7], (L, CONV_K, CONV_W), CONV_K ** -0.5),
        "ret_gn_g": 1.0 + nrm(ks[8], (L, RET_V), 0.02),
        "fox_fb": 3.0 + nrm(ks[9], (L, FOX_HEADS), 0.1),
        "w_out": nrm(ks[10], (L, D_MIX, D), D_MIX ** -0.5),
        "router_w": nrm(ks[11], (D, N_EXPERTS), D ** -0.5),
        "router_b": nrm(ks[12], (N_EXPERTS,), 0.01),
        "moe_w_gate": nrm(ks[13], (L, N_EXPERTS, D, D_EXPERT), D ** -0.5),
        "moe_w_up": nrm(ks[14], (L, N_EXPERTS, D, D_EXPERT), D ** -0.5),
        "moe_w_down": nrm(ks[15], (L, N_EXPERTS, D_EXPERT, D), D_EXPERT ** -0.5),
        "final_g": 1.0 + nrm(ks[16], (D,), 0.02),
    }


def reference(x, c, ada_w, ada_b, norm1_g, norm2_g, w_in, conv_w, ret_gn_g, fox_fb, w_out,
              router_w, router_b, moe_w_gate, moe_w_up, moe_w_down, final_g):
    Bsz, S, _ = x.shape
    pos = jnp.arange(S, dtype=F32)
    split_at = np.cumsum(np.array(IN_SPLITS))[:-1].tolist()
    c_act = jax.nn.silu(c)
    for l in range(DEPTH):
        mod = c_act @ ada_w[l] + ada_b[l]
        sh1, sc1, g1, sh2, sc2, g2 = jnp.split(mod, 6, axis=-1)

        h = modulate(x, norm1_g[l], sh1, sc1)
        proj = h @ w_in[l]
        cb, cc, cu, rq, rk, rv, rg, fq, fk, fv, ff = jnp.split(proj, split_at, axis=-1)
        y_conv = short_conv(cb, cc, cu, conv_w[l])
        rq = rotary(rq.reshape(Bsz, S, RET_HEADS, RET_DK), pos)
        rk = rotary(rk.reshape(Bsz, S, RET_HEADS, RET_DK), pos)
        y_ret = retention(rq, rk, rv.reshape(Bsz, S, RET_HEADS, RET_DV), rg, ret_gn_g[l])
        y_fox = forgetting_attention(
            fq.reshape(Bsz, S, FOX_HEADS, FOX_DH), fk.reshape(Bsz, S, FOX_HEADS, FOX_DH),
            fv.reshape(Bsz, S, FOX_HEADS, FOX_DH), ff + fox_fb[l])
        mixed = jnp.concatenate([y_conv.astype(x.dtype), y_ret.astype(x.dtype), y_fox.astype(x.dtype)], axis=-1)
        x = x + g1[:, None, :] * (mixed @ w_out[l])

        h2 = modulate(x, norm2_g[l], sh2, sc2)
        x = x + g2[:, None, :] * moe_ffn(h2, router_w, router_b, moe_w_gate[l], moe_w_up[l], moe_w_down[l]).astype(x.dtype)
    return rms_norm(x, final_g)
```

```python
import functools
import math

import numpy as np
import jax
import jax.numpy as jnp
from jax import lax
from jax.experimental import pallas as pl
from jax.experimental.pallas import tpu as pltpu

F32 = jnp.float32
BF16 = jnp.bfloat16

CHUNK = 64
CONV_K = 3
RET_HEADS = 6
RET_DK = 64
RET_DV = 128
FOX_HEADS = 6
FOX_DH = 128
N_EXPERTS = 64
N_GROUPS = 8
GROUP_SIZE = N_EXPERTS // N_GROUPS
TOP_K = 2
ROPE_BASE = 10000.0
EPS = 1e-6

LANES = 128

MOD_TN = 1024
INPROJ_TM = 1024
INPROJ_TN = 512
CONV_TM = 1024
CONV_HALO = 16
RET_ROWS = 256
FOX_TQ = 256
FOX_TK = 512
OUT_TM = 512
MOE_ROWS = 256
COMB_TM = 512
VMEM_LIMIT = 56 * 1024 * 1024


def _cparams(sem):
    return pltpu.CompilerParams(dimension_semantics=sem, vmem_limit_bytes=VMEM_LIMIT)


def _mod_kernel(c_ref, w_ref, b_ref, o_ref):
    c = c_ref[...]
    ca = c * jax.nn.sigmoid(c)
    o_ref[0] = jnp.sum(w_ref[0] * ca, axis=0, keepdims=True) + b_ref[0]


def _ada_mod(c, ada_w, ada_b):
    L, D, N = ada_w.shape
    tn = MOD_TN
    return pl.pallas_call(
        _mod_kernel,
        grid=(L, N // tn),
        in_specs=[
            pl.BlockSpec((D, 1), lambda l, j: (0, 0)),
            pl.BlockSpec((1, D, tn), lambda l, j: (l, 0, j)),
            pl.BlockSpec((1, 1, tn), lambda l, j: (l, 0, j)),
        ],
        out_specs=pl.BlockSpec((1, 1, tn), lambda l, j: (l, 0, j)),
        out_shape=jax.ShapeDtypeStruct((L, 1, N), F32),
        compiler_params=_cparams(("arbitrary", "arbitrary")),
        name="ada_mod",
    )(c.reshape(D, 1), ada_w, ada_b.reshape(L, 1, N))


def _modulated_norm(x, g, sh, sc):
    ms = jnp.mean(x * x, axis=-1, keepdims=True)
    y = x * lax.rsqrt(ms + EPS) * g
    return y * (1.0 + sc) + sh


def _log_sigmoid(x):
    return jnp.minimum(x, 0.0) - jnp.log1p(jnp.exp(-jnp.abs(x)))


def _inproj_kernel(x_ref, g_ref, sh_ref, sc_ref, w_ref, wff_ref, fb_ref,
                   proj_ref, cum_ref, h_scr, carry_scr):
    i = pl.program_id(0)
    j = pl.program_id(1)
    tm = x_ref.shape[0]

    @pl.when(jnp.logical_and(i == 0, j == 0))
    def _():
        carry_scr[...] = jnp.zeros_like(carry_scr)

    @pl.when(j == 0)
    def _():
        h = _modulated_norm(x_ref[...], g_ref[...], sh_ref[...], sc_ref[...])
        hb = h.astype(BF16)
        h_scr[...] = hb
        ff = jnp.dot(hb, wff_ref[...], preferred_element_type=F32) + fb_ref[...]
        c = _log_sigmoid(ff)
        row = lax.broadcasted_iota(jnp.int32, c.shape, 0)
        d = 1
        while d < tm:
            c = c + jnp.where(row >= d, pltpu.roll(c, d, axis=0), 0.0)
            d *= 2
        c = c + carry_scr[...]
        cum_ref[...] = c
        carry_scr[...] = c[tm - 1:tm, :]

    proj_ref[...] = jnp.dot(h_scr[...], w_ref[...], preferred_element_type=F32).astype(BF16)


def _inproj(x, g, sh, sc, w_main, w_ff, fb):
    T, D = x.shape
    N = w_main.shape[1]
    tm, tn = min(INPROJ_TM, T), INPROJ_TN
    vec = pl.BlockSpec((1, D), lambda i, j: (0, 0))
    return pl.pallas_call(
        _inproj_kernel,
        grid=(T // tm, N // tn),
        in_specs=[
            pl.BlockSpec((tm, D), lambda i, j: (i, 0)),
            vec, vec, vec,
            pl.BlockSpec((D, tn), lambda i, j: (0, j)),
            pl.BlockSpec((D, LANES), lambda i, j: (0, 0)),
            pl.BlockSpec((1, LANES), lambda i, j: (0, 0)),
        ],
        out_specs=[
            pl.BlockSpec((tm, tn), lambda i, j: (i, j)),
            pl.BlockSpec((tm, LANES), lambda i, j: (i, 0)),
        ],
        out_shape=[
            jax.ShapeDtypeStruct((T, N), BF16),
            jax.ShapeDtypeStruct((T, LANES), F32),
        ],
        scratch_shapes=[pltpu.VMEM((tm, D), BF16), pltpu.VMEM((1, LANES), F32)],
        compiler_params=_cparams(("arbitrary", "arbitrary")),
        name="inproj",
    )(x, g, sh, sc, w_main, w_ff, fb)


def _conv_kernel(cb_ref, cc_ref, cu_ref, hc_ref, hu_ref, w_ref, o_ref):
    i = pl.program_id(0)
    z = cc_ref[...].astype(F32) * cu_ref[...].astype(F32)
    zh = hc_ref[...].astype(F32) * hu_ref[...].astype(F32)
    zh = jnp.where(i > 0, zh, 0.0)
    hl = zh.shape[0]
    zm1 = zh[hl - 1:hl, :]
    zm2 = zh[hl - 2:hl - 1, :]
    row = lax.broadcasted_iota(jnp.int32, z.shape, 0)
    z1 = jnp.where(row == 0, zm1, pltpu.roll(z, 1, axis=0))
    z2 = jnp.where(row == 0, zm2, jnp.where(row == 1, zm1, pltpu.roll(z, 2, axis=0)))
    w = w_ref[...]
    y = z2 * w[0:1, :] + z1 * w[1:2, :] + z * w[2:3, :]
    o_ref[...] = (cb_ref[...].astype(F32) * y).astype(BF16)


def _conv(proj, conv_w):
    T = proj.shape[0]
    W = conv_w.shape[1]
    tm = min(CONV_TM, T)
    hb = tm // CONV_HALO
    halo = lambda c: pl.BlockSpec((CONV_HALO, W), lambda i: (jnp.maximum(i * hb - 1, 0), c))
    return pl.pallas_call(
        _conv_kernel,
        grid=(T // tm,),
        in_specs=[
            pl.BlockSpec((tm, W), lambda i: (i, 0)),
            pl.BlockSpec((tm, W), lambda i: (i, 1)),
            pl.BlockSpec((tm, W), lambda i: (i, 2)),
            halo(1), halo(2),
            pl.BlockSpec((CONV_K, W), lambda i: (0, 0)),
        ],
        out_specs=pl.BlockSpec((tm, W), lambda i: (i, 0)),
        out_shape=jax.ShapeDtypeStruct((T, W), BF16),
        compiler_params=_cparams(("arbitrary",)),
        name="conv",
    )(proj, proj, proj, proj, proj, conv_w)


def _ret_gammas():
    return [1.0 - 2.0 ** (-5.0 - h) for h in range(RET_HEADS)]


def _ret_tables(R):
    n = np.arange(R, dtype=np.float64)
    chunk = np.arange(R) // CHUNK
    allowed = chunk[None, :] <= chunk[:, None]
    dm, qd, kd = [], [], []
    for g in _ret_gammas():
        lg = math.log(g)
        dm.append(np.where(allowed, np.exp(lg * np.abs(n[:, None] - n[None, :])), 0.0))
        qd.append(np.broadcast_to(np.exp(lg * (n + 1.0))[:, None], (R, LANES)))
        kd.append(np.broadcast_to(np.exp(lg * (R - 1.0 - n))[:, None], (R, LANES)))
    f = lambda a: jnp.asarray(np.stack(a), dtype=F32)
    return f(dm), f(qd), f(kd)


def _rope_tables(T):
    half = RET_DK // 2
    inv = ROPE_BASE ** (-jnp.arange(half, dtype=F32) / half)
    ang = jnp.arange(T, dtype=F32)[:, None] * inv[None, :]
    cos, sin = jnp.cos(ang), jnp.sin(ang)
    reps = LANES // RET_DK
    cos_t = jnp.tile(jnp.concatenate([cos, cos], axis=1), (1, reps))
    sin_t = jnp.tile(jnp.concatenate([-sin, sin], axis=1), (1, reps))
    return cos_t, sin_t


def _ret_kernel(q_ref, k_ref, v_ref, g_ref, cos_ref, sin_ref, dm_ref, qd_ref, kd_ref, gn_ref,
                o_ref, s_scr):
    i = pl.program_id(0)
    R = q_ref.shape[0]

    @pl.when(i == 0)
    def _():
        s_scr[...] = jnp.zeros_like(s_scr)

    lane = lax.broadcasted_iota(jnp.int32, (R, LANES), 1)
    first_half = (lane % RET_DK) < (RET_DK // 2)
    low_head = lane < RET_DK
    cosv = cos_ref[...]
    sinv = sin_ref[...]
    c_dec = [g ** R for g in _ret_gammas()]

    def rot(t):
        swapped = jnp.where(first_half, pltpu.roll(t, LANES - RET_DK // 2, axis=1),
                            pltpu.roll(t, RET_DK // 2, axis=1))
        return t * cosv + swapped * sinv

    heads_per_vreg = LANES // RET_DK
    for p in range(RET_HEADS // heads_per_vreg):
        cols = slice(p * LANES, (p + 1) * LANES)
        qr = rot(q_ref[:, cols].astype(F32))
        kb = (rot(k_ref[:, cols].astype(F32)) * (RET_DK ** -0.5)).astype(BF16)
        for hh in range(heads_per_vreg):
            h = p * heads_per_vreg + hh
            hc = slice(h * RET_DV, (h + 1) * RET_DV)
            mask = low_head if hh == 0 else jnp.logical_not(low_head)
            qm = jnp.where(mask, qr, 0.0).astype(BF16)
            s = lax.dot_general(qm, kb, (((1,), (1,)), ((), ())), preferred_element_type=F32)
            s = s * dm_ref[h]
            v = v_ref[:, hc]
            o = jnp.dot(s.astype(BF16), v, preferred_element_type=F32)
            state = s_scr[h]
            o = o + jnp.dot(qm, state.astype(BF16), preferred_element_type=F32) * qd_ref[h]
            vd = (v.astype(F32) * kd_ref[h]).astype(BF16)
            kv = lax.dot_general(kb, vd, (((0,), (0,)), ((), ())), preferred_element_type=F32)
            s_scr[h] = state * c_dec[h] + kv
            mu = jnp.mean(o, axis=-1, keepdims=True)
            d = o - mu
            var = jnp.mean(d * d, axis=-1, keepdims=True)
            on = d * lax.rsqrt(var + EPS) * gn_ref[:, hc]
            gate = g_ref[:, hc].astype(F32)
            o_ref[:, hc] = (gate * jax.nn.sigmoid(gate) * on).astype(BF16)


def _retention(proj, cos_t, sin_t, gn_g, col0):
    T = proj.shape[0]
    R = min(RET_ROWS, T)
    QK = RET_HEADS * RET_DK
    V = RET_HEADS * RET_DV
    dm, qd, kd = _ret_tables(R)
    q_blk = col0 // QK
    v_blk = (col0 + 2 * QK) // V
    full3 = lambda a: pl.BlockSpec(a.shape, lambda i: (0, 0, 0))
    return pl.pallas_call(
        _ret_kernel,
        grid=(T // R,),
        in_specs=[
            pl.BlockSpec((R, QK), lambda i: (i, q_blk)),
            pl.BlockSpec((R, QK), lambda i: (i, q_blk + 1)),
            pl.BlockSpec((R, V), lambda i: (i, v_blk)),
            pl.BlockSpec((R, V), lambda i: (i, v_blk + 1)),
            pl.BlockSpec((R, LANES), lambda i: (i, 0)),
            pl.BlockSpec((R, LANES), lambda i: (i, 0)),
            full3(dm), full3(qd), full3(kd),
            pl.BlockSpec((1, V), lambda i: (0, 0)),
        ],
        out_specs=pl.BlockSpec((R, V), lambda i: (i, 0)),
        out_shape=jax.ShapeDtypeStruct((T, V), BF16),
        scratch_shapes=[pltpu.VMEM((RET_HEADS, LANES, RET_DV), F32)],
        compiler_params=_cparams(("arbitrary",)),
        name="retention",
    )(proj, proj, proj, proj, cos_t, sin_t, dm, qd, kd, gn_g)


def _fox_kernel(q_ref, k_ref, v_ref, cum_ref, cumt_ref, o_ref, m_scr, l_scr, acc_scr):
    h = pl.program_id(0)
    qi = pl.program_id(1)
    tq = q_ref.shape[0]
    tk = FOX_TK if k_ref.shape[0] >= FOX_TK else k_ref.shape[0]
    scale = FOX_DH ** -0.5

    qs = (q_ref[...].astype(F32) * scale).astype(BF16)
    lane = lax.broadcasted_iota(jnp.int32, cum_ref.shape, 1)
    cq = jnp.sum(jnp.where(lane == h, cum_ref[...], 0.0), axis=1, keepdims=True)

    m_scr[...] = jnp.full_like(m_scr, -jnp.inf)
    l_scr[...] = jnp.zeros_like(l_scr)
    acc_scr[...] = jnp.zeros_like(acc_scr)

    def step(c, masked):
        k0 = pl.multiple_of(c * tk, tk)
        k = k_ref[pl.ds(k0, tk), :]
        v = v_ref[pl.ds(k0, tk), :]
        ck = cumt_ref[pl.ds(h, 1), pl.ds(k0, tk)]
        s = lax.dot_general(qs, k, (((1,), (1,)), ((), ())), preferred_element_type=F32)
        s = s + cq - ck
        if masked:
            rowp = qi * tq + lax.broadcasted_iota(jnp.int32, s.shape, 0)
            colp = k0 + lax.broadcasted_iota(jnp.int32, s.shape, 1)
            s = jnp.where(colp <= rowp, s, -jnp.inf)
        m_old = m_scr[...]
        m_new = jnp.maximum(m_old, jnp.max(s, axis=1, keepdims=True))
        alpha = jnp.exp(m_old - m_new)
        p = jnp.exp(s - m_new)
        l_scr[...] = alpha * l_scr[...] + jnp.sum(p, axis=1, keepdims=True)
        acc_scr[...] = alpha * acc_scr[...] + jnp.dot(p.astype(BF16), v, preferred_element_type=F32)
        m_scr[...] = m_new

    n_full = (qi * tq) // tk

    def body(c, carry):
        step(c, False)
        return carry

    lax.fori_loop(0, n_full, body, 0)
    step(n_full, True)
    o_ref[...] = (acc_scr[...] / l_scr[...]).astype(BF16)


def _fox(proj, cum, cum_t, col0):
    T = proj.shape[0]
    tq = min(FOX_TQ, T)
    W = FOX_HEADS * FOX_DH
    qb = col0 // FOX_DH
    kb = qb + FOX_HEADS
    vb = kb + FOX_HEADS
    return pl.pallas_call(
        _fox_kernel,
        grid=(FOX_HEADS, T // tq),
        in_specs=[
            pl.BlockSpec((tq, FOX_DH), lambda h, i: (i, qb + h)),
            pl.BlockSpec((T, FOX_DH), lambda h, i: (0, kb + h)),
            pl.BlockSpec((T, FOX_DH), lambda h, i: (0, vb + h)),
            pl.BlockSpec((tq, LANES), lambda h, i: (i, 0)),
            pl.BlockSpec(cum_t.shape, lambda h, i: (0, 0)),
        ],
        out_specs=pl.BlockSpec((tq, FOX_DH), lambda h, i: (i, h)),
        out_shape=jax.ShapeDtypeStruct((T, W), BF16),
        scratch_shapes=[pltpu.VMEM((tq, 1), F32), pltpu.VMEM((tq, 1), F32),
                        pltpu.VMEM((tq, FOX_DH), F32)],
        compiler_params=_cparams(("arbitrary", "arbitrary")),
        name="fox",
    )(proj, proj, proj, cum, cum_t)


def _outproj_kernel(yc_ref, yr_ref, yf_ref, wc_ref, wr_ref, wf_ref, x_ref, g1_ref,
                    g_ref, sh_ref, sc_ref, rw_ref, xo_ref, h_ref, lg_ref):
    mix = jnp.dot(yc_ref[...], wc_ref[...], preferred_element_type=F32)
    mix = mix + jnp.dot(yr_ref[...], wr_ref[...], preferred_element_type=F32)
    mix = mix + jnp.dot(yf_ref[...], wf_ref[...], preferred_element_type=F32)
    x = x_ref[...] + g1_ref[...] * mix
    xo_ref[...] = x
    h = _modulated_norm(x, g_ref[...], sh_ref[...], sc_ref[...])
    h_ref[...] = h
    lg_ref[...] = lax.dot_general(rw_ref[...], h, (((1,), (1,)), ((), ())),
                                  precision=lax.Precision.HIGHEST, preferred_element_type=F32)


def _outproj(yc, yr, yf, wc, wr, wf, x, g1, g, sh, sc, rw_t):
    T, D = x.shape
    tm = min(OUT_TM, T)
    E = rw_t.shape[0]
    vec = pl.BlockSpec((1, D), lambda i: (0, 0))
    rows = lambda a: pl.BlockSpec((tm, a.shape[1]), lambda i: (i, 0))
    whole = lambda a: pl.BlockSpec(a.shape, lambda i: (0, 0))
    return pl.pallas_call(
        _outproj_kernel,
        grid=(T // tm,),
        in_specs=[rows(yc), rows(yr), rows(yf), whole(wc), whole(wr), whole(wf), rows(x),
                  vec, vec, vec, vec, whole(rw_t)],
        out_specs=[rows(x), rows(x), pl.BlockSpec((E, tm), lambda i: (0, i))],
        out_shape=[jax.ShapeDtypeStruct((T, D), F32), jax.ShapeDtypeStruct((T, D), F32),
                   jax.ShapeDtypeStruct((E, T), F32)],
        compiler_params=_cparams(("arbitrary",)),
        name="outproj",
    )(yc, yr, yf, wc, wr, wf, x, g1, g, sh, sc, rw_t)


def _route(logits_t, router_b, rows_per_block):
    E, T = logits_t.shape
    aff = jax.nn.sigmoid(logits_t.T)
    sel = aff + router_b[None, :]
    g3 = sel.reshape(T, N_GROUPS, GROUP_SIZE)
    m1 = jnp.max(g3, axis=-1)
    i1 = jnp.argmax(g3, axis=-1)
    g3m = jnp.where(jnp.arange(GROUP_SIZE)[None, None, :] == i1[..., None], -jnp.inf, g3)
    m2 = jnp.max(g3m, axis=-1)
    g_idx = jnp.argmax(m1 + m2, axis=-1)
    in_group = (jnp.arange(E) // GROUP_SIZE)[None, :] == g_idx[:, None]
    cand = jnp.where(in_group, sel, -jnp.inf)
    e0 = jnp.argmax(cand, axis=-1)
    cand2 = jnp.where(jnp.arange(E)[None, :] == e0[:, None], -jnp.inf, cand)
    e1 = jnp.argmax(cand2, axis=-1)
    top_idx = jnp.stack([e0, e1], axis=-1).astype(jnp.int32)
    top_aff = jnp.take_along_axis(aff, top_idx, axis=-1)
    top_w = top_aff / jnp.sum(top_aff, axis=-1, keepdims=True)

    A = T * TOP_K
    B = rows_per_block
    oh = (top_idx[:, :, None] == jnp.arange(E)[None, None, :]).astype(jnp.int32).sum(axis=1)
    incl = jnp.cumsum(oh, axis=0)
    counts = incl[-1]
    start = jnp.cumsum(counts) - counts
    rank = jnp.take_along_axis(incl - oh, top_idx, axis=-1)
    pos = start[top_idx] + rank
    sorted_a = jnp.zeros((A + B,), jnp.int32).at[pos.reshape(A)].set(jnp.arange(A, dtype=jnp.int32))

    nb = (A + E * (B - 1) + B - 1) // B
    nblk = (counts + B - 1) // B
    blk_end = jnp.cumsum(nblk)
    total = blk_end[-1]
    b = jnp.arange(nb, dtype=jnp.int32)
    bc = jnp.minimum(b, total - 1)
    blk_e = jnp.minimum(jnp.searchsorted(blk_end, bc, side="right"), E - 1).astype(jnp.int32)
    local = bc - (blk_end[blk_e] - nblk[blk_e])
    valid = b < total
    blk_start = (start[blk_e] + local * B).astype(jnp.int32)
    blk_n = jnp.where(valid, jnp.clip(counts[blk_e] - local * B, 0, B), 0).astype(jnp.int32)
    blk_first = jnp.logical_and(valid, local == 0).astype(jnp.int32)
    return top_w, sorted_a, blk_e, blk_start, blk_n, blk_first, nb


def _moe_kernel(blk_e_ref, blk_start_ref, blk_n_ref, blk_first_ref, sorted_ref,
                h_hbm, wg_ref, wu_ref, wd_ref, y_hbm,
                wg_b, wu_b, wd_b, xbuf, ybuf, gsem, ssem):
    b = pl.program_id(0)
    n = blk_n_ref[b]
    base = blk_start_ref[b]

    @pl.when(b == 0)
    def _():
        xbuf[...] = jnp.zeros_like(xbuf)

    @pl.when(blk_first_ref[b] == 1)
    def _():
        wg_b[...] = wg_ref[...].astype(BF16)
        wu_b[...] = wu_ref[...].astype(BF16)
        wd_b[...] = wd_ref[...].astype(BF16)

    def gather(r):
        tok = sorted_ref[base + r] // TOP_K
        return pltpu.make_async_copy(h_hbm.at[pl.ds(tok, 1), :], xbuf.at[pl.ds(r, 1), :], gsem)

    def scatter(r):
        a = sorted_ref[base + r]
        return pltpu.make_async_copy(ybuf.at[pl.ds(r, 1), :], y_hbm.at[pl.ds(a, 1), :], ssem)

    def each(fn):
        def body(r, carry):
            fn(r)
            return carry
        lax.fori_loop(0, n, body, 0)

    @pl.when(n > 0)
    def _():
        each(lambda r: gather(r).start())
        each(lambda r: gather(r).wait())
        x = xbuf[...].astype(BF16)
        g = jnp.dot(x, wg_b[...], preferred_element_type=F32)
        u = jnp.dot(x, wu_b[...], preferred_element_type=F32)
        a = (g * jax.nn.sigmoid(g) * u).astype(BF16)
        ybuf[...] = jnp.dot(a, wd_b[...], preferred_element_type=F32)
        each(lambda r: scatter(r).start())
        each(lambda r: scatter(r).wait())


def _moe(h2, layer, w_gate, w_up, w_down, sorted_a, blk_e, blk_start, blk_n, blk_first, nb):
    T, D = h2.shape
    DE = w_gate.shape[-1]
    B = MOE_ROWS
    wspec = lambda r, c: pl.BlockSpec((None, None, r, c), lambda b, be, bs, bn, bf, sa: (layer, be[b], 0, 0))
    grid_spec = pltpu.PrefetchScalarGridSpec(
        num_scalar_prefetch=5,
        grid=(nb,),
        in_specs=[pl.BlockSpec(memory_space=pl.ANY), wspec(D, DE), wspec(D, DE), wspec(DE, D)],
        out_specs=pl.BlockSpec(memory_space=pl.ANY),
        scratch_shapes=[
            pltpu.VMEM((D, DE), BF16), pltpu.VMEM((D, DE), BF16), pltpu.VMEM((DE, D), BF16),
            pltpu.VMEM((B, D), F32), pltpu.VMEM((B, D), F32),
            pltpu.SemaphoreType.DMA(()), pltpu.SemaphoreType.DMA(()),
        ],
    )
    return pl.pallas_call(
        _moe_kernel,
        grid_spec=grid_spec,
        out_shape=jax.ShapeDtypeStruct((T * TOP_K, D), F32),
        compiler_params=_cparams(("arbitrary",)),
        name="moe",
    )(blk_e, blk_start, blk_n, blk_first, sorted_a, h2, w_gate, w_up, w_down)


def _combine_kernel(x_ref, y_ref, w_ref, g2_ref, fg_ref, o_ref, *, final):
    D = x_ref.shape[1]
    w = w_ref[...]
    y = y_ref[:, :D] * w[:, 0:1] + y_ref[:, D:] * w[:, 1:2]
    x = x_ref[...] + g2_ref[...] * y
    if final:
        ms = jnp.mean(x * x, axis=-1, keepdims=True)
        x = x * lax.rsqrt(ms + EPS) * fg_ref[...]
    o_ref[...] = x


def _combine(x, y2, top_w, g2, final_g, final):
    T, D = x.shape
    tm = min(COMB_TM, T)
    vec = pl.BlockSpec((1, D), lambda i: (0, 0))
    return pl.pallas_call(
        functools.partial(_combine_kernel, final=final),
        grid=(T // tm,),
        in_specs=[
            pl.BlockSpec((tm, D), lambda i: (i, 0)),
            pl.BlockSpec((tm, TOP_K * D), lambda i: (i, 0)),
            pl.BlockSpec((tm, TOP_K), lambda i: (i, 0)),
            vec, vec,
        ],
        out_specs=pl.BlockSpec((tm, D), lambda i: (i, 0)),
        out_shape=jax.ShapeDtypeStruct((T, D), F32),
        compiler_params=_cparams(("arbitrary",)),
        name="combine",
    )(x, y2.reshape(T, TOP_K * D), top_w, g2, final_g)


def kernel(x, c, ada_w, ada_b, norm1_g, norm2_g, w_in, conv_w, ret_gn_g, fox_fb, w_out,
           router_w, router_b, moe_w_gate, moe_w_up, moe_w_down, final_g):
    Bsz, S, D = x.shape
    assert Bsz == 1, "one sequence per call"
    L = ada_w.shape[0]
    T = S
    conv_wd = conv_w.shape[-1]
    ret_qk = RET_HEADS * RET_DK
    ret_v = RET_HEADS * RET_DV
    fox_w = FOX_HEADS * FOX_DH
    n_main = 3 * conv_wd + 2 * ret_qk + 2 * ret_v + 3 * fox_w
    ret_col0 = 3 * conv_wd
    fox_col0 = ret_col0 + 2 * ret_qk + 2 * ret_v

    mod = _ada_mod(c, ada_w, ada_b)
    cos_t, sin_t = _rope_tables(T)
    rw_t = router_w.T
    xt = x.reshape(T, D)

    for l in range(L):
        sh1, sc1, g1, sh2, sc2, g2 = [mod[l, :, k * D:(k + 1) * D] for k in range(6)]
        w_main = w_in[l, :, :n_main].astype(BF16)
        w_ff = jnp.pad(w_in[l, :, n_main:], ((0, 0), (0, LANES - FOX_HEADS))).astype(BF16)
        fb = jnp.pad(fox_fb[l], (0, LANES - FOX_HEADS)).reshape(1, LANES)
        proj, cum = _inproj(xt, norm1_g[l].reshape(1, D), sh1, sc1, w_main, w_ff, fb)

        y_conv = _conv(proj, conv_w[l])
        y_ret = _retention(proj, cos_t, sin_t, ret_gn_g[l].reshape(1, ret_v), ret_col0)
        y_fox = _fox(proj, cum, cum[:, :8].T, fox_col0)

        wo = w_out[l].astype(BF16)
        xt, h2, logits_t = _outproj(
            y_conv, y_ret, y_fox, wo[:conv_wd], wo[conv_wd:conv_wd + ret_v], wo[conv_wd + ret_v:],
            xt, g1, norm2_g[l].reshape(1, D), sh2, sc2, rw_t)

        top_w, sorted_a, blk_e, blk_start, blk_n, blk_first, nb = _route(logits_t, router_b, MOE_ROWS)
        y2 = _moe(h2, l, moe_w_gate, moe_w_up, moe_w_down, sorted_a, blk_e, blk_start, blk_n, blk_first, nb)
        xt = _combine(xt, y2, top_w, g2, final_g.reshape(1, D), final=(l == L - 1))

    return xt.reshape(Bsz, S, D)
```

```python
import functools
import math

import numpy as np
import jax
import jax.numpy as jnp
from jax import lax
from jax.experimental import pallas as pl
from jax.experimental.pallas import tpu as pltpu

F32 = jnp.float32
BF16 = jnp.bfloat16

CHUNK = 64
CONV_K = 3
RET_HEADS = 6
RET_DK = 64
RET_DV = 128
FOX_HEADS = 6
FOX_DH = 128
N_EXPERTS = 64
N_GROUPS = 8
GROUP_SIZE = N_EXPERTS // N_GROUPS
TOP_K = 2
ROPE_BASE = 10000.0
EPS = 1e-6
LOG2E = 1.4426950408889634

LANES = 128

MOD_TN = 1024
INPROJ_TM = 1024
INPROJ_TN = 768
CONV_TM = 1024
CONV_HALO = 16
RET_ROWS = 256
FOX_TQ = 512
FOX_TK = 512
OUT_TM = 512
MOE_ROWS = 256
COMB_TM = 512
VMEM_LIMIT = 56 * 1024 * 1024


def _cparams(sem, **kw):
    return pltpu.CompilerParams(dimension_semantics=sem, vmem_limit_bytes=VMEM_LIMIT, **kw)


def _mod_kernel(c_ref, w_ref, b_ref, o_ref):
    c = c_ref[...]
    ca = c * jax.nn.sigmoid(c)
    o_ref[0] = jnp.sum(w_ref[0] * ca, axis=0, keepdims=True) + b_ref[0]


def _ada_mod(c, ada_w, ada_b):
    L, D, N = ada_w.shape
    tn = MOD_TN
    return pl.pallas_call(
        _mod_kernel,
        grid=(L, N // tn),
        in_specs=[
            pl.BlockSpec((D, 1), lambda l, j: (0, 0)),
            pl.BlockSpec((1, D, tn), lambda l, j: (l, 0, j)),
            pl.BlockSpec((1, 1, tn), lambda l, j: (l, 0, j)),
        ],
        out_specs=pl.BlockSpec((1, 1, tn), lambda l, j: (l, 0, j)),
        out_shape=jax.ShapeDtypeStruct((L, 1, N), F32),
        compiler_params=_cparams(("arbitrary", "arbitrary")),
        name="ada_mod",
    )(c.reshape(D, 1), ada_w, ada_b.reshape(L, 1, N))


def _modulated_norm(x, g, sh, sc):
    ms = jnp.mean(x * x, axis=-1, keepdims=True)
    y = x * lax.rsqrt(ms + EPS) * g
    return y * (1.0 + sc) + sh


def _log_sigmoid(x):
    return jnp.minimum(x, 0.0) - jnp.log1p(jnp.exp(-jnp.abs(x)))


def _inproj_kernel(x_ref, g_ref, sh_ref, sc_ref, w_ref, wt_ref, wff_ref, fb_ref,
                   proj_ref, projt_ref, cum_ref, h_scr, carry_scr, *, q_block, v_block, q_scale):
    i = pl.program_id(0)
    j = pl.program_id(1)
    tm = x_ref.shape[0]

    @pl.when(jnp.logical_and(i == 0, j == 0))
    def _():
        carry_scr[...] = jnp.zeros_like(carry_scr)

    @pl.when(j == 0)
    def _():
        h = _modulated_norm(x_ref[...], g_ref[...], sh_ref[...], sc_ref[...])
        hb = h.astype(BF16)
        h_scr[...] = hb
        ff = jnp.dot(hb, wff_ref[...], preferred_element_type=F32) + fb_ref[...]
        c = _log_sigmoid(ff)
        row = lax.broadcasted_iota(jnp.int32, c.shape, 0)
        d = 1
        while d < tm:
            c = c + jnp.where(row >= d, pltpu.roll(c, d, axis=0), 0.0)
            d *= 2
        c = c + carry_scr[...]
        cum_ref[...] = c
        carry_scr[...] = c[tm - 1:tm, :]

    transposed = jnp.logical_or(j == q_block, j == v_block)

    @pl.when(jnp.logical_not(transposed))
    def _():
        proj_ref[...] = jnp.dot(h_scr[...], w_ref[...].astype(BF16),
                                preferred_element_type=F32).astype(BF16)

    @pl.when(transposed)
    def _():
        r = lax.dot_general(wt_ref[...], h_scr[...], (((1,), (1,)), ((), ())),
                            preferred_element_type=F32)
        r = r * jnp.where(j == q_block, q_scale, 1.0)
        projt_ref[...] = r.astype(BF16)


def _inproj(x, g, sh, sc, w_in, layer, w_t, w_ff, fb, q_block, v_block, q_scale):
    T, D = x.shape
    tm, tn = min(INPROJ_TM, T), INPROJ_TN
    nj = v_block + 1
    is_q = lambda j: (j == q_block).astype(jnp.int32)
    is_v = lambda j: (j == v_block).astype(jnp.int32)
    skip = lambda j: j - is_q(j) - is_v(j)
    out_col = lambda j: j - (j >= q_block).astype(jnp.int32) - (j >= v_block).astype(jnp.int32)
    vec = pl.BlockSpec((1, D), lambda i, j: (0, 0))
    return pl.pallas_call(
        functools.partial(_inproj_kernel, q_block=q_block, v_block=v_block, q_scale=q_scale),
        grid=(T // tm, nj),
        in_specs=[
            pl.BlockSpec((tm, D), lambda i, j: (i, 0), pipeline_mode=pl.Buffered(1)),
            vec, vec, vec,
            pl.BlockSpec((None, D, tn), lambda i, j: (layer, 0, skip(j))),
            pl.BlockSpec((None, tn, D), lambda i, j: (is_v(j), 0, 0)),
            pl.BlockSpec((D, LANES), lambda i, j: (0, 0)),
            pl.BlockSpec((1, LANES), lambda i, j: (0, 0)),
        ],
        out_specs=[
            pl.BlockSpec((tm, tn), lambda i, j: (i, out_col(j))),
            pl.BlockSpec((tn, tm), lambda i, j: (is_v(j), i)),
            pl.BlockSpec((tm, LANES), lambda i, j: (i, 0)),
        ],
        out_shape=[
            jax.ShapeDtypeStruct((T, (nj - 2) * tn), BF16),
            jax.ShapeDtypeStruct((2 * tn, T), BF16),
            jax.ShapeDtypeStruct((T, LANES), F32),
        ],
        scratch_shapes=[pltpu.VMEM((tm, D), BF16), pltpu.VMEM((1, LANES), F32)],
        compiler_params=_cparams(("arbitrary", "arbitrary")),
        name="inproj",
    )(x, g, sh, sc, w_in, w_t, w_ff, fb)


def _conv_kernel(cb_ref, cc_ref, cu_ref, hc_ref, hu_ref, w_ref, o_ref):
    i = pl.program_id(0)
    z = cc_ref[...].astype(F32) * cu_ref[...].astype(F32)
    zh = hc_ref[...].astype(F32) * hu_ref[...].astype(F32)
    zh = jnp.where(i > 0, zh, 0.0)
    hl = zh.shape[0]
    zm1 = zh[hl - 1:hl, :]
    zm2 = zh[hl - 2:hl - 1, :]
    row = lax.broadcasted_iota(jnp.int32, z.shape, 0)
    z1 = jnp.where(row == 0, zm1, pltpu.roll(z, 1, axis=0))
    z2 = jnp.where(row == 0, zm2, jnp.where(row == 1, zm1, pltpu.roll(z, 2, axis=0)))
    w = w_ref[...]
    y = z2 * w[0:1, :] + z1 * w[1:2, :] + z * w[2:3, :]
    o_ref[...] = (cb_ref[...].astype(F32) * y).astype(BF16)


def _conv(proj, conv_w):
    T = proj.shape[0]
    W = conv_w.shape[1]
    tm = min(CONV_TM, T)
    hb = tm // CONV_HALO
    halo = lambda c: pl.BlockSpec((CONV_HALO, W), lambda i: (jnp.maximum(i * hb - 1, 0), c))
    return pl.pallas_call(
        _conv_kernel,
        grid=(T // tm,),
        in_specs=[
            pl.BlockSpec((tm, W), lambda i: (i, 0)),
            pl.BlockSpec((tm, W), lambda i: (i, 1)),
            pl.BlockSpec((tm, W), lambda i: (i, 2)),
            halo(1), halo(2),
            pl.BlockSpec((CONV_K, W), lambda i: (0, 0)),
        ],
        out_specs=pl.BlockSpec((tm, W), lambda i: (i, 0)),
        out_shape=jax.ShapeDtypeStruct((T, W), BF16),
        compiler_params=_cparams(("arbitrary",)),
        name="conv",
    )(proj, proj, proj, proj, proj, conv_w)


def _ret_gammas():
    return [1.0 - 2.0 ** (-5.0 - h) for h in range(RET_HEADS)]


def _ret_tables(R):
    n = np.arange(R, dtype=np.float64)
    chunk = np.arange(R) // CHUNK
    allowed = chunk[None, :] <= chunk[:, None]
    dm, qd, kd = [], [], []
    for g in _ret_gammas():
        lg = math.log(g)
        dm.append(np.where(allowed, np.exp(lg * np.abs(n[:, None] - n[None, :])), 0.0))
        qd.append(np.broadcast_to(np.exp(lg * (n + 1.0))[:, None], (R, LANES)))
        kd.append(np.broadcast_to(np.exp(lg * (R - 1.0 - n))[:, None], (R, LANES)))
    f = lambda a: jnp.asarray(np.stack(a), dtype=F32)
    return f(dm), f(qd), f(kd)


def _rope_tables(T):
    half = RET_DK // 2
    inv = ROPE_BASE ** (-jnp.arange(half, dtype=F32) / half)
    ang = jnp.arange(T, dtype=F32)[:, None] * inv[None, :]
    cos, sin = jnp.cos(ang), jnp.sin(ang)
    reps = LANES // RET_DK
    cos_t = jnp.tile(jnp.concatenate([cos, cos], axis=1), (1, reps))
    sin_t = jnp.tile(jnp.concatenate([-sin, sin], axis=1), (1, reps))
    return cos_t, sin_t


def _ret_kernel(q_ref, k_ref, v_ref, g_ref, cos_ref, sin_ref, dm_ref, qd_ref, kd_ref, gn_ref,
                o_ref, s_scr):
    i = pl.program_id(0)
    R = q_ref.shape[0]

    @pl.when(i == 0)
    def _():
        s_scr[...] = jnp.zeros_like(s_scr)

    lane = lax.broadcasted_iota(jnp.int32, (R, LANES), 1)
    first_half = (lane % RET_DK) < (RET_DK // 2)
    low_head = lane < RET_DK
    cosv = cos_ref[...]
    sinv = sin_ref[...]
    c_dec = [g ** R for g in _ret_gammas()]

    def rot(t):
        swapped = jnp.where(first_half, pltpu.roll(t, LANES - RET_DK // 2, axis=1),
                            pltpu.roll(t, RET_DK // 2, axis=1))
        return t * cosv + swapped * sinv

    heads_per_vreg = LANES // RET_DK
    for p in range(RET_HEADS // heads_per_vreg):
        cols = slice(p * LANES, (p + 1) * LANES)
        qr = rot(q_ref[:, cols].astype(F32))
        kb = (rot(k_ref[:, cols].astype(F32)) * (RET_DK ** -0.5)).astype(BF16)
        for hh in range(heads_per_vreg):
            h = p * heads_per_vreg + hh
            hc = slice(h * RET_DV, (h + 1) * RET_DV)
            mask = low_head if hh == 0 else jnp.logical_not(low_head)
            qm = jnp.where(mask, qr, 0.0).astype(BF16)
            s = lax.dot_general(qm, kb, (((1,), (1,)), ((), ())), preferred_element_type=F32)
            s = s * dm_ref[h]
            v = v_ref[:, hc]
            o = jnp.dot(s.astype(BF16), v, preferred_element_type=F32)
            state = s_scr[h]
            o = o + jnp.dot(qm, state.astype(BF16), preferred_element_type=F32) * qd_ref[h]
            vd = (v.astype(F32) * kd_ref[h]).astype(BF16)
            kv = lax.dot_general(kb, vd, (((0,), (0,)), ((), ())), preferred_element_type=F32)
            s_scr[h] = state * c_dec[h] + kv
            mu = jnp.mean(o, axis=-1, keepdims=True)
            d = o - mu
            var = jnp.mean(d * d, axis=-1, keepdims=True)
            on = d * lax.rsqrt(var + EPS) * gn_ref[:, hc]
            gate = g_ref[:, hc].astype(F32)
            o_ref[:, hc] = (gate * jax.nn.sigmoid(gate) * on).astype(BF16)


def _retention(proj, cos_t, sin_t, gn_g, col0):
    T = proj.shape[0]
    R = min(RET_ROWS, T)
    QK = RET_HEADS * RET_DK
    V = RET_HEADS * RET_DV
    dm, qd, kd = _ret_tables(R)
    q_blk = col0 // QK
    v_blk = (col0 + 2 * QK) // V
    full3 = lambda a: pl.BlockSpec(a.shape, lambda i: (0, 0, 0))
    return pl.pallas_call(
        _ret_kernel,
        grid=(T // R,),
        in_specs=[
            pl.BlockSpec((R, QK), lambda i: (i, q_blk)),
            pl.BlockSpec((R, QK), lambda i: (i, q_blk + 1)),
            pl.BlockSpec((R, V), lambda i: (i, v_blk)),
            pl.BlockSpec((R, V), lambda i: (i, v_blk + 1)),
            pl.BlockSpec((R, LANES), lambda i: (i, 0)),
            pl.BlockSpec((R, LANES), lambda i: (i, 0)),
            full3(dm), full3(qd), full3(kd),
            pl.BlockSpec((1, V), lambda i: (0, 0)),
        ],
        out_specs=pl.BlockSpec((R, V), lambda i: (i, 0)),
        out_shape=jax.ShapeDtypeStruct((T, V), BF16),
        scratch_shapes=[pltpu.VMEM((RET_HEADS, LANES, RET_DV), F32)],
        compiler_params=_cparams(("arbitrary",)),
        name="retention",
    )(proj, proj, proj, proj, cos_t, sin_t, dm, qd, kd, gn_g)


def _fox_kernel(qt_ref, k_ref, vt_ref, cum_ref, o_ref, ck_scr, m_scr, l_scr, acc_scr):
    h = pl.program_id(0)
    qi = pl.program_id(1)
    tq = qt_ref.shape[1]
    tk = min(FOX_TK, k_ref.shape[0])
    reps = tq // LANES

    lane = lax.broadcasted_iota(jnp.int32, cum_ref.shape, 1)
    col = jnp.sum(jnp.where(lane == h, cum_ref[...], 0.0), axis=1, keepdims=True) * LOG2E
    q0 = pl.multiple_of(qi * tq, tq)
    ck_scr[pl.ds(q0, tq), :] = jnp.broadcast_to(col, cum_ref.shape)

    m_scr[...] = jnp.full_like(m_scr, -jnp.inf)
    l_scr[...] = jnp.zeros_like(l_scr)
    acc_scr[...] = jnp.zeros_like(acc_scr)
    qt = qt_ref[...]

    def step(c, masked):
        k0 = pl.multiple_of(c * tk, tk)
        s = jnp.dot(k_ref[pl.ds(k0, tk), :], qt, preferred_element_type=F32)
        bias = ck_scr[pl.ds(k0, tk), :]
        s = s - jnp.concatenate([bias] * reps, axis=1)
        if masked:
            kpos = k0 + lax.broadcasted_iota(jnp.int32, s.shape, 0)
            qpos = q0 + lax.broadcasted_iota(jnp.int32, s.shape, 1)
            s = jnp.where(kpos <= qpos, s, -jnp.inf)
        m_old = m_scr[...]
        m_new = jnp.maximum(m_old, jnp.max(s, axis=0, keepdims=True))
        alpha = jnp.exp2(m_old - m_new)
        p = jnp.exp2(s - m_new)
        l_scr[...] = alpha * l_scr[...] + jnp.sum(p, axis=0, keepdims=True)
        pv = jnp.dot(vt_ref[:, pl.ds(k0, tk)], p.astype(BF16), preferred_element_type=F32)
        acc_scr[...] = alpha * acc_scr[...] + pv
        m_scr[...] = m_new

    n_full = (qi * tq) // tk

    def pair(c2, carry):
        step(2 * c2, False)
        step(2 * c2 + 1, False)
        return carry

    lax.fori_loop(0, n_full // 2, pair, 0)

    @pl.when(n_full % 2 == 1)
    def _():
        step(n_full - 1, False)

    step(n_full, True)
    o_ref[...] = (acc_scr[...] / l_scr[...]).T.astype(BF16)


def _fox(proj, proj_t, cum, k_col0):
    T = proj.shape[0]
    tq = min(FOX_TQ, T)
    W = FOX_HEADS * FOX_DH
    kb = k_col0 // FOX_DH
    return pl.pallas_call(
        _fox_kernel,
        grid=(FOX_HEADS, T // tq),
        in_specs=[
            pl.BlockSpec((FOX_DH, tq), lambda h, i: (h, i)),
            pl.BlockSpec((T, FOX_DH), lambda h, i: (0, kb + h)),
            pl.BlockSpec((FOX_DH, T), lambda h, i: (FOX_HEADS + h, 0)),
            pl.BlockSpec((tq, LANES), lambda h, i: (i, 0)),
        ],
        out_specs=pl.BlockSpec((tq, FOX_DH), lambda h, i: (i, h)),
        out_shape=jax.ShapeDtypeStruct((T, W), BF16),
        scratch_shapes=[pltpu.VMEM((T, LANES), F32), pltpu.VMEM((1, tq), F32),
                        pltpu.VMEM((1, tq), F32), pltpu.VMEM((FOX_DH, tq), F32)],
        compiler_params=_cparams(("arbitrary", "arbitrary")),
        name="fox",
    )(proj_t, proj, proj_t, cum)


def _outproj_kernel(yc_ref, yr_ref, yf_ref, wc_ref, wr_ref, wf_ref, x_ref, g1_ref,
                    g_ref, sh_ref, sc_ref, rw_ref, xo_ref, h_ref, lg_ref):
    mix = jnp.dot(yc_ref[...], wc_ref[...], preferred_element_type=F32)
    mix = mix + jnp.dot(yr_ref[...], wr_ref[...], preferred_element_type=F32)
    mix = mix + jnp.dot(yf_ref[...], wf_ref[...], preferred_element_type=F32)
    x = x_ref[...] + g1_ref[...] * mix
    xo_ref[...] = x
    h = _modulated_norm(x, g_ref[...], sh_ref[...], sc_ref[...])
    h_ref[...] = h
    lg_ref[...] = lax.dot_general(rw_ref[...], h, (((1,), (1,)), ((), ())),
                                  precision=lax.Precision.HIGHEST, preferred_element_type=F32)


def _outproj(yc, yr, yf, wc, wr, wf, x, g1, g, sh, sc, rw_t):
    T, D = x.shape
    tm = min(OUT_TM, T)
    E = rw_t.shape[0]
    vec = pl.BlockSpec((1, D), lambda i: (0, 0))
    rows = lambda a: pl.BlockSpec((tm, a.shape[1]), lambda i: (i, 0))
    whole = lambda a: pl.BlockSpec(a.shape, lambda i: (0, 0))
    return pl.pallas_call(
        _outproj_kernel,
        grid=(T // tm,),
        in_specs=[rows(yc), rows(yr), rows(yf), whole(wc), whole(wr), whole(wf), rows(x),
                  vec, vec, vec, vec, whole(rw_t)],
        out_specs=[rows(x), rows(x), pl.BlockSpec((E, tm), lambda i: (0, i))],
        out_shape=[jax.ShapeDtypeStruct((T, D), F32), jax.ShapeDtypeStruct((T, D), F32),
                   jax.ShapeDtypeStruct((E, T), F32)],
        compiler_params=_cparams(("arbitrary",)),
        name="outproj",
    )(yc, yr, yf, wc, wr, wf, x, g1, g, sh, sc, rw_t)


def _route(logits_t, router_b, rows_per_block):
    E, T = logits_t.shape
    aff = jax.nn.sigmoid(logits_t.T)
    sel = aff + router_b[None, :]
    g3 = sel.reshape(T, N_GROUPS, GROUP_SIZE)
    m1 = jnp.max(g3, axis=-1)
    i1 = jnp.argmax(g3, axis=-1)
    g3m = jnp.where(jnp.arange(GROUP_SIZE)[None, None, :] == i1[..., None], -jnp.inf, g3)
    m2 = jnp.max(g3m, axis=-1)
    g_idx = jnp.argmax(m1 + m2, axis=-1)
    in_group = (jnp.arange(E) // GROUP_SIZE)[None, :] == g_idx[:, None]
    cand = jnp.where(in_group, sel, -jnp.inf)
    e0 = jnp.argmax(cand, axis=-1)
    cand2 = jnp.where(jnp.arange(E)[None, :] == e0[:, None], -jnp.inf, cand)
    e1 = jnp.argmax(cand2, axis=-1)
    top_idx = jnp.stack([e0, e1], axis=-1).astype(jnp.int32)
    top_aff = jnp.take_along_axis(aff, top_idx, axis=-1)
    top_w = top_aff / jnp.sum(top_aff, axis=-1, keepdims=True)

    A = T * TOP_K
    B = rows_per_block
    oh = (top_idx[:, :, None] == jnp.arange(E)[None, None, :]).astype(jnp.int32).sum(axis=1)
    incl = jnp.cumsum(oh, axis=0)
    counts = incl[-1]
    start = jnp.cumsum(counts) - counts
    rank = jnp.take_along_axis(incl - oh, top_idx, axis=-1)
    pos = start[top_idx] + rank
    sorted_a = jnp.zeros((A + B,), jnp.int32).at[pos.reshape(A)].set(jnp.arange(A, dtype=jnp.int32))

    nb = (A + E * (B - 1) + B - 1) // B
    nblk = (counts + B - 1) // B
    blk_end = jnp.cumsum(nblk)
    total = blk_end[-1]
    b = jnp.arange(nb, dtype=jnp.int32)
    bc = jnp.minimum(b, total - 1)
    blk_e = jnp.minimum(jnp.searchsorted(blk_end, bc, side="right"), E - 1).astype(jnp.int32)
    local = bc - (blk_end[blk_e] - nblk[blk_e])
    valid = b < total
    blk_start = (start[blk_e] + local * B).astype(jnp.int32)
    blk_n = jnp.where(valid, jnp.clip(counts[blk_e] - local * B, 0, B), 0).astype(jnp.int32)
    blk_first = jnp.logical_and(valid, local == 0).astype(jnp.int32)
    return top_w, sorted_a, blk_e, blk_start, blk_n, blk_first, total.astype(jnp.int32).reshape(1), nb


def _moe_kernel(blk_e_ref, blk_start_ref, blk_n_ref, blk_first_ref, total_ref, sorted_ref,
                h_hbm, wg_ref, wu_ref, wd_ref, y_hbm,
                wg_b, wu_b, wd_b, xbuf, ybuf, gsem, ssem):
    b = pl.program_id(0)
    total = total_ref[0]
    B = ybuf.shape[0]
    T = h_hbm.shape[0]
    slot = b % 2

    def issue_gather(blk, to_slot):
        base = blk_start_ref[blk]

        def body(r, carry):
            tok = lax.shift_right_logical(sorted_ref[base + r], 1)
            pltpu.make_async_copy(h_hbm.at[pl.ds(tok, 1), :], xbuf.at[to_slot, pl.ds(r, 1), :],
                                  gsem.at[to_slot]).start()
            return carry

        lax.fori_loop(0, B, body, 0, unroll=8)

    def issue_scatter():
        base = blk_start_ref[b]
        n = blk_n_ref[b]

        def body(r, carry):
            a = sorted_ref[base + r]
            live = r < n
            k = jnp.where(live, a & 1, 0)
            t = jnp.where(live, lax.shift_right_logical(a, 1), T + r)
            pltpu.make_async_copy(ybuf.at[pl.ds(r, 1), :], y_hbm.at[k, pl.ds(t, 1), :], ssem).start()
            return carry

        lax.fori_loop(0, B, body, 0, unroll=8)

    def wait_scatter():
        pltpu.make_async_copy(ybuf, y_hbm.at[0, pl.ds(0, B), :], ssem).wait()

    @pl.when(b == 0)
    def _():
        issue_gather(0, 0)
        ybuf[...] = jnp.zeros_like(ybuf)
        for k in range(TOP_K):
            spare = pltpu.make_async_copy(ybuf, y_hbm.at[k, pl.ds(T, B), :], ssem)
            spare.start()
            spare.wait()

    @pl.when(b + 1 < total)
    def _():
        issue_gather(b + 1, 1 - slot)

    @pl.when(b < total)
    def _():
        pltpu.make_async_copy(h_hbm.at[pl.ds(0, B), :], xbuf.at[slot], gsem.at[slot]).wait()

        @pl.when(blk_first_ref[b] == 1)
        def _():
            wg_b[...] = wg_ref[...].astype(BF16)
            wu_b[...] = wu_ref[...].astype(BF16)
            wd_b[...] = wd_ref[...].astype(BF16)

        x = xbuf[slot].astype(BF16)
        g = jnp.dot(x, wg_b[...], preferred_element_type=F32)
        u = jnp.dot(x, wu_b[...], preferred_element_type=F32)
        a = (g * jax.nn.sigmoid(g) * u).astype(BF16)
        y = jnp.dot(a, wd_b[...], preferred_element_type=F32)

        @pl.when(b > 0)
        def _():
            wait_scatter()

        ybuf[...] = y
        issue_scatter()

        @pl.when(b == total - 1)
        def _():
            wait_scatter()


def _moe(h2, layer, w_gate, w_up, w_down, sorted_a, blk_e, blk_start, blk_n, blk_first, total, nb):
    T, D = h2.shape
    DE = w_gate.shape[-1]
    B = MOE_ROWS
    wspec = lambda r, c: pl.BlockSpec((None, None, r, c), lambda b, be, *_: (layer, be[b], 0, 0))
    grid_spec = pltpu.PrefetchScalarGridSpec(
        num_scalar_prefetch=6,
        grid=(nb,),
        in_specs=[pl.BlockSpec(memory_space=pl.ANY), wspec(D, DE), wspec(D, DE), wspec(DE, D)],
        out_specs=pl.BlockSpec(memory_space=pl.ANY),
        scratch_shapes=[
            pltpu.VMEM((D, DE), BF16), pltpu.VMEM((D, DE), BF16), pltpu.VMEM((DE, D), BF16),
            pltpu.VMEM((2, B, D), F32), pltpu.VMEM((B, D), F32),
            pltpu.SemaphoreType.DMA((2,)), pltpu.SemaphoreType.DMA(()),
        ],
    )
    return pl.pallas_call(
        _moe_kernel,
        grid_spec=grid_spec,
        out_shape=jax.ShapeDtypeStruct((TOP_K, T + B, D), F32),
        compiler_params=_cparams(("arbitrary",), disable_bounds_checks=True),
        name="moe",
    )(blk_e, blk_start, blk_n, blk_first, total, sorted_a, h2, w_gate, w_up, w_down)


def _combine_kernel(x_ref, y_ref, w_ref, g2_ref, fg_ref, o_ref, *, final):
    w = w_ref[...]
    y = y_ref[0] * w[:, 0:1] + y_ref[1] * w[:, 1:2]
    x = x_ref[...] + g2_ref[...] * y
    if final:
        ms = jnp.mean(x * x, axis=-1, keepdims=True)
        x = x * lax.rsqrt(ms + EPS) * fg_ref[...]
    o_ref[...] = x


def _combine(x, y2, top_w, g2, final_g, final):
    T, D = x.shape
    tm = min(COMB_TM, T)
    vec = pl.BlockSpec((1, D), lambda i: (0, 0))
    return pl.pallas_call(
        functools.partial(_combine_kernel, final=final),
        grid=(T // tm,),
        in_specs=[
            pl.BlockSpec((tm, D), lambda i: (i, 0)),
            pl.BlockSpec((TOP_K, tm, D), lambda i: (0, i, 0)),
            pl.BlockSpec((tm, TOP_K), lambda i: (i, 0)),
            vec, vec,
        ],
        out_specs=pl.BlockSpec((tm, D), lambda i: (i, 0)),
        out_shape=jax.ShapeDtypeStruct((T, D), F32),
        compiler_params=_cparams(("arbitrary",)),
        name="combine",
    )(x, y2, top_w, g2, final_g)


def kernel(x, c, ada_w, ada_b, norm1_g, norm2_g, w_in, conv_w, ret_gn_g, fox_fb, w_out,
           router_w, router_b, moe_w_gate, moe_w_up, moe_w_down, final_g):
    Bsz, S, D = x.shape
    assert Bsz == 1, "one sequence per call"
    L = ada_w.shape[0]
    T = S
    conv_wd = conv_w.shape[-1]
    ret_qk = RET_HEADS * RET_DK
    ret_v = RET_HEADS * RET_DV
    fox_w = FOX_HEADS * FOX_DH
    tn = INPROJ_TN
    ret_col0 = 3 * conv_wd
    fox_col0 = ret_col0 + 2 * ret_qk + 2 * ret_v
    n_main = fox_col0 + 3 * fox_w
    assert fox_w == tn and fox_col0 % tn == 0
    q_block = fox_col0 // tn
    v_block = q_block + 2
    k_col0 = fox_col0

    mod = _ada_mod(c, ada_w, ada_b)
    cos_t, sin_t = _rope_tables(T)
    rw_t = router_w.T
    xt = x.reshape(T, D)

    for l in range(L):
        sh1, sc1, g1, sh2, sc2, g2 = [mod[l, :, k * D:(k + 1) * D] for k in range(6)]
        w_t = jnp.stack([w_in[l, :, q_block * tn:(q_block + 1) * tn].T,
                         w_in[l, :, v_block * tn:(v_block + 1) * tn].T]).astype(BF16)
        w_ff = jnp.pad(w_in[l, :, n_main:], ((0, 0), (0, LANES - FOX_HEADS))).astype(BF16)
        fb = jnp.pad(fox_fb[l], (0, LANES - FOX_HEADS)).reshape(1, LANES)
        proj, proj_t, cum = _inproj(xt, norm1_g[l].reshape(1, D), sh1, sc1, w_in, l, w_t, w_ff, fb,
                                    q_block, v_block, LOG2E * FOX_DH ** -0.5)

        y_conv = _conv(proj, conv_w[l])
        y_ret = _retention(proj, cos_t, sin_t, ret_gn_g[l].reshape(1, ret_v), ret_col0)
        y_fox = _fox(proj, proj_t, cum, k_col0)

        wo = w_out[l].astype(BF16)
        xt, h2, logits_t = _outproj(
            y_conv, y_ret, y_fox, wo[:conv_wd], wo[conv_wd:conv_wd + ret_v], wo[conv_wd + ret_v:],
            xt, g1, norm2_g[l].reshape(1, D), sh2, sc2, rw_t)

        top_w, sorted_a, blk_e, blk_start, blk_n, blk_first, total, nb = _route(
            logits_t, router_b, MOE_ROWS)
        y2 = _moe(h2, l, moe_w_gate, moe_w_up, moe_w_down, sorted_a, blk_e, blk_start, blk_n,
                  blk_first, total, nb)
        xt = _combine(xt, y2, top_w, g2, final_g.reshape(1, D), final=(l == L - 1))

    return xt.reshape(Bsz, S, D)
```

```python
import functools
import math

import numpy as np
import jax
import jax.numpy as jnp
from jax import lax
from jax.experimental import pallas as pl
from jax.experimental.pallas import tpu as pltpu

F32 = jnp.float32
BF16 = jnp.bfloat16

CHUNK = 64
CONV_K = 3
RET_HEADS = 6
RET_DK = 64
RET_DV = 128
FOX_HEADS = 6
FOX_DH = 128
N_EXPERTS = 64
N_GROUPS = 8
GROUP_SIZE = N_EXPERTS // N_GROUPS
TOP_K = 2
ROPE_BASE = 10000.0
EPS = 1e-6
LOG2E = 1.4426950408889634

LANES = 128
SUBLANES = 8

MOD_TN = 1024
INPROJ_TM = 1024
INPROJ_TN = 768
CONV_TM = 1024
CONV_HALO = 16
RET_ROWS = 256
FOX_TQ = 1024
FOX_SLAB = 16
OUT_TM = 512
MOE_ROWS = 256
ROUTE_TT = 1024
DISPATCH_TT = 2048
COMB_TM = 512
VMEM_LIMIT = 56 * 1024 * 1024


def _cparams(sem, **kw):
    return pltpu.CompilerParams(dimension_semantics=sem, vmem_limit_bytes=VMEM_LIMIT, **kw)


def _mod_kernel(c_ref, w_ref, b_ref, o_ref):
    c = c_ref[...]
    ca = c * jax.nn.sigmoid(c)
    o_ref[0] = jnp.sum(w_ref[0] * ca, axis=0, keepdims=True) + b_ref[0]


def _ada_mod(c, ada_w, ada_b):
    L, D, N = ada_w.shape
    tn = MOD_TN
    return pl.pallas_call(
        _mod_kernel,
        grid=(L, N // tn),
        in_specs=[
            pl.BlockSpec((D, 1), lambda l, j: (0, 0)),
            pl.BlockSpec((1, D, tn), lambda l, j: (l, 0, j)),
            pl.BlockSpec((1, 1, tn), lambda l, j: (l, 0, j)),
        ],
        out_specs=pl.BlockSpec((1, 1, tn), lambda l, j: (l, 0, j)),
        out_shape=jax.ShapeDtypeStruct((L, 1, N), F32),
        compiler_params=_cparams(("arbitrary", "arbitrary")),
        name="ada_mod",
    )(c.reshape(D, 1), ada_w, ada_b.reshape(L, 1, N))


def _modulated_norm(x, g, sh, sc):
    ms = jnp.mean(x * x, axis=-1, keepdims=True)
    y = x * lax.rsqrt(ms + EPS) * g
    return y * (1.0 + sc) + sh


def _log_sigmoid(x):
    return jnp.minimum(x, 0.0) - jnp.log1p(jnp.exp(-jnp.abs(x)))


def _inproj_kernel(x_ref, g_ref, sh_ref, sc_ref, w_ref, wff_ref, fb_ref,
                   proj_ref, projt_ref, cum_ref, h_scr, wt_scr, carry_scr,
                   *, q_block, v_block, q_scale):
    i = pl.program_id(0)
    j = pl.program_id(1)
    tm = x_ref.shape[0]

    @pl.when(jnp.logical_and(i == 0, j == 0))
    def _():
        carry_scr[...] = jnp.zeros_like(carry_scr)

    @pl.when(j == 0)
    def _():
        h = _modulated_norm(x_ref[...], g_ref[...], sh_ref[...], sc_ref[...])
        hb = h.astype(BF16)
        h_scr[...] = hb
        ff = jnp.dot(hb, wff_ref[...], preferred_element_type=F32) + fb_ref[...]
        c = _log_sigmoid(ff)
        row = lax.broadcasted_iota(jnp.int32, c.shape, 0)
        d = 1
        while d < tm:
            c = c + jnp.where(row >= d, pltpu.roll(c, d, axis=0), 0.0)
            d *= 2
        c = c + carry_scr[...]
        cum_ref[...] = c
        carry_scr[...] = c[tm - 1:tm, :]

    @pl.when(jnp.logical_and(j != q_block, j != v_block))
    def _():
        proj_ref[...] = jnp.dot(h_scr[...], w_ref[...].astype(BF16),
                                preferred_element_type=F32).astype(BF16)

    def feature_major(slot, scale):
        @pl.when(i == 0)
        def _():
            wt_scr[slot] = w_ref[...].T.astype(BF16)

        r = lax.dot_general(wt_scr[slot], h_scr[...], (((1,), (1,)), ((), ())),
                            preferred_element_type=F32)
        projt_ref[...] = (r * scale).astype(BF16)

    @pl.when(j == q_block)
    def _():
        feature_major(0, q_scale)

    @pl.when(j == v_block)
    def _():
        feature_major(1, 1.0)


def _inproj(x, g, sh, sc, w_in, layer, w_ff, fb, q_block, v_block, q_scale):
    T, D = x.shape
    tm, tn = min(INPROJ_TM, T), INPROJ_TN
    nj = v_block + 1
    is_q = lambda j: (j == q_block).astype(jnp.int32)
    is_v = lambda j: (j == v_block).astype(jnp.int32)
    w_col = lambda i, j: jnp.where(i == 0, j, j - is_q(j) - is_v(j))
    out_col = lambda j: j - (j >= q_block).astype(jnp.int32) - (j >= v_block).astype(jnp.int32)
    vec = pl.BlockSpec((1, D), lambda i, j: (0, 0))
    return pl.pallas_call(
        functools.partial(_inproj_kernel, q_block=q_block, v_block=v_block, q_scale=q_scale),
        grid=(T // tm, nj),
        in_specs=[
            pl.BlockSpec((tm, D), lambda i, j: (i, 0), pipeline_mode=pl.Buffered(1)),
            vec, vec, vec,
            pl.BlockSpec((None, D, tn), lambda i, j: (layer, 0, w_col(i, j))),
            pl.BlockSpec((D, LANES), lambda i, j: (0, 0)),
            pl.BlockSpec((1, LANES), lambda i, j: (0, 0)),
        ],
        out_specs=[
            pl.BlockSpec((tm, tn), lambda i, j: (i, out_col(j))),
            pl.BlockSpec((tn, tm), lambda i, j: (is_v(j), i)),
            pl.BlockSpec((tm, LANES), lambda i, j: (i, 0)),
        ],
        out_shape=[
            jax.ShapeDtypeStruct((T, (nj - 2) * tn), BF16),
            jax.ShapeDtypeStruct((2 * tn, T), BF16),
            jax.ShapeDtypeStruct((T, LANES), F32),
        ],
        scratch_shapes=[pltpu.VMEM((tm, D), BF16), pltpu.VMEM((2, tn, D), BF16),
                        pltpu.VMEM((1, LANES), F32)],
        compiler_params=_cparams(("arbitrary", "arbitrary")),
        name="inproj",
    )(x, g, sh, sc, w_in, w_ff, fb)


def _conv_kernel(cb_ref, cc_ref, cu_ref, hc_ref, hu_ref, w_ref, o_ref):
    i = pl.program_id(0)
    z = cc_ref[...].astype(F32) * cu_ref[...].astype(F32)
    zh = hc_ref[...].astype(F32) * hu_ref[...].astype(F32)
    zh = jnp.where(i > 0, zh, 0.0)
    hl = zh.shape[0]
    zm1 = zh[hl - 1:hl, :]
    zm2 = zh[hl - 2:hl - 1, :]
    row = lax.broadcasted_iota(jnp.int32, z.shape, 0)
    z1 = jnp.where(row == 0, zm1, pltpu.roll(z, 1, axis=0))
    z2 = jnp.where(row == 0, zm2, jnp.where(row == 1, zm1, pltpu.roll(z, 2, axis=0)))
    w = w_ref[...]
    y = z2 * w[0:1, :] + z1 * w[1:2, :] + z * w[2:3, :]
    o_ref[...] = (cb_ref[...].astype(F32) * y).astype(BF16)


def _conv(proj, conv_w):
    T = proj.shape[0]
    W = conv_w.shape[1]
    tm = min(CONV_TM, T)
    hb = tm // CONV_HALO
    halo = lambda c: pl.BlockSpec((CONV_HALO, W), lambda i: (jnp.maximum(i * hb - 1, 0), c))
    return pl.pallas_call(
        _conv_kernel,
        grid=(T // tm,),
        in_specs=[
            pl.BlockSpec((tm, W), lambda i: (i, 0)),
            pl.BlockSpec((tm, W), lambda i: (i, 1)),
            pl.BlockSpec((tm, W), lambda i: (i, 2)),
            halo(1), halo(2),
            pl.BlockSpec((CONV_K, W), lambda i: (0, 0)),
        ],
        out_specs=pl.BlockSpec((tm, W), lambda i: (i, 0)),
        out_shape=jax.ShapeDtypeStruct((T, W), BF16),
        compiler_params=_cparams(("arbitrary",)),
        name="conv",
    )(proj, proj, proj, proj, proj, conv_w)


def _ret_gammas():
    return [1.0 - 2.0 ** (-5.0 - h) for h in range(RET_HEADS)]


def _ret_tables(R):
    n = np.arange(R, dtype=np.float64)
    chunk = np.arange(R) // CHUNK
    allowed = chunk[None, :] <= chunk[:, None]
    dm, qd, kd = [], [], []
    for g in _ret_gammas():
        lg = math.log(g)
        dm.append(np.where(allowed, np.exp(lg * np.abs(n[:, None] - n[None, :])), 0.0))
        qd.append(np.broadcast_to(np.exp(lg * (n + 1.0))[:, None], (R, LANES)))
        kd.append(np.broadcast_to(np.exp(lg * (R - 1.0 - n))[:, None], (R, LANES)))
    f = lambda a: jnp.asarray(np.stack(a), dtype=F32)
    return f(dm), f(qd), f(kd)


def _rope_tables(T):
    half = RET_DK // 2
    inv = ROPE_BASE ** (-jnp.arange(half, dtype=F32) / half)
    ang = jnp.arange(T, dtype=F32)[:, None] * inv[None, :]
    cos, sin = jnp.cos(ang), jnp.sin(ang)
    reps = LANES // RET_DK
    cos_t = jnp.tile(jnp.concatenate([cos, cos], axis=1), (1, reps))
    sin_t = jnp.tile(jnp.concatenate([-sin, sin], axis=1), (1, reps))
    return cos_t, sin_t


def _ret_kernel(q_ref, k_ref, v_ref, g_ref, cos_ref, sin_ref, dm_ref, qd_ref, kd_ref, gn_ref,
                o_ref, s_scr):
    i = pl.program_id(0)
    R = q_ref.shape[0]

    @pl.when(i == 0)
    def _():
        s_scr[...] = jnp.zeros_like(s_scr)

    lane = lax.broadcasted_iota(jnp.int32, (R, LANES), 1)
    first_half = (lane % RET_DK) < (RET_DK // 2)
    low_head = lane < RET_DK
    cosv = cos_ref[...]
    sinv = sin_ref[...]
    c_dec = [g ** R for g in _ret_gammas()]

    def rot(t):
        swapped = jnp.where(first_half, pltpu.roll(t, LANES - RET_DK // 2, axis=1),
                            pltpu.roll(t, RET_DK // 2, axis=1))
        return t * cosv + swapped * sinv

    heads_per_vreg = LANES // RET_DK
    for p in range(RET_HEADS // heads_per_vreg):
        cols = slice(p * LANES, (p + 1) * LANES)
        qr = rot(q_ref[:, cols].astype(F32))
        kb = (rot(k_ref[:, cols].astype(F32)) * (RET_DK ** -0.5)).astype(BF16)
        for hh in range(heads_per_vreg):
            h = p * heads_per_vreg + hh
            hc = slice(h * RET_DV, (h + 1) * RET_DV)
            mask = low_head if hh == 0 else jnp.logical_not(low_head)
            qm = jnp.where(mask, qr, 0.0).astype(BF16)
            s = lax.dot_general(qm, kb, (((1,), (1,)), ((), ())), preferred_element_type=F32)
            s = s * dm_ref[h]
            v = v_ref[:, hc]
            o = jnp.dot(s.astype(BF16), v, preferred_element_type=F32)
            state = s_scr[h]
            o = o + jnp.dot(qm, state.astype(BF16), preferred_element_type=F32) * qd_ref[h]
            vd = (v.astype(F32) * kd_ref[h]).astype(BF16)
            kv = lax.dot_general(kb, vd, (((0,), (0,)), ((), ())), preferred_element_type=F32)
            s_scr[h] = state * c_dec[h] + kv
            mu = jnp.mean(o, axis=-1, keepdims=True)
            d = o - mu
            var = jnp.mean(d * d, axis=-1, keepdims=True)
            on = d * lax.rsqrt(var + EPS) * gn_ref[:, hc]
            gate = g_ref[:, hc].astype(F32)
            o_ref[:, hc] = (gate * jax.nn.sigmoid(gate) * on).astype(BF16)


def _retention(proj, cos_t, sin_t, gn_g, col0):
    T = proj.shape[0]
    R = min(RET_ROWS, T)
    QK = RET_HEADS * RET_DK
    V = RET_HEADS * RET_DV
    dm, qd, kd = _ret_tables(R)
    q_blk = col0 // QK
    v_blk = (col0 + 2 * QK) // V
    full3 = lambda a: pl.BlockSpec(a.shape, lambda i: (0, 0, 0))
    return pl.pallas_call(
        _ret_kernel,
        grid=(T // R,),
        in_specs=[
            pl.BlockSpec((R, QK), lambda i: (i, q_blk)),
            pl.BlockSpec((R, QK), lambda i: (i, q_blk + 1)),
            pl.BlockSpec((R, V), lambda i: (i, v_blk)),
            pl.BlockSpec((R, V), lambda i: (i, v_blk + 1)),
            pl.BlockSpec((R, LANES), lambda i: (i, 0)),
            pl.BlockSpec((R, LANES), lambda i: (i, 0)),
            full3(dm), full3(qd), full3(kd),
            pl.BlockSpec((1, V), lambda i: (0, 0)),
        ],
        out_specs=pl.BlockSpec((R, V), lambda i: (i, 0)),
        out_shape=jax.ShapeDtypeStruct((T, V), BF16),
        scratch_shapes=[pltpu.VMEM((RET_HEADS, LANES, RET_DV), F32)],
        compiler_params=_cparams(("arbitrary",)),
        name="retention",
    )(proj, proj, proj, proj, cos_t, sin_t, dm, qd, kd, gn_g)


def _fox_kernel(qt_ref, k_ref, vt_ref, cum_ref, o_ref, ck_scr, s0_scr, s1_scr, p0_scr, p1_scr,
                a0_scr, a1_scr, m_scr, l_scr, acc_scr):
    h = pl.program_id(0)
    qi = pl.program_id(1)
    tq = qt_ref.shape[1]
    tk = s0_scr.shape[0]
    assert tq == 2 * tk
    reps = tq // LANES
    s_scr, p_scr, a_scr = (s0_scr, s1_scr), (p0_scr, p1_scr), (a0_scr, a1_scr)

    lane = lax.broadcasted_iota(jnp.int32, cum_ref.shape, 1)
    col = jnp.sum(jnp.where(lane == h, cum_ref[...], 0.0), axis=1, keepdims=True) * LOG2E
    q0 = pl.multiple_of(qi * tq, tq)
    ck_scr[pl.ds(q0, tq), :] = jnp.broadcast_to(col, cum_ref.shape)

    m_scr[...] = jnp.full_like(m_scr, -jnp.inf)
    l_scr[...] = jnp.zeros_like(l_scr)
    acc_scr[...] = jnp.zeros_like(acc_scr)
    p1_scr[...] = jnp.zeros_like(p1_scr)
    a1_scr[...] = jnp.ones_like(a1_scr)

    def score(c, slot):
        k0 = pl.multiple_of(c * tk, tk)
        s = jnp.dot(k_ref[pl.ds(k0, tk), :], qt_ref[...], preferred_element_type=F32)
        bias = ck_scr[pl.ds(k0, tk), :]
        s_scr[slot][...] = s - jnp.concatenate([bias] * reps, axis=1)

    def softmax(c, slot, masked):
        if masked:
            s = s_scr[slot][...]
            kpos = c * tk + lax.broadcasted_iota(jnp.int32, s.shape, 0)
            qpos = q0 + lax.broadcasted_iota(jnp.int32, s.shape, 1)
            s_scr[slot][...] = jnp.where(kpos <= qpos, s, -jnp.inf)
        sref, pref = s_scr[slot], p_scr[slot]
        mx = sref[0:FOX_SLAB, :]
        for r in range(FOX_SLAB, tk, FOX_SLAB):
            mx = jnp.maximum(mx, sref[r:r + FOX_SLAB, :])
        m_old = m_scr[...]
        m_new = jnp.maximum(m_old, jnp.max(mx, axis=0, keepdims=True))
        alpha = jnp.exp2(m_old - m_new)
        m_rows = jnp.broadcast_to(m_new, (FOX_SLAB, tq))
        psum = jnp.zeros((FOX_SLAB, tq), F32)
        for r in range(0, tk, FOX_SLAB):
            p = jnp.exp2(sref[r:r + FOX_SLAB, :] - m_rows)
            psum = psum + p
            pref[r:r + FOX_SLAB, :] = p.astype(BF16)
        l_scr[...] = alpha * l_scr[...] + jnp.sum(psum, axis=0, keepdims=True)
        a_scr[slot][...] = alpha
        m_scr[...] = m_new

    def accumulate(c, slot):
        k0 = pl.multiple_of(jnp.maximum(c, 0) * tk, tk)
        pv = jnp.dot(vt_ref[:, pl.ds(k0, tk)], p_scr[slot][...], preferred_element_type=F32)
        acc_scr[...] = a_scr[slot][...] * acc_scr[...] + pv

    score(0, 0)

    def pair(i, carry):
        c = 2 * i
        score(c + 1, 1)
        accumulate(c - 1, 1)
        softmax(c, 0, False)
        score(c + 2, 0)
        accumulate(c, 0)
        softmax(c + 1, 1, False)
        return carry

    lax.fori_loop(0, qi, pair, 0)
    c = 2 * qi
    score(c + 1, 1)
    accumulate(c - 1, 1)
    softmax(c, 0, True)
    accumulate(c, 0)
    softmax(c + 1, 1, True)
    accumulate(c + 1, 1)
    o_ref[...] = (acc_scr[...] / l_scr[...]).T.astype(BF16)


def _fox(proj, proj_t, cum, k_col0):
    T = proj.shape[0]
    tq = min(FOX_TQ, T)
    tk = tq // 2
    W = FOX_HEADS * FOX_DH
    kb = k_col0 // FOX_DH
    return pl.pallas_call(
        _fox_kernel,
        grid=(FOX_HEADS, T // tq),
        in_specs=[
            pl.BlockSpec((FOX_DH, tq), lambda h, i: (h, i)),
            pl.BlockSpec((T, FOX_DH), lambda h, i: (0, kb + h)),
            pl.BlockSpec((FOX_DH, T), lambda h, i: (FOX_HEADS + h, 0)),
            pl.BlockSpec((tq, LANES), lambda h, i: (i, 0)),
        ],
        out_specs=pl.BlockSpec((tq, FOX_DH), lambda h, i: (i, h)),
        out_shape=jax.ShapeDtypeStruct((T, W), BF16),
        scratch_shapes=[pltpu.VMEM((T, LANES), F32),
                        pltpu.VMEM((tk, tq), F32), pltpu.VMEM((tk, tq), F32),
                        pltpu.VMEM((tk, tq), BF16), pltpu.VMEM((tk, tq), BF16),
                        pltpu.VMEM((1, tq), F32), pltpu.VMEM((1, tq), F32),
                        pltpu.VMEM((1, tq), F32), pltpu.VMEM((1, tq), F32),
                        pltpu.VMEM((FOX_DH, tq), F32)],
        compiler_params=_cparams(("arbitrary", "arbitrary")),
        name="fox",
    )(proj_t, proj, proj_t, cum)


def _outproj_kernel(yc_ref, yr_ref, yf_ref, wc_ref, wr_ref, wf_ref, x_ref, g1_ref,
                    g_ref, sh_ref, sc_ref, rw_ref, xo_ref, h_ref, lg_ref):
    mix = jnp.dot(yc_ref[...], wc_ref[...], preferred_element_type=F32)
    mix = mix + jnp.dot(yr_ref[...], wr_ref[...], preferred_element_type=F32)
    mix = mix + jnp.dot(yf_ref[...], wf_ref[...], preferred_element_type=F32)
    x = x_ref[...] + g1_ref[...] * mix
    xo_ref[...] = x
    h = _modulated_norm(x, g_ref[...], sh_ref[...], sc_ref[...])
    h_ref[...] = h
    lg_ref[...] = lax.dot_general(rw_ref[...], h, (((1,), (1,)), ((), ())),
                                  precision=lax.Precision.HIGHEST, preferred_element_type=F32)


def _outproj(yc, yr, yf, wc, wr, wf, x, g1, g, sh, sc, rw_t):
    T, D = x.shape
    tm = min(OUT_TM, T)
    E = rw_t.shape[0]
    vec = pl.BlockSpec((1, D), lambda i: (0, 0))
    rows = lambda a: pl.BlockSpec((tm, a.shape[1]), lambda i: (i, 0))
    whole = lambda a: pl.BlockSpec(a.shape, lambda i: (0, 0))
    return pl.pallas_call(
        _outproj_kernel,
        grid=(T // tm,),
        in_specs=[rows(yc), rows(yr), rows(yf), whole(wc), whole(wr), whole(wf), rows(x),
                  vec, vec, vec, vec, whole(rw_t)],
        out_specs=[rows(x), rows(x), pl.BlockSpec((E, tm), lambda i: (0, i))],
        out_shape=[jax.ShapeDtypeStruct((T, D), F32), jax.ShapeDtypeStruct((T, D), F32),
                   jax.ShapeDtypeStruct((E, T), F32)],
        compiler_params=_cparams(("arbitrary",)),
        name="outproj",
    )(yc, yr, yf, wc, wr, wf, x, g1, g, sh, sc, rw_t)


def _route_kernel(lg_ref, b_ref, tri_ref, e_ref, r_ref, w_ref, cnt_ref, carry_scr):
    i = pl.program_id(0)
    E, tt = lg_ref.shape

    @pl.when(i == 0)
    def _():
        carry_scr[...] = jnp.zeros_like(carry_scr)

    aff = jax.nn.sigmoid(lg_ref[...])
    sel = aff + b_ref[...]
    row8 = lax.broadcasted_iota(jnp.int32, (GROUP_SIZE, tt), 0)
    best = None
    for g in range(N_GROUPS):
        slab = sel[g * GROUP_SIZE:(g + 1) * GROUP_SIZE, :]
        m1 = jnp.max(slab, axis=0, keepdims=True)
        i1 = jnp.min(jnp.where(slab == m1, row8, GROUP_SIZE), axis=0, keepdims=True)
        rest = jnp.where(row8 == i1, -jnp.inf, slab)
        m2 = jnp.max(rest, axis=0, keepdims=True)
        i2 = jnp.min(jnp.where(rest == m2, row8, GROUP_SIZE), axis=0, keepdims=True)
        cand = (m1 + m2, g * GROUP_SIZE + i1, g * GROUP_SIZE + i2)
        if best is None:
            best = cand
        else:
            upd = cand[0] > best[0]
            best = tuple(jnp.where(upd, n, o) for n, o in zip(cand, best))
    _, e0, e1 = best

    row = lax.broadcasted_iota(jnp.int32, (E, tt), 0)
    oh0 = row == e0
    oh1 = row == e1
    a0 = jnp.sum(jnp.where(oh0, aff, 0.0), axis=0, keepdims=True)
    a1 = jnp.sum(jnp.where(oh1, aff, 0.0), axis=0, keepdims=True)
    w_ref[0:1, :] = a0 / (a0 + a1)
    w_ref[1:2, :] = a1 / (a0 + a1)
    e_ref[0:1, :] = e0
    e_ref[1:2, :] = e1

    oh = jnp.logical_or(oh0, oh1)
    ohf = jnp.where(oh, 1.0, 0.0)
    before = jnp.dot(ohf.astype(BF16), tri_ref[...], preferred_element_type=F32) + carry_scr[:, 0:1]
    r_ref[0:1, :] = jnp.sum(jnp.where(oh0, before, 0.0), axis=0, keepdims=True).astype(jnp.int32)
    r_ref[1:2, :] = jnp.sum(jnp.where(oh1, before, 0.0), axis=0, keepdims=True).astype(jnp.int32)
    carry = carry_scr[...] + jnp.sum(ohf, axis=1, keepdims=True)
    carry_scr[...] = carry
    cnt_ref[...] = carry


def _route(logits_t, router_b):
    E, T = logits_t.shape
    tt = min(ROUTE_TT, T)
    tri = jnp.asarray(np.triu(np.ones((tt, tt), np.float32), k=1), dtype=BF16)
    pair = lambda dt: jax.ShapeDtypeStruct((TOP_K, T), dt)
    return pl.pallas_call(
        _route_kernel,
        grid=(T // tt,),
        in_specs=[
            pl.BlockSpec((E, tt), lambda i: (0, i)),
            pl.BlockSpec((E, 1), lambda i: (0, 0)),
            pl.BlockSpec((tt, tt), lambda i: (0, 0)),
        ],
        out_specs=[
            pl.BlockSpec((TOP_K, tt), lambda i: (0, i)),
            pl.BlockSpec((TOP_K, tt), lambda i: (0, i)),
            pl.BlockSpec((TOP_K, tt), lambda i: (0, i)),
            pl.BlockSpec((E, LANES), lambda i: (0, 0)),
        ],
        out_shape=[pair(jnp.int32), pair(jnp.int32), pair(F32), jax.ShapeDtypeStruct((E, LANES), F32)],
        scratch_shapes=[pltpu.VMEM((E, LANES), F32)],
        compiler_params=_cparams(("arbitrary",)),
        name="route",
    )(logits_t, router_b.reshape(E, 1), tri)


def _block_tables(counts, T):
    E = counts.shape[0]
    A = T * TOP_K
    B = MOE_ROWS
    seg = (counts + SUBLANES - 1) // SUBLANES * SUBLANES
    seg_end = jnp.cumsum(seg)
    seg_start = seg_end - seg
    nb = (A + E * (B - 1) + B - 1) // B
    nblk = (counts + B - 1) // B
    blk_end = jnp.cumsum(nblk)
    total = blk_end[-1]
    b = jnp.arange(nb, dtype=jnp.int32)
    bc = jnp.minimum(b, total - 1)
    blk_e = jnp.minimum(jnp.searchsorted(blk_end, bc, side="right"), E - 1).astype(jnp.int32)
    local = bc - (blk_end[blk_e] - nblk[blk_e])
    blk_start = (seg_start[blk_e] + local * B).astype(jnp.int32)
    blk_first = jnp.logical_and(b < total, local == 0).astype(jnp.int32)
    rows = A + E * (SUBLANES - 1) + B
    rows = (rows + SUBLANES - 1) // SUBLANES * SUBLANES
    return (seg_start.astype(jnp.int32), seg_end.astype(jnp.int32), blk_e, blk_start, blk_first,
            total.astype(jnp.int32).reshape(1), nb, rows)


def _zero_rows_from(zero_ref, hbm_ref, start, sem):
    piece = zero_ref.shape[0]
    rows = hbm_ref.shape[0]

    def body(j, carry):
        at = pl.multiple_of(jnp.minimum(start + j * piece, rows - piece), SUBLANES)
        cp = pltpu.make_async_copy(zero_ref, hbm_ref.at[pl.ds(at, piece), :], sem)
        cp.start()
        cp.wait()
        return carry

    lax.fori_loop(0, (rows - start + piece - 1) // piece, body, 0)


def _dispatch_kernel(seg_end_ref, pos_ref, h_hbm, xs_hbm, zero_scr, sem, zsem):
    i = pl.program_id(0)
    td = pos_ref.shape[1]
    E = seg_end_ref.shape[0]

    @pl.when(i == 0)
    def _():
        zero_scr[...] = jnp.zeros_like(zero_scr)

        def tail(e, carry):
            end = seg_end_ref[e]
            at = pl.multiple_of(jnp.maximum(end - SUBLANES, 0), SUBLANES)
            cp = pltpu.make_async_copy(zero_scr.at[pl.ds(0, SUBLANES), :],
                                       xs_hbm.at[pl.ds(at, SUBLANES), :], zsem)
            cp.start()
            cp.wait()
            return carry

        lax.fori_loop(0, E, tail, 0)
        _zero_rows_from(zero_scr, xs_hbm, seg_end_ref[E - 1], zsem)

    t0 = i * td

    def body(r, carry):
        for k in range(TOP_K):
            pltpu.make_async_copy(h_hbm.at[pl.ds(t0 + r, 1), :],
                                  xs_hbm.at[pl.ds(pos_ref[k, r], 1), :], sem).start()
        return carry

    lax.fori_loop(0, td, body, 0, unroll=8)
    pltpu.make_async_copy(h_hbm.at[pl.ds(0, TOP_K * td), :], xs_hbm.at[pl.ds(0, TOP_K * td), :],
                          sem).wait()


def _dispatch(h2, pos, seg_end, rows):
    T, D = h2.shape
    td = min(DISPATCH_TT, T)
    grid_spec = pltpu.PrefetchScalarGridSpec(
        num_scalar_prefetch=1,
        grid=(T // td,),
        in_specs=[pl.BlockSpec((TOP_K, td), lambda i, se: (0, i), memory_space=pltpu.SMEM),
                  pl.BlockSpec(memory_space=pl.ANY)],
        out_specs=pl.BlockSpec(memory_space=pl.ANY),
        scratch_shapes=[pltpu.VMEM((MOE_ROWS, D), F32), pltpu.SemaphoreType.DMA(()),
                        pltpu.SemaphoreType.DMA(())],
    )
    return pl.pallas_call(
        _dispatch_kernel,
        grid_spec=grid_spec,
        out_shape=jax.ShapeDtypeStruct((rows, D), F32),
        compiler_params=_cparams(("arbitrary",), disable_bounds_checks=True),
        name="dispatch",
    )(seg_end, pos, h2)


def _experts_kernel(blk_e_ref, blk_start_ref, blk_first_ref, total_ref, tail_ref,
                    xs_hbm, wg_ref, wu_ref, wd_ref, ys_hbm,
                    wg_b, wu_b, wd_b, xbuf, ybuf, isem, osem):
    b = pl.program_id(0)
    total = total_ref[0]
    B = xbuf.shape[1]
    slot = b % 2

    def load(blk, to_slot):
        at = pl.multiple_of(blk_start_ref[blk], SUBLANES)
        return pltpu.make_async_copy(xs_hbm.at[pl.ds(at, B), :], xbuf.at[to_slot], isem.at[to_slot])

    def store(blk, from_slot):
        at = pl.multiple_of(blk_start_ref[blk], SUBLANES)
        return pltpu.make_async_copy(ybuf.at[from_slot], ys_hbm.at[pl.ds(at, B), :], osem)

    @pl.when(b == 0)
    def _():
        load(0, 0).start()
        ybuf[1] = jnp.zeros(ybuf.shape[1:], ybuf.dtype)
        _zero_rows_from(ybuf.at[1], ys_hbm, tail_ref[0], osem)

    @pl.when(b + 1 < total)
    def _():
        load(b + 1, 1 - slot).start()

    @pl.when(b < total)
    def _():
        load(b, slot).wait()

        @pl.when(blk_first_ref[b] == 1)
        def _():
            wg_b[...] = wg_ref[...].astype(BF16)
            wu_b[...] = wu_ref[...].astype(BF16)
            wd_b[...] = wd_ref[...].astype(BF16)

        x = xbuf[slot].astype(BF16)
        g = jnp.dot(x, wg_b[...], preferred_element_type=F32)
        u = jnp.dot(x, wu_b[...], preferred_element_type=F32)
        a = (g * jax.nn.sigmoid(g) * u).astype(BF16)
        ybuf[slot] = jnp.dot(a, wd_b[...], preferred_element_type=F32)

        @pl.when(b > 0)
        def _():
            store(b - 1, 1 - slot).wait()

        store(b, slot).start()

        @pl.when(b == total - 1)
        def _():
            store(b, slot).wait()


def _experts(xs, layer, w_gate, w_up, w_down, blk_e, blk_start, blk_first, total, tail, nb):
    rows, D = xs.shape
    DE = w_gate.shape[-1]
    B = MOE_ROWS
    wspec = lambda r, c: pl.BlockSpec((None, None, r, c), lambda b, be, *_: (layer, be[b], 0, 0))
    grid_spec = pltpu.PrefetchScalarGridSpec(
        num_scalar_prefetch=5,
        grid=(nb,),
        in_specs=[pl.BlockSpec(memory_space=pl.ANY), wspec(D, DE), wspec(D, DE), wspec(DE, D)],
        out_specs=pl.BlockSpec(memory_space=pl.ANY),
        scratch_shapes=[
            pltpu.VMEM((D, DE), BF16), pltpu.VMEM((D, DE), BF16), pltpu.VMEM((DE, D), BF16),
            pltpu.VMEM((2, B, D), F32), pltpu.VMEM((2, B, D), F32),
            pltpu.SemaphoreType.DMA((2,)), pltpu.SemaphoreType.DMA(()),
        ],
    )
    return pl.pallas_call(
        _experts_kernel,
        grid_spec=grid_spec,
        out_shape=jax.ShapeDtypeStruct((rows, D), F32),
        compiler_params=_cparams(("arbitrary",)),
        name="experts",
    )(blk_e, blk_start, blk_first, total, tail, xs, w_gate, w_up, w_down)


def _combine_kernel(pos_ref, posn_ref, x_ref, ys_hbm, w_ref, g2_ref, fg_ref, o_ref, gbuf, sem,
                    *, final):
    i = pl.program_id(0)
    n = pl.num_programs(0)
    tm = x_ref.shape[0]
    slot = i % 2

    def gather(p_ref, to_slot):
        def body(r, carry):
            for k in range(TOP_K):
                pltpu.make_async_copy(ys_hbm.at[pl.ds(p_ref[k, r], 1), :],
                                      gbuf.at[to_slot, k, pl.ds(r, 1), :], sem.at[to_slot]).start()
            return carry

        lax.fori_loop(0, tm, body, 0, unroll=8)

    @pl.when(i == 0)
    def _():
        gather(pos_ref, 0)

    @pl.when(i + 1 < n)
    def _():
        gather(posn_ref, 1 - slot)

    for k in range(TOP_K):
        pltpu.make_async_copy(ys_hbm.at[pl.ds(0, tm), :], gbuf.at[slot, k], sem.at[slot]).wait()

    w = w_ref[...]
    y = gbuf[slot, 0] * w[:, 0:1] + gbuf[slot, 1] * w[:, 1:2]
    x = x_ref[...] + g2_ref[...] * y
    if final:
        ms = jnp.mean(x * x, axis=-1, keepdims=True)
        x = x * lax.rsqrt(ms + EPS) * fg_ref[...]
    o_ref[...] = x


def _combine(x, ys, pos, top_w, g2, final_g, final):
    T, D = x.shape
    tm = min(COMB_TM, T)
    n = T // tm
    vec = pl.BlockSpec((1, D), lambda i: (0, 0))
    return pl.pallas_call(
        functools.partial(_combine_kernel, final=final),
        grid=(n,),
        in_specs=[
            pl.BlockSpec((TOP_K, tm), lambda i: (0, i), memory_space=pltpu.SMEM),
            pl.BlockSpec((TOP_K, tm), lambda i: (0, jnp.minimum(i + 1, n - 1)), memory_space=pltpu.SMEM),
            pl.BlockSpec((tm, D), lambda i: (i, 0)),
            pl.BlockSpec(memory_space=pl.ANY),
            pl.BlockSpec((tm, TOP_K), lambda i: (i, 0)),
            vec, vec,
        ],
        out_specs=pl.BlockSpec((tm, D), lambda i: (i, 0)),
        out_shape=jax.ShapeDtypeStruct((T, D), F32),
        scratch_shapes=[pltpu.VMEM((2, TOP_K, tm, D), F32), pltpu.SemaphoreType.DMA((2,))],
        compiler_params=_cparams(("arbitrary",), disable_bounds_checks=True),
        name="combine",
    )(pos, pos, x, ys, top_w, g2, final_g)


def kernel(x, c, ada_w, ada_b, norm1_g, norm2_g, w_in, conv_w, ret_gn_g, fox_fb, w_out,
           router_w, router_b, moe_w_gate, moe_w_up, moe_w_down, final_g):
    Bsz, S, D = x.shape
    assert Bsz == 1, "one sequence per call"
    L = ada_w.shape[0]
    T = S
    conv_wd = conv_w.shape[-1]
    ret_qk = RET_HEADS * RET_DK
    ret_v = RET_HEADS * RET_DV
    fox_w = FOX_HEADS * FOX_DH
    tn = INPROJ_TN
    ret_col0 = 3 * conv_wd
    fox_col0 = ret_col0 + 2 * ret_qk + 2 * ret_v
    n_main = fox_col0 + 3 * fox_w
    assert fox_w == tn and fox_col0 % tn == 0
    q_block = fox_col0 // tn
    v_block = q_block + 2
    k_col0 = fox_col0

    mod = _ada_mod(c, ada_w, ada_b)
    cos_t, sin_t = _rope_tables(T)
    rw_t = router_w.T
    xt = x.reshape(T, D)

    for l in range(L):
        sh1, sc1, g1, sh2, sc2, g2 = [mod[l, :, k * D:(k + 1) * D] for k in range(6)]
        w_ff = jnp.pad(w_in[l, :, n_main:], ((0, 0), (0, LANES - FOX_HEADS))).astype(BF16)
        fb = jnp.pad(fox_fb[l], (0, LANES - FOX_HEADS)).reshape(1, LANES)
        proj, proj_t, cum = _inproj(xt, norm1_g[l].reshape(1, D), sh1, sc1, w_in, l, w_ff, fb,
                                    q_block, v_block, LOG2E * FOX_DH ** -0.5)

        y_conv = _conv(proj, conv_w[l])
        y_ret = _retention(proj, cos_t, sin_t, ret_gn_g[l].reshape(1, ret_v), ret_col0)
        y_fox = _fox(proj, proj_t, cum, k_col0)

        wo = w_out[l].astype(BF16)
        xt, h2, logits_t = _outproj(
            y_conv, y_ret, y_fox, wo[:conv_wd], wo[conv_wd:conv_wd + ret_v], wo[conv_wd + ret_v:],
            xt, g1, norm2_g[l].reshape(1, D), sh2, sc2, rw_t)

        top_e, rank, top_w, cnt = _route(logits_t, router_b)
        counts = cnt[:, 0].astype(jnp.int32)
        seg_start, seg_end, blk_e, blk_start, blk_first, total, nb, rows = _block_tables(counts, T)
        pos = seg_start[top_e] + rank
        xs = _dispatch(h2, pos, seg_end, rows)
        ys = _experts(xs, l, moe_w_gate, moe_w_up, moe_w_down, blk_e, blk_start, blk_first, total,
                      seg_end[-1:], nb)
        xt = _combine(xt, ys, pos, top_w.T, g2, final_g.reshape(1, D), final=(l == L - 1))

    return xt.reshape(Bsz, S, D)
```

```python
import functools
import math

import numpy as np
import jax
import jax.numpy as jnp
from jax import lax
from jax.experimental import pallas as pl
from jax.experimental.pallas import tpu as pltpu

F32 = jnp.float32
BF16 = jnp.bfloat16

CHUNK = 64
CONV_K = 3
RET_HEADS = 6
RET_DK = 64
RET_DV = 128
FOX_HEADS = 6
FOX_DH = 128
N_EXPERTS = 64
N_GROUPS = 8
GROUP_SIZE = N_EXPERTS // N_GROUPS
TOP_K = 2
ROPE_BASE = 10000.0
EPS = 1e-6
LOG2E = 1.4426950408889634

LANES = 128
SUBLANES = 8

MOD_TN = 1024
INPROJ_TM = 1024
INPROJ_TN = 768
CONV_TM = 1024
CONV_HALO = 16
RET_ROWS = 256
FOX_TQ = 1024
FOX_SLAB = 16
OUT_TM = 512
MOE_ROWS = 256
ROUTE_TT = 1024
DISPATCH_TT = 512
COMB_TM = 512
VMEM_LIMIT = 56 * 1024 * 1024


def _cparams(sem, **kw):
    return pltpu.CompilerParams(dimension_semantics=sem, vmem_limit_bytes=VMEM_LIMIT, **kw)


def _mod_kernel(c_ref, w_ref, b_ref, o_ref):
    c = c_ref[...]
    ca = c * jax.nn.sigmoid(c)
    o_ref[0] = jnp.sum(w_ref[0] * ca, axis=0, keepdims=True) + b_ref[0]


def _ada_mod(c, ada_w, ada_b):
    L, D, N = ada_w.shape
    tn = MOD_TN
    return pl.pallas_call(
        _mod_kernel,
        grid=(L, N // tn),
        in_specs=[
            pl.BlockSpec((D, 1), lambda l, j: (0, 0)),
            pl.BlockSpec((1, D, tn), lambda l, j: (l, 0, j)),
            pl.BlockSpec((1, 1, tn), lambda l, j: (l, 0, j)),
        ],
        out_specs=pl.BlockSpec((1, 1, tn), lambda l, j: (l, 0, j)),
        out_shape=jax.ShapeDtypeStruct((L, 1, N), F32),
        compiler_params=_cparams(("arbitrary", "arbitrary")),
        name="ada_mod",
    )(c.reshape(D, 1), ada_w, ada_b.reshape(L, 1, N))


def _modulated_norm(x, g, sh, sc):
    ms = jnp.mean(x * x, axis=-1, keepdims=True)
    y = x * lax.rsqrt(ms + EPS) * g
    return y * (1.0 + sc) + sh


def _log_sigmoid(x):
    return jnp.minimum(x, 0.0) - jnp.log1p(jnp.exp(-jnp.abs(x)))


def _inproj_kernel(x_ref, g_ref, sh_ref, sc_ref, w_ref, wff_ref, fb_ref,
                   proj_ref, projt_ref, cum_ref, h_scr, wt_scr, carry_scr,
                   *, q_block, v_block, q_scale):
    i = pl.program_id(0)
    j = pl.program_id(1)
    tm = x_ref.shape[0]

    @pl.when(jnp.logical_and(i == 0, j == 0))
    def _():
        carry_scr[...] = jnp.zeros_like(carry_scr)

    @pl.when(j == 0)
    def _():
        h = _modulated_norm(x_ref[...], g_ref[...], sh_ref[...], sc_ref[...])
        hb = h.astype(BF16)
        h_scr[...] = hb
        ff = jnp.dot(hb, wff_ref[...].astype(BF16), preferred_element_type=F32) + fb_ref[...]
        c = _log_sigmoid(ff)
        row = lax.broadcasted_iota(jnp.int32, c.shape, 0)
        d = 1
        while d < tm:
            c = c + jnp.where(row >= d, pltpu.roll(c, d, axis=0), 0.0)
            d *= 2
        c = c + carry_scr[...]
        cum_ref[...] = c
        carry_scr[...] = c[tm - 1:tm, :]

    @pl.when(jnp.logical_and(j != q_block, j != v_block))
    def _():
        proj_ref[...] = jnp.dot(h_scr[...], w_ref[...].astype(BF16),
                                preferred_element_type=F32).astype(BF16)

    def feature_major(slot, scale):
        @pl.when(i == 0)
        def _():
            wt_scr[slot] = w_ref[...].T.astype(BF16)

        r = lax.dot_general(wt_scr[slot], h_scr[...], (((1,), (1,)), ((), ())),
                            preferred_element_type=F32)
        projt_ref[...] = (r * scale).astype(BF16)

    @pl.when(j == q_block)
    def _():
        feature_major(0, q_scale)

    @pl.when(j == v_block)
    def _():
        feature_major(1, 1.0)


def _inproj(x, g, sh, sc, w_in, layer, w_ff, fb, q_block, v_block, q_scale):
    T, D = x.shape
    tm, tn = min(INPROJ_TM, T), INPROJ_TN
    nj = v_block + 1
    is_q = lambda j: (j == q_block).astype(jnp.int32)
    is_v = lambda j: (j == v_block).astype(jnp.int32)
    w_col = lambda i, j: jnp.where(i == 0, j, j - is_q(j) - is_v(j))
    out_col = lambda j: j - (j >= q_block).astype(jnp.int32) - (j >= v_block).astype(jnp.int32)
    vec = pl.BlockSpec((1, D), lambda i, j: (0, 0))
    return pl.pallas_call(
        functools.partial(_inproj_kernel, q_block=q_block, v_block=v_block, q_scale=q_scale),
        grid=(T // tm, nj),
        in_specs=[
            pl.BlockSpec((tm, D), lambda i, j: (i, 0), pipeline_mode=pl.Buffered(1)),
            vec, vec, vec,
            pl.BlockSpec((None, D, tn), lambda i, j: (layer, 0, w_col(i, j))),
            pl.BlockSpec((D, LANES), lambda i, j: (0, 0)),
            pl.BlockSpec((1, LANES), lambda i, j: (0, 0)),
        ],
        out_specs=[
            pl.BlockSpec((tm, tn), lambda i, j: (i, out_col(j))),
            pl.BlockSpec((tn, tm), lambda i, j: (is_v(j), i)),
            pl.BlockSpec((tm, LANES), lambda i, j: (i, 0)),
        ],
        out_shape=[
            jax.ShapeDtypeStruct((T, (nj - 2) * tn), BF16),
            jax.ShapeDtypeStruct((2 * tn, T), BF16),
            jax.ShapeDtypeStruct((T, LANES), F32),
        ],
        scratch_shapes=[pltpu.VMEM((tm, D), BF16), pltpu.VMEM((2, tn, D), BF16),
                        pltpu.VMEM((1, LANES), F32)],
        compiler_params=_cparams(("arbitrary", "arbitrary")),
        name="inproj",
    )(x, g, sh, sc, w_in, w_ff, fb)


def _conv_kernel(cb_ref, cc_ref, cu_ref, hc_ref, hu_ref, w_ref, o_ref):
    i = pl.program_id(0)
    z = cc_ref[...].astype(F32) * cu_ref[...].astype(F32)
    zh = hc_ref[...].astype(F32) * hu_ref[...].astype(F32)
    zh = jnp.where(i > 0, zh, 0.0)
    hl = zh.shape[0]
    zm1 = zh[hl - 1:hl, :]
    zm2 = zh[hl - 2:hl - 1, :]
    row = lax.broadcasted_iota(jnp.int32, z.shape, 0)
    z1 = jnp.where(row == 0, zm1, pltpu.roll(z, 1, axis=0))
    z2 = jnp.where(row == 0, zm2, jnp.where(row == 1, zm1, pltpu.roll(z, 2, axis=0)))
    w = w_ref[...]
    y = z2 * w[0:1, :] + z1 * w[1:2, :] + z * w[2:3, :]
    o_ref[...] = (cb_ref[...].astype(F32) * y).astype(BF16)


def _conv(proj, conv_w):
    T = proj.shape[0]
    W = conv_w.shape[1]
    tm = min(CONV_TM, T)
    hb = tm // CONV_HALO
    halo = lambda c: pl.BlockSpec((CONV_HALO, W), lambda i: (jnp.maximum(i * hb - 1, 0), c))
    return pl.pallas_call(
        _conv_kernel,
        grid=(T // tm,),
        in_specs=[
            pl.BlockSpec((tm, W), lambda i: (i, 0)),
            pl.BlockSpec((tm, W), lambda i: (i, 1)),
            pl.BlockSpec((tm, W), lambda i: (i, 2)),
            halo(1), halo(2),
            pl.BlockSpec((CONV_K, W), lambda i: (0, 0)),
        ],
        out_specs=pl.BlockSpec((tm, W), lambda i: (i, 0)),
        out_shape=jax.ShapeDtypeStruct((T, W), BF16),
        compiler_params=_cparams(("arbitrary",)),
        name="conv",
    )(proj, proj, proj, proj, proj, conv_w)


def _ret_gammas():
    return [1.0 - 2.0 ** (-5.0 - h) for h in range(RET_HEADS)]


def _ret_tables(R):
    n = np.arange(R, dtype=np.float64)
    chunk = np.arange(R) // CHUNK
    allowed = chunk[None, :] <= chunk[:, None]
    dm, qd, kd = [], [], []
    for g in _ret_gammas():
        lg = math.log(g)
        dm.append(np.where(allowed, np.exp(lg * np.abs(n[:, None] - n[None, :])), 0.0))
        qd.append(np.broadcast_to(np.exp(lg * (n + 1.0))[:, None], (R, LANES)))
        kd.append(np.broadcast_to(np.exp(lg * (R - 1.0 - n))[:, None], (R, LANES)))
    f = lambda a: jnp.asarray(np.stack(a), dtype=F32)
    return f(dm), f(qd), f(kd)


def _rope_tables(T):
    half = RET_DK // 2
    inv = ROPE_BASE ** (-jnp.arange(half, dtype=F32) / half)
    ang = jnp.arange(T, dtype=F32)[:, None] * inv[None, :]
    cos, sin = jnp.cos(ang), jnp.sin(ang)
    reps = LANES // RET_DK
    cos_t = jnp.tile(jnp.concatenate([cos, cos], axis=1), (1, reps))
    sin_t = jnp.tile(jnp.concatenate([-sin, sin], axis=1), (1, reps))
    return cos_t, sin_t


def _ret_kernel(q_ref, k_ref, v_ref, g_ref, cos_ref, sin_ref, dm_ref, qd_ref, kd_ref, gn_ref,
                o_ref, s_scr):
    i = pl.program_id(0)
    R = q_ref.shape[0]

    @pl.when(i == 0)
    def _():
        s_scr[...] = jnp.zeros_like(s_scr)

    lane = lax.broadcasted_iota(jnp.int32, (R, LANES), 1)
    first_half = (lane % RET_DK) < (RET_DK // 2)
    low_head = lane < RET_DK
    cosv = cos_ref[...]
    sinv = sin_ref[...]
    c_dec = [g ** R for g in _ret_gammas()]

    def rot(t):
        swapped = jnp.where(first_half, pltpu.roll(t, LANES - RET_DK // 2, axis=1),
                            pltpu.roll(t, RET_DK // 2, axis=1))
        return t * cosv + swapped * sinv

    heads_per_vreg = LANES // RET_DK
    for p in range(RET_HEADS // heads_per_vreg):
        cols = slice(p * LANES, (p + 1) * LANES)
        qr = rot(q_ref[:, cols].astype(F32))
        kb = (rot(k_ref[:, cols].astype(F32)) * (RET_DK ** -0.5)).astype(BF16)
        for hh in range(heads_per_vreg):
            h = p * heads_per_vreg + hh
            hc = slice(h * RET_DV, (h + 1) * RET_DV)
            mask = low_head if hh == 0 else jnp.logical_not(low_head)
            qm = jnp.where(mask, qr, 0.0).astype(BF16)
            s = lax.dot_general(qm, kb, (((1,), (1,)), ((), ())), preferred_element_type=F32)
            s = s * dm_ref[h]
            v = v_ref[:, hc]
            o = jnp.dot(s.astype(BF16), v, preferred_element_type=F32)
            state = s_scr[h]
            o = o + jnp.dot(qm, state.astype(BF16), preferred_element_type=F32) * qd_ref[h]
            vd = (v.astype(F32) * kd_ref[h]).astype(BF16)
            kv = lax.dot_general(kb, vd, (((0,), (0,)), ((), ())), preferred_element_type=F32)
            s_scr[h] = state * c_dec[h] + kv
            mu = jnp.mean(o, axis=-1, keepdims=True)
            d = o - mu
            var = jnp.mean(d * d, axis=-1, keepdims=True)
            on = d * lax.rsqrt(var + EPS) * gn_ref[:, hc]
            gate = g_ref[:, hc].astype(F32)
            o_ref[:, hc] = (gate * jax.nn.sigmoid(gate) * on).astype(BF16)


def _retention(proj, cos_t, sin_t, gn_g, col0):
    T = proj.shape[0]
    R = min(RET_ROWS, T)
    QK = RET_HEADS * RET_DK
    V = RET_HEADS * RET_DV
    dm, qd, kd = _ret_tables(R)
    q_blk = col0 // QK
    v_blk = (col0 + 2 * QK) // V
    full3 = lambda a: pl.BlockSpec(a.shape, lambda i: (0, 0, 0))
    return pl.pallas_call(
        _ret_kernel,
        grid=(T // R,),
        in_specs=[
            pl.BlockSpec((R, QK), lambda i: (i, q_blk)),
            pl.BlockSpec((R, QK), lambda i: (i, q_blk + 1)),
            pl.BlockSpec((R, V), lambda i: (i, v_blk)),
            pl.BlockSpec((R, V), lambda i: (i, v_blk + 1)),
            pl.BlockSpec((R, LANES), lambda i: (i, 0)),
            pl.BlockSpec((R, LANES), lambda i: (i, 0)),
            full3(dm), full3(qd), full3(kd),
            pl.BlockSpec((1, V), lambda i: (0, 0)),
        ],
        out_specs=pl.BlockSpec((R, V), lambda i: (i, 0)),
        out_shape=jax.ShapeDtypeStruct((T, V), BF16),
        scratch_shapes=[pltpu.VMEM((RET_HEADS, LANES, RET_DV), F32)],
        compiler_params=_cparams(("arbitrary",)),
        name="retention",
    )(proj, proj, proj, proj, cos_t, sin_t, dm, qd, kd, gn_g)


def _fox_kernel(qt_ref, k_ref, vt_ref, cum_ref, o_ref, ck_scr, s0_scr, s1_scr, p0_scr, p1_scr,
                a0_scr, a1_scr, m_scr, l_scr, acc_scr):
    h = pl.program_id(0)
    qi = pl.program_id(1)
    tq = qt_ref.shape[1]
    tk = s0_scr.shape[0]
    assert tq == 2 * tk
    reps = tq // LANES
    s_scr, p_scr, a_scr = (s0_scr, s1_scr), (p0_scr, p1_scr), (a0_scr, a1_scr)

    lane = lax.broadcasted_iota(jnp.int32, cum_ref.shape, 1)
    col = jnp.sum(jnp.where(lane == h, cum_ref[...], 0.0), axis=1, keepdims=True) * LOG2E
    q0 = pl.multiple_of(qi * tq, tq)
    ck_scr[pl.ds(q0, tq), :] = jnp.broadcast_to(col, cum_ref.shape)

    m_scr[...] = jnp.full_like(m_scr, -jnp.inf)
    l_scr[...] = jnp.zeros_like(l_scr)
    acc_scr[...] = jnp.zeros_like(acc_scr)
    p1_scr[...] = jnp.zeros_like(p1_scr)
    a1_scr[...] = jnp.ones_like(a1_scr)

    def score(c, slot):
        k0 = pl.multiple_of(c * tk, tk)
        s = jnp.dot(k_ref[pl.ds(k0, tk), :], qt_ref[...], preferred_element_type=F32)
        bias = ck_scr[pl.ds(k0, tk), :]
        s_scr[slot][...] = s - jnp.concatenate([bias] * reps, axis=1)

    def softmax(c, slot, masked):
        if masked:
            s = s_scr[slot][...]
            kpos = c * tk + lax.broadcasted_iota(jnp.int32, s.shape, 0)
            qpos = q0 + lax.broadcasted_iota(jnp.int32, s.shape, 1)
            s_scr[slot][...] = jnp.where(kpos <= qpos, s, -jnp.inf)
        sref, pref = s_scr[slot], p_scr[slot]
        mx = sref[0:FOX_SLAB, :]
        for r in range(FOX_SLAB, tk, FOX_SLAB):
            mx = jnp.maximum(mx, sref[r:r + FOX_SLAB, :])
        m_old = m_scr[...]
        m_new = jnp.maximum(m_old, jnp.max(mx, axis=0, keepdims=True))
        alpha = jnp.exp2(m_old - m_new)
        m_rows = jnp.broadcast_to(m_new, (FOX_SLAB, tq))
        psum = jnp.zeros((FOX_SLAB, tq), F32)
        for r in range(0, tk, FOX_SLAB):
            p = jnp.exp2(sref[r:r + FOX_SLAB, :] - m_rows)
            psum = psum + p
            pref[r:r + FOX_SLAB, :] = p.astype(BF16)
        l_scr[...] = alpha * l_scr[...] + jnp.sum(psum, axis=0, keepdims=True)
        a_scr[slot][...] = alpha
        m_scr[...] = m_new

    def accumulate(c, slot):
        k0 = pl.multiple_of(jnp.maximum(c, 0) * tk, tk)
        pv = jnp.dot(vt_ref[:, pl.ds(k0, tk)], p_scr[slot][...], preferred_element_type=F32)
        acc_scr[...] = a_scr[slot][...] * acc_scr[...] + pv

    score(0, 0)

    def pair(i, carry):
        c = 2 * i
        score(c + 1, 1)
        accumulate(c - 1, 1)
        softmax(c, 0, False)
        score(c + 2, 0)
        accumulate(c, 0)
        softmax(c + 1, 1, False)
        return carry

    lax.fori_loop(0, qi, pair, 0)
    c = 2 * qi
    score(c + 1, 1)
    accumulate(c - 1, 1)
    softmax(c, 0, True)
    accumulate(c, 0)
    softmax(c + 1, 1, True)
    accumulate(c + 1, 1)
    o_ref[...] = (acc_scr[...] / l_scr[...]).T.astype(BF16)


def _fox(proj, proj_t, cum, k_col0):
    T = proj.shape[0]
    tq = min(FOX_TQ, T)
    tk = tq // 2
    W = FOX_HEADS * FOX_DH
    kb = k_col0 // FOX_DH
    return pl.pallas_call(
        _fox_kernel,
        grid=(FOX_HEADS, T // tq),
        in_specs=[
            pl.BlockSpec((FOX_DH, tq), lambda h, i: (h, i)),
            pl.BlockSpec((T, FOX_DH), lambda h, i: (0, kb + h)),
            pl.BlockSpec((FOX_DH, T), lambda h, i: (FOX_HEADS + h, 0)),
            pl.BlockSpec((tq, LANES), lambda h, i: (i, 0)),
        ],
        out_specs=pl.BlockSpec((tq, FOX_DH), lambda h, i: (i, h)),
        out_shape=jax.ShapeDtypeStruct((T, W), BF16),
        scratch_shapes=[pltpu.VMEM((T, LANES), F32),
                        pltpu.VMEM((tk, tq), F32), pltpu.VMEM((tk, tq), F32),
                        pltpu.VMEM((tk, tq), BF16), pltpu.VMEM((tk, tq), BF16),
                        pltpu.VMEM((1, tq), F32), pltpu.VMEM((1, tq), F32),
                        pltpu.VMEM((1, tq), F32), pltpu.VMEM((1, tq), F32),
                        pltpu.VMEM((FOX_DH, tq), F32)],
        compiler_params=_cparams(("arbitrary", "arbitrary")),
        name="fox",
    )(proj_t, proj, proj_t, cum)


def _outproj_kernel(yc_ref, yr_ref, yf_ref, wc_ref, wr_ref, wf_ref, x_ref, g1_ref,
                    g_ref, sh_ref, sc_ref, rw_ref, xo_ref, h_ref, lg_ref):
    mix = jnp.dot(yc_ref[...], wc_ref[...], preferred_element_type=F32)
    mix = mix + jnp.dot(yr_ref[...], wr_ref[...], preferred_element_type=F32)
    mix = mix + jnp.dot(yf_ref[...], wf_ref[...], preferred_element_type=F32)
    x = x_ref[...] + g1_ref[...] * mix
    xo_ref[...] = x
    h = _modulated_norm(x, g_ref[...], sh_ref[...], sc_ref[...])
    h_ref[...] = h
    lg_ref[...] = lax.dot_general(rw_ref[...], h, (((1,), (1,)), ((), ())),
                                  precision=lax.Precision.HIGHEST, preferred_element_type=F32)


def _outproj(yc, yr, yf, wc, wr, wf, x, g1, g, sh, sc, rw_t):
    T, D = x.shape
    tm = min(OUT_TM, T)
    E = rw_t.shape[0]
    vec = pl.BlockSpec((1, D), lambda i: (0, 0))
    rows = lambda a: pl.BlockSpec((tm, a.shape[1]), lambda i: (i, 0))
    whole = lambda a: pl.BlockSpec(a.shape, lambda i: (0, 0))
    return pl.pallas_call(
        _outproj_kernel,
        grid=(T // tm,),
        in_specs=[rows(yc), rows(yr), rows(yf), whole(wc), whole(wr), whole(wf), rows(x),
                  vec, vec, vec, vec, whole(rw_t)],
        out_specs=[rows(x), rows(x), pl.BlockSpec((E, tm), lambda i: (0, i))],
        out_shape=[jax.ShapeDtypeStruct((T, D), F32), jax.ShapeDtypeStruct((T, D), F32),
                   jax.ShapeDtypeStruct((E, T), F32)],
        compiler_params=_cparams(("arbitrary",)),
        name="outproj",
    )(yc, yr, yf, wc, wr, wf, x, g1, g, sh, sc, rw_t)


def _route_kernel(lg_ref, b_ref, tri_ref, e_ref, r_ref, w_ref, cnt_ref, carry_scr):
    i = pl.program_id(0)
    E, tt = lg_ref.shape

    @pl.when(i == 0)
    def _():
        carry_scr[...] = jnp.zeros_like(carry_scr)

    aff = jax.nn.sigmoid(lg_ref[...])
    sel = aff + b_ref[...]
    row8 = lax.broadcasted_iota(jnp.int32, (GROUP_SIZE, tt), 0)
    best = None
    for g in range(N_GROUPS):
        slab = sel[g * GROUP_SIZE:(g + 1) * GROUP_SIZE, :]
        m1 = jnp.max(slab, axis=0, keepdims=True)
        i1 = jnp.min(jnp.where(slab == m1, row8, GROUP_SIZE), axis=0, keepdims=True)
        rest = jnp.where(row8 == i1, -jnp.inf, slab)
        m2 = jnp.max(rest, axis=0, keepdims=True)
        i2 = jnp.min(jnp.where(rest == m2, row8, GROUP_SIZE), axis=0, keepdims=True)
        cand = (m1 + m2, g * GROUP_SIZE + i1, g * GROUP_SIZE + i2)
        if best is None:
            best = cand
        else:
            upd = cand[0] > best[0]
            best = tuple(jnp.where(upd, n, o) for n, o in zip(cand, best))
    _, e0, e1 = best

    row = lax.broadcasted_iota(jnp.int32, (E, tt), 0)
    oh0 = row == e0
    oh1 = row == e1
    a0 = jnp.sum(jnp.where(oh0, aff, 0.0), axis=0, keepdims=True)
    a1 = jnp.sum(jnp.where(oh1, aff, 0.0), axis=0, keepdims=True)
    w_ref[0:1, :] = a0 / (a0 + a1)
    w_ref[1:2, :] = a1 / (a0 + a1)
    e_ref[0:1, :] = e0
    e_ref[1:2, :] = e1

    oh = jnp.logical_or(oh0, oh1)
    ohf = jnp.where(oh, 1.0, 0.0)
    before = jnp.dot(ohf.astype(BF16), tri_ref[...], preferred_element_type=F32) + carry_scr[:, 0:1]
    r_ref[0:1, :] = jnp.sum(jnp.where(oh0, before, 0.0), axis=0, keepdims=True).astype(jnp.int32)
    r_ref[1:2, :] = jnp.sum(jnp.where(oh1, before, 0.0), axis=0, keepdims=True).astype(jnp.int32)
    carry = carry_scr[...] + jnp.sum(ohf, axis=1, keepdims=True)
    carry_scr[...] = carry
    cnt_ref[...] = carry


def _route(logits_t, router_b):
    E, T = logits_t.shape
    tt = min(ROUTE_TT, T)
    tri = jnp.asarray(np.triu(np.ones((tt, tt), np.float32), k=1), dtype=BF16)
    pair = lambda dt: jax.ShapeDtypeStruct((TOP_K, T), dt)
    return pl.pallas_call(
        _route_kernel,
        grid=(T // tt,),
        in_specs=[
            pl.BlockSpec((E, tt), lambda i: (0, i)),
            pl.BlockSpec((E, 1), lambda i: (0, 0)),
            pl.BlockSpec((tt, tt), lambda i: (0, 0)),
        ],
        out_specs=[
            pl.BlockSpec((TOP_K, tt), lambda i: (0, i)),
            pl.BlockSpec((TOP_K, tt), lambda i: (0, i)),
            pl.BlockSpec((TOP_K, tt), lambda i: (0, i)),
            pl.BlockSpec((E, LANES), lambda i: (0, 0)),
        ],
        out_shape=[pair(jnp.int32), pair(jnp.int32), pair(F32), jax.ShapeDtypeStruct((E, LANES), F32)],
        scratch_shapes=[pltpu.VMEM((E, LANES), F32)],
        compiler_params=_cparams(("arbitrary",)),
        name="route",
    )(logits_t, router_b.reshape(E, 1), tri)


def _block_tables(counts, T):
    E = counts.shape[0]
    A = T * TOP_K
    B = MOE_ROWS
    seg = (counts + SUBLANES - 1) // SUBLANES * SUBLANES
    seg_end = jnp.cumsum(seg)
    seg_start = seg_end - seg
    nb = (A + E * (B - 1) + B - 1) // B
    nblk = (counts + B - 1) // B
    blk_end = jnp.cumsum(nblk)
    total = blk_end[-1]
    b = jnp.arange(nb, dtype=jnp.int32)
    bc = jnp.minimum(b, total - 1)
    blk_e = jnp.minimum(jnp.searchsorted(blk_end, bc, side="right"), E - 1).astype(jnp.int32)
    local = bc - (blk_end[blk_e] - nblk[blk_e])
    blk_start = (seg_start[blk_e] + local * B).astype(jnp.int32)
    blk_first = jnp.logical_and(b < total, local == 0).astype(jnp.int32)
    rows = A + E * (SUBLANES - 1) + B
    rows = (rows + SUBLANES - 1) // SUBLANES * SUBLANES
    return (seg_start.astype(jnp.int32), seg_end.astype(jnp.int32), blk_e, blk_start, blk_first,
            total.astype(jnp.int32).reshape(1), nb, rows)


def _zero_rows_from(zero_ref, hbm_ref, start, sem):
    piece = zero_ref.shape[0]
    rows = hbm_ref.shape[0]

    def body(j, carry):
        at = pl.multiple_of(jnp.minimum(start + j * piece, rows - piece), SUBLANES)
        cp = pltpu.make_async_copy(zero_ref, hbm_ref.at[pl.ds(at, piece), :], sem)
        cp.start()
        cp.wait()
        return carry

    lax.fori_loop(0, (rows - start + piece - 1) // piece, body, 0)


def _dispatch_kernel(seg_end_ref, pos_ref, h_ref, xs_hbm, zero_scr, sem, zsem):
    i = pl.program_id(0)
    td = pos_ref.shape[1]
    E = seg_end_ref.shape[0]

    @pl.when(i == 0)
    def _():
        zero_scr[...] = jnp.zeros_like(zero_scr)

        def tail(e, carry):
            end = seg_end_ref[e]
            at = pl.multiple_of(jnp.maximum(end - SUBLANES, 0), SUBLANES)
            cp = pltpu.make_async_copy(zero_scr.at[pl.ds(0, SUBLANES), :],
                                       xs_hbm.at[pl.ds(at, SUBLANES), :], zsem)
            cp.start()
            cp.wait()
            return carry

        lax.fori_loop(0, E, tail, 0)
        _zero_rows_from(zero_scr, xs_hbm, seg_end_ref[E - 1], zsem)

    def body(r, carry):
        for k in range(TOP_K):
            pltpu.make_async_copy(h_ref.at[pl.ds(r, 1), :],
                                  xs_hbm.at[pl.ds(pos_ref[k, r], 1), :], sem).start()
        return carry

    lax.fori_loop(0, td, body, 0, unroll=8)
    for k in range(TOP_K):
        pltpu.make_async_copy(h_ref, xs_hbm.at[pl.ds(0, td), :], sem).wait()


def _dispatch(h2, pos, seg_end, rows):
    T, D = h2.shape
    td = min(DISPATCH_TT, T)
    grid_spec = pltpu.PrefetchScalarGridSpec(
        num_scalar_prefetch=1,
        grid=(T // td,),
        in_specs=[pl.BlockSpec((TOP_K, td), lambda i, se: (0, i), memory_space=pltpu.SMEM),
                  pl.BlockSpec((td, D), lambda i, se: (i, 0))],
        out_specs=pl.BlockSpec(memory_space=pl.ANY),
        scratch_shapes=[pltpu.VMEM((MOE_ROWS, D), F32), pltpu.SemaphoreType.DMA(()),
                        pltpu.SemaphoreType.DMA(())],
    )
    return pl.pallas_call(
        _dispatch_kernel,
        grid_spec=grid_spec,
        out_shape=jax.ShapeDtypeStruct((rows, D), F32),
        compiler_params=_cparams(("arbitrary",), disable_bounds_checks=True),
        name="dispatch",
    )(seg_end, pos, h2)


def _experts_kernel(blk_e_ref, blk_start_ref, blk_first_ref, total_ref, tail_ref,
                    xs_hbm, wg_ref, wu_ref, wd_ref, ys_hbm,
                    wg_b, wu_b, wd_b, xbuf, ybuf, isem, osem):
    b = pl.program_id(0)
    total = total_ref[0]
    B = xbuf.shape[1]
    slot = b % 2

    def load(blk, to_slot):
        at = pl.multiple_of(blk_start_ref[blk], SUBLANES)
        return pltpu.make_async_copy(xs_hbm.at[pl.ds(at, B), :], xbuf.at[to_slot], isem.at[to_slot])

    def store(blk, from_slot):
        at = pl.multiple_of(blk_start_ref[blk], SUBLANES)
        return pltpu.make_async_copy(ybuf.at[from_slot], ys_hbm.at[pl.ds(at, B), :], osem)

    @pl.when(b == 0)
    def _():
        load(0, 0).start()
        ybuf[1] = jnp.zeros(ybuf.shape[1:], ybuf.dtype)
        _zero_rows_from(ybuf.at[1], ys_hbm, tail_ref[0], osem)

    @pl.when(b + 1 < total)
    def _():
        load(b + 1, 1 - slot).start()

    @pl.when(b < total)
    def _():
        load(b, slot).wait()

        @pl.when(blk_first_ref[b] == 1)
        def _():
            wg_b[...] = wg_ref[...].astype(BF16)
            wu_b[...] = wu_ref[...].astype(BF16)
            wd_b[...] = wd_ref[...].astype(BF16)

        x = xbuf[slot].astype(BF16)
        g = jnp.dot(x, wg_b[...], preferred_element_type=F32)
        u = jnp.dot(x, wu_b[...], preferred_element_type=F32)
        a = (g * jax.nn.sigmoid(g) * u).astype(BF16)
        ybuf[slot] = jnp.dot(a, wd_b[...], preferred_element_type=F32)

        @pl.when(b > 0)
        def _():
            store(b - 1, 1 - slot).wait()

        store(b, slot).start()

        @pl.when(b == total - 1)
        def _():
            store(b, slot).wait()


def _experts(xs, layer, w_gate, w_up, w_down, blk_e, blk_start, blk_first, total, tail, nb):
    rows, D = xs.shape
    DE = w_gate.shape[-1]
    B = MOE_ROWS
    wspec = lambda r, c: pl.BlockSpec((None, None, r, c), lambda b, be, *_: (layer, be[b], 0, 0))
    grid_spec = pltpu.PrefetchScalarGridSpec(
        num_scalar_prefetch=5,
        grid=(nb,),
        in_specs=[pl.BlockSpec(memory_space=pl.ANY), wspec(D, DE), wspec(D, DE), wspec(DE, D)],
        out_specs=pl.BlockSpec(memory_space=pl.ANY),
        scratch_shapes=[
            pltpu.VMEM((D, DE), BF16), pltpu.VMEM((D, DE), BF16), pltpu.VMEM((DE, D), BF16),
            pltpu.VMEM((2, B, D), F32), pltpu.VMEM((2, B, D), F32),
            pltpu.SemaphoreType.DMA((2,)), pltpu.SemaphoreType.DMA(()),
        ],
    )
    return pl.pallas_call(
        _experts_kernel,
        grid_spec=grid_spec,
        out_shape=jax.ShapeDtypeStruct((rows, D), F32),
        compiler_params=_cparams(("arbitrary",)),
        name="experts",
    )(blk_e, blk_start, blk_first, total, tail, xs, w_gate, w_up, w_down)


def _combine_kernel(pos_ref, posn_ref, x_ref, ys_hbm, w_ref, g2_ref, fg_ref, o_ref, gbuf, sem,
                    *, final):
    i = pl.program_id(0)
    n = pl.num_programs(0)
    tm = x_ref.shape[0]
    slot = i % 2

    def gather(p_ref, to_slot):
        def body(r, carry):
            for k in range(TOP_K):
                pltpu.make_async_copy(ys_hbm.at[pl.ds(p_ref[k, r], 1), :],
                                      gbuf.at[to_slot, k, pl.ds(r, 1), :], sem.at[to_slot]).start()
            return carry

        lax.fori_loop(0, tm, body, 0, unroll=8)

    @pl.when(i == 0)
    def _():
        gather(pos_ref, 0)

    @pl.when(i + 1 < n)
    def _():
        gather(posn_ref, 1 - slot)

    for k in range(TOP_K):
        pltpu.make_async_copy(ys_hbm.at[pl.ds(0, tm), :], gbuf.at[slot, k], sem.at[slot]).wait()

    w = w_ref[...]
    y = gbuf[slot, 0] * w[:, 0:1] + gbuf[slot, 1] * w[:, 1:2]
    x = x_ref[...] + g2_ref[...] * y
    if final:
        ms = jnp.mean(x * x, axis=-1, keepdims=True)
        x = x * lax.rsqrt(ms + EPS) * fg_ref[...]
    o_ref[...] = x


def _combine(x, ys, pos, top_w, g2, final_g, final):
    T, D = x.shape
    tm = min(COMB_TM, T)
    n = T // tm
    vec = pl.BlockSpec((1, D), lambda i: (0, 0))
    return pl.pallas_call(
        functools.partial(_combine_kernel, final=final),
        grid=(n,),
        in_specs=[
            pl.BlockSpec((TOP_K, tm), lambda i: (0, i), memory_space=pltpu.SMEM),
            pl.BlockSpec((TOP_K, tm), lambda i: (0, jnp.minimum(i + 1, n - 1)), memory_space=pltpu.SMEM),
            pl.BlockSpec((tm, D), lambda i: (i, 0)),
            pl.BlockSpec(memory_space=pl.ANY),
            pl.BlockSpec((tm, TOP_K), lambda i: (i, 0)),
            vec, vec,
        ],
        out_specs=pl.BlockSpec((tm, D), lambda i: (i, 0)),
        out_shape=jax.ShapeDtypeStruct((T, D), F32),
        scratch_shapes=[pltpu.VMEM((2, TOP_K, tm, D), F32), pltpu.SemaphoreType.DMA((2,))],
        compiler_params=_cparams(("arbitrary",), disable_bounds_checks=True),
        name="combine",
    )(pos, pos, x, ys, top_w, g2, final_g)


def kernel(x, c, ada_w, ada_b, norm1_g, norm2_g, w_in, conv_w, ret_gn_g, fox_fb, w_out,
           router_w, router_b, moe_w_gate, moe_w_up, moe_w_down, final_g):
    Bsz, S, D = x.shape
    assert Bsz == 1, "one sequence per call"
    L = ada_w.shape[0]
    T = S
    conv_wd = conv_w.shape[-1]
    ret_qk = RET_HEADS * RET_DK
    ret_v = RET_HEADS * RET_DV
    fox_w = FOX_HEADS * FOX_DH
    tn = INPROJ_TN
    ret_col0 = 3 * conv_wd
    fox_col0 = ret_col0 + 2 * ret_qk + 2 * ret_v
    n_main = fox_col0 + 3 * fox_w
    assert fox_w == tn and fox_col0 % tn == 0
    q_block = fox_col0 // tn
    v_block = q_block + 2
    k_col0 = fox_col0

    mod = _ada_mod(c, ada_w, ada_b)
    cos_t, sin_t = _rope_tables(T)
    rw_t = router_w.T
    xt = x.reshape(T, D)

    for l in range(L):
        sh1, sc1, g1, sh2, sc2, g2 = [mod[l, :, k * D:(k + 1) * D] for k in range(6)]
        w_ff = jnp.pad(w_in[l, :, n_main:], ((0, 0), (0, LANES - FOX_HEADS)))
        fb = jnp.pad(fox_fb[l], (0, LANES - FOX_HEADS)).reshape(1, LANES)
        proj, proj_t, cum = _inproj(xt, norm1_g[l].reshape(1, D), sh1, sc1, w_in, l, w_ff, fb,
                                    q_block, v_block, LOG2E * FOX_DH ** -0.5)

        y_conv = _conv(proj, conv_w[l])
        y_ret = _retention(proj, cos_t, sin_t, ret_gn_g[l].reshape(1, ret_v), ret_col0)
        y_fox = _fox(proj, proj_t, cum, k_col0)

        wo = w_out[l].astype(BF16)
        xt, h2, logits_t = _outproj(
            y_conv, y_ret, y_fox, wo[:conv_wd], wo[conv_wd:conv_wd + ret_v], wo[conv_wd + ret_v:],
            xt, g1, norm2_g[l].reshape(1, D), sh2, sc2, rw_t)

        top_e, rank, top_w, cnt = _route(logits_t, router_b)
        counts = cnt[:, 0].astype(jnp.int32)
        seg_start, seg_end, blk_e, blk_start, blk_first, total, nb, rows = _block_tables(counts, T)
        hit = top_e[None] == jnp.arange(N_EXPERTS, dtype=jnp.int32)[:, None, None]
        pos = rank + jnp.sum(jnp.where(hit, seg_start[:, None, None], 0), axis=0)
        xs = _dispatch(h2, pos, seg_end, rows)
        ys = _experts(xs, l, moe_w_gate, moe_w_up, moe_w_down, blk_e, blk_start, blk_first, total,
                      seg_end[-1:], nb)
        xt = _combine(xt, ys, pos, top_w.T, g2, final_g.reshape(1, D), final=(l == L - 1))

    return xt.reshape(Bsz, S, D)
```

```python
import functools
import math

import numpy as np
import jax
import jax.numpy as jnp
from jax import lax
from jax.experimental import pallas as pl
from jax.experimental.pallas import tpu as pltpu

F32 = jnp.float32
BF16 = jnp.bfloat16

CHUNK = 64
CONV_K = 3
RET_HEADS = 6
RET_DK = 64
RET_DV = 128
FOX_HEADS = 6
FOX_DH = 128
N_EXPERTS = 64
N_GROUPS = 8
GROUP_SIZE = N_EXPERTS // N_GROUPS
TOP_K = 2
ROPE_BASE = 10000.0
EPS = 1e-6
LOG2E = 1.4426950408889634

LANES = 128
SUBLANES = 8

MOD_TN = 1024
INPROJ_TM = 1024
INPROJ_TN = 768
CONV_TM = 1024
CONV_HALO = 16
RET_ROWS = 256
FOX_TQ = 1024
FOX_SLAB = 16
FOX_SKIP_LOG2 = 160.0
FOX_NORM_SLACK = 1.02
OUT_TM = 512
MOE_ROWS = 256
ROUTE_TT = 1024
DISPATCH_TT = 512
COMB_TM = 512
VMEM_LIMIT = 56 * 1024 * 1024


def _cparams(sem, **kw):
    return pltpu.CompilerParams(dimension_semantics=sem, vmem_limit_bytes=VMEM_LIMIT, **kw)


def _mod_kernel(c_ref, w_ref, b_ref, o_ref):
    c = c_ref[...]
    ca = c * jax.nn.sigmoid(c)
    o_ref[0] = jnp.sum(w_ref[0] * ca, axis=0, keepdims=True) + b_ref[0]


def _ada_mod(c, ada_w, ada_b):
    L, D, N = ada_w.shape
    tn = MOD_TN
    return pl.pallas_call(
        _mod_kernel,
        grid=(L, N // tn),
        in_specs=[
            pl.BlockSpec((D, 1), lambda l, j: (0, 0)),
            pl.BlockSpec((1, D, tn), lambda l, j: (l, 0, j)),
            pl.BlockSpec((1, 1, tn), lambda l, j: (l, 0, j)),
        ],
        out_specs=pl.BlockSpec((1, 1, tn), lambda l, j: (l, 0, j)),
        out_shape=jax.ShapeDtypeStruct((L, 1, N), F32),
        compiler_params=_cparams(("arbitrary", "arbitrary")),
        name="ada_mod",
    )(c.reshape(D, 1), ada_w, ada_b.reshape(L, 1, N))


def _modulated_norm(x, g, sh, sc):
    ms = jnp.mean(x * x, axis=-1, keepdims=True)
    y = x * lax.rsqrt(ms + EPS) * g
    return y * (1.0 + sc) + sh


def _log_sigmoid(x):
    return jnp.minimum(x, 0.0) - jnp.log1p(jnp.exp(-jnp.abs(x)))


def _inproj_kernel(x_ref, g_ref, sh_ref, sc_ref, w_ref, wff_ref, fb_ref,
                   proj_ref, projt_ref, cum_ref, h_scr, wt_scr, carry_scr,
                   *, q_block, v_block, q_scale):
    i = pl.program_id(0)
    j = pl.program_id(1)
    tm = x_ref.shape[0]

    @pl.when(jnp.logical_and(i == 0, j == 0))
    def _():
        carry_scr[...] = jnp.zeros_like(carry_scr)

    @pl.when(j == 0)
    def _():
        h = _modulated_norm(x_ref[...], g_ref[...], sh_ref[...], sc_ref[...])
        hb = h.astype(BF16)
        h_scr[...] = hb
        ff = jnp.dot(hb, wff_ref[...].astype(BF16), preferred_element_type=F32) + fb_ref[...]
        c = _log_sigmoid(ff)
        row = lax.broadcasted_iota(jnp.int32, c.shape, 0)
        d = 1
        while d < tm:
            c = c + jnp.where(row >= d, pltpu.roll(c, d, axis=0), 0.0)
            d *= 2
        c = c + carry_scr[...]
        cum_ref[...] = c
        carry_scr[...] = c[tm - 1:tm, :]

    @pl.when(jnp.logical_and(j != q_block, j != v_block))
    def _():
        proj_ref[...] = jnp.dot(h_scr[...], w_ref[...].astype(BF16),
                                preferred_element_type=F32).astype(BF16)

    def feature_major(slot, scale):
        @pl.when(i == 0)
        def _():
            wt_scr[slot] = w_ref[...].T.astype(BF16)

        r = lax.dot_general(wt_scr[slot], h_scr[...], (((1,), (1,)), ((), ())),
                            preferred_element_type=F32)
        projt_ref[...] = (r * scale).astype(BF16)

    @pl.when(j == q_block)
    def _():
        feature_major(0, q_scale)

    @pl.when(j == v_block)
    def _():
        feature_major(1, 1.0)


def _inproj(x, g, sh, sc, w_in, layer, w_ff, fb, q_block, v_block, q_scale):
    T, D = x.shape
    tm, tn = min(INPROJ_TM, T), INPROJ_TN
    nj = v_block + 1
    is_q = lambda j: (j == q_block).astype(jnp.int32)
    is_v = lambda j: (j == v_block).astype(jnp.int32)
    w_col = lambda i, j: jnp.where(i == 0, j, j - is_q(j) - is_v(j))
    out_col = lambda j: j - (j >= q_block).astype(jnp.int32) - (j >= v_block).astype(jnp.int32)
    vec = pl.BlockSpec((1, D), lambda i, j: (0, 0))
    return pl.pallas_call(
        functools.partial(_inproj_kernel, q_block=q_block, v_block=v_block, q_scale=q_scale),
        grid=(T // tm, nj),
        in_specs=[
            pl.BlockSpec((tm, D), lambda i, j: (i, 0), pipeline_mode=pl.Buffered(1)),
            vec, vec, vec,
            pl.BlockSpec((None, D, tn), lambda i, j: (layer, 0, w_col(i, j))),
            pl.BlockSpec((D, LANES), lambda i, j: (0, 0)),
            pl.BlockSpec((1, LANES), lambda i, j: (0, 0)),
        ],
        out_specs=[
            pl.BlockSpec((tm, tn), lambda i, j: (i, out_col(j))),
            pl.BlockSpec((tn, tm), lambda i, j: (is_v(j), i)),
            pl.BlockSpec((tm, LANES), lambda i, j: (i, 0)),
        ],
        out_shape=[
            jax.ShapeDtypeStruct((T, (nj - 2) * tn), BF16),
            jax.ShapeDtypeStruct((2 * tn, T), BF16),
            jax.ShapeDtypeStruct((T, LANES), F32),
        ],
        scratch_shapes=[pltpu.VMEM((tm, D), BF16), pltpu.VMEM((2, tn, D), BF16),
                        pltpu.VMEM((1, LANES), F32)],
        compiler_params=_cparams(("arbitrary", "arbitrary")),
        name="inproj",
    )(x, g, sh, sc, w_in, w_ff, fb)


def _conv_kernel(cb_ref, cc_ref, cu_ref, hc_ref, hu_ref, w_ref, o_ref):
    i = pl.program_id(0)
    z = cc_ref[...].astype(F32) * cu_ref[...].astype(F32)
    zh = hc_ref[...].astype(F32) * hu_ref[...].astype(F32)
    zh = jnp.where(i > 0, zh, 0.0)
    hl = zh.shape[0]
    zm1 = zh[hl - 1:hl, :]
    zm2 = zh[hl - 2:hl - 1, :]
    row = lax.broadcasted_iota(jnp.int32, z.shape, 0)
    z1 = jnp.where(row == 0, zm1, pltpu.roll(z, 1, axis=0))
    z2 = jnp.where(row == 0, zm2, jnp.where(row == 1, zm1, pltpu.roll(z, 2, axis=0)))
    w = w_ref[...]
    y = z2 * w[0:1, :] + z1 * w[1:2, :] + z * w[2:3, :]
    o_ref[...] = (cb_ref[...].astype(F32) * y).astype(BF16)


def _conv(proj, conv_w):
    T = proj.shape[0]
    W = conv_w.shape[1]
    tm = min(CONV_TM, T)
    hb = tm // CONV_HALO
    halo = lambda c: pl.BlockSpec((CONV_HALO, W), lambda i: (jnp.maximum(i * hb - 1, 0), c))
    return pl.pallas_call(
        _conv_kernel,
        grid=(T // tm,),
        in_specs=[
            pl.BlockSpec((tm, W), lambda i: (i, 0)),
            pl.BlockSpec((tm, W), lambda i: (i, 1)),
            pl.BlockSpec((tm, W), lambda i: (i, 2)),
            halo(1), halo(2),
            pl.BlockSpec((CONV_K, W), lambda i: (0, 0)),
        ],
        out_specs=pl.BlockSpec((tm, W), lambda i: (i, 0)),
        out_shape=jax.ShapeDtypeStruct((T, W), BF16),
        compiler_params=_cparams(("arbitrary",)),
        name="conv",
    )(proj, proj, proj, proj, proj, conv_w)


def _ret_gammas():
    return [1.0 - 2.0 ** (-5.0 - h) for h in range(RET_HEADS)]


def _ret_tables(R):
    n = np.arange(R, dtype=np.float64)
    chunk = np.arange(R) // CHUNK
    allowed = chunk[None, :] <= chunk[:, None]
    dm, qd, kd = [], [], []
    for g in _ret_gammas():
        lg = math.log(g)
        dm.append(np.where(allowed, np.exp(lg * np.abs(n[:, None] - n[None, :])), 0.0))
        qd.append(np.broadcast_to(np.exp(lg * (n + 1.0))[:, None], (R, LANES)))
        kd.append(np.broadcast_to(np.exp(lg * (R - 1.0 - n))[:, None], (R, LANES)))
    f = lambda a: jnp.asarray(np.stack(a), dtype=F32)
    return f(dm), f(qd), f(kd)


def _rope_tables(T):
    half = RET_DK // 2
    inv = ROPE_BASE ** (-jnp.arange(half, dtype=F32) / half)
    ang = jnp.arange(T, dtype=F32)[:, None] * inv[None, :]
    cos, sin = jnp.cos(ang), jnp.sin(ang)
    reps = LANES // RET_DK
    cos_t = jnp.tile(jnp.concatenate([cos, cos], axis=1), (1, reps))
    sin_t = jnp.tile(jnp.concatenate([-sin, sin], axis=1), (1, reps))
    return cos_t, sin_t


def _ret_kernel(q_ref, k_ref, v_ref, g_ref, cos_ref, sin_ref, dm_ref, qd_ref, kd_ref, gn_ref,
                o_ref, s_scr):
    i = pl.program_id(0)
    R = q_ref.shape[0]

    @pl.when(i == 0)
    def _():
        s_scr[...] = jnp.zeros_like(s_scr)

    lane = lax.broadcasted_iota(jnp.int32, (R, LANES), 1)
    first_half = (lane % RET_DK) < (RET_DK // 2)
    low_head = lane < RET_DK
    cosv = cos_ref[...]
    sinv = sin_ref[...]
    c_dec = [g ** R for g in _ret_gammas()]

    def rot(t):
        swapped = jnp.where(first_half, pltpu.roll(t, LANES - RET_DK // 2, axis=1),
                            pltpu.roll(t, RET_DK // 2, axis=1))
        return t * cosv + swapped * sinv

    heads_per_vreg = LANES // RET_DK
    for p in range(RET_HEADS // heads_per_vreg):
        cols = slice(p * LANES, (p + 1) * LANES)
        qr = rot(q_ref[:, cols].astype(F32))
        kb = (rot(k_ref[:, cols].astype(F32)) * (RET_DK ** -0.5)).astype(BF16)
        for hh in range(heads_per_vreg):
            h = p * heads_per_vreg + hh
            hc = slice(h * RET_DV, (h + 1) * RET_DV)
            mask = low_head if hh == 0 else jnp.logical_not(low_head)
            qm = jnp.where(mask, qr, 0.0).astype(BF16)
            s = lax.dot_general(qm, kb, (((1,), (1,)), ((), ())), preferred_element_type=F32)
            s = s * dm_ref[h]
            v = v_ref[:, hc]
            o = jnp.dot(s.astype(BF16), v, preferred_element_type=F32)
            state = s_scr[h]
            o = o + jnp.dot(qm, state.astype(BF16), preferred_element_type=F32) * qd_ref[h]
            vd = (v.astype(F32) * kd_ref[h]).astype(BF16)
            kv = lax.dot_general(kb, vd, (((0,), (0,)), ((), ())), preferred_element_type=F32)
            s_scr[h] = state * c_dec[h] + kv
            mu = jnp.mean(o, axis=-1, keepdims=True)
            d = o - mu
            var = jnp.mean(d * d, axis=-1, keepdims=True)
            on = d * lax.rsqrt(var + EPS) * gn_ref[:, hc]
            gate = g_ref[:, hc].astype(F32)
            o_ref[:, hc] = (gate * jax.nn.sigmoid(gate) * on).astype(BF16)


def _retention(proj, cos_t, sin_t, gn_g, col0):
    T = proj.shape[0]
    R = min(RET_ROWS, T)
    QK = RET_HEADS * RET_DK
    V = RET_HEADS * RET_DV
    dm, qd, kd = _ret_tables(R)
    q_blk = col0 // QK
    v_blk = (col0 + 2 * QK) // V
    full3 = lambda a: pl.BlockSpec(a.shape, lambda i: (0, 0, 0))
    return pl.pallas_call(
        _ret_kernel,
        grid=(T // R,),
        in_specs=[
            pl.BlockSpec((R, QK), lambda i: (i, q_blk)),
            pl.BlockSpec((R, QK), lambda i: (i, q_blk + 1)),
            pl.BlockSpec((R, V), lambda i: (i, v_blk)),
            pl.BlockSpec((R, V), lambda i: (i, v_blk + 1)),
            pl.BlockSpec((R, LANES), lambda i: (i, 0)),
            pl.BlockSpec((R, LANES), lambda i: (i, 0)),
            full3(dm), full3(qd), full3(kd),
            pl.BlockSpec((1, V), lambda i: (0, 0)),
        ],
        out_specs=pl.BlockSpec((R, V), lambda i: (i, 0)),
        out_shape=jax.ShapeDtypeStruct((T, V), BF16),
        scratch_shapes=[pltpu.VMEM((RET_HEADS, LANES, RET_DV), F32)],
        compiler_params=_cparams(("arbitrary",)),
        name="retention",
    )(proj, proj, proj, proj, cos_t, sin_t, dm, qd, kd, gn_g)


def _fox_kernel(qt_ref, k_ref, vt_ref, cum_ref, o_ref, ck_scr, cend_scr, kn_scr,
                s0_scr, s1_scr, p0_scr, p1_scr, a0_scr, a1_scr, m_scr, l_scr, acc_scr):
    h = pl.program_id(0)
    qi = pl.program_id(1)
    tq = qt_ref.shape[1]
    tk = s0_scr.shape[0]
    T = k_ref.shape[0]
    assert tq == 2 * tk
    reps = tq // LANES
    s_scr, p_scr, a_scr = (s0_scr, s1_scr), (p0_scr, p1_scr), (a0_scr, a1_scr)

    @pl.when(qi == 0)
    def _():
        cend_scr[...] = jnp.zeros_like(cend_scr)
        ones = jnp.ones((FOX_DH, LANES), BF16)
        kn2 = jnp.zeros((SUBLANES, LANES), F32)
        for r in range(0, T, tq):
            kf = k_ref[r:r + tq, :].astype(F32)
            n2 = jnp.dot((kf * kf).astype(BF16), ones, preferred_element_type=F32)
            kn2 = jnp.maximum(kn2, jnp.max(n2.reshape(tq // SUBLANES, SUBLANES, LANES), axis=0))
        kn_scr[...] = jnp.max(kn2, axis=0, keepdims=True)

    lane = lax.broadcasted_iota(jnp.int32, cum_ref.shape, 1)
    col = jnp.sum(jnp.where(lane == h, cum_ref[...], 0.0), axis=1, keepdims=True) * LOG2E
    q0 = pl.multiple_of(qi * tq, tq)
    ck_rep = jnp.broadcast_to(col, cum_ref.shape)
    ck_scr[pl.ds(q0, tq), :] = ck_rep
    cend_scr[pl.ds(2 * qi, 1), :] = ck_rep[tk - 1:tk, :]
    cend_scr[pl.ds(2 * qi + 1, 1), :] = ck_rep[tq - 1:tq, :]

    qf = qt_ref[...].astype(F32)
    qn2 = jnp.max(jnp.sum(qf * qf, axis=0, keepdims=True), axis=1, keepdims=True)
    reach = 2.0 * FOX_NORM_SLACK * jnp.sqrt(qn2 * kn_scr[...]) + FOX_SKIP_LOG2
    decay = cend_scr[...] - ck_rep[0:1, :]
    chunk_id = lax.broadcasted_iota(jnp.int32, decay.shape, 0)
    dead = jnp.logical_and(decay >= reach, chunk_id < 2 * qi)
    n_dead = jnp.max(jnp.sum(dead.astype(jnp.int32), axis=0, keepdims=True))
    first_pair = n_dead // 2

    m_scr[...] = jnp.full_like(m_scr, -jnp.inf)
    l_scr[...] = jnp.zeros_like(l_scr)
    acc_scr[...] = jnp.zeros_like(acc_scr)
    p1_scr[...] = jnp.zeros_like(p1_scr)
    a1_scr[...] = jnp.ones_like(a1_scr)

    def score(c, slot):
        k0 = pl.multiple_of(c * tk, tk)
        s = jnp.dot(k_ref[pl.ds(k0, tk), :], qt_ref[...], preferred_element_type=F32)
        bias = ck_scr[pl.ds(k0, tk), :]
        s_scr[slot][...] = s - jnp.concatenate([bias] * reps, axis=1)

    def softmax(c, slot, masked):
        if masked:
            s = s_scr[slot][...]
            kpos = c * tk + lax.broadcasted_iota(jnp.int32, s.shape, 0)
            qpos = q0 + lax.broadcasted_iota(jnp.int32, s.shape, 1)
            s_scr[slot][...] = jnp.where(kpos <= qpos, s, -jnp.inf)
        sref, pref = s_scr[slot], p_scr[slot]
        mx = sref[0:FOX_SLAB, :]
        for r in range(FOX_SLAB, tk, FOX_SLAB):
            mx = jnp.maximum(mx, sref[r:r + FOX_SLAB, :])
        m_old = m_scr[...]
        m_new = jnp.maximum(m_old, jnp.max(mx, axis=0, keepdims=True))
        alpha = jnp.exp2(m_old - m_new)
        m_rows = jnp.broadcast_to(m_new, (FOX_SLAB, tq))
        psum = jnp.zeros((FOX_SLAB, tq), F32)
        for r in range(0, tk, FOX_SLAB):
            p = jnp.exp2(sref[r:r + FOX_SLAB, :] - m_rows)
            psum = psum + p
            pref[r:r + FOX_SLAB, :] = p.astype(BF16)
        l_scr[...] = alpha * l_scr[...] + jnp.sum(psum, axis=0, keepdims=True)
        a_scr[slot][...] = alpha
        m_scr[...] = m_new

    def accumulate(c, slot):
        k0 = pl.multiple_of(jnp.maximum(c, 0) * tk, tk)
        pv = jnp.dot(vt_ref[:, pl.ds(k0, tk)], p_scr[slot][...], preferred_element_type=F32)
        acc_scr[...] = a_scr[slot][...] * acc_scr[...] + pv

    score(2 * first_pair, 0)

    def pair(i, carry):
        c = 2 * i
        score(c + 1, 1)
        accumulate(c - 1, 1)
        softmax(c, 0, False)
        score(c + 2, 0)
        accumulate(c, 0)
        softmax(c + 1, 1, False)
        return carry

    lax.fori_loop(first_pair, qi, pair, 0)
    c = 2 * qi
    score(c + 1, 1)
    accumulate(c - 1, 1)
    softmax(c, 0, True)
    accumulate(c, 0)
    softmax(c + 1, 1, True)
    accumulate(c + 1, 1)
    o_ref[...] = (acc_scr[...] / l_scr[...]).T.astype(BF16)


def _fox(proj, proj_t, cum, k_col0):
    T = proj.shape[0]
    tq = min(FOX_TQ, T)
    tk = tq // 2
    W = FOX_HEADS * FOX_DH
    kb = k_col0 // FOX_DH
    return pl.pallas_call(
        _fox_kernel,
        grid=(FOX_HEADS, T // tq),
        in_specs=[
            pl.BlockSpec((FOX_DH, tq), lambda h, i: (h, i)),
            pl.BlockSpec((T, FOX_DH), lambda h, i: (0, kb + h)),
            pl.BlockSpec((FOX_DH, T), lambda h, i: (FOX_HEADS + h, 0)),
            pl.BlockSpec((tq, LANES), lambda h, i: (i, 0)),
        ],
        out_specs=pl.BlockSpec((tq, FOX_DH), lambda h, i: (i, h)),
        out_shape=jax.ShapeDtypeStruct((T, W), BF16),
        scratch_shapes=[pltpu.VMEM((T, LANES), F32),
                        pltpu.VMEM((-(-(T // tk) // SUBLANES) * SUBLANES, LANES), F32),
                        pltpu.VMEM((1, LANES), F32),
                        pltpu.VMEM((tk, tq), F32), pltpu.VMEM((tk, tq), F32),
                        pltpu.VMEM((tk, tq), BF16), pltpu.VMEM((tk, tq), BF16),
                        pltpu.VMEM((1, tq), F32), pltpu.VMEM((1, tq), F32),
                        pltpu.VMEM((1, tq), F32), pltpu.VMEM((1, tq), F32),
                        pltpu.VMEM((FOX_DH, tq), F32)],
        compiler_params=_cparams(("arbitrary", "arbitrary")),
        name="fox",
    )(proj_t, proj, proj_t, cum)


def _outproj_kernel(yc_ref, yr_ref, yf_ref, wc_ref, wr_ref, wf_ref, x_ref, g1_ref,
                    g_ref, sh_ref, sc_ref, rw_ref, xo_ref, h_ref, lg_ref):
    mix = jnp.dot(yc_ref[...], wc_ref[...], preferred_element_type=F32)
    mix = mix + jnp.dot(yr_ref[...], wr_ref[...], preferred_element_type=F32)
    mix = mix + jnp.dot(yf_ref[...], wf_ref[...], preferred_element_type=F32)
    x = x_ref[...] + g1_ref[...] * mix
    xo_ref[...] = x
    h = _modulated_norm(x, g_ref[...], sh_ref[...], sc_ref[...])
    h_ref[...] = h
    lg_ref[...] = lax.dot_general(rw_ref[...], h, (((1,), (1,)), ((), ())),
                                  precision=lax.Precision.HIGHEST, preferred_element_type=F32)


def _outproj(yc, yr, yf, wc, wr, wf, x, g1, g, sh, sc, rw_t):
    T, D = x.shape
    tm = min(OUT_TM, T)
    E = rw_t.shape[0]
    vec = pl.BlockSpec((1, D), lambda i: (0, 0))
    rows = lambda a: pl.BlockSpec((tm, a.shape[1]), lambda i: (i, 0))
    whole = lambda a: pl.BlockSpec(a.shape, lambda i: (0, 0))
    return pl.pallas_call(
        _outproj_kernel,
        grid=(T // tm,),
        in_specs=[rows(yc), rows(yr), rows(yf), whole(wc), whole(wr), whole(wf), rows(x),
                  vec, vec, vec, vec, whole(rw_t)],
        out_specs=[rows(x), rows(x), pl.BlockSpec((E, tm), lambda i: (0, i))],
        out_shape=[jax.ShapeDtypeStruct((T, D), F32), jax.ShapeDtypeStruct((T, D), F32),
                   jax.ShapeDtypeStruct((E, T), F32)],
        compiler_params=_cparams(("arbitrary",)),
        name="outproj",
    )(yc, yr, yf, wc, wr, wf, x, g1, g, sh, sc, rw_t)


def _route_kernel(lg_ref, b_ref, tri_ref, e_ref, r_ref, w_ref, cnt_ref, carry_scr):
    i = pl.program_id(0)
    E, tt = lg_ref.shape

    @pl.when(i == 0)
    def _():
        carry_scr[...] = jnp.zeros_like(carry_scr)

    aff = jax.nn.sigmoid(lg_ref[...])
    sel = aff + b_ref[...]
    row8 = lax.broadcasted_iota(jnp.int32, (GROUP_SIZE, tt), 0)
    best = None
    for g in range(N_GROUPS):
        slab = sel[g * GROUP_SIZE:(g + 1) * GROUP_SIZE, :]
        m1 = jnp.max(slab, axis=0, keepdims=True)
        i1 = jnp.min(jnp.where(slab == m1, row8, GROUP_SIZE), axis=0, keepdims=True)
        rest = jnp.where(row8 == i1, -jnp.inf, slab)
        m2 = jnp.max(rest, axis=0, keepdims=True)
        i2 = jnp.min(jnp.where(rest == m2, row8, GROUP_SIZE), axis=0, keepdims=True)
        cand = (m1 + m2, g * GROUP_SIZE + i1, g * GROUP_SIZE + i2)
        if best is None:
            best = cand
        else:
            upd = cand[0] > best[0]
            best = tuple(jnp.where(upd, n, o) for n, o in zip(cand, best))
    _, e0, e1 = best

    row = lax.broadcasted_iota(jnp.int32, (E, tt), 0)
    oh0 = row == e0
    oh1 = row == e1
    a0 = jnp.sum(jnp.where(oh0, aff, 0.0), axis=0, keepdims=True)
    a1 = jnp.sum(jnp.where(oh1, aff, 0.0), axis=0, keepdims=True)
    w_ref[0:1, :] = a0 / (a0 + a1)
    w_ref[1:2, :] = a1 / (a0 + a1)
    e_ref[0:1, :] = e0
    e_ref[1:2, :] = e1

    oh = jnp.logical_or(oh0, oh1)
    ohf = jnp.where(oh, 1.0, 0.0)
    before = jnp.dot(ohf.astype(BF16), tri_ref[...], preferred_element_type=F32) + carry_scr[:, 0:1]
    r_ref[0:1, :] = jnp.sum(jnp.where(oh0, before, 0.0), axis=0, keepdims=True).astype(jnp.int32)
    r_ref[1:2, :] = jnp.sum(jnp.where(oh1, before, 0.0), axis=0, keepdims=True).astype(jnp.int32)
    carry = carry_scr[...] + jnp.sum(ohf, axis=1, keepdims=True)
    carry_scr[...] = carry
    cnt_ref[...] = carry


def _route(logits_t, router_b):
    E, T = logits_t.shape
    tt = min(ROUTE_TT, T)
    tri = jnp.asarray(np.triu(np.ones((tt, tt), np.float32), k=1), dtype=BF16)
    pair = lambda dt: jax.ShapeDtypeStruct((TOP_K, T), dt)
    return pl.pallas_call(
        _route_kernel,
        grid=(T // tt,),
        in_specs=[
            pl.BlockSpec((E, tt), lambda i: (0, i)),
            pl.BlockSpec((E, 1), lambda i: (0, 0)),
            pl.BlockSpec((tt, tt), lambda i: (0, 0)),
        ],
        out_specs=[
            pl.BlockSpec((TOP_K, tt), lambda i: (0, i)),
            pl.BlockSpec((TOP_K, tt), lambda i: (0, i)),
            pl.BlockSpec((TOP_K, tt), lambda i: (0, i)),
            pl.BlockSpec((E, LANES), lambda i: (0, 0)),
        ],
        out_shape=[pair(jnp.int32), pair(jnp.int32), pair(F32), jax.ShapeDtypeStruct((E, LANES), F32)],
        scratch_shapes=[pltpu.VMEM((E, LANES), F32)],
        compiler_params=_cparams(("arbitrary",)),
        name="route",
    )(logits_t, router_b.reshape(E, 1), tri)


def _block_tables(counts, T):
    E = counts.shape[0]
    A = T * TOP_K
    B = MOE_ROWS
    seg = (counts + SUBLANES - 1) // SUBLANES * SUBLANES
    seg_end = jnp.cumsum(seg)
    seg_start = seg_end - seg
    nb = (A + E * (B - 1) + B - 1) // B
    nblk = (counts + B - 1) // B
    blk_end = jnp.cumsum(nblk)
    total = blk_end[-1]
    b = jnp.arange(nb, dtype=jnp.int32)
    bc = jnp.minimum(b, total - 1)
    blk_e = jnp.minimum(jnp.searchsorted(blk_end, bc, side="right"), E - 1).astype(jnp.int32)
    local = bc - (blk_end[blk_e] - nblk[blk_e])
    blk_start = (seg_start[blk_e] + local * B).astype(jnp.int32)
    blk_first = jnp.logical_and(b < total, local == 0).astype(jnp.int32)
    rows = A + E * (SUBLANES - 1) + B
    rows = (rows + SUBLANES - 1) // SUBLANES * SUBLANES
    return (seg_start.astype(jnp.int32), seg_end.astype(jnp.int32), blk_e, blk_start, blk_first,
            total.astype(jnp.int32).reshape(1), nb, rows)


def _zero_rows_from(zero_ref, hbm_ref, start, sem):
    piece = zero_ref.shape[0]
    rows = hbm_ref.shape[0]

    def body(j, carry):
        at = pl.multiple_of(jnp.minimum(start + j * piece, rows - piece), SUBLANES)
        cp = pltpu.make_async_copy(zero_ref, hbm_ref.at[pl.ds(at, piece), :], sem)
        cp.start()
        cp.wait()
        return carry

    lax.fori_loop(0, (rows - start + piece - 1) // piece, body, 0)


def _dispatch_kernel(seg_end_ref, pos_ref, h_ref, xs_hbm, zero_scr, sem, zsem):
    i = pl.program_id(0)
    td = pos_ref.shape[1]
    E = seg_end_ref.shape[0]

    @pl.when(i == 0)
    def _():
        zero_scr[...] = jnp.zeros_like(zero_scr)

        def tail(e, carry):
            end = seg_end_ref[e]
            at = pl.multiple_of(jnp.maximum(end - SUBLANES, 0), SUBLANES)
            cp = pltpu.make_async_copy(zero_scr.at[pl.ds(0, SUBLANES), :],
                                       xs_hbm.at[pl.ds(at, SUBLANES), :], zsem)
            cp.start()
            cp.wait()
            return carry

        lax.fori_loop(0, E, tail, 0)
        _zero_rows_from(zero_scr, xs_hbm, seg_end_ref[E - 1], zsem)

    def body(r, carry):
        for k in range(TOP_K):
            pltpu.make_async_copy(h_ref.at[pl.ds(r, 1), :],
                                  xs_hbm.at[pl.ds(pos_ref[k, r], 1), :], sem).start()
        return carry

    lax.fori_loop(0, td, body, 0, unroll=8)
    for k in range(TOP_K):
        pltpu.make_async_copy(h_ref, xs_hbm.at[pl.ds(0, td), :], sem).wait()


def _dispatch(h2, pos, seg_end, rows):
    T, D = h2.shape
    td = min(DISPATCH_TT, T)
    grid_spec = pltpu.PrefetchScalarGridSpec(
        num_scalar_prefetch=1,
        grid=(T // td,),
        in_specs=[pl.BlockSpec((TOP_K, td), lambda i, se: (0, i), memory_space=pltpu.SMEM),
                  pl.BlockSpec((td, D), lambda i, se: (i, 0))],
        out_specs=pl.BlockSpec(memory_space=pl.ANY),
        scratch_shapes=[pltpu.VMEM((MOE_ROWS, D), F32), pltpu.SemaphoreType.DMA(()),
                        pltpu.SemaphoreType.DMA(())],
    )
    return pl.pallas_call(
        _dispatch_kernel,
        grid_spec=grid_spec,
        out_shape=jax.ShapeDtypeStruct((rows, D), F32),
        compiler_params=_cparams(("arbitrary",), disable_bounds_checks=True),
        name="dispatch",
    )(seg_end, pos, h2)


def _experts_kernel(blk_e_ref, blk_start_ref, blk_first_ref, total_ref, tail_ref,
                    xs_hbm, wg_ref, wu_ref, wd_ref, ys_hbm,
                    wg_b, wu_b, wd_b, xbuf, ybuf, isem, osem):
    b = pl.program_id(0)
    total = total_ref[0]
    B = xbuf.shape[1]
    slot = b % 2

    def load(blk, to_slot):
        at = pl.multiple_of(blk_start_ref[blk], SUBLANES)
        return pltpu.make_async_copy(xs_hbm.at[pl.ds(at, B), :], xbuf.at[to_slot], isem.at[to_slot])

    def store(blk, from_slot):
        at = pl.multiple_of(blk_start_ref[blk], SUBLANES)
        return pltpu.make_async_copy(ybuf.at[from_slot], ys_hbm.at[pl.ds(at, B), :], osem)

    @pl.when(b == 0)
    def _():
        load(0, 0).start()
        ybuf[1] = jnp.zeros(ybuf.shape[1:], ybuf.dtype)
        _zero_rows_from(ybuf.at[1], ys_hbm, tail_ref[0], osem)

    @pl.when(b + 1 < total)
    def _():
        load(b + 1, 1 - slot).start()

    @pl.when(b < total)
    def _():
        load(b, slot).wait()

        @pl.when(blk_first_ref[b] == 1)
        def _():
            wg_b[...] = wg_ref[...].astype(BF16)
            wu_b[...] = wu_ref[...].astype(BF16)
            wd_b[...] = wd_ref[...].astype(BF16)

        x = xbuf[slot].astype(BF16)
        g = jnp.dot(x, wg_b[...], preferred_element_type=F32)
        u = jnp.dot(x, wu_b[...], preferred_element_type=F32)
        a = (g * jax.nn.sigmoid(g) * u).astype(BF16)
        ybuf[slot] = jnp.dot(a, wd_b[...], preferred_element_type=F32)

        @pl.when(b > 0)
        def _():
            store(b - 1, 1 - slot).wait()

        store(b, slot).start()

        @pl.when(b == total - 1)
        def _():
            store(b, slot).wait()


def _experts(xs, layer, w_gate, w_up, w_down, blk_e, blk_start, blk_first, total, tail, nb):
    rows, D = xs.shape
    DE = w_gate.shape[-1]
    B = MOE_ROWS
    wspec = lambda r, c: pl.BlockSpec((None, None, r, c), lambda b, be, *_: (layer, be[b], 0, 0))
    grid_spec = pltpu.PrefetchScalarGridSpec(
        num_scalar_prefetch=5,
        grid=(nb,),
        in_specs=[pl.BlockSpec(memory_space=pl.ANY), wspec(D, DE), wspec(D, DE), wspec(DE, D)],
        out_specs=pl.BlockSpec(memory_space=pl.ANY),
        scratch_shapes=[
            pltpu.VMEM((D, DE), BF16), pltpu.VMEM((D, DE), BF16), pltpu.VMEM((DE, D), BF16),
            pltpu.VMEM((2, B, D), F32), pltpu.VMEM((2, B, D), F32),
            pltpu.SemaphoreType.DMA((2,)), pltpu.SemaphoreType.DMA(()),
        ],
    )
    return pl.pallas_call(
        _experts_kernel,
        grid_spec=grid_spec,
        out_shape=jax.ShapeDtypeStruct((rows, D), F32),
        compiler_params=_cparams(("arbitrary",)),
        name="experts",
    )(blk_e, blk_start, blk_first, total, tail, xs, w_gate, w_up, w_down)


def _combine_kernel(pos_ref, posn_ref, x_ref, ys_hbm, w_ref, g2_ref, fg_ref, o_ref, gbuf, sem,
                    *, final):
    i = pl.program_id(0)
    n = pl.num_programs(0)
    tm = x_ref.shape[0]
    slot = i % 2

    def gather(p_ref, to_slot):
        def body(r, carry):
            for k in range(TOP_K):
                pltpu.make_async_copy(ys_hbm.at[pl.ds(p_ref[k, r], 1), :],
                                      gbuf.at[to_slot, k, pl.ds(r, 1), :], sem.at[to_slot]).start()
            return carry

        lax.fori_loop(0, tm, body, 0, unroll=8)

    @pl.when(i == 0)
    def _():
        gather(pos_ref, 0)

    @pl.when(i + 1 < n)
    def _():
        gather(posn_ref, 1 - slot)

    for k in range(TOP_K):
        pltpu.make_async_copy(ys_hbm.at[pl.ds(0, tm), :], gbuf.at[slot, k], sem.at[slot]).wait()

    w = w_ref[...]
    y = gbuf[slot, 0] * w[:, 0:1] + gbuf[slot, 1] * w[:, 1:2]
    x = x_ref[...] + g2_ref[...] * y
    if final:
        ms = jnp.mean(x * x, axis=-1, keepdims=True)
        x = x * lax.rsqrt(ms + EPS) * fg_ref[...]
    o_ref[...] = x


def _combine(x, ys, pos, top_w, g2, final_g, final):
    T, D = x.shape
    tm = min(COMB_TM, T)
    n = T // tm
    vec = pl.BlockSpec((1, D), lambda i: (0, 0))
    return pl.pallas_call(
        functools.partial(_combine_kernel, final=final),
        grid=(n,),
        in_specs=[
            pl.BlockSpec((TOP_K, tm), lambda i: (0, i), memory_space=pltpu.SMEM),
            pl.BlockSpec((TOP_K, tm), lambda i: (0, jnp.minimum(i + 1, n - 1)), memory_space=pltpu.SMEM),
            pl.BlockSpec((tm, D), lambda i: (i, 0)),
            pl.BlockSpec(memory_space=pl.ANY),
            pl.BlockSpec((tm, TOP_K), lambda i: (i, 0)),
            vec, vec,
        ],
        out_specs=pl.BlockSpec((tm, D), lambda i: (i, 0)),
        out_shape=jax.ShapeDtypeStruct((T, D), F32),
        scratch_shapes=[pltpu.VMEM((2, TOP_K, tm, D), F32), pltpu.SemaphoreType.DMA((2,))],
        compiler_params=_cparams(("arbitrary",), disable_bounds_checks=True),
        name="combine",
    )(pos, pos, x, ys, top_w, g2, final_g)


def kernel(x, c, ada_w, ada_b, norm1_g, norm2_g, w_in, conv_w, ret_gn_g, fox_fb, w_out,
           router_w, router_b, moe_w_gate, moe_w_up, moe_w_down, final_g):
    Bsz, S, D = x.shape
    assert Bsz == 1, "one sequence per call"
    L = ada_w.shape[0]
    T = S
    conv_wd = conv_w.shape[-1]
    ret_qk = RET_HEADS * RET_DK
    ret_v = RET_HEADS * RET_DV
    fox_w = FOX_HEADS * FOX_DH
    tn = INPROJ_TN
    ret_col0 = 3 * conv_wd
    fox_col0 = ret_col0 + 2 * ret_qk + 2 * ret_v
    n_main = fox_col0 + 3 * fox_w
    assert fox_w == tn and fox_col0 % tn == 0
    q_block = fox_col0 // tn
    v_block = q_block + 2
    k_col0 = fox_col0

    mod = _ada_mod(c, ada_w, ada_b)
    cos_t, sin_t = _rope_tables(T)
    rw_t = router_w.T
    xt = x.reshape(T, D)

    for l in range(L):
        sh1, sc1, g1, sh2, sc2, g2 = [mod[l, :, k * D:(k + 1) * D] for k in range(6)]
        w_ff = jnp.pad(w_in[l, :, n_main:], ((0, 0), (0, LANES - FOX_HEADS)))
        fb = jnp.pad(fox_fb[l], (0, LANES - FOX_HEADS)).reshape(1, LANES)
        proj, proj_t, cum = _inproj(xt, norm1_g[l].reshape(1, D), sh1, sc1, w_in, l, w_ff, fb,
                                    q_block, v_block, LOG2E * FOX_DH ** -0.5)

        y_conv = _conv(proj, conv_w[l])
        y_ret = _retention(proj, cos_t, sin_t, ret_gn_g[l].reshape(1, ret_v), ret_col0)
        y_fox = _fox(proj, proj_t, cum, k_col0)

        wo = w_out[l].astype(BF16)
        xt, h2, logits_t = _outproj(
            y_conv, y_ret, y_fox, wo[:conv_wd], wo[conv_wd:conv_wd + ret_v], wo[conv_wd + ret_v:],
            xt, g1, norm2_g[l].reshape(1, D), sh2, sc2, rw_t)

        top_e, rank, top_w, cnt = _route(logits_t, router_b)
        counts = cnt[:, 0].astype(jnp.int32)
        seg_start, seg_end, blk_e, blk_start, blk_first, total, nb, rows = _block_tables(counts, T)
        hit = top_e[None] == jnp.arange(N_EXPERTS, dtype=jnp.int32)[:, None, None]
        pos = rank + jnp.sum(jnp.where(hit, seg_start[:, None, None], 0), axis=0)
        xs = _dispatch(h2, pos, seg_end, rows)
        ys = _experts(xs, l, moe_w_gate, moe_w_up, moe_w_down, blk_e, blk_start, blk_first, total,
                      seg_end[-1:], nb)
        xt = _combine(xt, ys, pos, top_w.T, g2, final_g.reshape(1, D), final=(l == L - 1))

    return xt.reshape(Bsz, S, D)
```

```python
import functools
import math

import numpy as np
import jax
import jax.numpy as jnp
from jax import lax
from jax.experimental import pallas as pl
from jax.experimental.pallas import tpu as pltpu

F32 = jnp.float32
BF16 = jnp.bfloat16

CHUNK = 64
CONV_K = 3
RET_HEADS = 6
RET_DK = 64
RET_DV = 128
FOX_HEADS = 6
FOX_DH = 128
N_EXPERTS = 64
N_GROUPS = 8
GROUP_SIZE = N_EXPERTS // N_GROUPS
TOP_K = 2
ROPE_BASE = 10000.0
EPS = 1e-6
LOG2E = 1.4426950408889634

LANES = 128
SUBLANES = 8

MOD_TN = 1024
INPROJ_TM = 1024
INPROJ_TN = 768
CONV_TM = 1024
CONV_HALO = 16
RET_ROWS = 256
FOX_TQ = 1024
FOX_SLAB = 16
FOX_SKIP_LOG2 = 160.0
FOX_NORM_SLACK = 1.02
OUT_TM = 512
MOE_ROWS = 256
ROUTE_TT = 1024
DISPATCH_TT = 512
COMB_TM = 512
VMEM_LIMIT = 56 * 1024 * 1024


def _cparams(sem, **kw):
    return pltpu.CompilerParams(dimension_semantics=sem, vmem_limit_bytes=VMEM_LIMIT, **kw)


U32 = jnp.uint32


def _pack_halves(x):
    half = x.shape[1] // 2
    xb = x.astype(BF16).astype(F32)
    lo = lax.bitcast_convert_type(xb[:, :half], U32)
    hi = lax.bitcast_convert_type(xb[:, half:], U32)
    return lax.shift_right_logical(lo, U32(16)) | (hi & U32(0xFFFF0000))


def _unpack_halves(w):
    lo = lax.bitcast_convert_type(lax.shift_left(w, U32(16)), F32)
    hi = lax.bitcast_convert_type(w & U32(0xFFFF0000), F32)
    return lo, hi


def _mod_kernel(c_ref, w_ref, b_ref, o_ref):
    c = c_ref[...]
    ca = c * jax.nn.sigmoid(c)
    o_ref[0] = jnp.sum(w_ref[0] * ca, axis=0, keepdims=True) + b_ref[0]


def _ada_mod(c, ada_w, ada_b):
    L, D, N = ada_w.shape
    tn = MOD_TN
    return pl.pallas_call(
        _mod_kernel,
        grid=(L, N // tn),
        in_specs=[
            pl.BlockSpec((D, 1), lambda l, j: (0, 0)),
            pl.BlockSpec((1, D, tn), lambda l, j: (l, 0, j)),
            pl.BlockSpec((1, 1, tn), lambda l, j: (l, 0, j)),
        ],
        out_specs=pl.BlockSpec((1, 1, tn), lambda l, j: (l, 0, j)),
        out_shape=jax.ShapeDtypeStruct((L, 1, N), F32),
        compiler_params=_cparams(("arbitrary", "arbitrary")),
        name="ada_mod",
    )(c.reshape(D, 1), ada_w, ada_b.reshape(L, 1, N))


def _modulated_norm(x, g, sh, sc):
    ms = jnp.mean(x * x, axis=-1, keepdims=True)
    y = x * lax.rsqrt(ms + EPS) * g
    return y * (1.0 + sc) + sh


def _log_sigmoid(x):
    return jnp.minimum(x, 0.0) - jnp.log1p(jnp.exp(-jnp.abs(x)))


def _inproj_kernel(x_ref, g_ref, sh_ref, sc_ref, w_ref, wff_ref, fb_ref,
                   proj_ref, projt_ref, cum_ref, h_scr, wt_scr, carry_scr,
                   *, q_block, v_block, q_scale):
    i = pl.program_id(0)
    j = pl.program_id(1)
    tm = x_ref.shape[0]

    @pl.when(jnp.logical_and(i == 0, j == 0))
    def _():
        carry_scr[...] = jnp.zeros_like(carry_scr)

    @pl.when(j == 0)
    def _():
        h = _modulated_norm(x_ref[...], g_ref[...], sh_ref[...], sc_ref[...])
        hb = h.astype(BF16)
        h_scr[...] = hb
        ff = jnp.dot(hb, wff_ref[...].astype(BF16), preferred_element_type=F32) + fb_ref[...]
        c = _log_sigmoid(ff)
        row = lax.broadcasted_iota(jnp.int32, c.shape, 0)
        d = 1
        while d < tm:
            c = c + jnp.where(row >= d, pltpu.roll(c, d, axis=0), 0.0)
            d *= 2
        c = c + carry_scr[...]
        cum_ref[...] = c
        carry_scr[...] = c[tm - 1:tm, :]

    @pl.when(jnp.logical_and(j != q_block, j != v_block))
    def _():
        proj_ref[...] = jnp.dot(h_scr[...], w_ref[...].astype(BF16),
                                preferred_element_type=F32).astype(BF16)

    def feature_major(slot, scale):
        @pl.when(i == 0)
        def _():
            wt_scr[slot] = w_ref[...].T.astype(BF16)

        r = lax.dot_general(wt_scr[slot], h_scr[...], (((1,), (1,)), ((), ())),
                            preferred_element_type=F32)
        projt_ref[...] = (r * scale).astype(BF16)

    @pl.when(j == q_block)
    def _():
        feature_major(0, q_scale)

    @pl.when(j == v_block)
    def _():
        feature_major(1, 1.0)


def _inproj(x, g, sh, sc, w_in, layer, w_ff, fb, q_block, v_block, q_scale):
    T, D = x.shape
    tm, tn = min(INPROJ_TM, T), INPROJ_TN
    nj = v_block + 1
    is_q = lambda j: (j == q_block).astype(jnp.int32)
    is_v = lambda j: (j == v_block).astype(jnp.int32)
    w_col = lambda i, j: jnp.where(i == 0, j, j - is_q(j) - is_v(j))
    out_col = lambda j: j - (j >= q_block).astype(jnp.int32) - (j >= v_block).astype(jnp.int32)
    vec = pl.BlockSpec((1, D), lambda i, j: (0, 0))
    return pl.pallas_call(
        functools.partial(_inproj_kernel, q_block=q_block, v_block=v_block, q_scale=q_scale),
        grid=(T // tm, nj),
        in_specs=[
            pl.BlockSpec((tm, D), lambda i, j: (i, 0), pipeline_mode=pl.Buffered(1)),
            vec, vec, vec,
            pl.BlockSpec((None, D, tn), lambda i, j: (layer, 0, w_col(i, j))),
            pl.BlockSpec((D, LANES), lambda i, j: (0, 0)),
            pl.BlockSpec((1, LANES), lambda i, j: (0, 0)),
        ],
        out_specs=[
            pl.BlockSpec((tm, tn), lambda i, j: (i, out_col(j))),
            pl.BlockSpec((tn, tm), lambda i, j: (is_v(j), i)),
            pl.BlockSpec((tm, LANES), lambda i, j: (i, 0)),
        ],
        out_shape=[
            jax.ShapeDtypeStruct((T, (nj - 2) * tn), BF16),
            jax.ShapeDtypeStruct((2 * tn, T), BF16),
            jax.ShapeDtypeStruct((T, LANES), F32),
        ],
        scratch_shapes=[pltpu.VMEM((tm, D), BF16), pltpu.VMEM((2, tn, D), BF16),
                        pltpu.VMEM((1, LANES), F32)],
        compiler_params=_cparams(("arbitrary", "arbitrary")),
        name="inproj",
    )(x, g, sh, sc, w_in, w_ff, fb)


def _conv_kernel(cb_ref, cc_ref, cu_ref, hc_ref, hu_ref, w_ref, o_ref):
    i = pl.program_id(0)
    z = cc_ref[...].astype(F32) * cu_ref[...].astype(F32)
    zh = hc_ref[...].astype(F32) * hu_ref[...].astype(F32)
    zh = jnp.where(i > 0, zh, 0.0)
    hl = zh.shape[0]
    zm1 = zh[hl - 1:hl, :]
    zm2 = zh[hl - 2:hl - 1, :]
    row = lax.broadcasted_iota(jnp.int32, z.shape, 0)
    z1 = jnp.where(row == 0, zm1, pltpu.roll(z, 1, axis=0))
    z2 = jnp.where(row == 0, zm2, jnp.where(row == 1, zm1, pltpu.roll(z, 2, axis=0)))
    w = w_ref[...]
    y = z2 * w[0:1, :] + z1 * w[1:2, :] + z * w[2:3, :]
    o_ref[...] = (cb_ref[...].astype(F32) * y).astype(BF16)


def _conv(proj, conv_w):
    T = proj.shape[0]
    W = conv_w.shape[1]
    tm = min(CONV_TM, T)
    hb = tm // CONV_HALO
    halo = lambda c: pl.BlockSpec((CONV_HALO, W), lambda i: (jnp.maximum(i * hb - 1, 0), c))
    return pl.pallas_call(
        _conv_kernel,
        grid=(T // tm,),
        in_specs=[
            pl.BlockSpec((tm, W), lambda i: (i, 0)),
            pl.BlockSpec((tm, W), lambda i: (i, 1)),
            pl.BlockSpec((tm, W), lambda i: (i, 2)),
            halo(1), halo(2),
            pl.BlockSpec((CONV_K, W), lambda i: (0, 0)),
        ],
        out_specs=pl.BlockSpec((tm, W), lambda i: (i, 0)),
        out_shape=jax.ShapeDtypeStruct((T, W), BF16),
        compiler_params=_cparams(("arbitrary",)),
        name="conv",
    )(proj, proj, proj, proj, proj, conv_w)


def _ret_gammas():
    return [1.0 - 2.0 ** (-5.0 - h) for h in range(RET_HEADS)]


def _ret_tables(R):
    n = np.arange(R, dtype=np.float64)
    chunk = np.arange(R) // CHUNK
    allowed = chunk[None, :] <= chunk[:, None]
    dm, qd, kd = [], [], []
    for g in _ret_gammas():
        lg = math.log(g)
        dm.append(np.where(allowed, np.exp(lg * np.abs(n[:, None] - n[None, :])), 0.0))
        qd.append(np.broadcast_to(np.exp(lg * (n + 1.0))[:, None], (R, LANES)))
        kd.append(np.broadcast_to(np.exp(lg * (R - 1.0 - n))[:, None], (R, LANES)))
    f = lambda a: jnp.asarray(np.stack(a), dtype=F32)
    return f(dm), f(qd), f(kd)


def _rope_tables(T):
    half = RET_DK // 2
    inv = ROPE_BASE ** (-jnp.arange(half, dtype=F32) / half)
    ang = jnp.arange(T, dtype=F32)[:, None] * inv[None, :]
    cos, sin = jnp.cos(ang), jnp.sin(ang)
    reps = LANES // RET_DK
    cos_t = jnp.tile(jnp.concatenate([cos, cos], axis=1), (1, reps))
    sin_t = jnp.tile(jnp.concatenate([-sin, sin], axis=1), (1, reps))
    return cos_t, sin_t


def _ret_kernel(q_ref, k_ref, v_ref, g_ref, cos_ref, sin_ref, dm_ref, qd_ref, kd_ref, gn_ref,
                o_ref, s_scr):
    i = pl.program_id(0)
    R = q_ref.shape[0]

    @pl.when(i == 0)
    def _():
        s_scr[...] = jnp.zeros_like(s_scr)

    lane = lax.broadcasted_iota(jnp.int32, (R, LANES), 1)
    first_half = (lane % RET_DK) < (RET_DK // 2)
    low_head = lane < RET_DK
    cosv = cos_ref[...]
    sinv = sin_ref[...]
    c_dec = [g ** R for g in _ret_gammas()]

    def rot(t):
        swapped = jnp.where(first_half, pltpu.roll(t, LANES - RET_DK // 2, axis=1),
                            pltpu.roll(t, RET_DK // 2, axis=1))
        return t * cosv + swapped * sinv

    heads_per_vreg = LANES // RET_DK
    for p in range(RET_HEADS // heads_per_vreg):
        cols = slice(p * LANES, (p + 1) * LANES)
        qr = rot(q_ref[:, cols].astype(F32))
        kb = (rot(k_ref[:, cols].astype(F32)) * (RET_DK ** -0.5)).astype(BF16)
        for hh in range(heads_per_vreg):
            h = p * heads_per_vreg + hh
            hc = slice(h * RET_DV, (h + 1) * RET_DV)
            mask = low_head if hh == 0 else jnp.logical_not(low_head)
            qm = jnp.where(mask, qr, 0.0).astype(BF16)
            s = lax.dot_general(qm, kb, (((1,), (1,)), ((), ())), preferred_element_type=F32)
            s = s * dm_ref[h]
            v = v_ref[:, hc]
            o = jnp.dot(s.astype(BF16), v, preferred_element_type=F32)
            state = s_scr[h]
            o = o + jnp.dot(qm, state.astype(BF16), preferred_element_type=F32) * qd_ref[h]
            vd = (v.astype(F32) * kd_ref[h]).astype(BF16)
            kv = lax.dot_general(kb, vd, (((0,), (0,)), ((), ())), preferred_element_type=F32)
            s_scr[h] = state * c_dec[h] + kv
            mu = jnp.mean(o, axis=-1, keepdims=True)
            d = o - mu
            var = jnp.mean(d * d, axis=-1, keepdims=True)
            on = d * lax.rsqrt(var + EPS) * gn_ref[:, hc]
            gate = g_ref[:, hc].astype(F32)
            o_ref[:, hc] = (gate * jax.nn.sigmoid(gate) * on).astype(BF16)


def _retention(proj, cos_t, sin_t, gn_g, col0):
    T = proj.shape[0]
    R = min(RET_ROWS, T)
    QK = RET_HEADS * RET_DK
    V = RET_HEADS * RET_DV
    dm, qd, kd = _ret_tables(R)
    q_blk = col0 // QK
    v_blk = (col0 + 2 * QK) // V
    full3 = lambda a: pl.BlockSpec(a.shape, lambda i: (0, 0, 0))
    return pl.pallas_call(
        _ret_kernel,
        grid=(T // R,),
        in_specs=[
            pl.BlockSpec((R, QK), lambda i: (i, q_blk)),
            pl.BlockSpec((R, QK), lambda i: (i, q_blk + 1)),
            pl.BlockSpec((R, V), lambda i: (i, v_blk)),
            pl.BlockSpec((R, V), lambda i: (i, v_blk + 1)),
            pl.BlockSpec((R, LANES), lambda i: (i, 0)),
            pl.BlockSpec((R, LANES), lambda i: (i, 0)),
            full3(dm), full3(qd), full3(kd),
            pl.BlockSpec((1, V), lambda i: (0, 0)),
        ],
        out_specs=pl.BlockSpec((R, V), lambda i: (i, 0)),
        out_shape=jax.ShapeDtypeStruct((T, V), BF16),
        scratch_shapes=[pltpu.VMEM((RET_HEADS, LANES, RET_DV), F32)],
        compiler_params=_cparams(("arbitrary",)),
        name="retention",
    )(proj, proj, proj, proj, cos_t, sin_t, dm, qd, kd, gn_g)


def _fox_kernel(qt_ref, k_ref, vt_ref, cum_ref, o_ref, ck_scr, cend_scr, kn_scr,
                s0_scr, s1_scr, p0_scr, p1_scr, a0_scr, a1_scr, m_scr, l_scr, acc_scr):
    h = pl.program_id(0)
    qi = pl.program_id(1)
    tq = qt_ref.shape[1]
    tk = s0_scr.shape[0]
    T = k_ref.shape[0]
    assert tq == 2 * tk
    reps = tq // LANES
    s_scr, p_scr, a_scr = (s0_scr, s1_scr), (p0_scr, p1_scr), (a0_scr, a1_scr)

    @pl.when(qi == 0)
    def _():
        cend_scr[...] = jnp.zeros_like(cend_scr)
        ones = jnp.ones((FOX_DH, LANES), BF16)
        kn2 = jnp.zeros((SUBLANES, LANES), F32)
        for r in range(0, T, tq):
            kf = k_ref[r:r + tq, :].astype(F32)
            n2 = jnp.dot((kf * kf).astype(BF16), ones, preferred_element_type=F32)
            kn2 = jnp.maximum(kn2, jnp.max(n2.reshape(tq // SUBLANES, SUBLANES, LANES), axis=0))
        kn_scr[...] = jnp.max(kn2, axis=0, keepdims=True)

    lane = lax.broadcasted_iota(jnp.int32, cum_ref.shape, 1)
    col = jnp.sum(jnp.where(lane == h, cum_ref[...], 0.0), axis=1, keepdims=True) * LOG2E
    q0 = pl.multiple_of(qi * tq, tq)
    ck_rep = jnp.broadcast_to(col, cum_ref.shape)
    ck_scr[pl.ds(q0, tq), :] = ck_rep
    cend_scr[pl.ds(2 * qi, 1), :] = ck_rep[tk - 1:tk, :]
    cend_scr[pl.ds(2 * qi + 1, 1), :] = ck_rep[tq - 1:tq, :]

    qf = qt_ref[...].astype(F32)
    qn2 = jnp.max(jnp.sum(qf * qf, axis=0, keepdims=True), axis=1, keepdims=True)
    reach = 2.0 * FOX_NORM_SLACK * jnp.sqrt(qn2 * kn_scr[...]) + FOX_SKIP_LOG2
    decay = cend_scr[...] - ck_rep[0:1, :]
    chunk_id = lax.broadcasted_iota(jnp.int32, decay.shape, 0)
    dead = jnp.logical_and(decay >= reach, chunk_id < 2 * qi)
    n_dead = jnp.max(jnp.sum(dead.astype(jnp.int32), axis=0, keepdims=True))
    first_pair = n_dead // 2

    m_scr[...] = jnp.full_like(m_scr, -jnp.inf)
    l_scr[...] = jnp.zeros_like(l_scr)
    acc_scr[...] = jnp.zeros_like(acc_scr)
    p1_scr[...] = jnp.zeros_like(p1_scr)
    a1_scr[...] = jnp.ones_like(a1_scr)

    def score(c, slot):
        k0 = pl.multiple_of(c * tk, tk)
        s = jnp.dot(k_ref[pl.ds(k0, tk), :], qt_ref[...], preferred_element_type=F32)
        bias = ck_scr[pl.ds(k0, tk), :]
        s_scr[slot][...] = s - jnp.concatenate([bias] * reps, axis=1)

    def softmax(c, slot, masked):
        if masked:
            s = s_scr[slot][...]
            kpos = c * tk + lax.broadcasted_iota(jnp.int32, s.shape, 0)
            qpos = q0 + lax.broadcasted_iota(jnp.int32, s.shape, 1)
            s_scr[slot][...] = jnp.where(kpos <= qpos, s, -jnp.inf)
        sref, pref = s_scr[slot], p_scr[slot]
        mx = sref[0:FOX_SLAB, :]
        for r in range(FOX_SLAB, tk, FOX_SLAB):
            mx = jnp.maximum(mx, sref[r:r + FOX_SLAB, :])
        m_old = m_scr[...]
        m_new = jnp.maximum(m_old, jnp.max(mx, axis=0, keepdims=True))
        alpha = jnp.exp2(m_old - m_new)
        m_rows = jnp.broadcast_to(m_new, (FOX_SLAB, tq))
        psum = jnp.zeros((FOX_SLAB, tq), F32)
        for r in range(0, tk, FOX_SLAB):
            p = jnp.exp2(sref[r:r + FOX_SLAB, :] - m_rows)
            psum = psum + p
            pref[r:r + FOX_SLAB, :] = p.astype(BF16)
        l_scr[...] = alpha * l_scr[...] + jnp.sum(psum, axis=0, keepdims=True)
        a_scr[slot][...] = alpha
        m_scr[...] = m_new

    def accumulate(c, slot):
        k0 = pl.multiple_of(jnp.maximum(c, 0) * tk, tk)
        pv = jnp.dot(vt_ref[:, pl.ds(k0, tk)], p_scr[slot][...], preferred_element_type=F32)
        acc_scr[...] = a_scr[slot][...] * acc_scr[...] + pv

    score(2 * first_pair, 0)

    def pair(i, carry):
        c = 2 * i
        score(c + 1, 1)
        accumulate(c - 1, 1)
        softmax(c, 0, False)
        score(c + 2, 0)
        accumulate(c, 0)
        softmax(c + 1, 1, False)
        return carry

    lax.fori_loop(first_pair, qi, pair, 0)
    c = 2 * qi
    score(c + 1, 1)
    accumulate(c - 1, 1)
    softmax(c, 0, True)
    accumulate(c, 0)
    softmax(c + 1, 1, True)
    accumulate(c + 1, 1)
    o_ref[...] = (acc_scr[...] / l_scr[...]).T.astype(BF16)


def _fox(proj, proj_t, cum, k_col0):
    T = proj.shape[0]
    tq = min(FOX_TQ, T)
    tk = tq // 2
    W = FOX_HEADS * FOX_DH
    kb = k_col0 // FOX_DH
    return pl.pallas_call(
        _fox_kernel,
        grid=(FOX_HEADS, T // tq),
        in_specs=[
            pl.BlockSpec((FOX_DH, tq), lambda h, i: (h, i)),
            pl.BlockSpec((T, FOX_DH), lambda h, i: (0, kb + h)),
            pl.BlockSpec((FOX_DH, T), lambda h, i: (FOX_HEADS + h, 0)),
            pl.BlockSpec((tq, LANES), lambda h, i: (i, 0)),
        ],
        out_specs=pl.BlockSpec((tq, FOX_DH), lambda h, i: (i, h)),
        out_shape=jax.ShapeDtypeStruct((T, W), BF16),
        scratch_shapes=[pltpu.VMEM((T, LANES), F32),
                        pltpu.VMEM((-(-(T // tk) // SUBLANES) * SUBLANES, LANES), F32),
                        pltpu.VMEM((1, LANES), F32),
                        pltpu.VMEM((tk, tq), F32), pltpu.VMEM((tk, tq), F32),
                        pltpu.VMEM((tk, tq), BF16), pltpu.VMEM((tk, tq), BF16),
                        pltpu.VMEM((1, tq), F32), pltpu.VMEM((1, tq), F32),
                        pltpu.VMEM((1, tq), F32), pltpu.VMEM((1, tq), F32),
                        pltpu.VMEM((FOX_DH, tq), F32)],
        compiler_params=_cparams(("arbitrary", "arbitrary")),
        name="fox",
    )(proj_t, proj, proj_t, cum)


def _outproj_kernel(yc_ref, yr_ref, yf_ref, wc_ref, wr_ref, wf_ref, x_ref, g1_ref,
                    g_ref, sh_ref, sc_ref, rwh_ref, rwl_ref, xo_ref, h_ref, lg_ref):
    mix = jnp.dot(yc_ref[...], wc_ref[...], preferred_element_type=F32)
    mix = mix + jnp.dot(yr_ref[...], wr_ref[...], preferred_element_type=F32)
    mix = mix + jnp.dot(yf_ref[...], wf_ref[...], preferred_element_type=F32)
    x = x_ref[...] + g1_ref[...] * mix
    xo_ref[...] = x
    h = _modulated_norm(x, g_ref[...], sh_ref[...], sc_ref[...])
    h_ref[...] = _pack_halves(h)
    h_hi = h.astype(BF16)
    h_lo = (h - h_hi.astype(F32)).astype(BF16)
    nt = lambda a, b: lax.dot_general(a, b, (((1,), (1,)), ((), ())), preferred_element_type=F32)
    lg_ref[...] = nt(rwh_ref[...], h_hi) + (nt(rwh_ref[...], h_lo) + nt(rwl_ref[...], h_hi))


def _outproj(yc, yr, yf, wc, wr, wf, x, g1, g, sh, sc, rw_t):
    T, D = x.shape
    tm = min(OUT_TM, T)
    E = rw_t.shape[0]
    rw_hi = rw_t.astype(BF16)
    rw_lo = (rw_t - rw_hi.astype(F32)).astype(BF16)
    vec = pl.BlockSpec((1, D), lambda i: (0, 0))
    rows = lambda a: pl.BlockSpec((tm, a.shape[1]), lambda i: (i, 0))
    whole = lambda a: pl.BlockSpec(a.shape, lambda i: (0, 0))
    return pl.pallas_call(
        _outproj_kernel,
        grid=(T // tm,),
        in_specs=[rows(yc), rows(yr), rows(yf), whole(wc), whole(wr), whole(wf), rows(x),
                  vec, vec, vec, vec, whole(rw_hi), whole(rw_lo)],
        out_specs=[rows(x), pl.BlockSpec((tm, D // 2), lambda i: (i, 0)),
                   pl.BlockSpec((E, tm), lambda i: (0, i))],
        out_shape=[jax.ShapeDtypeStruct((T, D), F32), jax.ShapeDtypeStruct((T, D // 2), U32),
                   jax.ShapeDtypeStruct((E, T), F32)],
        compiler_params=_cparams(("arbitrary",)),
        name="outproj",
    )(yc, yr, yf, wc, wr, wf, x, g1, g, sh, sc, rw_hi, rw_lo)


def _route_kernel(lg_ref, b_ref, tri_ref, e_ref, r_ref, w_ref, cnt_ref, carry_scr):
    i = pl.program_id(0)
    E, tt = lg_ref.shape

    @pl.when(i == 0)
    def _():
        carry_scr[...] = jnp.zeros_like(carry_scr)

    aff = jax.nn.sigmoid(lg_ref[...])
    sel = aff + b_ref[...]
    row8 = lax.broadcasted_iota(jnp.int32, (GROUP_SIZE, tt), 0)
    best = None
    for g in range(N_GROUPS):
        slab = sel[g * GROUP_SIZE:(g + 1) * GROUP_SIZE, :]
        m1 = jnp.max(slab, axis=0, keepdims=True)
        i1 = jnp.min(jnp.where(slab == m1, row8, GROUP_SIZE), axis=0, keepdims=True)
        rest = jnp.where(row8 == i1, -jnp.inf, slab)
        m2 = jnp.max(rest, axis=0, keepdims=True)
        i2 = jnp.min(jnp.where(rest == m2, row8, GROUP_SIZE), axis=0, keepdims=True)
        cand = (m1 + m2, g * GROUP_SIZE + i1, g * GROUP_SIZE + i2)
        if best is None:
            best = cand
        else:
            upd = cand[0] > best[0]
            best = tuple(jnp.where(upd, n, o) for n, o in zip(cand, best))
    _, e0, e1 = best

    row = lax.broadcasted_iota(jnp.int32, (E, tt), 0)
    oh0 = row == e0
    oh1 = row == e1
    a0 = jnp.sum(jnp.where(oh0, aff, 0.0), axis=0, keepdims=True)
    a1 = jnp.sum(jnp.where(oh1, aff, 0.0), axis=0, keepdims=True)
    w_ref[0:1, :] = a0 / (a0 + a1)
    w_ref[1:2, :] = a1 / (a0 + a1)
    e_ref[0:1, :] = e0
    e_ref[1:2, :] = e1

    oh = jnp.logical_or(oh0, oh1)
    ohf = jnp.where(oh, 1.0, 0.0)
    before = jnp.dot(ohf.astype(BF16), tri_ref[...], preferred_element_type=F32) + carry_scr[:, 0:1]
    r_ref[0:1, :] = jnp.sum(jnp.where(oh0, before, 0.0), axis=0, keepdims=True).astype(jnp.int32)
    r_ref[1:2, :] = jnp.sum(jnp.where(oh1, before, 0.0), axis=0, keepdims=True).astype(jnp.int32)
    carry = carry_scr[...] + jnp.sum(ohf, axis=1, keepdims=True)
    carry_scr[...] = carry
    cnt_ref[...] = carry


def _route(logits_t, router_b):
    E, T = logits_t.shape
    tt = min(ROUTE_TT, T)
    tri = jnp.asarray(np.triu(np.ones((tt, tt), np.float32), k=1), dtype=BF16)
    pair = lambda dt: jax.ShapeDtypeStruct((TOP_K, T), dt)
    return pl.pallas_call(
        _route_kernel,
        grid=(T // tt,),
        in_specs=[
            pl.BlockSpec((E, tt), lambda i: (0, i)),
            pl.BlockSpec((E, 1), lambda i: (0, 0)),
            pl.BlockSpec((tt, tt), lambda i: (0, 0)),
        ],
        out_specs=[
            pl.BlockSpec((TOP_K, tt), lambda i: (0, i)),
            pl.BlockSpec((TOP_K, tt), lambda i: (0, i)),
            pl.BlockSpec((TOP_K, tt), lambda i: (0, i)),
            pl.BlockSpec((E, LANES), lambda i: (0, 0)),
        ],
        out_shape=[pair(jnp.int32), pair(jnp.int32), pair(F32), jax.ShapeDtypeStruct((E, LANES), F32)],
        scratch_shapes=[pltpu.VMEM((E, LANES), F32)],
        compiler_params=_cparams(("arbitrary",)),
        name="route",
    )(logits_t, router_b.reshape(E, 1), tri)


def _block_tables(counts, T):
    E = counts.shape[0]
    A = T * TOP_K
    B = MOE_ROWS
    seg = (counts + SUBLANES - 1) // SUBLANES * SUBLANES
    seg_end = jnp.cumsum(seg)
    seg_start = seg_end - seg
    nb = (A + E * (B - 1) + B - 1) // B
    nblk = (counts + B - 1) // B
    blk_end = jnp.cumsum(nblk)
    total = blk_end[-1]
    b = jnp.arange(nb, dtype=jnp.int32)
    bc = jnp.minimum(b, total - 1)
    blk_e = jnp.minimum(jnp.searchsorted(blk_end, bc, side="right"), E - 1).astype(jnp.int32)
    local = bc - (blk_end[blk_e] - nblk[blk_e])
    blk_start = (seg_start[blk_e] + local * B).astype(jnp.int32)
    blk_first = jnp.logical_and(b < total, local == 0).astype(jnp.int32)
    rows = A + E * (SUBLANES - 1) + B
    rows = (rows + SUBLANES - 1) // SUBLANES * SUBLANES
    return (seg_start.astype(jnp.int32), seg_end.astype(jnp.int32), blk_e, blk_start, blk_first,
            total.astype(jnp.int32).reshape(1), nb, rows)


def _zero_rows_from(zero_ref, hbm_ref, start, sem):
    piece = zero_ref.shape[0]
    rows = hbm_ref.shape[0]

    def body(j, carry):
        at = pl.multiple_of(jnp.minimum(start + j * piece, rows - piece), SUBLANES)
        cp = pltpu.make_async_copy(zero_ref, hbm_ref.at[pl.ds(at, piece), :], sem)
        cp.start()
        cp.wait()
        return carry

    lax.fori_loop(0, (rows - start + piece - 1) // piece, body, 0)


def _dispatch_kernel(seg_end_ref, pos_ref, h_ref, xs_hbm, zero_scr, sem, zsem):
    i = pl.program_id(0)
    td = pos_ref.shape[1]
    E = seg_end_ref.shape[0]

    @pl.when(i == 0)
    def _():
        zero_scr[...] = jnp.zeros_like(zero_scr)

        def tail(e, carry):
            end = seg_end_ref[e]
            at = pl.multiple_of(jnp.maximum(end - SUBLANES, 0), SUBLANES)
            cp = pltpu.make_async_copy(zero_scr.at[pl.ds(0, SUBLANES), :],
                                       xs_hbm.at[pl.ds(at, SUBLANES), :], zsem)
            cp.start()
            cp.wait()
            return carry

        lax.fori_loop(0, E, tail, 0)
        _zero_rows_from(zero_scr, xs_hbm, seg_end_ref[E - 1], zsem)

    def body(r, carry):
        for k in range(TOP_K):
            pltpu.make_async_copy(h_ref.at[pl.ds(r, 1), :],
                                  xs_hbm.at[pl.ds(pos_ref[k, r], 1), :], sem).start()
        return carry

    lax.fori_loop(0, td, body, 0, unroll=8)
    for k in range(TOP_K):
        pltpu.make_async_copy(h_ref, xs_hbm.at[pl.ds(0, td), :], sem).wait()


def _dispatch(h2, pos, seg_end, rows):
    T, D = h2.shape
    td = min(DISPATCH_TT, T)
    grid_spec = pltpu.PrefetchScalarGridSpec(
        num_scalar_prefetch=1,
        grid=(T // td,),
        in_specs=[pl.BlockSpec((TOP_K, td), lambda i, se: (0, i), memory_space=pltpu.SMEM),
                  pl.BlockSpec((td, D), lambda i, se: (i, 0))],
        out_specs=pl.BlockSpec(memory_space=pl.ANY),
        scratch_shapes=[pltpu.VMEM((MOE_ROWS, D), h2.dtype), pltpu.SemaphoreType.DMA(()),
                        pltpu.SemaphoreType.DMA(())],
    )
    return pl.pallas_call(
        _dispatch_kernel,
        grid_spec=grid_spec,
        out_shape=jax.ShapeDtypeStruct((rows, D), h2.dtype),
        compiler_params=_cparams(("arbitrary",), disable_bounds_checks=True),
        name="dispatch",
    )(seg_end, pos, h2)


def _experts_kernel(blk_e_ref, blk_start_ref, blk_first_ref, total_ref, tail_ref,
                    xs_hbm, wg_ref, wu_ref, wd_ref, ys_hbm,
                    wg_b, wu_b, wd_b, xbuf, ybuf, isem, osem):
    b = pl.program_id(0)
    total = total_ref[0]
    B = xbuf.shape[1]
    slot = b % 2

    def load(blk, to_slot):
        at = pl.multiple_of(blk_start_ref[blk], SUBLANES)
        return pltpu.make_async_copy(xs_hbm.at[pl.ds(at, B), :], xbuf.at[to_slot], isem.at[to_slot])

    def store(blk, from_slot):
        at = pl.multiple_of(blk_start_ref[blk], SUBLANES)
        return pltpu.make_async_copy(ybuf.at[from_slot], ys_hbm.at[pl.ds(at, B), :], osem)

    @pl.when(b == 0)
    def _():
        load(0, 0).start()
        ybuf[1] = jnp.zeros(ybuf.shape[1:], ybuf.dtype)
        _zero_rows_from(ybuf.at[1], ys_hbm, tail_ref[0], osem)

    @pl.when(b + 1 < total)
    def _():
        load(b + 1, 1 - slot).start()

    @pl.when(b < total)
    def _():
        load(b, slot).wait()

        @pl.when(blk_first_ref[b] == 1)
        def _():
            wg_b[...] = wg_ref[...].astype(BF16)
            wu_b[...] = wu_ref[...].astype(BF16)
            wd_b[...] = wd_ref[...].astype(BF16)

        lo, hi = _unpack_halves(xbuf[slot])
        lo, hi = lo.astype(BF16), hi.astype(BF16)
        half = lo.shape[1]
        g = (jnp.dot(lo, wg_b[:half, :], preferred_element_type=F32)
             + jnp.dot(hi, wg_b[half:, :], preferred_element_type=F32))
        u = (jnp.dot(lo, wu_b[:half, :], preferred_element_type=F32)
             + jnp.dot(hi, wu_b[half:, :], preferred_element_type=F32))
        a = (g * jax.nn.sigmoid(g) * u).astype(BF16)
        ybuf[slot] = _pack_halves(jnp.dot(a, wd_b[...], preferred_element_type=F32))

        @pl.when(b > 0)
        def _():
            store(b - 1, 1 - slot).wait()

        store(b, slot).start()

        @pl.when(b == total - 1)
        def _():
            store(b, slot).wait()


def _experts(xs, layer, w_gate, w_up, w_down, blk_e, blk_start, blk_first, total, tail, nb):
    rows, DP = xs.shape
    D, DE = w_gate.shape[-2:]
    B = MOE_ROWS
    wspec = lambda r, c: pl.BlockSpec((None, None, r, c), lambda b, be, *_: (layer, be[b], 0, 0))
    grid_spec = pltpu.PrefetchScalarGridSpec(
        num_scalar_prefetch=5,
        grid=(nb,),
        in_specs=[pl.BlockSpec(memory_space=pl.ANY), wspec(D, DE), wspec(D, DE), wspec(DE, D)],
        out_specs=pl.BlockSpec(memory_space=pl.ANY),
        scratch_shapes=[
            pltpu.VMEM((D, DE), BF16), pltpu.VMEM((D, DE), BF16), pltpu.VMEM((DE, D), BF16),
            pltpu.VMEM((2, B, DP), U32), pltpu.VMEM((2, B, DP), U32),
            pltpu.SemaphoreType.DMA((2,)), pltpu.SemaphoreType.DMA(()),
        ],
    )
    return pl.pallas_call(
        _experts_kernel,
        grid_spec=grid_spec,
        out_shape=jax.ShapeDtypeStruct((rows, DP), U32),
        compiler_params=_cparams(("arbitrary",)),
        name="experts",
    )(blk_e, blk_start, blk_first, total, tail, xs, w_gate, w_up, w_down)


def _combine_kernel(pos_ref, posn_ref, x_ref, ys_hbm, w_ref, g2_ref, fg_ref, o_ref, gbuf, sem,
                    *, final):
    i = pl.program_id(0)
    n = pl.num_programs(0)
    tm = x_ref.shape[0]
    slot = i % 2

    def gather(p_ref, to_slot):
        def body(r, carry):
            for k in range(TOP_K):
                pltpu.make_async_copy(ys_hbm.at[pl.ds(p_ref[k, r], 1), :],
                                      gbuf.at[to_slot, k, pl.ds(r, 1), :], sem.at[to_slot]).start()
            return carry

        lax.fori_loop(0, tm, body, 0, unroll=8)

    @pl.when(i == 0)
    def _():
        gather(pos_ref, 0)

    @pl.when(i + 1 < n)
    def _():
        gather(posn_ref, 1 - slot)

    for k in range(TOP_K):
        pltpu.make_async_copy(ys_hbm.at[pl.ds(0, tm), :], gbuf.at[slot, k], sem.at[slot]).wait()

    w = w_ref[...]
    D = x_ref.shape[1]
    half = D // 2
    lo0, hi0 = _unpack_halves(gbuf[slot, 0])
    lo1, hi1 = _unpack_halves(gbuf[slot, 1])
    xl = x_ref[:, :half] + g2_ref[:, :half] * (lo0 * w[:, 0:1] + lo1 * w[:, 1:2])
    xh = x_ref[:, half:] + g2_ref[:, half:] * (hi0 * w[:, 0:1] + hi1 * w[:, 1:2])
    if final:
        ms = (jnp.sum(xl * xl, axis=-1, keepdims=True) + jnp.sum(xh * xh, axis=-1, keepdims=True)) / D
        r = lax.rsqrt(ms + EPS)
        xl = xl * r * fg_ref[:, :half]
        xh = xh * r * fg_ref[:, half:]
    o_ref[:, :half] = xl
    o_ref[:, half:] = xh


def _combine(x, ys, pos, top_w, g2, final_g, final):
    T, D = x.shape
    tm = min(COMB_TM, T)
    n = T // tm
    vec = pl.BlockSpec((1, D), lambda i: (0, 0))
    return pl.pallas_call(
        functools.partial(_combine_kernel, final=final),
        grid=(n,),
        in_specs=[
            pl.BlockSpec((TOP_K, tm), lambda i: (0, i), memory_space=pltpu.SMEM),
            pl.BlockSpec((TOP_K, tm), lambda i: (0, jnp.minimum(i + 1, n - 1)), memory_space=pltpu.SMEM),
            pl.BlockSpec((tm, D), lambda i: (i, 0)),
            pl.BlockSpec(memory_space=pl.ANY),
            pl.BlockSpec((tm, TOP_K), lambda i: (i, 0)),
            vec, vec,
        ],
        out_specs=pl.BlockSpec((tm, D), lambda i: (i, 0)),
        out_shape=jax.ShapeDtypeStruct((T, D), F32),
        scratch_shapes=[pltpu.VMEM((2, TOP_K, tm, ys.shape[1]), ys.dtype), pltpu.SemaphoreType.DMA((2,))],
        compiler_params=_cparams(("arbitrary",), disable_bounds_checks=True),
        name="combine",
    )(pos, pos, x, ys, top_w, g2, final_g)


def kernel(x, c, ada_w, ada_b, norm1_g, norm2_g, w_in, conv_w, ret_gn_g, fox_fb, w_out,
           router_w, router_b, moe_w_gate, moe_w_up, moe_w_down, final_g):
    Bsz, S, D = x.shape
    assert Bsz == 1, "one sequence per call"
    L = ada_w.shape[0]
    T = S
    conv_wd = conv_w.shape[-1]
    ret_qk = RET_HEADS * RET_DK
    ret_v = RET_HEADS * RET_DV
    fox_w = FOX_HEADS * FOX_DH
    tn = INPROJ_TN
    ret_col0 = 3 * conv_wd
    fox_col0 = ret_col0 + 2 * ret_qk + 2 * ret_v
    n_main = fox_col0 + 3 * fox_w
    assert fox_w == tn and fox_col0 % tn == 0
    q_block = fox_col0 // tn
    v_block = q_block + 2
    k_col0 = fox_col0

    mod = _ada_mod(c, ada_w, ada_b)
    cos_t, sin_t = _rope_tables(T)
    rw_t = router_w.T
    xt = x.reshape(T, D)

    for l in range(L):
        sh1, sc1, g1, sh2, sc2, g2 = [mod[l, :, k * D:(k + 1) * D] for k in range(6)]
        w_ff = jnp.pad(w_in[l, :, n_main:], ((0, 0), (0, LANES - FOX_HEADS)))
        fb = jnp.pad(fox_fb[l], (0, LANES - FOX_HEADS)).reshape(1, LANES)
        proj, proj_t, cum = _inproj(xt, norm1_g[l].reshape(1, D), sh1, sc1, w_in, l, w_ff, fb,
                                    q_block, v_block, LOG2E * FOX_DH ** -0.5)

        y_conv = _conv(proj, conv_w[l])
        y_ret = _retention(proj, cos_t, sin_t, ret_gn_g[l].reshape(1, ret_v), ret_col0)
        y_fox = _fox(proj, proj_t, cum, k_col0)

        wo = w_out[l].astype(BF16)
        xt, h2, logits_t = _outproj(
            y_conv, y_ret, y_fox, wo[:conv_wd], wo[conv_wd:conv_wd + ret_v], wo[conv_wd + ret_v:],
            xt, g1, norm2_g[l].reshape(1, D), sh2, sc2, rw_t)

        top_e, rank, top_w, cnt = _route(logits_t, router_b)
        counts = cnt[:, 0].astype(jnp.int32)
        seg_start, seg_end, blk_e, blk_start, blk_first, total, nb, rows = _block_tables(counts, T)
        hit = top_e[None] == jnp.arange(N_EXPERTS, dtype=jnp.int32)[:, None, None]
        pos = rank + jnp.sum(jnp.where(hit, seg_start[:, None, None], 0), axis=0)
        xs = _dispatch(h2, pos, seg_end, rows)
        ys = _experts(xs, l, moe_w_gate, moe_w_up, moe_w_down, blk_e, blk_start, blk_first, total,
                      seg_end[-1:], nb)
        xt = _combine(xt, ys, pos, top_w.T, g2, final_g.reshape(1, D), final=(l == L - 1))

    return xt.reshape(Bsz, S, D)
```

```python
import functools
import math

import numpy as np
import jax
import jax.numpy as jnp
from jax import lax
from jax.experimental import pallas as pl
from jax.experimental.pallas import tpu as pltpu

F32 = jnp.float32
BF16 = jnp.bfloat16

CHUNK = 64
CONV_K = 3
RET_HEADS = 6
RET_DK = 64
RET_DV = 128
FOX_HEADS = 6
FOX_DH = 128
N_EXPERTS = 64
N_GROUPS = 8
GROUP_SIZE = N_EXPERTS // N_GROUPS
TOP_K = 2
ROPE_BASE = 10000.0
EPS = 1e-6
LOG2E = 1.4426950408889634

LANES = 128
SUBLANES = 8

MOD_TN = 1024
INPROJ_TM = 1024
INPROJ_TN = 768
CONV_TM = 1024
CONV_HALO = 16
RET_ROWS = 256
FOX_TQ = 1024
FOX_SLAB = 16
FOX_SKIP_LOG2 = 160.0
FOX_NORM_SLACK = 1.02
OUT_TM = 512
MOE_ROWS = 256
ROUTE_TT = 1024
DISPATCH_TT = 512
COMB_TM = 512
VMEM_LIMIT = 56 * 1024 * 1024


def _cparams(sem, **kw):
    return pltpu.CompilerParams(dimension_semantics=sem, vmem_limit_bytes=VMEM_LIMIT, **kw)


U32 = jnp.uint32


def _pack_halves(x):
    half = x.shape[1] // 2
    xb = x.astype(BF16).astype(F32)
    lo = lax.bitcast_convert_type(xb[:, :half], U32)
    hi = lax.bitcast_convert_type(xb[:, half:], U32)
    return lax.shift_right_logical(lo, U32(16)) | (hi & U32(0xFFFF0000))


def _unpack_halves(w):
    lo = lax.bitcast_convert_type(lax.shift_left(w, U32(16)), F32)
    hi = lax.bitcast_convert_type(w & U32(0xFFFF0000), F32)
    return lo, hi


def _mod_kernel(c_ref, w_ref, b_ref, o_ref):
    c = c_ref[...]
    ca = c * jax.nn.sigmoid(c)
    o_ref[0] = jnp.sum(w_ref[0] * ca, axis=0, keepdims=True) + b_ref[0]


def _ada_mod(c, ada_w, ada_b):
    L, D, N = ada_w.shape
    tn = MOD_TN
    return pl.pallas_call(
        _mod_kernel,
        grid=(L, N // tn),
        in_specs=[
            pl.BlockSpec((D, 1), lambda l, j: (0, 0)),
            pl.BlockSpec((1, D, tn), lambda l, j: (l, 0, j)),
            pl.BlockSpec((1, 1, tn), lambda l, j: (l, 0, j)),
        ],
        out_specs=pl.BlockSpec((1, 1, tn), lambda l, j: (l, 0, j)),
        out_shape=jax.ShapeDtypeStruct((L, 1, N), F32),
        compiler_params=_cparams(("arbitrary", "arbitrary")),
        name="ada_mod",
    )(c.reshape(D, 1), ada_w, ada_b.reshape(L, 1, N))


def _modulated_norm(x, g, sh, sc):
    ms = jnp.mean(x * x, axis=-1, keepdims=True)
    y = x * lax.rsqrt(ms + EPS) * g
    return y * (1.0 + sc) + sh


def _log_sigmoid(x):
    return jnp.minimum(x, 0.0) - jnp.log1p(jnp.exp(-jnp.abs(x)))


def _inproj_kernel(x_ref, g_ref, sh_ref, sc_ref, w_ref, wff_ref, fb_ref,
                   proj_ref, projt_ref, cum_ref, h_scr, wt_scr, carry_scr,
                   *, q_block, v_block, q_scale):
    i = pl.program_id(0)
    j = pl.program_id(1)
    tm = x_ref.shape[0]

    @pl.when(jnp.logical_and(i == 0, j == 0))
    def _():
        carry_scr[...] = jnp.zeros_like(carry_scr)

    @pl.when(j == 0)
    def _():
        h = _modulated_norm(x_ref[...], g_ref[...], sh_ref[...], sc_ref[...])
        hb = h.astype(BF16)
        h_scr[...] = hb
        ff = jnp.dot(hb, wff_ref[...].astype(BF16), preferred_element_type=F32) + fb_ref[...]
        c = _log_sigmoid(ff)
        row = lax.broadcasted_iota(jnp.int32, c.shape, 0)
        d = 1
        while d < tm:
            c = c + jnp.where(row >= d, pltpu.roll(c, d, axis=0), 0.0)
            d *= 2
        c = c + carry_scr[...]
        cum_ref[...] = c
        carry_scr[...] = c[tm - 1:tm, :]

    @pl.when(jnp.logical_and(j != q_block, j != v_block))
    def _():
        proj_ref[...] = jnp.dot(h_scr[...], w_ref[...].astype(BF16),
                                preferred_element_type=F32).astype(BF16)

    def feature_major(slot, scale):
        @pl.when(i == 0)
        def _():
            wt_scr[slot] = w_ref[...].T.astype(BF16)

        r = lax.dot_general(wt_scr[slot], h_scr[...], (((1,), (1,)), ((), ())),
                            preferred_element_type=F32)
        projt_ref[...] = (r * scale).astype(BF16)

    @pl.when(j == q_block)
    def _():
        feature_major(0, q_scale)

    @pl.when(j == v_block)
    def _():
        feature_major(1, 1.0)


def _inproj(x, g, sh, sc, w_in, layer, w_ff, fb, q_block, v_block, q_scale):
    T, D = x.shape
    tm, tn = min(INPROJ_TM, T), INPROJ_TN
    nj = v_block + 1
    is_q = lambda j: (j == q_block).astype(jnp.int32)
    is_v = lambda j: (j == v_block).astype(jnp.int32)
    w_col = lambda i, j: jnp.where(i == 0, j, j - is_q(j) - is_v(j))
    out_col = lambda j: j - (j >= q_block).astype(jnp.int32) - (j >= v_block).astype(jnp.int32)
    vec = pl.BlockSpec((1, D), lambda i, j: (0, 0))
    return pl.pallas_call(
        functools.partial(_inproj_kernel, q_block=q_block, v_block=v_block, q_scale=q_scale),
        grid=(T // tm, nj),
        in_specs=[
            pl.BlockSpec((tm, D), lambda i, j: (i, 0), pipeline_mode=pl.Buffered(1)),
            vec, vec, vec,
            pl.BlockSpec((None, D, tn), lambda i, j: (layer, 0, w_col(i, j))),
            pl.BlockSpec((D, LANES), lambda i, j: (0, 0)),
            pl.BlockSpec((1, LANES), lambda i, j: (0, 0)),
        ],
        out_specs=[
            pl.BlockSpec((tm, tn), lambda i, j: (i, out_col(j))),
            pl.BlockSpec((tn, tm), lambda i, j: (is_v(j), i)),
            pl.BlockSpec((tm, LANES), lambda i, j: (i, 0)),
        ],
        out_shape=[
            jax.ShapeDtypeStruct((T, (nj - 2) * tn), BF16),
            jax.ShapeDtypeStruct((2 * tn, T), BF16),
            jax.ShapeDtypeStruct((T, LANES), F32),
        ],
        scratch_shapes=[pltpu.VMEM((tm, D), BF16), pltpu.VMEM((2, tn, D), BF16),
                        pltpu.VMEM((1, LANES), F32)],
        compiler_params=_cparams(("arbitrary", "arbitrary")),
        name="inproj",
    )(x, g, sh, sc, w_in, w_ff, fb)


def _conv_kernel(cb_ref, cc_ref, cu_ref, hc_ref, hu_ref, w_ref, o_ref):
    i = pl.program_id(0)
    z = cc_ref[...].astype(F32) * cu_ref[...].astype(F32)
    zh = hc_ref[...].astype(F32) * hu_ref[...].astype(F32)
    zh = jnp.where(i > 0, zh, 0.0)
    hl = zh.shape[0]
    zm1 = zh[hl - 1:hl, :]
    zm2 = zh[hl - 2:hl - 1, :]
    row = lax.broadcasted_iota(jnp.int32, z.shape, 0)
    z1 = jnp.where(row == 0, zm1, pltpu.roll(z, 1, axis=0))
    z2 = jnp.where(row == 0, zm2, jnp.where(row == 1, zm1, pltpu.roll(z, 2, axis=0)))
    w = w_ref[...]
    y = z2 * w[0:1, :] + z1 * w[1:2, :] + z * w[2:3, :]
    o_ref[...] = (cb_ref[...].astype(F32) * y).astype(BF16)


def _conv(proj, conv_w):
    T = proj.shape[0]
    W = conv_w.shape[1]
    tm = min(CONV_TM, T)
    hb = tm // CONV_HALO
    halo = lambda c: pl.BlockSpec((CONV_HALO, W), lambda i: (jnp.maximum(i * hb - 1, 0), c))
    return pl.pallas_call(
        _conv_kernel,
        grid=(T // tm,),
        in_specs=[
            pl.BlockSpec((tm, W), lambda i: (i, 0)),
            pl.BlockSpec((tm, W), lambda i: (i, 1)),
            pl.BlockSpec((tm, W), lambda i: (i, 2)),
            halo(1), halo(2),
            pl.BlockSpec((CONV_K, W), lambda i: (0, 0)),
        ],
        out_specs=pl.BlockSpec((tm, W), lambda i: (i, 0)),
        out_shape=jax.ShapeDtypeStruct((T, W), BF16),
        compiler_params=_cparams(("arbitrary",)),
        name="conv",
    )(proj, proj, proj, proj, proj, conv_w)


def _ret_gammas():
    return [1.0 - 2.0 ** (-5.0 - h) for h in range(RET_HEADS)]


def _ret_tables(R):
    n = np.arange(R, dtype=np.float64)
    chunk = np.arange(R) // CHUNK
    allowed = chunk[None, :] <= chunk[:, None]
    dm, qd, kd = [], [], []
    for g in _ret_gammas():
        lg = math.log(g)
        dm.append(np.where(allowed, np.exp(lg * np.abs(n[:, None] - n[None, :])), 0.0))
        qd.append(np.broadcast_to(np.exp(lg * (n + 1.0))[:, None], (R, LANES)))
        kd.append(np.broadcast_to(np.exp(lg * (R - 1.0 - n))[:, None], (R, LANES)))
    f = lambda a: jnp.asarray(np.stack(a), dtype=F32)
    return f(dm), f(qd), f(kd)


def _rope_tables(T):
    half = RET_DK // 2
    inv = ROPE_BASE ** (-jnp.arange(half, dtype=F32) / half)
    ang = jnp.arange(T, dtype=F32)[:, None] * inv[None, :]
    cos, sin = jnp.cos(ang), jnp.sin(ang)
    reps = LANES // RET_DK
    cos_t = jnp.tile(jnp.concatenate([cos, cos], axis=1), (1, reps))
    sin_t = jnp.tile(jnp.concatenate([-sin, sin], axis=1), (1, reps))
    return cos_t, sin_t


def _ret_kernel(q_ref, k_ref, v_ref, g_ref, cos_ref, sin_ref, dm_ref, qd_ref, kd_ref, gn_ref,
                o_ref, s_scr):
    i = pl.program_id(0)
    R = q_ref.shape[0]

    @pl.when(i == 0)
    def _():
        s_scr[...] = jnp.zeros_like(s_scr)

    lane = lax.broadcasted_iota(jnp.int32, (R, LANES), 1)
    first_half = (lane % RET_DK) < (RET_DK // 2)
    low_head = lane < RET_DK
    cosv = cos_ref[...]
    sinv = sin_ref[...]
    c_dec = [g ** R for g in _ret_gammas()]

    def rot(t):
        swapped = jnp.where(first_half, pltpu.roll(t, LANES - RET_DK // 2, axis=1),
                            pltpu.roll(t, RET_DK // 2, axis=1))
        return t * cosv + swapped * sinv

    heads_per_vreg = LANES // RET_DK
    for p in range(RET_HEADS // heads_per_vreg):
        cols = slice(p * LANES, (p + 1) * LANES)
        qr = rot(q_ref[:, cols].astype(F32))
        kb = (rot(k_ref[:, cols].astype(F32)) * (RET_DK ** -0.5)).astype(BF16)
        for hh in range(heads_per_vreg):
            h = p * heads_per_vreg + hh
            hc = slice(h * RET_DV, (h + 1) * RET_DV)
            mask = low_head if hh == 0 else jnp.logical_not(low_head)
            qm = jnp.where(mask, qr, 0.0).astype(BF16)
            s = lax.dot_general(qm, kb, (((1,), (1,)), ((), ())), preferred_element_type=F32)
            s = s * dm_ref[h]
            v = v_ref[:, hc]
            o = jnp.dot(s.astype(BF16), v, preferred_element_type=F32)
            state = s_scr[h]
            o = o + jnp.dot(qm, state.astype(BF16), preferred_element_type=F32) * qd_ref[h]
            vd = (v.astype(F32) * kd_ref[h]).astype(BF16)
            kv = lax.dot_general(kb, vd, (((0,), (0,)), ((), ())), preferred_element_type=F32)
            s_scr[h] = state * c_dec[h] + kv
            mu = jnp.mean(o, axis=-1, keepdims=True)
            d = o - mu
            var = jnp.mean(d * d, axis=-1, keepdims=True)
            on = d * lax.rsqrt(var + EPS) * gn_ref[:, hc]
            gate = g_ref[:, hc].astype(F32)
            o_ref[:, hc] = (gate * jax.nn.sigmoid(gate) * on).astype(BF16)


def _retention(proj, cos_t, sin_t, gn_g, col0):
    T = proj.shape[0]
    R = min(RET_ROWS, T)
    QK = RET_HEADS * RET_DK
    V = RET_HEADS * RET_DV
    dm, qd, kd = _ret_tables(R)
    q_blk = col0 // QK
    v_blk = (col0 + 2 * QK) // V
    full3 = lambda a: pl.BlockSpec(a.shape, lambda i: (0, 0, 0))
    return pl.pallas_call(
        _ret_kernel,
        grid=(T // R,),
        in_specs=[
            pl.BlockSpec((R, QK), lambda i: (i, q_blk)),
            pl.BlockSpec((R, QK), lambda i: (i, q_blk + 1)),
            pl.BlockSpec((R, V), lambda i: (i, v_blk)),
            pl.BlockSpec((R, V), lambda i: (i, v_blk + 1)),
            pl.BlockSpec((R, LANES), lambda i: (i, 0)),
            pl.BlockSpec((R, LANES), lambda i: (i, 0)),
            full3(dm), full3(qd), full3(kd),
            pl.BlockSpec((1, V), lambda i: (0, 0)),
        ],
        out_specs=pl.BlockSpec((R, V), lambda i: (i, 0)),
        out_shape=jax.ShapeDtypeStruct((T, V), BF16),
        scratch_shapes=[pltpu.VMEM((RET_HEADS, LANES, RET_DV), F32)],
        compiler_params=_cparams(("arbitrary",)),
        name="retention",
    )(proj, proj, proj, proj, cos_t, sin_t, dm, qd, kd, gn_g)


def _fox_kernel(qt_ref, k_ref, vt_ref, cum_ref, o_ref, ck_scr, cend_scr, kn_scr,
                s0_scr, s1_scr, p0_scr, p1_scr, a0_scr, a1_scr, m_scr, l_scr, acc_scr):
    h = pl.program_id(0)
    qi = pl.program_id(1)
    tq = qt_ref.shape[1]
    tk = s0_scr.shape[0]
    T = k_ref.shape[0]
    assert tq == 2 * tk
    reps = tq // LANES
    s_scr, p_scr, a_scr = (s0_scr, s1_scr), (p0_scr, p1_scr), (a0_scr, a1_scr)

    @pl.when(qi == 0)
    def _():
        cend_scr[...] = jnp.zeros_like(cend_scr)
        ones = jnp.ones((FOX_DH, LANES), BF16)
        kn2 = jnp.zeros((SUBLANES, LANES), F32)
        for r in range(0, T, tq):
            kf = k_ref[r:r + tq, :].astype(F32)
            n2 = jnp.dot((kf * kf).astype(BF16), ones, preferred_element_type=F32)
            kn2 = jnp.maximum(kn2, jnp.max(n2.reshape(tq // SUBLANES, SUBLANES, LANES), axis=0))
        kn_scr[...] = jnp.max(kn2, axis=0, keepdims=True)

    lane = lax.broadcasted_iota(jnp.int32, cum_ref.shape, 1)
    col = jnp.sum(jnp.where(lane == h, cum_ref[...], 0.0), axis=1, keepdims=True) * LOG2E
    q0 = pl.multiple_of(qi * tq, tq)
    ck_rep = jnp.broadcast_to(col, cum_ref.shape)
    ck_scr[pl.ds(q0, tq), :] = ck_rep
    cend_scr[pl.ds(2 * qi, 1), :] = ck_rep[tk - 1:tk, :]
    cend_scr[pl.ds(2 * qi + 1, 1), :] = ck_rep[tq - 1:tq, :]

    qf = qt_ref[...].astype(F32)
    qn2 = jnp.max(jnp.sum(qf * qf, axis=0, keepdims=True), axis=1, keepdims=True)
    reach = 2.0 * FOX_NORM_SLACK * jnp.sqrt(qn2 * kn_scr[...]) + FOX_SKIP_LOG2
    decay = cend_scr[...] - ck_rep[0:1, :]
    chunk_id = lax.broadcasted_iota(jnp.int32, decay.shape, 0)
    dead = jnp.logical_and(decay >= reach, chunk_id < 2 * qi)
    n_dead = jnp.max(jnp.sum(dead.astype(jnp.int32), axis=0, keepdims=True))
    first_pair = n_dead // 2

    m_scr[...] = jnp.full_like(m_scr, -jnp.inf)
    l_scr[...] = jnp.zeros_like(l_scr)
    acc_scr[...] = jnp.zeros_like(acc_scr)
    p1_scr[...] = jnp.zeros_like(p1_scr)
    a1_scr[...] = jnp.ones_like(a1_scr)

    def score(c, slot):
        k0 = pl.multiple_of(c * tk, tk)
        s = jnp.dot(k_ref[pl.ds(k0, tk), :], qt_ref[...], preferred_element_type=F32)
        bias = ck_scr[pl.ds(k0, tk), :]
        s_scr[slot][...] = s - jnp.concatenate([bias] * reps, axis=1)

    def softmax(c, slot, masked):
        if masked:
            s = s_scr[slot][...]
            kpos = c * tk + lax.broadcasted_iota(jnp.int32, s.shape, 0)
            qpos = q0 + lax.broadcasted_iota(jnp.int32, s.shape, 1)
            s_scr[slot][...] = jnp.where(kpos <= qpos, s, -jnp.inf)
        sref, pref = s_scr[slot], p_scr[slot]
        mx = sref[0:FOX_SLAB, :]
        for r in range(FOX_SLAB, tk, FOX_SLAB):
            mx = jnp.maximum(mx, sref[r:r + FOX_SLAB, :])
        m_old = m_scr[...]
        m_new = jnp.maximum(m_old, jnp.max(mx, axis=0, keepdims=True))
        alpha = jnp.exp2(m_old - m_new)
        m_rows = jnp.broadcast_to(m_new, (FOX_SLAB, tq))
        psum = jnp.zeros((FOX_SLAB, tq), F32)
        for r in range(0, tk, FOX_SLAB):
            p = jnp.exp2(sref[r:r + FOX_SLAB, :] - m_rows)
            psum = psum + p
            pref[r:r + FOX_SLAB, :] = p.astype(BF16)
        l_scr[...] = alpha * l_scr[...] + jnp.sum(psum, axis=0, keepdims=True)
        a_scr[slot][...] = alpha
        m_scr[...] = m_new

    def accumulate(c, slot):
        k0 = pl.multiple_of(jnp.maximum(c, 0) * tk, tk)
        pv = jnp.dot(vt_ref[:, pl.ds(k0, tk)], p_scr[slot][...], preferred_element_type=F32)
        acc_scr[...] = a_scr[slot][...] * acc_scr[...] + pv

    score(2 * first_pair, 0)

    def pair(i, carry):
        c = 2 * i
        score(c + 1, 1)
        accumulate(c - 1, 1)
        softmax(c, 0, False)
        score(c + 2, 0)
        accumulate(c, 0)
        softmax(c + 1, 1, False)
        return carry

    lax.fori_loop(first_pair, qi, pair, 0)
    c = 2 * qi
    score(c + 1, 1)
    accumulate(c - 1, 1)
    softmax(c, 0, True)
    accumulate(c, 0)
    softmax(c + 1, 1, True)
    accumulate(c + 1, 1)
    o_ref[...] = (acc_scr[...] / l_scr[...]).T.astype(BF16)


def _fox(proj, proj_t, cum, k_col0):
    T = proj.shape[0]
    tq = min(FOX_TQ, T)
    tk = tq // 2
    W = FOX_HEADS * FOX_DH
    kb = k_col0 // FOX_DH
    return pl.pallas_call(
        _fox_kernel,
        grid=(FOX_HEADS, T // tq),
        in_specs=[
            pl.BlockSpec((FOX_DH, tq), lambda h, i: (h, i)),
            pl.BlockSpec((T, FOX_DH), lambda h, i: (0, kb + h)),
            pl.BlockSpec((FOX_DH, T), lambda h, i: (FOX_HEADS + h, 0)),
            pl.BlockSpec((tq, LANES), lambda h, i: (i, 0)),
        ],
        out_specs=pl.BlockSpec((tq, FOX_DH), lambda h, i: (i, h)),
        out_shape=jax.ShapeDtypeStruct((T, W), BF16),
        scratch_shapes=[pltpu.VMEM((T, LANES), F32),
                        pltpu.VMEM((-(-(T // tk) // SUBLANES) * SUBLANES, LANES), F32),
                        pltpu.VMEM((1, LANES), F32),
                        pltpu.VMEM((tk, tq), F32), pltpu.VMEM((tk, tq), F32),
                        pltpu.VMEM((tk, tq), BF16), pltpu.VMEM((tk, tq), BF16),
                        pltpu.VMEM((1, tq), F32), pltpu.VMEM((1, tq), F32),
                        pltpu.VMEM((1, tq), F32), pltpu.VMEM((1, tq), F32),
                        pltpu.VMEM((FOX_DH, tq), F32)],
        compiler_params=_cparams(("arbitrary", "arbitrary")),
        name="fox",
    )(proj_t, proj, proj_t, cum)


def _outproj_kernel(yc_ref, yr_ref, yf_ref, wc_ref, wr_ref, wf_ref, x_ref, g1_ref,
                    g_ref, sh_ref, sc_ref, rwh_ref, rwl_ref, xo_ref, h_ref, lg_ref):
    mix = jnp.dot(yc_ref[...], wc_ref[...], preferred_element_type=F32)
    mix = mix + jnp.dot(yr_ref[...], wr_ref[...], preferred_element_type=F32)
    mix = mix + jnp.dot(yf_ref[...], wf_ref[...], preferred_element_type=F32)
    x = x_ref[...] + g1_ref[...] * mix
    xo_ref[...] = x
    h = _modulated_norm(x, g_ref[...], sh_ref[...], sc_ref[...])
    h_ref[...] = _pack_halves(h)
    h_hi = h.astype(BF16)
    h_lo = (h - h_hi.astype(F32)).astype(BF16)
    nt = lambda a, b: lax.dot_general(a, b, (((1,), (1,)), ((), ())), preferred_element_type=F32)
    lg_ref[...] = nt(rwh_ref[...], h_hi) + (nt(rwh_ref[...], h_lo) + nt(rwl_ref[...], h_hi))


def _outproj(yc, yr, yf, wc, wr, wf, x, g1, g, sh, sc, rw_t):
    T, D = x.shape
    tm = min(OUT_TM, T)
    E = rw_t.shape[0]
    rw_hi = rw_t.astype(BF16)
    rw_lo = (rw_t - rw_hi.astype(F32)).astype(BF16)
    vec = pl.BlockSpec((1, D), lambda i: (0, 0))
    rows = lambda a: pl.BlockSpec((tm, a.shape[1]), lambda i: (i, 0))
    whole = lambda a: pl.BlockSpec(a.shape, lambda i: (0, 0))
    return pl.pallas_call(
        _outproj_kernel,
        grid=(T // tm,),
        in_specs=[rows(yc), rows(yr), rows(yf), whole(wc), whole(wr), whole(wf), rows(x),
                  vec, vec, vec, vec, whole(rw_hi), whole(rw_lo)],
        out_specs=[rows(x), pl.BlockSpec((tm, D // 2), lambda i: (i, 0)),
                   pl.BlockSpec((E, tm), lambda i: (0, i))],
        out_shape=[jax.ShapeDtypeStruct((T, D), F32), jax.ShapeDtypeStruct((T, D // 2), U32),
                   jax.ShapeDtypeStruct((E, T), F32)],
        compiler_params=_cparams(("arbitrary",)),
        name="outproj",
    )(yc, yr, yf, wc, wr, wf, x, g1, g, sh, sc, rw_hi, rw_lo)


def _route_kernel(lg_ref, b_ref, tri_ref, e_ref, r_ref, w_ref, cnt_ref, carry_scr):
    i = pl.program_id(0)
    E, tt = lg_ref.shape

    @pl.when(i == 0)
    def _():
        carry_scr[...] = jnp.zeros_like(carry_scr)

    aff = jax.nn.sigmoid(lg_ref[...])
    sel = aff + b_ref[...]
    row8 = lax.broadcasted_iota(jnp.int32, (GROUP_SIZE, tt), 0)
    best = None
    for g in range(N_GROUPS):
        slab = sel[g * GROUP_SIZE:(g + 1) * GROUP_SIZE, :]
        m1 = jnp.max(slab, axis=0, keepdims=True)
        i1 = jnp.min(jnp.where(slab == m1, row8, GROUP_SIZE), axis=0, keepdims=True)
        rest = jnp.where(row8 == i1, -jnp.inf, slab)
        m2 = jnp.max(rest, axis=0, keepdims=True)
        i2 = jnp.min(jnp.where(rest == m2, row8, GROUP_SIZE), axis=0, keepdims=True)
        cand = (m1 + m2, g * GROUP_SIZE + i1, g * GROUP_SIZE + i2)
        if best is None:
            best = cand
        else:
            upd = cand[0] > best[0]
            best = tuple(jnp.where(upd, n, o) for n, o in zip(cand, best))
    _, e0, e1 = best

    row = lax.broadcasted_iota(jnp.int32, (E, tt), 0)
    oh0 = row == e0
    oh1 = row == e1
    a0 = jnp.sum(jnp.where(oh0, aff, 0.0), axis=0, keepdims=True)
    a1 = jnp.sum(jnp.where(oh1, aff, 0.0), axis=0, keepdims=True)
    w_ref[0:1, :] = a0 / (a0 + a1)
    w_ref[1:2, :] = a1 / (a0 + a1)
    e_ref[0:1, :] = e0
    e_ref[1:2, :] = e1

    oh = jnp.logical_or(oh0, oh1)
    ohf = jnp.where(oh, 1.0, 0.0)
    before = jnp.dot(ohf.astype(BF16), tri_ref[...], preferred_element_type=F32) + carry_scr[:, 0:1]
    r_ref[0:1, :] = jnp.sum(jnp.where(oh0, before, 0.0), axis=0, keepdims=True).astype(jnp.int32)
    r_ref[1:2, :] = jnp.sum(jnp.where(oh1, before, 0.0), axis=0, keepdims=True).astype(jnp.int32)
    carry = carry_scr[...] + jnp.sum(ohf, axis=1, keepdims=True)
    carry_scr[...] = carry
    cnt_ref[...] = carry


def _route(logits_t, router_b):
    E, T = logits_t.shape
    tt = min(ROUTE_TT, T)
    tri = jnp.asarray(np.triu(np.ones((tt, tt), np.float32), k=1), dtype=BF16)
    pair = lambda dt: jax.ShapeDtypeStruct((TOP_K, T), dt)
    return pl.pallas_call(
        _route_kernel,
        grid=(T // tt,),
        in_specs=[
            pl.BlockSpec((E, tt), lambda i: (0, i)),
            pl.BlockSpec((E, 1), lambda i: (0, 0)),
            pl.BlockSpec((tt, tt), lambda i: (0, 0)),
        ],
        out_specs=[
            pl.BlockSpec((TOP_K, tt), lambda i: (0, i)),
            pl.BlockSpec((TOP_K, tt), lambda i: (0, i)),
            pl.BlockSpec((TOP_K, tt), lambda i: (0, i)),
            pl.BlockSpec((E, LANES), lambda i: (0, 0)),
        ],
        out_shape=[pair(jnp.int32), pair(jnp.int32), pair(F32), jax.ShapeDtypeStruct((E, LANES), F32)],
        scratch_shapes=[pltpu.VMEM((E, LANES), F32)],
        compiler_params=_cparams(("arbitrary",)),
        name="route",
    )(logits_t, router_b.reshape(E, 1), tri)


def _block_tables(counts, T):
    E = counts.shape[0]
    A = T * TOP_K
    B = MOE_ROWS
    seg = (counts + SUBLANES - 1) // SUBLANES * SUBLANES
    seg_end = jnp.cumsum(seg)
    seg_start = seg_end - seg
    nb = (A + E * (B - 1) + B - 1) // B
    nblk = (counts + B - 1) // B
    blk_end = jnp.cumsum(nblk)
    total = blk_end[-1]
    b = jnp.arange(nb, dtype=jnp.int32)
    bc = jnp.minimum(b, total - 1)
    blk_e = jnp.minimum(jnp.searchsorted(blk_end, bc, side="right"), E - 1).astype(jnp.int32)
    local = bc - (blk_end[blk_e] - nblk[blk_e])
    blk_start = (seg_start[blk_e] + local * B).astype(jnp.int32)
    blk_first = jnp.logical_and(b < total, local == 0).astype(jnp.int32)
    nonempty = counts > 0
    order = jnp.cumsum(nonempty.astype(jnp.int32)) - 1
    ids = jnp.where(nonempty, jnp.arange(E, dtype=jnp.int32), E)
    later = jnp.concatenate([lax.cummin(ids, reverse=True)[1:], jnp.full((1,), E, jnp.int32)])
    next_e = jnp.where(later < E, later, -1)
    blk_wslot = (order[blk_e] % 2).astype(jnp.int32)
    blk_next_e = next_e[blk_e].astype(jnp.int32)
    rows = A + E * (SUBLANES - 1) + B
    rows = (rows + SUBLANES - 1) // SUBLANES * SUBLANES
    blocks = (blk_e, blk_start, blk_first, blk_wslot, blk_next_e, total.astype(jnp.int32).reshape(1))
    return seg_start.astype(jnp.int32), seg_end.astype(jnp.int32), blocks, nb, rows


def _zero_rows_from(zero_ref, hbm_ref, start, sem):
    piece = zero_ref.shape[0]
    rows = hbm_ref.shape[0]

    def body(j, carry):
        at = pl.multiple_of(jnp.minimum(start + j * piece, rows - piece), SUBLANES)
        cp = pltpu.make_async_copy(zero_ref, hbm_ref.at[pl.ds(at, piece), :], sem)
        cp.start()
        cp.wait()
        return carry

    lax.fori_loop(0, (rows - start + piece - 1) // piece, body, 0)


def _dispatch_kernel(seg_end_ref, pos_ref, h_ref, xs_hbm, zero_scr, sem, zsem):
    i = pl.program_id(0)
    td = pos_ref.shape[1]
    E = seg_end_ref.shape[0]

    @pl.when(i == 0)
    def _():
        zero_scr[...] = jnp.zeros_like(zero_scr)

        def tail(e, carry):
            end = seg_end_ref[e]
            at = pl.multiple_of(jnp.maximum(end - SUBLANES, 0), SUBLANES)
            cp = pltpu.make_async_copy(zero_scr.at[pl.ds(0, SUBLANES), :],
                                       xs_hbm.at[pl.ds(at, SUBLANES), :], zsem)
            cp.start()
            cp.wait()
            return carry

        lax.fori_loop(0, E, tail, 0)
        _zero_rows_from(zero_scr, xs_hbm, seg_end_ref[E - 1], zsem)

    def body(r, carry):
        for k in range(TOP_K):
            pltpu.make_async_copy(h_ref.at[pl.ds(r, 1), :],
                                  xs_hbm.at[pl.ds(pos_ref[k, r], 1), :], sem).start()
        return carry

    lax.fori_loop(0, td, body, 0, unroll=8)
    for k in range(TOP_K):
        pltpu.make_async_copy(h_ref, xs_hbm.at[pl.ds(0, td), :], sem).wait()


def _dispatch(h2, pos, seg_end, rows):
    T, D = h2.shape
    td = min(DISPATCH_TT, T)
    grid_spec = pltpu.PrefetchScalarGridSpec(
        num_scalar_prefetch=1,
        grid=(T // td,),
        in_specs=[pl.BlockSpec((TOP_K, td), lambda i, se: (0, i), memory_space=pltpu.SMEM),
                  pl.BlockSpec((td, D), lambda i, se: (i, 0))],
        out_specs=pl.BlockSpec(memory_space=pl.ANY),
        scratch_shapes=[pltpu.VMEM((MOE_ROWS, D), h2.dtype), pltpu.SemaphoreType.DMA(()),
                        pltpu.SemaphoreType.DMA(())],
    )
    return pl.pallas_call(
        _dispatch_kernel,
        grid_spec=grid_spec,
        out_shape=jax.ShapeDtypeStruct((rows, D), h2.dtype),
        compiler_params=_cparams(("arbitrary",), disable_bounds_checks=True),
        name="dispatch",
    )(seg_end, pos, h2)


def _experts_kernel(blk_e_ref, blk_start_ref, blk_first_ref, blk_wslot_ref, blk_next_ref, total_ref,
                    tail_ref, xs_hbm, wg_hbm, wu_hbm, wd_hbm, ys_hbm,
                    wg_f, wu_f, wd_f, wg_b, wu_b, wd_b, xbuf, ybuf, isem, osem, wsem, *, layer):
    b = pl.program_id(0)
    total = total_ref[0]
    B = xbuf.shape[1]
    slot = b % 2

    def load(blk, to_slot):
        at = pl.multiple_of(blk_start_ref[blk], SUBLANES)
        return pltpu.make_async_copy(xs_hbm.at[pl.ds(at, B), :], xbuf.at[to_slot], isem.at[to_slot])

    def store(blk, from_slot):
        at = pl.multiple_of(blk_start_ref[blk], SUBLANES)
        return pltpu.make_async_copy(ybuf.at[from_slot], ys_hbm.at[pl.ds(at, B), :], osem)

    def weights(e, ws):
        return [pltpu.make_async_copy(src.at[layer, e], dst.at[ws], wsem.at[ws])
                for src, dst in ((wg_hbm, wg_f), (wu_hbm, wu_f), (wd_hbm, wd_f))]

    @pl.when(b == 0)
    def _():
        load(0, 0).start()
        for cp in weights(blk_e_ref[0], blk_wslot_ref[0]):
            cp.start()
        ybuf[1] = jnp.zeros(ybuf.shape[1:], ybuf.dtype)
        _zero_rows_from(ybuf.at[1], ys_hbm, tail_ref[0], osem)

    @pl.when(b + 1 < total)
    def _():
        load(b + 1, 1 - slot).start()

    @pl.when(b < total)
    def _():
        @pl.when(blk_first_ref[b] == 1)
        def _():
            ws = blk_wslot_ref[b]
            for cp in weights(blk_e_ref[b], ws):
                cp.wait()
            wg_b[...] = wg_f[ws].astype(BF16)
            wu_b[...] = wu_f[ws].astype(BF16)
            wd_b[...] = wd_f[ws].astype(BF16)
            nxt = blk_next_ref[b]

            @pl.when(nxt >= 0)
            def _():
                for cp in weights(nxt, 1 - ws):
                    cp.start()

        load(b, slot).wait()
        lo, hi = _unpack_halves(xbuf[slot])
        lo, hi = lo.astype(BF16), hi.astype(BF16)
        half = lo.shape[1]
        g = (jnp.dot(lo, wg_b[:half, :], preferred_element_type=F32)
             + jnp.dot(hi, wg_b[half:, :], preferred_element_type=F32))
        u = (jnp.dot(lo, wu_b[:half, :], preferred_element_type=F32)
             + jnp.dot(hi, wu_b[half:, :], preferred_element_type=F32))
        a = (g * jax.nn.sigmoid(g) * u).astype(BF16)
        ybuf[slot] = _pack_halves(jnp.dot(a, wd_b[...], preferred_element_type=F32))

        @pl.when(b > 0)
        def _():
            store(b - 1, 1 - slot).wait()

        store(b, slot).start()

        @pl.when(b == total - 1)
        def _():
            store(b, slot).wait()


def _experts(xs, layer, w_gate, w_up, w_down, blocks, tail, nb):
    rows, DP = xs.shape
    D, DE = w_gate.shape[-2:]
    B = MOE_ROWS
    anywhere = pl.BlockSpec(memory_space=pl.ANY)
    grid_spec = pltpu.PrefetchScalarGridSpec(
        num_scalar_prefetch=len(blocks) + 1,
        grid=(nb,),
        in_specs=[anywhere, anywhere, anywhere, anywhere],
        out_specs=anywhere,
        scratch_shapes=[
            pltpu.VMEM((2, D, DE), F32), pltpu.VMEM((2, D, DE), F32), pltpu.VMEM((2, DE, D), F32),
            pltpu.VMEM((D, DE), BF16), pltpu.VMEM((D, DE), BF16), pltpu.VMEM((DE, D), BF16),
            pltpu.VMEM((2, B, DP), U32), pltpu.VMEM((2, B, DP), U32),
            pltpu.SemaphoreType.DMA((2,)), pltpu.SemaphoreType.DMA(()), pltpu.SemaphoreType.DMA((2,)),
        ],
    )
    return pl.pallas_call(
        functools.partial(_experts_kernel, layer=layer),
        grid_spec=grid_spec,
        out_shape=jax.ShapeDtypeStruct((rows, DP), U32),
        compiler_params=_cparams(("arbitrary",)),
        name="experts",
    )(*blocks, tail, xs, w_gate, w_up, w_down)


def _combine_kernel(pos_ref, posn_ref, x_ref, ys_hbm, w_ref, g2_ref, fg_ref, o_ref, gbuf, sem,
                    *, final):
    i = pl.program_id(0)
    n = pl.num_programs(0)
    tm = x_ref.shape[0]
    slot = i % 2

    def gather(p_ref, to_slot):
        def body(r, carry):
            for k in range(TOP_K):
                pltpu.make_async_copy(ys_hbm.at[pl.ds(p_ref[k, r], 1), :],
                                      gbuf.at[to_slot, k, pl.ds(r, 1), :], sem.at[to_slot]).start()
            return carry

        lax.fori_loop(0, tm, body, 0, unroll=8)

    @pl.when(i == 0)
    def _():
        gather(pos_ref, 0)

    @pl.when(i + 1 < n)
    def _():
        gather(posn_ref, 1 - slot)

    for k in range(TOP_K):
        pltpu.make_async_copy(ys_hbm.at[pl.ds(0, tm), :], gbuf.at[slot, k], sem.at[slot]).wait()

    w = w_ref[...]
    D = x_ref.shape[1]
    half = D // 2
    lo0, hi0 = _unpack_halves(gbuf[slot, 0])
    lo1, hi1 = _unpack_halves(gbuf[slot, 1])
    xl = x_ref[:, :half] + g2_ref[:, :half] * (lo0 * w[:, 0:1] + lo1 * w[:, 1:2])
    xh = x_ref[:, half:] + g2_ref[:, half:] * (hi0 * w[:, 0:1] + hi1 * w[:, 1:2])
    if final:
        ms = (jnp.sum(xl * xl, axis=-1, keepdims=True) + jnp.sum(xh * xh, axis=-1, keepdims=True)) / D
        r = lax.rsqrt(ms + EPS)
        xl = xl * r * fg_ref[:, :half]
        xh = xh * r * fg_ref[:, half:]
    o_ref[:, :half] = xl
    o_ref[:, half:] = xh


def _combine(x, ys, pos, top_w, g2, final_g, final):
    T, D = x.shape
    tm = min(COMB_TM, T)
    n = T // tm
    vec = pl.BlockSpec((1, D), lambda i: (0, 0))
    return pl.pallas_call(
        functools.partial(_combine_kernel, final=final),
        grid=(n,),
        in_specs=[
            pl.BlockSpec((TOP_K, tm), lambda i: (0, i), memory_space=pltpu.SMEM),
            pl.BlockSpec((TOP_K, tm), lambda i: (0, jnp.minimum(i + 1, n - 1)), memory_space=pltpu.SMEM),
            pl.BlockSpec((tm, D), lambda i: (i, 0)),
            pl.BlockSpec(memory_space=pl.ANY),
            pl.BlockSpec((tm, TOP_K), lambda i: (i, 0)),
            vec, vec,
        ],
        out_specs=pl.BlockSpec((tm, D), lambda i: (i, 0)),
        out_shape=jax.ShapeDtypeStruct((T, D), F32),
        scratch_shapes=[pltpu.VMEM((2, TOP_K, tm, ys.shape[1]), ys.dtype), pltpu.SemaphoreType.DMA((2,))],
        compiler_params=_cparams(("arbitrary",), disable_bounds_checks=True),
        name="combine",
    )(pos, pos, x, ys, top_w, g2, final_g)


def kernel(x, c, ada_w, ada_b, norm1_g, norm2_g, w_in, conv_w, ret_gn_g, fox_fb, w_out,
           router_w, router_b, moe_w_gate, moe_w_up, moe_w_down, final_g):
    Bsz, S, D = x.shape
    assert Bsz == 1, "one sequence per call"
    L = ada_w.shape[0]
    T = S
    conv_wd = conv_w.shape[-1]
    ret_qk = RET_HEADS * RET_DK
    ret_v = RET_HEADS * RET_DV
    fox_w = FOX_HEADS * FOX_DH
    tn = INPROJ_TN
    ret_col0 = 3 * conv_wd
    fox_col0 = ret_col0 + 2 * ret_qk + 2 * ret_v
    n_main = fox_col0 + 3 * fox_w
    assert fox_w == tn and fox_col0 % tn == 0
    q_block = fox_col0 // tn
    v_block = q_block + 2
    k_col0 = fox_col0

    mod = _ada_mod(c, ada_w, ada_b)
    cos_t, sin_t = _rope_tables(T)
    rw_t = router_w.T
    xt = x.reshape(T, D)

    for l in range(L):
        sh1, sc1, g1, sh2, sc2, g2 = [mod[l, :, k * D:(k + 1) * D] for k in range(6)]
        w_ff = jnp.pad(w_in[l, :, n_main:], ((0, 0), (0, LANES - FOX_HEADS)))
        fb = jnp.pad(fox_fb[l], (0, LANES - FOX_HEADS)).reshape(1, LANES)
        proj, proj_t, cum = _inproj(xt, norm1_g[l].reshape(1, D), sh1, sc1, w_in, l, w_ff, fb,
                                    q_block, v_block, LOG2E * FOX_DH ** -0.5)

        y_conv = _conv(proj, conv_w[l])
        y_ret = _retention(proj, cos_t, sin_t, ret_gn_g[l].reshape(1, ret_v), ret_col0)
        y_fox = _fox(proj, proj_t, cum, k_col0)

        wo = w_out[l].astype(BF16)
        xt, h2, logits_t = _outproj(
            y_conv, y_ret, y_fox, wo[:conv_wd], wo[conv_wd:conv_wd + ret_v], wo[conv_wd + ret_v:],
            xt, g1, norm2_g[l].reshape(1, D), sh2, sc2, rw_t)

        top_e, rank, top_w, cnt = _route(logits_t, router_b)
        counts = cnt[:, 0].astype(jnp.int32)
        seg_start, seg_end, blocks, nb, rows = _block_tables(counts, T)
        hit = top_e[None] == jnp.arange(N_EXPERTS, dtype=jnp.int32)[:, None, None]
        pos = rank + jnp.sum(jnp.where(hit, seg_start[:, None, None], 0), axis=0)
        xs = _dispatch(h2, pos, seg_end, rows)
        ys = _experts(xs, l, moe_w_gate, moe_w_up, moe_w_down, blocks, seg_end[-1:], nb)
        xt = _combine(xt, ys, pos, top_w.T, g2, final_g.reshape(1, D), final=(l == L - 1))

    return xt.reshape(Bsz, S, D)
```

```python
import functools
import math

import numpy as np
import jax
import jax.numpy as jnp
from jax import lax
from jax.experimental import pallas as pl
from jax.experimental.pallas import tpu as pltpu

F32 = jnp.float32
BF16 = jnp.bfloat16

CHUNK = 64
CONV_K = 3
RET_HEADS = 6
RET_DK = 64
RET_DV = 128
FOX_HEADS = 6
FOX_DH = 128
N_EXPERTS = 64
N_GROUPS = 8
GROUP_SIZE = N_EXPERTS // N_GROUPS
TOP_K = 2
ROPE_BASE = 10000.0
EPS = 1e-6
LOG2E = 1.4426950408889634

LANES = 128
SUBLANES = 8

MOD_TN = 1024
INPROJ_TM = 1024
INPROJ_TN = 768
CONV_TM = 1024
CONV_HALO = 16
RET_ROWS = 256
FOX_TQ = 1024
FOX_SLAB = 16
FOX_SKIP_LOG2 = 160.0
FOX_NORM_SLACK = 1.02
OUT_TM = 512
MOE_ROWS = 256
ROUTE_TT = 1024
DISPATCH_TT = 512
COMB_TM = 512
VMEM_LIMIT = 56 * 1024 * 1024


def _cparams(sem, **kw):
    return pltpu.CompilerParams(dimension_semantics=sem, vmem_limit_bytes=VMEM_LIMIT, **kw)


U32 = jnp.uint32


def _pack_halves(x):
    half = x.shape[1] // 2
    xb = x.astype(BF16).astype(F32)
    lo = lax.bitcast_convert_type(xb[:, :half], U32)
    hi = lax.bitcast_convert_type(xb[:, half:], U32)
    return lax.shift_right_logical(lo, U32(16)) | (hi & U32(0xFFFF0000))


def _unpack_halves(w):
    lo = lax.bitcast_convert_type(lax.shift_left(w, U32(16)), F32)
    hi = lax.bitcast_convert_type(w & U32(0xFFFF0000), F32)
    return lo, hi


def _mod_kernel(c_ref, w_ref, b_ref, o_ref):
    c = c_ref[...]
    ca = c * jax.nn.sigmoid(c)
    o_ref[0] = jnp.sum(w_ref[0] * ca, axis=0, keepdims=True) + b_ref[0]


def _ada_mod(c, ada_w, ada_b):
    L, D, N = ada_w.shape
    tn = MOD_TN
    return pl.pallas_call(
        _mod_kernel,
        grid=(L, N // tn),
        in_specs=[
            pl.BlockSpec((D, 1), lambda l, j: (0, 0)),
            pl.BlockSpec((1, D, tn), lambda l, j: (l, 0, j)),
            pl.BlockSpec((1, 1, tn), lambda l, j: (l, 0, j)),
        ],
        out_specs=pl.BlockSpec((1, 1, tn), lambda l, j: (l, 0, j)),
        out_shape=jax.ShapeDtypeStruct((L, 1, N), F32),
        compiler_params=_cparams(("arbitrary", "arbitrary")),
        name="ada_mod",
    )(c.reshape(D, 1), ada_w, ada_b.reshape(L, 1, N))


def _modulated_norm(x, g, sh, sc):
    ms = jnp.mean(x * x, axis=-1, keepdims=True)
    y = x * lax.rsqrt(ms + EPS) * g
    return y * (1.0 + sc) + sh


def _log_sigmoid(x):
    return jnp.minimum(x, 0.0) - jnp.log1p(jnp.exp(-jnp.abs(x)))


def _inproj_kernel(x_ref, g_ref, sh_ref, sc_ref, w_ref, wff_ref, fb_ref,
                   proj_ref, projt_ref, cum_ref, h_scr, wt_scr, carry_scr,
                   *, q_block, v_block, q_scale):
    i = pl.program_id(0)
    j = pl.program_id(1)
    tm = x_ref.shape[0]

    @pl.when(jnp.logical_and(i == 0, j == 0))
    def _():
        carry_scr[...] = jnp.zeros_like(carry_scr)

    @pl.when(j == 0)
    def _():
        h = _modulated_norm(x_ref[...], g_ref[...], sh_ref[...], sc_ref[...])
        hb = h.astype(BF16)
        h_scr[...] = hb
        ff = jnp.dot(hb, wff_ref[...].astype(BF16), preferred_element_type=F32) + fb_ref[...]
        c = _log_sigmoid(ff)
        row = lax.broadcasted_iota(jnp.int32, c.shape, 0)
        d = 1
        while d < tm:
            c = c + jnp.where(row >= d, pltpu.roll(c, d, axis=0), 0.0)
            d *= 2
        c = c + carry_scr[...]
        cum_ref[...] = c
        carry_scr[...] = c[tm - 1:tm, :]

    @pl.when(jnp.logical_and(j != q_block, j != v_block))
    def _():
        proj_ref[...] = jnp.dot(h_scr[...], w_ref[...].astype(BF16),
                                preferred_element_type=F32).astype(BF16)

    def feature_major(slot, scale):
        @pl.when(i == 0)
        def _():
            wt_scr[slot] = w_ref[...].T.astype(BF16)

        r = lax.dot_general(wt_scr[slot], h_scr[...], (((1,), (1,)), ((), ())),
                            preferred_element_type=F32)
        projt_ref[...] = (r * scale).astype(BF16)

    @pl.when(j == q_block)
    def _():
        feature_major(0, q_scale)

    @pl.when(j == v_block)
    def _():
        feature_major(1, 1.0)


def _inproj(x, g, sh, sc, w_in, layer, w_ff, fb, q_block, v_block, q_scale):
    T, D = x.shape
    tm, tn = min(INPROJ_TM, T), INPROJ_TN
    nj = v_block + 1
    is_q = lambda j: (j == q_block).astype(jnp.int32)
    is_v = lambda j: (j == v_block).astype(jnp.int32)
    w_col = lambda i, j: jnp.where(i == 0, j, j - is_q(j) - is_v(j))
    out_col = lambda j: j - (j >= q_block).astype(jnp.int32) - (j >= v_block).astype(jnp.int32)
    vec = pl.BlockSpec((1, D), lambda i, j: (0, 0))
    return pl.pallas_call(
        functools.partial(_inproj_kernel, q_block=q_block, v_block=v_block, q_scale=q_scale),
        grid=(T // tm, nj),
        in_specs=[
            pl.BlockSpec((tm, D), lambda i, j: (i, 0), pipeline_mode=pl.Buffered(1)),
            vec, vec, vec,
            pl.BlockSpec((None, D, tn), lambda i, j: (layer, 0, w_col(i, j))),
            pl.BlockSpec((D, LANES), lambda i, j: (0, 0)),
            pl.BlockSpec((1, LANES), lambda i, j: (0, 0)),
        ],
        out_specs=[
            pl.BlockSpec((tm, tn), lambda i, j: (i, out_col(j))),
            pl.BlockSpec((tn, tm), lambda i, j: (is_v(j), i)),
            pl.BlockSpec((tm, LANES), lambda i, j: (i, 0)),
        ],
        out_shape=[
            jax.ShapeDtypeStruct((T, (nj - 2) * tn), BF16),
            jax.ShapeDtypeStruct((2 * tn, T), BF16),
            jax.ShapeDtypeStruct((T, LANES), F32),
        ],
        scratch_shapes=[pltpu.VMEM((tm, D), BF16), pltpu.VMEM((2, tn, D), BF16),
                        pltpu.VMEM((1, LANES), F32)],
        compiler_params=_cparams(("arbitrary", "arbitrary")),
        name="inproj",
    )(x, g, sh, sc, w_in, w_ff, fb)


def _conv_kernel(cb_ref, cc_ref, cu_ref, hc_ref, hu_ref, w_ref, o_ref):
    i = pl.program_id(0)
    z = cc_ref[...].astype(F32) * cu_ref[...].astype(F32)
    zh = hc_ref[...].astype(F32) * hu_ref[...].astype(F32)
    zh = jnp.where(i > 0, zh, 0.0)
    hl = zh.shape[0]
    zm1 = zh[hl - 1:hl, :]
    zm2 = zh[hl - 2:hl - 1, :]
    row = lax.broadcasted_iota(jnp.int32, z.shape, 0)
    z1 = jnp.where(row == 0, zm1, pltpu.roll(z, 1, axis=0))
    z2 = jnp.where(row == 0, zm2, jnp.where(row == 1, zm1, pltpu.roll(z, 2, axis=0)))
    w = w_ref[...]
    y = z2 * w[0:1, :] + z1 * w[1:2, :] + z * w[2:3, :]
    o_ref[...] = (cb_ref[...].astype(F32) * y).astype(BF16)


def _conv(proj, conv_w):
    T = proj.shape[0]
    W = conv_w.shape[1]
    tm = min(CONV_TM, T)
    hb = tm // CONV_HALO
    halo = lambda c: pl.BlockSpec((CONV_HALO, W), lambda i: (jnp.maximum(i * hb - 1, 0), c))
    return pl.pallas_call(
        _conv_kernel,
        grid=(T // tm,),
        in_specs=[
            pl.BlockSpec((tm, W), lambda i: (i, 0)),
            pl.BlockSpec((tm, W), lambda i: (i, 1)),
            pl.BlockSpec((tm, W), lambda i: (i, 2)),
            halo(1), halo(2),
            pl.BlockSpec((CONV_K, W), lambda i: (0, 0)),
        ],
        out_specs=pl.BlockSpec((tm, W), lambda i: (i, 0)),
        out_shape=jax.ShapeDtypeStruct((T, W), BF16),
        compiler_params=_cparams(("arbitrary",)),
        name="conv",
    )(proj, proj, proj, proj, proj, conv_w)


def _ret_gammas():
    return [1.0 - 2.0 ** (-5.0 - h) for h in range(RET_HEADS)]


def _ret_tables(R):
    n = np.arange(R, dtype=np.float64)
    chunk = np.arange(R) // CHUNK
    allowed = chunk[None, :] <= chunk[:, None]
    dm, qd, kd = [], [], []
    for g in _ret_gammas():
        lg = math.log(g)
        dm.append(np.where(allowed, np.exp(lg * np.abs(n[:, None] - n[None, :])), 0.0))
        qd.append(np.broadcast_to(np.exp(lg * (n + 1.0))[:, None], (R, LANES)))
        kd.append(np.broadcast_to(np.exp(lg * (R - 1.0 - n))[:, None], (R, LANES)))
    f = lambda a: jnp.asarray(np.stack(a), dtype=F32)
    return f(dm), f(qd), f(kd)


def _rope_tables(T):
    half = RET_DK // 2
    inv = ROPE_BASE ** (-jnp.arange(half, dtype=F32) / half)
    ang = jnp.arange(T, dtype=F32)[:, None] * inv[None, :]
    cos, sin = jnp.cos(ang), jnp.sin(ang)
    reps = LANES // RET_DK
    cos_t = jnp.tile(jnp.concatenate([cos, cos], axis=1), (1, reps))
    sin_t = jnp.tile(jnp.concatenate([-sin, sin], axis=1), (1, reps))
    return cos_t, sin_t


def _ret_kernel(q_ref, k_ref, v_ref, g_ref, cos_ref, sin_ref, dm_ref, qd_ref, kd_ref, gn_ref,
                o_ref, s_scr):
    i = pl.program_id(0)
    R = q_ref.shape[0]

    @pl.when(i == 0)
    def _():
        s_scr[...] = jnp.zeros_like(s_scr)

    lane = lax.broadcasted_iota(jnp.int32, (R, LANES), 1)
    first_half = (lane % RET_DK) < (RET_DK // 2)
    low_head = lane < RET_DK
    cosv = cos_ref[...]
    sinv = sin_ref[...]
    c_dec = [g ** R for g in _ret_gammas()]

    def rot(t):
        swapped = jnp.where(first_half, pltpu.roll(t, LANES - RET_DK // 2, axis=1),
                            pltpu.roll(t, RET_DK // 2, axis=1))
        return t * cosv + swapped * sinv

    heads_per_vreg = LANES // RET_DK
    for p in range(RET_HEADS // heads_per_vreg):
        cols = slice(p * LANES, (p + 1) * LANES)
        qr = rot(q_ref[:, cols].astype(F32))
        kb = (rot(k_ref[:, cols].astype(F32)) * (RET_DK ** -0.5)).astype(BF16)
        for hh in range(heads_per_vreg):
            h = p * heads_per_vreg + hh
            hc = slice(h * RET_DV, (h + 1) * RET_DV)
            mask = low_head if hh == 0 else jnp.logical_not(low_head)
            qm = jnp.where(mask, qr, 0.0).astype(BF16)
            s = lax.dot_general(qm, kb, (((1,), (1,)), ((), ())), preferred_element_type=F32)
            s = s * dm_ref[h]
            v = v_ref[:, hc]
            o = jnp.dot(s.astype(BF16), v, preferred_element_type=F32)
            state = s_scr[h]
            o = o + jnp.dot(qm, state.astype(BF16), preferred_element_type=F32) * qd_ref[h]
            vd = (v.astype(F32) * kd_ref[h]).astype(BF16)
            kv = lax.dot_general(kb, vd, (((0,), (0,)), ((), ())), preferred_element_type=F32)
            s_scr[h] = state * c_dec[h] + kv
            mu = jnp.mean(o, axis=-1, keepdims=True)
            d = o - mu
            var = jnp.mean(d * d, axis=-1, keepdims=True)
            on = d * lax.rsqrt(var + EPS) * gn_ref[:, hc]
            gate = g_ref[:, hc].astype(F32)
            o_ref[:, hc] = (gate * jax.nn.sigmoid(gate) * on).astype(BF16)


def _retention(proj, cos_t, sin_t, gn_g, col0):
    T = proj.shape[0]
    R = min(RET_ROWS, T)
    QK = RET_HEADS * RET_DK
    V = RET_HEADS * RET_DV
    dm, qd, kd = _ret_tables(R)
    q_blk = col0 // QK
    v_blk = (col0 + 2 * QK) // V
    full3 = lambda a: pl.BlockSpec(a.shape, lambda i: (0, 0, 0))
    return pl.pallas_call(
        _ret_kernel,
        grid=(T // R,),
        in_specs=[
            pl.BlockSpec((R, QK), lambda i: (i, q_blk)),
            pl.BlockSpec((R, QK), lambda i: (i, q_blk + 1)),
            pl.BlockSpec((R, V), lambda i: (i, v_blk)),
            pl.BlockSpec((R, V), lambda i: (i, v_blk + 1)),
            pl.BlockSpec((R, LANES), lambda i: (i, 0)),
            pl.BlockSpec((R, LANES), lambda i: (i, 0)),
            full3(dm), full3(qd), full3(kd),
            pl.BlockSpec((1, V), lambda i: (0, 0)),
        ],
        out_specs=pl.BlockSpec((R, V), lambda i: (i, 0)),
        out_shape=jax.ShapeDtypeStruct((T, V), BF16),
        scratch_shapes=[pltpu.VMEM((RET_HEADS, LANES, RET_DV), F32)],
        compiler_params=_cparams(("arbitrary",)),
        name="retention",
    )(proj, proj, proj, proj, cos_t, sin_t, dm, qd, kd, gn_g)


def _fox_kernel(qt_ref, k_ref, vt_ref, cum_ref, o_ref, ck_scr, cend_scr, kn_scr,
                s0_scr, s1_scr, p0_scr, p1_scr, a0_scr, a1_scr, m_scr, l_scr, acc_scr):
    h = pl.program_id(0)
    qi = pl.program_id(1)
    tq = qt_ref.shape[1]
    tk = s0_scr.shape[0]
    T = k_ref.shape[0]
    assert tq == 2 * tk
    reps = tq // LANES
    s_scr, p_scr, a_scr = (s0_scr, s1_scr), (p0_scr, p1_scr), (a0_scr, a1_scr)

    @pl.when(qi == 0)
    def _():
        cend_scr[...] = jnp.zeros_like(cend_scr)
        ones = jnp.ones((FOX_DH, LANES), BF16)
        kn2 = jnp.zeros((SUBLANES, LANES), F32)
        for r in range(0, T, tq):
            kf = k_ref[r:r + tq, :].astype(F32)
            n2 = jnp.dot((kf * kf).astype(BF16), ones, preferred_element_type=F32)
            kn2 = jnp.maximum(kn2, jnp.max(n2.reshape(tq // SUBLANES, SUBLANES, LANES), axis=0))
        kn_scr[...] = jnp.max(kn2, axis=0, keepdims=True)

    lane = lax.broadcasted_iota(jnp.int32, cum_ref.shape, 1)
    col = jnp.sum(jnp.where(lane == h, cum_ref[...], 0.0), axis=1, keepdims=True) * LOG2E
    q0 = pl.multiple_of(qi * tq, tq)
    ck_rep = jnp.broadcast_to(col, cum_ref.shape)
    ck_scr[pl.ds(q0, tq), :] = ck_rep
    cend_scr[pl.ds(2 * qi, 1), :] = ck_rep[tk - 1:tk, :]
    cend_scr[pl.ds(2 * qi + 1, 1), :] = ck_rep[tq - 1:tq, :]

    qf = qt_ref[...].astype(F32)
    qn2 = jnp.max(jnp.sum(qf * qf, axis=0, keepdims=True), axis=1, keepdims=True)
    reach = 2.0 * FOX_NORM_SLACK * jnp.sqrt(qn2 * kn_scr[...]) + FOX_SKIP_LOG2
    decay = cend_scr[...] - ck_rep[0:1, :]
    chunk_id = lax.broadcasted_iota(jnp.int32, decay.shape, 0)
    dead = jnp.logical_and(decay >= reach, chunk_id < 2 * qi)
    n_dead = jnp.max(jnp.sum(dead.astype(jnp.int32), axis=0, keepdims=True))
    first_pair = n_dead // 2

    m_scr[...] = jnp.full_like(m_scr, -jnp.inf)
    l_scr[...] = jnp.zeros_like(l_scr)
    acc_scr[...] = jnp.zeros_like(acc_scr)
    p1_scr[...] = jnp.zeros_like(p1_scr)
    a1_scr[...] = jnp.ones_like(a1_scr)

    def score(c, slot):
        k0 = pl.multiple_of(c * tk, tk)
        s = jnp.dot(k_ref[pl.ds(k0, tk), :], qt_ref[...], preferred_element_type=F32)
        bias = ck_scr[pl.ds(k0, tk), :]
        s_scr[slot][...] = s - jnp.concatenate([bias] * reps, axis=1)

    def softmax(c, slot, masked):
        if masked:
            s = s_scr[slot][...]
            kpos = c * tk + lax.broadcasted_iota(jnp.int32, s.shape, 0)
            qpos = q0 + lax.broadcasted_iota(jnp.int32, s.shape, 1)
            s_scr[slot][...] = jnp.where(kpos <= qpos, s, -jnp.inf)
        sref, pref = s_scr[slot], p_scr[slot]
        mx = sref[0:FOX_SLAB, :]
        for r in range(FOX_SLAB, tk, FOX_SLAB):
            mx = jnp.maximum(mx, sref[r:r + FOX_SLAB, :])
        m_old = m_scr[...]
        m_new = jnp.maximum(m_old, jnp.max(mx, axis=0, keepdims=True))
        alpha = jnp.exp2(m_old - m_new)
        m_rows = jnp.broadcast_to(m_new, (FOX_SLAB, tq))
        psum = jnp.zeros((FOX_SLAB, tq), F32)
        for r in range(0, tk, FOX_SLAB):
            p = jnp.exp2(sref[r:r + FOX_SLAB, :] - m_rows)
            psum = psum + p
            pref[r:r + FOX_SLAB, :] = p.astype(BF16)
        l_scr[...] = alpha * l_scr[...] + jnp.sum(psum, axis=0, keepdims=True)
        a_scr[slot][...] = alpha
        m_scr[...] = m_new

    def accumulate(c, slot):
        k0 = pl.multiple_of(jnp.maximum(c, 0) * tk, tk)
        pv = jnp.dot(vt_ref[:, pl.ds(k0, tk)], p_scr[slot][...], preferred_element_type=F32)
        acc_scr[...] = a_scr[slot][...] * acc_scr[...] + pv

    score(2 * first_pair, 0)

    def pair(i, carry):
        c = 2 * i
        score(c + 1, 1)
        accumulate(c - 1, 1)
        softmax(c, 0, False)
        score(c + 2, 0)
        accumulate(c, 0)
        softmax(c + 1, 1, False)
        return carry

    lax.fori_loop(first_pair, qi, pair, 0)
    c = 2 * qi
    score(c + 1, 1)
    accumulate(c - 1, 1)
    softmax(c, 0, True)
    accumulate(c, 0)
    softmax(c + 1, 1, True)
    accumulate(c + 1, 1)
    o_ref[...] = (acc_scr[...] / l_scr[...]).T.astype(BF16)


def _fox(proj, proj_t, cum, k_col0):
    T = proj.shape[0]
    tq = min(FOX_TQ, T)
    tk = tq // 2
    W = FOX_HEADS * FOX_DH
    kb = k_col0 // FOX_DH
    return pl.pallas_call(
        _fox_kernel,
        grid=(FOX_HEADS, T // tq),
        in_specs=[
            pl.BlockSpec((FOX_DH, tq), lambda h, i: (h, i)),
            pl.BlockSpec((T, FOX_DH), lambda h, i: (0, kb + h)),
            pl.BlockSpec((FOX_DH, T), lambda h, i: (FOX_HEADS + h, 0)),
            pl.BlockSpec((tq, LANES), lambda h, i: (i, 0)),
        ],
        out_specs=pl.BlockSpec((tq, FOX_DH), lambda h, i: (i, h)),
        out_shape=jax.ShapeDtypeStruct((T, W), BF16),
        scratch_shapes=[pltpu.VMEM((T, LANES), F32),
                        pltpu.VMEM((-(-(T // tk) // SUBLANES) * SUBLANES, LANES), F32),
                        pltpu.VMEM((1, LANES), F32),
                        pltpu.VMEM((tk, tq), F32), pltpu.VMEM((tk, tq), F32),
                        pltpu.VMEM((tk, tq), BF16), pltpu.VMEM((tk, tq), BF16),
                        pltpu.VMEM((1, tq), F32), pltpu.VMEM((1, tq), F32),
                        pltpu.VMEM((1, tq), F32), pltpu.VMEM((1, tq), F32),
                        pltpu.VMEM((FOX_DH, tq), F32)],
        compiler_params=_cparams(("arbitrary", "arbitrary")),
        name="fox",
    )(proj_t, proj, proj_t, cum)


def _outproj_kernel(yc_ref, yr_ref, yf_ref, wc_ref, wr_ref, wf_ref, x_ref, g1_ref,
                    g_ref, sh_ref, sc_ref, rwh_ref, rwl_ref, xo_ref, h_ref, lg_ref):
    mix = jnp.dot(yc_ref[...], wc_ref[...], preferred_element_type=F32)
    mix = mix + jnp.dot(yr_ref[...], wr_ref[...], preferred_element_type=F32)
    mix = mix + jnp.dot(yf_ref[...], wf_ref[...], preferred_element_type=F32)
    x = x_ref[...] + g1_ref[...] * mix
    xo_ref[...] = x
    h = _modulated_norm(x, g_ref[...], sh_ref[...], sc_ref[...])
    h_ref[...] = _pack_halves(h)
    h_hi = h.astype(BF16)
    h_lo = (h - h_hi.astype(F32)).astype(BF16)
    nt = lambda a, b: lax.dot_general(a, b, (((1,), (1,)), ((), ())), preferred_element_type=F32)
    lg_ref[...] = nt(rwh_ref[...], h_hi) + (nt(rwh_ref[...], h_lo) + nt(rwl_ref[...], h_hi))


def _outproj(yc, yr, yf, wc, wr, wf, x, g1, g, sh, sc, rw_t):
    T, D = x.shape
    tm = min(OUT_TM, T)
    E = rw_t.shape[0]
    rw_hi = rw_t.astype(BF16)
    rw_lo = (rw_t - rw_hi.astype(F32)).astype(BF16)
    vec = pl.BlockSpec((1, D), lambda i: (0, 0))
    rows = lambda a: pl.BlockSpec((tm, a.shape[1]), lambda i: (i, 0))
    whole = lambda a: pl.BlockSpec(a.shape, lambda i: (0, 0))
    return pl.pallas_call(
        _outproj_kernel,
        grid=(T // tm,),
        in_specs=[rows(yc), rows(yr), rows(yf), whole(wc), whole(wr), whole(wf), rows(x),
                  vec, vec, vec, vec, whole(rw_hi), whole(rw_lo)],
        out_specs=[rows(x), pl.BlockSpec((tm, D // 2), lambda i: (i, 0)),
                   pl.BlockSpec((E, tm), lambda i: (0, i))],
        out_shape=[jax.ShapeDtypeStruct((T, D), F32), jax.ShapeDtypeStruct((T, D // 2), U32),
                   jax.ShapeDtypeStruct((E, T), F32)],
        compiler_params=_cparams(("arbitrary",)),
        name="outproj",
    )(yc, yr, yf, wc, wr, wf, x, g1, g, sh, sc, rw_hi, rw_lo)


def _route_kernel(lg_ref, b_ref, tri_ref, e_ref, r_ref, w_ref, cnt_ref, carry_scr):
    i = pl.program_id(0)
    E, tt = lg_ref.shape

    @pl.when(i == 0)
    def _():
        carry_scr[...] = jnp.zeros_like(carry_scr)

    aff = jax.nn.sigmoid(lg_ref[...])
    sel = aff + b_ref[...]
    row8 = lax.broadcasted_iota(jnp.int32, (GROUP_SIZE, tt), 0)
    best = None
    for g in range(N_GROUPS):
        slab = sel[g * GROUP_SIZE:(g + 1) * GROUP_SIZE, :]
        m1 = jnp.max(slab, axis=0, keepdims=True)
        i1 = jnp.min(jnp.where(slab == m1, row8, GROUP_SIZE), axis=0, keepdims=True)
        rest = jnp.where(row8 == i1, -jnp.inf, slab)
        m2 = jnp.max(rest, axis=0, keepdims=True)
        i2 = jnp.min(jnp.where(rest == m2, row8, GROUP_SIZE), axis=0, keepdims=True)
        cand = (m1 + m2, g * GROUP_SIZE + i1, g * GROUP_SIZE + i2)
        if best is None:
            best = cand
        else:
            upd = cand[0] > best[0]
            best = tuple(jnp.where(upd, n, o) for n, o in zip(cand, best))
    _, e0, e1 = best

    row = lax.broadcasted_iota(jnp.int32, (E, tt), 0)
    oh0 = row == e0
    oh1 = row == e1
    a0 = jnp.sum(jnp.where(oh0, aff, 0.0), axis=0, keepdims=True)
    a1 = jnp.sum(jnp.where(oh1, aff, 0.0), axis=0, keepdims=True)
    w_ref[0:1, :] = a0 / (a0 + a1)
    w_ref[1:2, :] = a1 / (a0 + a1)
    e_ref[0:1, :] = e0
    e_ref[1:2, :] = e1

    oh = jnp.logical_or(oh0, oh1)
    ohf = jnp.where(oh, 1.0, 0.0)
    before = jnp.dot(ohf.astype(BF16), tri_ref[...], preferred_element_type=F32) + carry_scr[:, 0:1]
    r_ref[0:1, :] = jnp.sum(jnp.where(oh0, before, 0.0), axis=0, keepdims=True).astype(jnp.int32)
    r_ref[1:2, :] = jnp.sum(jnp.where(oh1, before, 0.0), axis=0, keepdims=True).astype(jnp.int32)
    carry = carry_scr[...] + jnp.sum(ohf, axis=1, keepdims=True)
    carry_scr[...] = carry
    cnt_ref[...] = carry


def _route(logits_t, router_b):
    E, T = logits_t.shape
    tt = min(ROUTE_TT, T)
    tri = jnp.asarray(np.triu(np.ones((tt, tt), np.float32), k=1), dtype=BF16)
    pair = lambda dt: jax.ShapeDtypeStruct((TOP_K, T), dt)
    return pl.pallas_call(
        _route_kernel,
        grid=(T // tt,),
        in_specs=[
            pl.BlockSpec((E, tt), lambda i: (0, i)),
            pl.BlockSpec((E, 1), lambda i: (0, 0)),
            pl.BlockSpec((tt, tt), lambda i: (0, 0)),
        ],
        out_specs=[
            pl.BlockSpec((TOP_K, tt), lambda i: (0, i)),
            pl.BlockSpec((TOP_K, tt), lambda i: (0, i)),
            pl.BlockSpec((TOP_K, tt), lambda i: (0, i)),
            pl.BlockSpec((E, LANES), lambda i: (0, 0)),
        ],
        out_shape=[pair(jnp.int32), pair(jnp.int32), pair(F32), jax.ShapeDtypeStruct((E, LANES), F32)],
        scratch_shapes=[pltpu.VMEM((E, LANES), F32)],
        compiler_params=_cparams(("arbitrary",)),
        name="route",
    )(logits_t, router_b.reshape(E, 1), tri)


def _block_tables(counts, T):
    E = counts.shape[0]
    A = T * TOP_K
    B = MOE_ROWS
    seg = (counts + SUBLANES - 1) // SUBLANES * SUBLANES
    seg_end = jnp.cumsum(seg)
    seg_start = seg_end - seg
    nb = (A + E * (B - 1) + B - 1) // B
    nblk = (counts + B - 1) // B
    blk_end = jnp.cumsum(nblk)
    total = blk_end[-1]
    b = jnp.arange(nb, dtype=jnp.int32)
    bc = jnp.minimum(b, total - 1)
    blk_e = jnp.minimum(jnp.searchsorted(blk_end, bc, side="right"), E - 1).astype(jnp.int32)
    local = bc - (blk_end[blk_e] - nblk[blk_e])
    blk_start = (seg_start[blk_e] + local * B).astype(jnp.int32)
    blk_first = jnp.logical_and(b < total, local == 0).astype(jnp.int32)
    nonempty = counts > 0
    order = jnp.cumsum(nonempty.astype(jnp.int32)) - 1
    ids = jnp.where(nonempty, jnp.arange(E, dtype=jnp.int32), E)
    later = jnp.concatenate([lax.cummin(ids, reverse=True)[1:], jnp.full((1,), E, jnp.int32)])
    next_e = jnp.where(later < E, later, -1)
    blk_wslot = (order[blk_e] % 2).astype(jnp.int32)
    blk_next_e = next_e[blk_e].astype(jnp.int32)
    rows = A + E * (SUBLANES - 1) + B
    rows = (rows + SUBLANES - 1) // SUBLANES * SUBLANES
    blocks = (blk_e, blk_start, blk_first, blk_wslot, blk_next_e, total.astype(jnp.int32).reshape(1))
    return seg_start.astype(jnp.int32), seg_end.astype(jnp.int32), blocks, nb, rows


def _zero_rows_from(zero_ref, hbm_ref, start, sem):
    piece = zero_ref.shape[0]
    rows = hbm_ref.shape[0]

    def body(j, carry):
        at = pl.multiple_of(jnp.minimum(start + j * piece, rows - piece), SUBLANES)
        cp = pltpu.make_async_copy(zero_ref, hbm_ref.at[pl.ds(at, piece), :], sem)
        cp.start()
        cp.wait()
        return carry

    lax.fori_loop(0, (rows - start + piece - 1) // piece, body, 0)


def _dispatch_kernel(seg_end_ref, pos_ref, h_ref, xs_hbm, zero_scr, sem, zsem):
    i = pl.program_id(0)
    td = pos_ref.shape[1]
    E = seg_end_ref.shape[0]

    @pl.when(i == 0)
    def _():
        zero_scr[...] = jnp.zeros_like(zero_scr)

        def tail(e, carry):
            end = seg_end_ref[e]
            at = pl.multiple_of(jnp.maximum(end - SUBLANES, 0), SUBLANES)
            cp = pltpu.make_async_copy(zero_scr.at[pl.ds(0, SUBLANES), :],
                                       xs_hbm.at[pl.ds(at, SUBLANES), :], zsem)
            cp.start()
            cp.wait()
            return carry

        lax.fori_loop(0, E, tail, 0)
        _zero_rows_from(zero_scr, xs_hbm, seg_end_ref[E - 1], zsem)

    def body(r, carry):
        for k in range(TOP_K):
            pltpu.make_async_copy(h_ref.at[pl.ds(r, 1), :],
                                  xs_hbm.at[pl.ds(pos_ref[k, r], 1), :], sem).start(priority=k)
        return carry

    lax.fori_loop(0, td, body, 0, unroll=8)
    for k in range(TOP_K):
        pltpu.make_async_copy(h_ref, xs_hbm.at[pl.ds(0, td), :], sem).wait()


def _dispatch(h2, pos, seg_end, rows):
    T, D = h2.shape
    td = min(DISPATCH_TT, T)
    grid_spec = pltpu.PrefetchScalarGridSpec(
        num_scalar_prefetch=1,
        grid=(T // td,),
        in_specs=[pl.BlockSpec((TOP_K, td), lambda i, se: (0, i), memory_space=pltpu.SMEM),
                  pl.BlockSpec((td, D), lambda i, se: (i, 0))],
        out_specs=pl.BlockSpec(memory_space=pl.ANY),
        scratch_shapes=[pltpu.VMEM((MOE_ROWS, D), h2.dtype), pltpu.SemaphoreType.DMA(()),
                        pltpu.SemaphoreType.DMA(())],
    )
    return pl.pallas_call(
        _dispatch_kernel,
        grid_spec=grid_spec,
        out_shape=jax.ShapeDtypeStruct((rows, D), h2.dtype),
        compiler_params=_cparams(("arbitrary",), disable_bounds_checks=True),
        name="dispatch",
    )(seg_end, pos, h2)


def _experts_kernel(blk_e_ref, blk_start_ref, blk_first_ref, blk_wslot_ref, blk_next_ref, total_ref,
                    tail_ref, xs_hbm, wg_hbm, wu_hbm, wd_hbm, ys_hbm,
                    wg_f, wu_f, wd_f, wg_b, wu_b, wd_b, xbuf, ybuf, isem, osem, wsem, *, layer):
    b = pl.program_id(0)
    total = total_ref[0]
    B = xbuf.shape[1]
    slot = b % 2

    def load(blk, to_slot):
        at = pl.multiple_of(blk_start_ref[blk], SUBLANES)
        return pltpu.make_async_copy(xs_hbm.at[pl.ds(at, B), :], xbuf.at[to_slot], isem.at[to_slot])

    def store(blk, from_slot):
        at = pl.multiple_of(blk_start_ref[blk], SUBLANES)
        return pltpu.make_async_copy(ybuf.at[from_slot], ys_hbm.at[pl.ds(at, B), :], osem)

    def weights(e, ws):
        return [pltpu.make_async_copy(src.at[layer, e], dst.at[ws], wsem.at[ws])
                for src, dst in ((wg_hbm, wg_f), (wu_hbm, wu_f), (wd_hbm, wd_f))]

    @pl.when(b == 0)
    def _():
        load(0, 0).start()
        for cp in weights(blk_e_ref[0], blk_wslot_ref[0]):
            cp.start()
        ybuf[1] = jnp.zeros(ybuf.shape[1:], ybuf.dtype)
        _zero_rows_from(ybuf.at[1], ys_hbm, tail_ref[0], osem)

    @pl.when(b + 1 < total)
    def _():
        load(b + 1, 1 - slot).start()

    @pl.when(b < total)
    def _():
        @pl.when(blk_first_ref[b] == 1)
        def _():
            ws = blk_wslot_ref[b]
            for cp in weights(blk_e_ref[b], ws):
                cp.wait()
            wg_b[...] = wg_f[ws].astype(BF16)
            wu_b[...] = wu_f[ws].astype(BF16)
            wd_b[...] = wd_f[ws].astype(BF16)
            nxt = blk_next_ref[b]

            @pl.when(nxt >= 0)
            def _():
                for cp in weights(nxt, 1 - ws):
                    cp.start()

        load(b, slot).wait()
        lo, hi = _unpack_halves(xbuf[slot])
        lo, hi = lo.astype(BF16), hi.astype(BF16)
        half = lo.shape[1]
        g = (jnp.dot(lo, wg_b[:half, :], preferred_element_type=F32)
             + jnp.dot(hi, wg_b[half:, :], preferred_element_type=F32))
        u = (jnp.dot(lo, wu_b[:half, :], preferred_element_type=F32)
             + jnp.dot(hi, wu_b[half:, :], preferred_element_type=F32))
        a = (g * jax.nn.sigmoid(g) * u).astype(BF16)
        ybuf[slot] = _pack_halves(jnp.dot(a, wd_b[...], preferred_element_type=F32))

        @pl.when(b > 0)
        def _():
            store(b - 1, 1 - slot).wait()

        store(b, slot).start()

        @pl.when(b == total - 1)
        def _():
            store(b, slot).wait()


def _experts(xs, layer, w_gate, w_up, w_down, blocks, tail, nb):
    rows, DP = xs.shape
    D, DE = w_gate.shape[-2:]
    B = MOE_ROWS
    anywhere = pl.BlockSpec(memory_space=pl.ANY)
    grid_spec = pltpu.PrefetchScalarGridSpec(
        num_scalar_prefetch=len(blocks) + 1,
        grid=(nb,),
        in_specs=[anywhere, anywhere, anywhere, anywhere],
        out_specs=anywhere,
        scratch_shapes=[
            pltpu.VMEM((2, D, DE), F32), pltpu.VMEM((2, D, DE), F32), pltpu.VMEM((2, DE, D), F32),
            pltpu.VMEM((D, DE), BF16), pltpu.VMEM((D, DE), BF16), pltpu.VMEM((DE, D), BF16),
            pltpu.VMEM((2, B, DP), U32), pltpu.VMEM((2, B, DP), U32),
            pltpu.SemaphoreType.DMA((2,)), pltpu.SemaphoreType.DMA(()), pltpu.SemaphoreType.DMA((2,)),
        ],
    )
    return pl.pallas_call(
        functools.partial(_experts_kernel, layer=layer),
        grid_spec=grid_spec,
        out_shape=jax.ShapeDtypeStruct((rows, DP), U32),
        compiler_params=_cparams(("arbitrary",)),
        name="experts",
    )(*blocks, tail, xs, w_gate, w_up, w_down)


def _combine_kernel(pos_ref, posn_ref, x_ref, ys_hbm, w_ref, g2_ref, fg_ref, o_ref, gbuf, sem,
                    *, final):
    i = pl.program_id(0)
    n = pl.num_programs(0)
    tm = x_ref.shape[0]
    slot = i % 2

    def gather(p_ref, to_slot):
        def body(r, carry):
            for k in range(TOP_K):
                pltpu.make_async_copy(ys_hbm.at[pl.ds(p_ref[k, r], 1), :],
                                      gbuf.at[to_slot, k, pl.ds(r, 1), :], sem.at[to_slot]).start(priority=k)
            return carry

        lax.fori_loop(0, tm, body, 0, unroll=8)

    @pl.when(i == 0)
    def _():
        gather(pos_ref, 0)

    @pl.when(i + 1 < n)
    def _():
        gather(posn_ref, 1 - slot)

    for k in range(TOP_K):
        pltpu.make_async_copy(ys_hbm.at[pl.ds(0, tm), :], gbuf.at[slot, k], sem.at[slot]).wait()

    w = w_ref[...]
    D = x_ref.shape[1]
    half = D // 2
    lo0, hi0 = _unpack_halves(gbuf[slot, 0])
    lo1, hi1 = _unpack_halves(gbuf[slot, 1])
    xl = x_ref[:, :half] + g2_ref[:, :half] * (lo0 * w[:, 0:1] + lo1 * w[:, 1:2])
    xh = x_ref[:, half:] + g2_ref[:, half:] * (hi0 * w[:, 0:1] + hi1 * w[:, 1:2])
    if final:
        ms = (jnp.sum(xl * xl, axis=-1, keepdims=True) + jnp.sum(xh * xh, axis=-1, keepdims=True)) / D
        r = lax.rsqrt(ms + EPS)
        xl = xl * r * fg_ref[:, :half]
        xh = xh * r * fg_ref[:, half:]
    o_ref[:, :half] = xl
    o_ref[:, half:] = xh


def _combine(x, ys, pos, top_w, g2, final_g, final):
    T, D = x.shape
    tm = min(COMB_TM, T)
    n = T // tm
    vec = pl.BlockSpec((1, D), lambda i: (0, 0))
    return pl.pallas_call(
        functools.partial(_combine_kernel, final=final),
        grid=(n,),
        in_specs=[
            pl.BlockSpec((TOP_K, tm), lambda i: (0, i), memory_space=pltpu.SMEM),
            pl.BlockSpec((TOP_K, tm), lambda i: (0, jnp.minimum(i + 1, n - 1)), memory_space=pltpu.SMEM),
            pl.BlockSpec((tm, D), lambda i: (i, 0)),
            pl.BlockSpec(memory_space=pl.ANY),
            pl.BlockSpec((tm, TOP_K), lambda i: (i, 0)),
            vec, vec,
        ],
        out_specs=pl.BlockSpec((tm, D), lambda i: (i, 0)),
        out_shape=jax.ShapeDtypeStruct((T, D), F32),
        scratch_shapes=[pltpu.VMEM((2, TOP_K, tm, ys.shape[1]), ys.dtype), pltpu.SemaphoreType.DMA((2,))],
        compiler_params=_cparams(("arbitrary",), disable_bounds_checks=True),
        name="combine",
    )(pos, pos, x, ys, top_w, g2, final_g)


def kernel(x, c, ada_w, ada_b, norm1_g, norm2_g, w_in, conv_w, ret_gn_g, fox_fb, w_out,
           router_w, router_b, moe_w_gate, moe_w_up, moe_w_down, final_g):
    Bsz, S, D = x.shape
    assert Bsz == 1, "one sequence per call"
    L = ada_w.shape[0]
    T = S
    conv_wd = conv_w.shape[-1]
    ret_qk = RET_HEADS * RET_DK
    ret_v = RET_HEADS * RET_DV
    fox_w = FOX_HEADS * FOX_DH
    tn = INPROJ_TN
    ret_col0 = 3 * conv_wd
    fox_col0 = ret_col0 + 2 * ret_qk + 2 * ret_v
    n_main = fox_col0 + 3 * fox_w
    assert fox_w == tn and fox_col0 % tn == 0
    q_block = fox_col0 // tn
    v_block = q_block + 2
    k_col0 = fox_col0

    mod = _ada_mod(c, ada_w, ada_b)
    cos_t, sin_t = _rope_tables(T)
    rw_t = router_w.T
    xt = x.reshape(T, D)

    for l in range(L):
        sh1, sc1, g1, sh2, sc2, g2 = [mod[l, :, k * D:(k + 1) * D] for k in range(6)]
        w_ff = jnp.pad(w_in[l, :, n_main:], ((0, 0), (0, LANES - FOX_HEADS)))
        fb = jnp.pad(fox_fb[l], (0, LANES - FOX_HEADS)).reshape(1, LANES)
        proj, proj_t, cum = _inproj(xt, norm1_g[l].reshape(1, D), sh1, sc1, w_in, l, w_ff, fb,
                                    q_block, v_block, LOG2E * FOX_DH ** -0.5)

        y_conv = _conv(proj, conv_w[l])
        y_ret = _retention(proj, cos_t, sin_t, ret_gn_g[l].reshape(1, ret_v), ret_col0)
        y_fox = _fox(proj, proj_t, cum, k_col0)

        wo = w_out[l].astype(BF16)
        xt, h2, logits_t = _outproj(
            y_conv, y_ret, y_fox, wo[:conv_wd], wo[conv_wd:conv_wd + ret_v], wo[conv_wd + ret_v:],
            xt, g1, norm2_g[l].reshape(1, D), sh2, sc2, rw_t)

        top_e, rank, top_w, cnt = _route(logits_t, router_b)
        counts = cnt[:, 0].astype(jnp.int32)
        seg_start, seg_end, blocks, nb, rows = _block_tables(counts, T)
        hit = top_e[None] == jnp.arange(N_EXPERTS, dtype=jnp.int32)[:, None, None]
        pos = rank + jnp.sum(jnp.where(hit, seg_start[:, None, None], 0), axis=0)
        xs = _dispatch(h2, pos, seg_end, rows)
        ys = _experts(xs, l, moe_w_gate, moe_w_up, moe_w_down, blocks, seg_end[-1:], nb)
        xt = _combine(xt, ys, pos, top_w.T, g2, final_g.reshape(1, D), final=(l == L - 1))

    return xt.reshape(Bsz, S, D)
```

```python
import functools
import math

import numpy as np
import jax
import jax.numpy as jnp
from jax import lax
from jax.experimental import pallas as pl
from jax.experimental.pallas import tpu as pltpu

F32 = jnp.float32
BF16 = jnp.bfloat16

CHUNK = 64
CONV_K = 3
RET_HEADS = 6
RET_DK = 64
RET_DV = 128
FOX_HEADS = 6
FOX_DH = 128
N_EXPERTS = 64
N_GROUPS = 8
GROUP_SIZE = N_EXPERTS // N_GROUPS
TOP_K = 2
ROPE_BASE = 10000.0
EPS = 1e-6
LOG2E = 1.4426950408889634

LANES = 128
SUBLANES = 8

MOD_TN = 1024
INPROJ_TM = 1024
INPROJ_TN = 768
CONV_TM = 1024
CONV_HALO = 16
RET_ROWS = 256
FOX_TQ = 1024
FOX_SLAB = 16
FOX_SKIP_LOG2 = 160.0
FOX_NORM_SLACK = 1.02
OUT_TM = 512
MOE_ROWS = 256
ROUTE_TT = 1024
DISPATCH_TT = 512
COMB_TM = 512
VMEM_LIMIT = 56 * 1024 * 1024


def _cparams(sem, **kw):
    return pltpu.CompilerParams(dimension_semantics=sem, vmem_limit_bytes=VMEM_LIMIT, **kw)


U32 = jnp.uint32


def _pack_halves(x):
    half = x.shape[1] // 2
    xb = x.astype(BF16).astype(F32)
    lo = lax.bitcast_convert_type(xb[:, :half], U32)
    hi = lax.bitcast_convert_type(xb[:, half:], U32)
    return lax.shift_right_logical(lo, U32(16)) | (hi & U32(0xFFFF0000))


def _unpack_halves(w):
    lo = lax.bitcast_convert_type(lax.shift_left(w, U32(16)), F32)
    hi = lax.bitcast_convert_type(w & U32(0xFFFF0000), F32)
    return lo, hi


def _mod_kernel(c_ref, w_ref, b_ref, o_ref):
    c = c_ref[...]
    ca = c * jax.nn.sigmoid(c)
    o_ref[0] = jnp.sum(w_ref[0] * ca, axis=0, keepdims=True) + b_ref[0]


def _ada_mod(c, ada_w, ada_b):
    L, D, N = ada_w.shape
    tn = MOD_TN
    return pl.pallas_call(
        _mod_kernel,
        grid=(L, N // tn),
        in_specs=[
            pl.BlockSpec((D, 1), lambda l, j: (0, 0)),
            pl.BlockSpec((1, D, tn), lambda l, j: (l, 0, j)),
            pl.BlockSpec((1, 1, tn), lambda l, j: (l, 0, j)),
        ],
        out_specs=pl.BlockSpec((1, 1, tn), lambda l, j: (l, 0, j)),
        out_shape=jax.ShapeDtypeStruct((L, 1, N), F32),
        compiler_params=_cparams(("arbitrary", "arbitrary")),
        name="ada_mod",
    )(c.reshape(D, 1), ada_w, ada_b.reshape(L, 1, N))


def _modulated_norm(x, g, sh, sc):
    ms = jnp.mean(x * x, axis=-1, keepdims=True)
    y = x * lax.rsqrt(ms + EPS) * g
    return y * (1.0 + sc) + sh


def _log_sigmoid(x):
    return jnp.minimum(x, 0.0) - jnp.log1p(jnp.exp(-jnp.abs(x)))


def _inproj_kernel(x_ref, g_ref, sh_ref, sc_ref, w_ref, wff_ref, fb_ref,
                   proj_ref, projt_ref, cum_ref, h_scr, wt_scr, carry_scr,
                   *, q_block, v_block, q_scale):
    i = pl.program_id(0)
    j = pl.program_id(1)
    tm = x_ref.shape[0]

    @pl.when(jnp.logical_and(i == 0, j == 0))
    def _():
        carry_scr[...] = jnp.zeros_like(carry_scr)

    @pl.when(j == 0)
    def _():
        h = _modulated_norm(x_ref[...], g_ref[...], sh_ref[...], sc_ref[...])
        hb = h.astype(BF16)
        h_scr[...] = hb
        ff = jnp.dot(hb, wff_ref[...].astype(BF16), preferred_element_type=F32) + fb_ref[...]
        c = _log_sigmoid(ff)
        row = lax.broadcasted_iota(jnp.int32, c.shape, 0)
        d = 1
        while d < tm:
            c = c + jnp.where(row >= d, pltpu.roll(c, d, axis=0), 0.0)
            d *= 2
        c = c + carry_scr[...]
        cum_ref[...] = c
        carry_scr[...] = c[tm - 1:tm, :]

    @pl.when(jnp.logical_and(j != q_block, j != v_block))
    def _():
        proj_ref[...] = jnp.dot(h_scr[...], w_ref[...].astype(BF16),
                                preferred_element_type=F32).astype(BF16)

    def feature_major(slot, scale):
        @pl.when(i == 0)
        def _():
            wt_scr[slot] = w_ref[...].T.astype(BF16)

        r = lax.dot_general(wt_scr[slot], h_scr[...], (((1,), (1,)), ((), ())),
                            preferred_element_type=F32)
        projt_ref[...] = (r * scale).astype(BF16)

    @pl.when(j == q_block)
    def _():
        feature_major(0, q_scale)

    @pl.when(j == v_block)
    def _():
        feature_major(1, 1.0)


def _inproj(x, g, sh, sc, w_in, layer, w_ff, fb, q_block, v_block, q_scale):
    T, D = x.shape
    tm, tn = min(INPROJ_TM, T), INPROJ_TN
    nj = v_block + 1
    is_q = lambda j: (j == q_block).astype(jnp.int32)
    is_v = lambda j: (j == v_block).astype(jnp.int32)
    w_col = lambda i, j: jnp.where(i == 0, j, j - is_q(j) - is_v(j))
    out_col = lambda j: j - (j >= q_block).astype(jnp.int32) - (j >= v_block).astype(jnp.int32)
    vec = pl.BlockSpec((1, D), lambda i, j: (0, 0))
    return pl.pallas_call(
        functools.partial(_inproj_kernel, q_block=q_block, v_block=v_block, q_scale=q_scale),
        grid=(T // tm, nj),
        in_specs=[
            pl.BlockSpec((tm, D), lambda i, j: (i, 0), pipeline_mode=pl.Buffered(1)),
            vec, vec, vec,
            pl.BlockSpec((None, D, tn), lambda i, j: (layer, 0, w_col(i, j))),
            pl.BlockSpec((D, LANES), lambda i, j: (0, 0)),
            pl.BlockSpec((1, LANES), lambda i, j: (0, 0)),
        ],
        out_specs=[
            pl.BlockSpec((tm, tn), lambda i, j: (i, out_col(j))),
            pl.BlockSpec((tn, tm), lambda i, j: (is_v(j), i)),
            pl.BlockSpec((tm, LANES), lambda i, j: (i, 0)),
        ],
        out_shape=[
            jax.ShapeDtypeStruct((T, (nj - 2) * tn), BF16),
            jax.ShapeDtypeStruct((2 * tn, T), BF16),
            jax.ShapeDtypeStruct((T, LANES), F32),
        ],
        scratch_shapes=[pltpu.VMEM((tm, D), BF16), pltpu.VMEM((2, tn, D), BF16),
                        pltpu.VMEM((1, LANES), F32)],
        compiler_params=_cparams(("arbitrary", "arbitrary")),
        name="inproj",
    )(x, g, sh, sc, w_in, w_ff, fb)


def _conv_kernel(cb_ref, cc_ref, cu_ref, hc_ref, hu_ref, w_ref, o_ref):
    i = pl.program_id(0)
    z = cc_ref[...].astype(F32) * cu_ref[...].astype(F32)
    zh = hc_ref[...].astype(F32) * hu_ref[...].astype(F32)
    zh = jnp.where(i > 0, zh, 0.0)
    hl = zh.shape[0]
    zm1 = zh[hl - 1:hl, :]
    zm2 = zh[hl - 2:hl - 1, :]
    row = lax.broadcasted_iota(jnp.int32, z.shape, 0)
    z1 = jnp.where(row == 0, zm1, pltpu.roll(z, 1, axis=0))
    z2 = jnp.where(row == 0, zm2, jnp.where(row == 1, zm1, pltpu.roll(z, 2, axis=0)))
    w = w_ref[...]
    y = z2 * w[0:1, :] + z1 * w[1:2, :] + z * w[2:3, :]
    o_ref[...] = (cb_ref[...].astype(F32) * y).astype(BF16)


def _conv(proj, conv_w):
    T = proj.shape[0]
    W = conv_w.shape[1]
    tm = min(CONV_TM, T)
    hb = tm // CONV_HALO
    halo = lambda c: pl.BlockSpec((CONV_HALO, W), lambda i: (jnp.maximum(i * hb - 1, 0), c))
    return pl.pallas_call(
        _conv_kernel,
        grid=(T // tm,),
        in_specs=[
            pl.BlockSpec((tm, W), lambda i: (i, 0)),
            pl.BlockSpec((tm, W), lambda i: (i, 1)),
            pl.BlockSpec((tm, W), lambda i: (i, 2)),
            halo(1), halo(2),
            pl.BlockSpec((CONV_K, W), lambda i: (0, 0)),
        ],
        out_specs=pl.BlockSpec((tm, W), lambda i: (i, 0)),
        out_shape=jax.ShapeDtypeStruct((T, W), BF16),
        compiler_params=_cparams(("arbitrary",)),
        name="conv",
    )(proj, proj, proj, proj, proj, conv_w)


def _ret_gammas():
    return [1.0 - 2.0 ** (-5.0 - h) for h in range(RET_HEADS)]


def _ret_tables(R):
    n = np.arange(R, dtype=np.float64)
    chunk = np.arange(R) // CHUNK
    allowed = chunk[None, :] <= chunk[:, None]
    dm, qd, kd = [], [], []
    for g in _ret_gammas():
        lg = math.log(g)
        dm.append(np.where(allowed, np.exp(lg * np.abs(n[:, None] - n[None, :])), 0.0))
        qd.append(np.broadcast_to(np.exp(lg * (n + 1.0))[:, None], (R, LANES)))
        kd.append(np.broadcast_to(np.exp(lg * (R - 1.0 - n))[:, None], (R, LANES)))
    f = lambda a: jnp.asarray(np.stack(a), dtype=F32)
    return f(dm), f(qd), f(kd)


def _rope_tables(T):
    half = RET_DK // 2
    inv = ROPE_BASE ** (-jnp.arange(half, dtype=F32) / half)
    ang = jnp.arange(T, dtype=F32)[:, None] * inv[None, :]
    cos, sin = jnp.cos(ang), jnp.sin(ang)
    reps = LANES // RET_DK
    cos_t = jnp.tile(jnp.concatenate([cos, cos], axis=1), (1, reps))
    sin_t = jnp.tile(jnp.concatenate([-sin, sin], axis=1), (1, reps))
    return cos_t, sin_t


def _ret_kernel(q_ref, k_ref, v_ref, g_ref, cos_ref, sin_ref, dm_ref, qd_ref, kd_ref, gn_ref,
                o_ref, s_scr):
    i = pl.program_id(0)
    R = q_ref.shape[0]

    @pl.when(i == 0)
    def _():
        s_scr[...] = jnp.zeros_like(s_scr)

    lane = lax.broadcasted_iota(jnp.int32, (R, LANES), 1)
    first_half = (lane % RET_DK) < (RET_DK // 2)
    low_head = lane < RET_DK
    cosv = cos_ref[...]
    sinv = sin_ref[...]
    c_dec = [g ** R for g in _ret_gammas()]

    def rot(t):
        swapped = jnp.where(first_half, pltpu.roll(t, LANES - RET_DK // 2, axis=1),
                            pltpu.roll(t, RET_DK // 2, axis=1))
        return t * cosv + swapped * sinv

    heads_per_vreg = LANES // RET_DK
    for p in range(RET_HEADS // heads_per_vreg):
        cols = slice(p * LANES, (p + 1) * LANES)
        qr = rot(q_ref[:, cols].astype(F32))
        kb = (rot(k_ref[:, cols].astype(F32)) * (RET_DK ** -0.5)).astype(BF16)
        for hh in range(heads_per_vreg):
            h = p * heads_per_vreg + hh
            hc = slice(h * RET_DV, (h + 1) * RET_DV)
            mask = low_head if hh == 0 else jnp.logical_not(low_head)
            qm = jnp.where(mask, qr, 0.0).astype(BF16)
            s = lax.dot_general(qm, kb, (((1,), (1,)), ((), ())), preferred_element_type=F32)
            s = s * dm_ref[h]
            v = v_ref[:, hc]
            o = jnp.dot(s.astype(BF16), v, preferred_element_type=F32)
            state = s_scr[h]
            o = o + jnp.dot(qm, state.astype(BF16), preferred_element_type=F32) * qd_ref[h]
            vd = (v.astype(F32) * kd_ref[h]).astype(BF16)
            kv = lax.dot_general(kb, vd, (((0,), (0,)), ((), ())), preferred_element_type=F32)
            s_scr[h] = state * c_dec[h] + kv
            mu = jnp.mean(o, axis=-1, keepdims=True)
            d = o - mu
            var = jnp.mean(d * d, axis=-1, keepdims=True)
            on = d * lax.rsqrt(var + EPS) * gn_ref[:, hc]
            gate = g_ref[:, hc].astype(F32)
            o_ref[:, hc] = (gate * jax.nn.sigmoid(gate) * on).astype(BF16)


def _retention(proj, cos_t, sin_t, gn_g, col0):
    T = proj.shape[0]
    R = min(RET_ROWS, T)
    QK = RET_HEADS * RET_DK
    V = RET_HEADS * RET_DV
    dm, qd, kd = _ret_tables(R)
    q_blk = col0 // QK
    v_blk = (col0 + 2 * QK) // V
    full3 = lambda a: pl.BlockSpec(a.shape, lambda i: (0, 0, 0))
    return pl.pallas_call(
        _ret_kernel,
        grid=(T // R,),
        in_specs=[
            pl.BlockSpec((R, QK), lambda i: (i, q_blk)),
            pl.BlockSpec((R, QK), lambda i: (i, q_blk + 1)),
            pl.BlockSpec((R, V), lambda i: (i, v_blk)),
            pl.BlockSpec((R, V), lambda i: (i, v_blk + 1)),
            pl.BlockSpec((R, LANES), lambda i: (i, 0)),
            pl.BlockSpec((R, LANES), lambda i: (i, 0)),
            full3(dm), full3(qd), full3(kd),
            pl.BlockSpec((1, V), lambda i: (0, 0)),
        ],
        out_specs=pl.BlockSpec((R, V), lambda i: (i, 0)),
        out_shape=jax.ShapeDtypeStruct((T, V), BF16),
        scratch_shapes=[pltpu.VMEM((RET_HEADS, LANES, RET_DV), F32)],
        compiler_params=_cparams(("arbitrary",)),
        name="retention",
    )(proj, proj, proj, proj, cos_t, sin_t, dm, qd, kd, gn_g)


def _fox_kernel(qt_ref, k_ref, vt_ref, cum_ref, o_ref, ck_scr, cend_scr, kn_scr,
                s0_scr, s1_scr, p0_scr, p1_scr, a0_scr, a1_scr, c0_scr, c1_scr, m_scr, l_scr, acc_scr):
    h = pl.program_id(0)
    qi = pl.program_id(1)
    tq = qt_ref.shape[1]
    tk = s0_scr.shape[0]
    T = k_ref.shape[0]
    assert tq == 2 * tk
    reps = tq // LANES
    s_scr, p_scr, a_scr = (s0_scr, s1_scr), (p0_scr, p1_scr), (a0_scr, a1_scr)
    cm_scr = (c0_scr, c1_scr)

    @pl.when(qi == 0)
    def _():
        cend_scr[...] = jnp.zeros_like(cend_scr)
        ones = jnp.ones((FOX_DH, LANES), BF16)
        kn2 = jnp.zeros((SUBLANES, LANES), F32)
        for r in range(0, T, tq):
            kf = k_ref[r:r + tq, :].astype(F32)
            n2 = jnp.dot((kf * kf).astype(BF16), ones, preferred_element_type=F32)
            kn2 = jnp.maximum(kn2, jnp.max(n2.reshape(tq // SUBLANES, SUBLANES, LANES), axis=0))
        kn_scr[...] = jnp.max(kn2, axis=0, keepdims=True)

    lane = lax.broadcasted_iota(jnp.int32, cum_ref.shape, 1)
    col = jnp.sum(jnp.where(lane == h, cum_ref[...], 0.0), axis=1, keepdims=True) * LOG2E
    q0 = pl.multiple_of(qi * tq, tq)
    ck_rep = jnp.broadcast_to(col, cum_ref.shape)
    ck_scr[pl.ds(q0, tq), :] = ck_rep
    cend_scr[pl.ds(2 * qi, 1), :] = ck_rep[tk - 1:tk, :]
    cend_scr[pl.ds(2 * qi + 1, 1), :] = ck_rep[tq - 1:tq, :]

    qf = qt_ref[...].astype(F32)
    qn2 = jnp.max(jnp.sum(qf * qf, axis=0, keepdims=True), axis=1, keepdims=True)
    reach = 2.0 * FOX_NORM_SLACK * jnp.sqrt(qn2 * kn_scr[...]) + FOX_SKIP_LOG2
    decay = cend_scr[...] - ck_rep[0:1, :]
    chunk_id = lax.broadcasted_iota(jnp.int32, decay.shape, 0)
    dead = jnp.logical_and(decay >= reach, chunk_id < 2 * qi)
    n_dead = jnp.max(jnp.sum(dead.astype(jnp.int32), axis=0, keepdims=True))
    first_pair = n_dead // 2

    m_scr[...] = jnp.full_like(m_scr, -jnp.inf)
    l_scr[...] = jnp.zeros_like(l_scr)
    acc_scr[...] = jnp.zeros_like(acc_scr)
    p1_scr[...] = jnp.zeros_like(p1_scr)
    a1_scr[...] = jnp.ones_like(a1_scr)

    def score(c, slot):
        k0 = pl.multiple_of(c * tk, tk)
        s = jnp.dot(k_ref[pl.ds(k0, tk), :], qt_ref[...], preferred_element_type=F32)
        bias = ck_scr[pl.ds(k0, tk), :]
        s = s - jnp.concatenate([bias] * reps, axis=1)
        s_scr[slot][...] = s
        cm_scr[slot][...] = jnp.max(s, axis=0, keepdims=True)

    def softmax(c, slot, masked):
        if masked:
            s = s_scr[slot][...]
            kpos = c * tk + lax.broadcasted_iota(jnp.int32, s.shape, 0)
            qpos = q0 + lax.broadcasted_iota(jnp.int32, s.shape, 1)
            s_scr[slot][...] = jnp.where(kpos <= qpos, s, -jnp.inf)
        sref, pref = s_scr[slot], p_scr[slot]
        if masked:
            mx = sref[0:FOX_SLAB, :]
            for r in range(FOX_SLAB, tk, FOX_SLAB):
                mx = jnp.maximum(mx, sref[r:r + FOX_SLAB, :])
            cmax = jnp.max(mx, axis=0, keepdims=True)
        else:
            cmax = cm_scr[slot][...]
        m_old = m_scr[...]
        m_new = jnp.maximum(m_old, cmax)
        alpha = jnp.exp2(m_old - m_new)
        m_rows = jnp.broadcast_to(m_new, (FOX_SLAB, tq))
        psum = jnp.zeros((FOX_SLAB, tq), F32)
        for r in range(0, tk, FOX_SLAB):
            p = jnp.exp2(sref[r:r + FOX_SLAB, :] - m_rows)
            psum = psum + p
            pref[r:r + FOX_SLAB, :] = p.astype(BF16)
        l_scr[...] = alpha * l_scr[...] + jnp.sum(psum, axis=0, keepdims=True)
        a_scr[slot][...] = alpha
        m_scr[...] = m_new

    def accumulate(c, slot):
        k0 = pl.multiple_of(jnp.maximum(c, 0) * tk, tk)
        pv = jnp.dot(vt_ref[:, pl.ds(k0, tk)], p_scr[slot][...], preferred_element_type=F32)
        acc_scr[...] = a_scr[slot][...] * acc_scr[...] + pv

    score(2 * first_pair, 0)

    def pair(i, carry):
        c = 2 * i
        score(c + 1, 1)
        accumulate(c - 1, 1)
        softmax(c, 0, False)
        score(c + 2, 0)
        accumulate(c, 0)
        softmax(c + 1, 1, False)
        return carry

    lax.fori_loop(first_pair, qi, pair, 0)
    c = 2 * qi
    score(c + 1, 1)
    accumulate(c - 1, 1)
    softmax(c, 0, True)
    accumulate(c, 0)
    softmax(c + 1, 1, True)
    accumulate(c + 1, 1)
    o_ref[...] = (acc_scr[...] / l_scr[...]).T.astype(BF16)


def _fox(proj, proj_t, cum, k_col0):
    T = proj.shape[0]
    tq = min(FOX_TQ, T)
    tk = tq // 2
    W = FOX_HEADS * FOX_DH
    kb = k_col0 // FOX_DH
    return pl.pallas_call(
        _fox_kernel,
        grid=(FOX_HEADS, T // tq),
        in_specs=[
            pl.BlockSpec((FOX_DH, tq), lambda h, i: (h, i)),
            pl.BlockSpec((T, FOX_DH), lambda h, i: (0, kb + h)),
            pl.BlockSpec((FOX_DH, T), lambda h, i: (FOX_HEADS + h, 0)),
            pl.BlockSpec((tq, LANES), lambda h, i: (i, 0)),
        ],
        out_specs=pl.BlockSpec((tq, FOX_DH), lambda h, i: (i, h)),
        out_shape=jax.ShapeDtypeStruct((T, W), BF16),
        scratch_shapes=[pltpu.VMEM((T, LANES), F32),
                        pltpu.VMEM((-(-(T // tk) // SUBLANES) * SUBLANES, LANES), F32),
                        pltpu.VMEM((1, LANES), F32),
                        pltpu.VMEM((tk, tq), F32), pltpu.VMEM((tk, tq), F32),
                        pltpu.VMEM((tk, tq), BF16), pltpu.VMEM((tk, tq), BF16),
                        pltpu.VMEM((1, tq), F32), pltpu.VMEM((1, tq), F32),
                        pltpu.VMEM((1, tq), F32), pltpu.VMEM((1, tq), F32),
                        pltpu.VMEM((1, tq), F32), pltpu.VMEM((1, tq), F32),
                        pltpu.VMEM((FOX_DH, tq), F32)],
        compiler_params=_cparams(("arbitrary", "arbitrary")),
        name="fox",
    )(proj_t, proj, proj_t, cum)


def _outproj_kernel(yc_ref, yr_ref, yf_ref, wc_ref, wr_ref, wf_ref, x_ref, g1_ref,
                    g_ref, sh_ref, sc_ref, rwh_ref, rwl_ref, xo_ref, h_ref, lg_ref):
    mix = jnp.dot(yc_ref[...], wc_ref[...], preferred_element_type=F32)
    mix = mix + jnp.dot(yr_ref[...], wr_ref[...], preferred_element_type=F32)
    mix = mix + jnp.dot(yf_ref[...], wf_ref[...], preferred_element_type=F32)
    x = x_ref[...] + g1_ref[...] * mix
    xo_ref[...] = x
    h = _modulated_norm(x, g_ref[...], sh_ref[...], sc_ref[...])
    h_ref[...] = _pack_halves(h)
    h_hi = h.astype(BF16)
    h_lo = (h - h_hi.astype(F32)).astype(BF16)
    nt = lambda a, b: lax.dot_general(a, b, (((1,), (1,)), ((), ())), preferred_element_type=F32)
    lg_ref[...] = nt(rwh_ref[...], h_hi) + (nt(rwh_ref[...], h_lo) + nt(rwl_ref[...], h_hi))


def _outproj(yc, yr, yf, wc, wr, wf, x, g1, g, sh, sc, rw_t):
    T, D = x.shape
    tm = min(OUT_TM, T)
    E = rw_t.shape[0]
    rw_hi = rw_t.astype(BF16)
    rw_lo = (rw_t - rw_hi.astype(F32)).astype(BF16)
    vec = pl.BlockSpec((1, D), lambda i: (0, 0))
    rows = lambda a: pl.BlockSpec((tm, a.shape[1]), lambda i: (i, 0))
    whole = lambda a: pl.BlockSpec(a.shape, lambda i: (0, 0))
    return pl.pallas_call(
        _outproj_kernel,
        grid=(T // tm,),
        in_specs=[rows(yc), rows(yr), rows(yf), whole(wc), whole(wr), whole(wf), rows(x),
                  vec, vec, vec, vec, whole(rw_hi), whole(rw_lo)],
        out_specs=[rows(x), pl.BlockSpec((tm, D // 2), lambda i: (i, 0)),
                   pl.BlockSpec((E, tm), lambda i: (0, i))],
        out_shape=[jax.ShapeDtypeStruct((T, D), F32), jax.ShapeDtypeStruct((T, D // 2), U32),
                   jax.ShapeDtypeStruct((E, T), F32)],
        compiler_params=_cparams(("arbitrary",)),
        name="outproj",
    )(yc, yr, yf, wc, wr, wf, x, g1, g, sh, sc, rw_hi, rw_lo)


def _route_kernel(lg_ref, b_ref, tri_ref, e_ref, r_ref, w_ref, cnt_ref, carry_scr):
    i = pl.program_id(0)
    E, tt = lg_ref.shape

    @pl.when(i == 0)
    def _():
        carry_scr[...] = jnp.zeros_like(carry_scr)

    aff = jax.nn.sigmoid(lg_ref[...])
    sel = aff + b_ref[...]
    row8 = lax.broadcasted_iota(jnp.int32, (GROUP_SIZE, tt), 0)
    best = None
    for g in range(N_GROUPS):
        slab = sel[g * GROUP_SIZE:(g + 1) * GROUP_SIZE, :]
        m1 = jnp.max(slab, axis=0, keepdims=True)
        i1 = jnp.min(jnp.where(slab == m1, row8, GROUP_SIZE), axis=0, keepdims=True)
        rest = jnp.where(row8 == i1, -jnp.inf, slab)
        m2 = jnp.max(rest, axis=0, keepdims=True)
        i2 = jnp.min(jnp.where(rest == m2, row8, GROUP_SIZE), axis=0, keepdims=True)
        cand = (m1 + m2, g * GROUP_SIZE + i1, g * GROUP_SIZE + i2)
        if best is None:
            best = cand
        else:
            upd = cand[0] > best[0]
            best = tuple(jnp.where(upd, n, o) for n, o in zip(cand, best))
    _, e0, e1 = best

    row = lax.broadcasted_iota(jnp.int32, (E, tt), 0)
    oh0 = row == e0
    oh1 = row == e1
    a0 = jnp.sum(jnp.where(oh0, aff, 0.0), axis=0, keepdims=True)
    a1 = jnp.sum(jnp.where(oh1, aff, 0.0), axis=0, keepdims=True)
    w_ref[0:1, :] = a0 / (a0 + a1)
    w_ref[1:2, :] = a1 / (a0 + a1)
    e_ref[0:1, :] = e0
    e_ref[1:2, :] = e1

    oh = jnp.logical_or(oh0, oh1)
    ohf = jnp.where(oh, 1.0, 0.0)
    before = jnp.dot(ohf.astype(BF16), tri_ref[...], preferred_element_type=F32) + carry_scr[:, 0:1]
    r_ref[0:1, :] = jnp.sum(jnp.where(oh0, before, 0.0), axis=0, keepdims=True).astype(jnp.int32)
    r_ref[1:2, :] = jnp.sum(jnp.where(oh1, before, 0.0), axis=0, keepdims=True).astype(jnp.int32)
    carry = carry_scr[...] + jnp.sum(ohf, axis=1, keepdims=True)
    carry_scr[...] = carry
    cnt_ref[...] = carry


def _route(logits_t, router_b):
    E, T = logits_t.shape
    tt = min(ROUTE_TT, T)
    tri = jnp.asarray(np.triu(np.ones((tt, tt), np.float32), k=1), dtype=BF16)
    pair = lambda dt: jax.ShapeDtypeStruct((TOP_K, T), dt)
    return pl.pallas_call(
        _route_kernel,
        grid=(T // tt,),
        in_specs=[
            pl.BlockSpec((E, tt), lambda i: (0, i)),
            pl.BlockSpec((E, 1), lambda i: (0, 0)),
            pl.BlockSpec((tt, tt), lambda i: (0, 0)),
        ],
        out_specs=[
            pl.BlockSpec((TOP_K, tt), lambda i: (0, i)),
            pl.BlockSpec((TOP_K, tt), lambda i: (0, i)),
            pl.BlockSpec((TOP_K, tt), lambda i: (0, i)),
            pl.BlockSpec((E, LANES), lambda i: (0, 0)),
        ],
        out_shape=[pair(jnp.int32), pair(jnp.int32), pair(F32), jax.ShapeDtypeStruct((E, LANES), F32)],
        scratch_shapes=[pltpu.VMEM((E, LANES), F32)],
        compiler_params=_cparams(("arbitrary",)),
        name="route",
    )(logits_t, router_b.reshape(E, 1), tri)


def _block_tables(counts, T):
    E = counts.shape[0]
    A = T * TOP_K
    B = MOE_ROWS
    seg = (counts + SUBLANES - 1) // SUBLANES * SUBLANES
    seg_end = jnp.cumsum(seg)
    seg_start = seg_end - seg
    nb = (A + E * (B - 1) + B - 1) // B
    nblk = (counts + B - 1) // B
    blk_end = jnp.cumsum(nblk)
    total = blk_end[-1]
    b = jnp.arange(nb, dtype=jnp.int32)
    bc = jnp.minimum(b, total - 1)
    blk_e = jnp.minimum(jnp.searchsorted(blk_end, bc, side="right"), E - 1).astype(jnp.int32)
    local = bc - (blk_end[blk_e] - nblk[blk_e])
    blk_start = (seg_start[blk_e] + local * B).astype(jnp.int32)
    blk_first = jnp.logical_and(b < total, local == 0).astype(jnp.int32)
    nonempty = counts > 0
    order = jnp.cumsum(nonempty.astype(jnp.int32)) - 1
    ids = jnp.where(nonempty, jnp.arange(E, dtype=jnp.int32), E)
    later = jnp.concatenate([lax.cummin(ids, reverse=True)[1:], jnp.full((1,), E, jnp.int32)])
    next_e = jnp.where(later < E, later, -1)
    blk_wslot = (order[blk_e] % 2).astype(jnp.int32)
    blk_next_e = next_e[blk_e].astype(jnp.int32)
    rows = A + E * (SUBLANES - 1) + B
    rows = (rows + SUBLANES - 1) // SUBLANES * SUBLANES
    blocks = (blk_e, blk_start, blk_first, blk_wslot, blk_next_e, total.astype(jnp.int32).reshape(1))
    return seg_start.astype(jnp.int32), seg_end.astype(jnp.int32), blocks, nb, rows


def _zero_rows_from(zero_ref, hbm_ref, start, sem):
    piece = zero_ref.shape[0]
    rows = hbm_ref.shape[0]

    def body(j, carry):
        at = pl.multiple_of(jnp.minimum(start + j * piece, rows - piece), SUBLANES)
        cp = pltpu.make_async_copy(zero_ref, hbm_ref.at[pl.ds(at, piece), :], sem)
        cp.start()
        cp.wait()
        return carry

    lax.fori_loop(0, (rows - start + piece - 1) // piece, body, 0)


def _dispatch_kernel(seg_end_ref, pos_ref, h_ref, xs_hbm, zero_scr, sem, zsem):
    i = pl.program_id(0)
    td = pos_ref.shape[1]
    E = seg_end_ref.shape[0]

    @pl.when(i == 0)
    def _():
        zero_scr[...] = jnp.zeros_like(zero_scr)

        def tail(e, carry):
            end = seg_end_ref[e]
            at = pl.multiple_of(jnp.maximum(end - SUBLANES, 0), SUBLANES)
            cp = pltpu.make_async_copy(zero_scr.at[pl.ds(0, SUBLANES), :],
                                       xs_hbm.at[pl.ds(at, SUBLANES), :], zsem)
            cp.start()
            cp.wait()
            return carry

        lax.fori_loop(0, E, tail, 0)
        _zero_rows_from(zero_scr, xs_hbm, seg_end_ref[E - 1], zsem)

    def body(r, carry):
        for k in range(TOP_K):
            pltpu.make_async_copy(h_ref.at[pl.ds(r, 1), :],
                                  xs_hbm.at[pl.ds(pos_ref[k, r], 1), :], sem).start(priority=k)
        return carry

    lax.fori_loop(0, td, body, 0, unroll=8)
    for k in range(TOP_K):
        pltpu.make_async_copy(h_ref, xs_hbm.at[pl.ds(0, td), :], sem).wait()


def _dispatch(h2, pos, seg_end, rows):
    T, D = h2.shape
    td = min(DISPATCH_TT, T)
    grid_spec = pltpu.PrefetchScalarGridSpec(
        num_scalar_prefetch=1,
        grid=(T // td,),
        in_specs=[pl.BlockSpec((TOP_K, td), lambda i, se: (0, i), memory_space=pltpu.SMEM),
                  pl.BlockSpec((td, D), lambda i, se: (i, 0))],
        out_specs=pl.BlockSpec(memory_space=pl.ANY),
        scratch_shapes=[pltpu.VMEM((MOE_ROWS, D), h2.dtype), pltpu.SemaphoreType.DMA(()),
                        pltpu.SemaphoreType.DMA(())],
    )
    return pl.pallas_call(
        _dispatch_kernel,
        grid_spec=grid_spec,
        out_shape=jax.ShapeDtypeStruct((rows, D), h2.dtype),
        compiler_params=_cparams(("arbitrary",), disable_bounds_checks=True),
        name="dispatch",
    )(seg_end, pos, h2)


def _experts_kernel(blk_e_ref, blk_start_ref, blk_first_ref, blk_wslot_ref, blk_next_ref, total_ref,
                    tail_ref, xs_hbm, wg_hbm, wu_hbm, wd_hbm, ys_hbm,
                    wg_f, wu_f, wd_f, wg_b, wu_b, wd_b, xbuf, ybuf, isem, osem, wsem, *, layer):
    b = pl.program_id(0)
    total = total_ref[0]
    B = xbuf.shape[1]
    slot = b % 2

    def load(blk, to_slot):
        at = pl.multiple_of(blk_start_ref[blk], SUBLANES)
        return pltpu.make_async_copy(xs_hbm.at[pl.ds(at, B), :], xbuf.at[to_slot], isem.at[to_slot])

    def store(blk, from_slot):
        at = pl.multiple_of(blk_start_ref[blk], SUBLANES)
        return pltpu.make_async_copy(ybuf.at[from_slot], ys_hbm.at[pl.ds(at, B), :], osem)

    def weights(e, ws):
        return [pltpu.make_async_copy(src.at[layer, e], dst.at[ws], wsem.at[ws])
                for src, dst in ((wg_hbm, wg_f), (wu_hbm, wu_f), (wd_hbm, wd_f))]

    @pl.when(b == 0)
    def _():
        load(0, 0).start()
        for cp in weights(blk_e_ref[0], blk_wslot_ref[0]):
            cp.start()
        ybuf[1] = jnp.zeros(ybuf.shape[1:], ybuf.dtype)
        _zero_rows_from(ybuf.at[1], ys_hbm, tail_ref[0], osem)

    @pl.when(b + 1 < total)
    def _():
        load(b + 1, 1 - slot).start()

    @pl.when(b < total)
    def _():
        @pl.when(blk_first_ref[b] == 1)
        def _():
            ws = blk_wslot_ref[b]
            for cp in weights(blk_e_ref[b], ws):
                cp.wait()
            wg_b[...] = wg_f[ws].astype(BF16)
            wu_b[...] = wu_f[ws].astype(BF16)
            wd_b[...] = wd_f[ws].astype(BF16)
            nxt = blk_next_ref[b]

            @pl.when(nxt >= 0)
            def _():
                for cp in weights(nxt, 1 - ws):
                    cp.start()

        load(b, slot).wait()
        lo, hi = _unpack_halves(xbuf[slot])
        lo, hi = lo.astype(BF16), hi.astype(BF16)
        half = lo.shape[1]
        g = (jnp.dot(lo, wg_b[:half, :], preferred_element_type=F32)
             + jnp.dot(hi, wg_b[half:, :], preferred_element_type=F32))
        u = (jnp.dot(lo, wu_b[:half, :], preferred_element_type=F32)
             + jnp.dot(hi, wu_b[half:, :], preferred_element_type=F32))
        a = (g * jax.nn.sigmoid(g) * u).astype(BF16)
        ybuf[slot] = _pack_halves(jnp.dot(a, wd_b[...], preferred_element_type=F32))

        @pl.when(b > 0)
        def _():
            store(b - 1, 1 - slot).wait()

        store(b, slot).start()

        @pl.when(b == total - 1)
        def _():
            store(b, slot).wait()


def _experts(xs, layer, w_gate, w_up, w_down, blocks, tail, nb):
    rows, DP = xs.shape
    D, DE = w_gate.shape[-2:]
    B = MOE_ROWS
    anywhere = pl.BlockSpec(memory_space=pl.ANY)
    grid_spec = pltpu.PrefetchScalarGridSpec(
        num_scalar_prefetch=len(blocks) + 1,
        grid=(nb,),
        in_specs=[anywhere, anywhere, anywhere, anywhere],
        out_specs=anywhere,
        scratch_shapes=[
            pltpu.VMEM((2, D, DE), F32), pltpu.VMEM((2, D, DE), F32), pltpu.VMEM((2, DE, D), F32),
            pltpu.VMEM((D, DE), BF16), pltpu.VMEM((D, DE), BF16), pltpu.VMEM((DE, D), BF16),
            pltpu.VMEM((2, B, DP), U32), pltpu.VMEM((2, B, DP), U32),
            pltpu.SemaphoreType.DMA((2,)), pltpu.SemaphoreType.DMA(()), pltpu.SemaphoreType.DMA((2,)),
        ],
    )
    return pl.pallas_call(
        functools.partial(_experts_kernel, layer=layer),
        grid_spec=grid_spec,
        out_shape=jax.ShapeDtypeStruct((rows, DP), U32),
        compiler_params=_cparams(("arbitrary",)),
        name="experts",
    )(*blocks, tail, xs, w_gate, w_up, w_down)


def _combine_kernel(pos_ref, posn_ref, x_ref, ys_hbm, w_ref, g2_ref, fg_ref, o_ref, gbuf, sem,
                    *, final):
    i = pl.program_id(0)
    n = pl.num_programs(0)
    tm = x_ref.shape[0]
    slot = i % 2

    def gather(p_ref, to_slot):
        def body(r, carry):
            for k in range(TOP_K):
                pltpu.make_async_copy(ys_hbm.at[pl.ds(p_ref[k, r], 1), :],
                                      gbuf.at[to_slot, k, pl.ds(r, 1), :], sem.at[to_slot]).start(priority=k)
            return carry

        lax.fori_loop(0, tm, body, 0, unroll=8)

    @pl.when(i == 0)
    def _():
        gather(pos_ref, 0)

    @pl.when(i + 1 < n)
    def _():
        gather(posn_ref, 1 - slot)

    for k in range(TOP_K):
        pltpu.make_async_copy(ys_hbm.at[pl.ds(0, tm), :], gbuf.at[slot, k], sem.at[slot]).wait()

    w = w_ref[...]
    D = x_ref.shape[1]
    half = D // 2
    lo0, hi0 = _unpack_halves(gbuf[slot, 0])
    lo1, hi1 = _unpack_halves(gbuf[slot, 1])
    xl = x_ref[:, :half] + g2_ref[:, :half] * (lo0 * w[:, 0:1] + lo1 * w[:, 1:2])
    xh = x_ref[:, half:] + g2_ref[:, half:] * (hi0 * w[:, 0:1] + hi1 * w[:, 1:2])
    if final:
        ms = (jnp.sum(xl * xl, axis=-1, keepdims=True) + jnp.sum(xh * xh, axis=-1, keepdims=True)) / D
        r = lax.rsqrt(ms + EPS)
        xl = xl * r * fg_ref[:, :half]
        xh = xh * r * fg_ref[:, half:]
    o_ref[:, :half] = xl
    o_ref[:, half:] = xh


def _combine(x, ys, pos, top_w, g2, final_g, final):
    T, D = x.shape
    tm = min(COMB_TM, T)
    n = T // tm
    vec = pl.BlockSpec((1, D), lambda i: (0, 0))
    return pl.pallas_call(
        functools.partial(_combine_kernel, final=final),
        grid=(n,),
        in_specs=[
            pl.BlockSpec((TOP_K, tm), lambda i: (0, i), memory_space=pltpu.SMEM),
            pl.BlockSpec((TOP_K, tm), lambda i: (0, jnp.minimum(i + 1, n - 1)), memory_space=pltpu.SMEM),
            pl.BlockSpec((tm, D), lambda i: (i, 0)),
            pl.BlockSpec(memory_space=pl.ANY),
            pl.BlockSpec((tm, TOP_K), lambda i: (i, 0)),
            vec, vec,
        ],
        out_specs=pl.BlockSpec((tm, D), lambda i: (i, 0)),
        out_shape=jax.ShapeDtypeStruct((T, D), F32),
        scratch_shapes=[pltpu.VMEM((2, TOP_K, tm, ys.shape[1]), ys.dtype), pltpu.SemaphoreType.DMA((2,))],
        compiler_params=_cparams(("arbitrary",), disable_bounds_checks=True),
        name="combine",
    )(pos, pos, x, ys, top_w, g2, final_g)


def kernel(x, c, ada_w, ada_b, norm1_g, norm2_g, w_in, conv_w, ret_gn_g, fox_fb, w_out,
           router_w, router_b, moe_w_gate, moe_w_up, moe_w_down, final_g):
    Bsz, S, D = x.shape
    assert Bsz == 1, "one sequence per call"
    L = ada_w.shape[0]
    T = S
    conv_wd = conv_w.shape[-1]
    ret_qk = RET_HEADS * RET_DK
    ret_v = RET_HEADS * RET_DV
    fox_w = FOX_HEADS * FOX_DH
    tn = INPROJ_TN
    ret_col0 = 3 * conv_wd
    fox_col0 = ret_col0 + 2 * ret_qk + 2 * ret_v
    n_main = fox_col0 + 3 * fox_w
    assert fox_w == tn and fox_col0 % tn == 0
    q_block = fox_col0 // tn
    v_block = q_block + 2
    k_col0 = fox_col0

    mod = _ada_mod(c, ada_w, ada_b)
    cos_t, sin_t = _rope_tables(T)
    rw_t = router_w.T
    xt = x.reshape(T, D)

    for l in range(L):
        sh1, sc1, g1, sh2, sc2, g2 = [mod[l, :, k * D:(k + 1) * D] for k in range(6)]
        w_ff = jnp.pad(w_in[l, :, n_main:], ((0, 0), (0, LANES - FOX_HEADS)))
        fb = jnp.pad(fox_fb[l], (0, LANES - FOX_HEADS)).reshape(1, LANES)
        proj, proj_t, cum = _inproj(xt, norm1_g[l].reshape(1, D), sh1, sc1, w_in, l, w_ff, fb,
                                    q_block, v_block, LOG2E * FOX_DH ** -0.5)

        y_conv = _conv(proj, conv_w[l])
        y_ret = _retention(proj, cos_t, sin_t, ret_gn_g[l].reshape(1, ret_v), ret_col0)
        y_fox = _fox(proj, proj_t, cum, k_col0)

        wo = w_out[l].astype(BF16)
        xt, h2, logits_t = _outproj(
            y_conv, y_ret, y_fox, wo[:conv_wd], wo[conv_wd:conv_wd + ret_v], wo[conv_wd + ret_v:],
            xt, g1, norm2_g[l].reshape(1, D), sh2, sc2, rw_t)

        top_e, rank, top_w, cnt = _route(logits_t, router_b)
        counts = cnt[:, 0].astype(jnp.int32)
        seg_start, seg_end, blocks, nb, rows = _block_tables(counts, T)
        hit = top_e[None] == jnp.arange(N_EXPERTS, dtype=jnp.int32)[:, None, None]
        pos = rank + jnp.sum(jnp.where(hit, seg_start[:, None, None], 0), axis=0)
        xs = _dispatch(h2, pos, seg_end, rows)
        ys = _experts(xs, l, moe_w_gate, moe_w_up, moe_w_down, blocks, seg_end[-1:], nb)
        xt = _combine(xt, ys, pos, top_w.T, g2, final_g.reshape(1, D), final=(l == L - 1))

    return xt.reshape(Bsz, S, D)
```

```python
import functools
import math

import numpy as np
import jax
import jax.numpy as jnp
from jax import lax
from jax.experimental import pallas as pl
from jax.experimental.pallas import tpu as pltpu

F32 = jnp.float32
BF16 = jnp.bfloat16

CHUNK = 64
CONV_K = 3
RET_HEADS = 6
RET_DK = 64
RET_DV = 128
FOX_HEADS = 6
FOX_DH = 128
N_EXPERTS = 64
N_GROUPS = 8
GROUP_SIZE = N_EXPERTS // N_GROUPS
TOP_K = 2
ROPE_BASE = 10000.0
EPS = 1e-6
LOG2E = 1.4426950408889634

LANES = 128
SUBLANES = 8

MOD_TN = 1024
INPROJ_TM = 1024
INPROJ_TN = 768
CONV_TM = 1024
CONV_HALO = 16
RET_ROWS = 256
FOX_TQ = 1024
FOX_SLAB = 16
FOX_SKIP_LOG2 = 160.0
FOX_NORM_SLACK = 1.02
OUT_TM = 512
MOE_ROWS = 256
ROUTE_TT = 1024
DISPATCH_TT = 512
COMB_TM = 512
VMEM_LIMIT = 56 * 1024 * 1024


def _cparams(sem, **kw):
    return pltpu.CompilerParams(dimension_semantics=sem, vmem_limit_bytes=VMEM_LIMIT, **kw)


U32 = jnp.uint32


def _pack_halves(x):
    half = x.shape[1] // 2
    xb = x.astype(BF16).astype(F32)
    lo = lax.bitcast_convert_type(xb[:, :half], U32)
    hi = lax.bitcast_convert_type(xb[:, half:], U32)
    return lax.shift_right_logical(lo, U32(16)) | (hi & U32(0xFFFF0000))


def _unpack_halves(w):
    lo = lax.bitcast_convert_type(lax.shift_left(w, U32(16)), F32)
    hi = lax.bitcast_convert_type(w & U32(0xFFFF0000), F32)
    return lo, hi


def _mod_kernel(c_ref, w_ref, b_ref, o_ref):
    c = c_ref[...]
    ca = c * jax.nn.sigmoid(c)
    o_ref[0] = jnp.sum(w_ref[0] * ca, axis=0, keepdims=True) + b_ref[0]


def _ada_mod(c, ada_w, ada_b):
    L, D, N = ada_w.shape
    tn = MOD_TN
    return pl.pallas_call(
        _mod_kernel,
        grid=(L, N // tn),
        in_specs=[
            pl.BlockSpec((D, 1), lambda l, j: (0, 0)),
            pl.BlockSpec((1, D, tn), lambda l, j: (l, 0, j)),
            pl.BlockSpec((1, 1, tn), lambda l, j: (l, 0, j)),
        ],
        out_specs=pl.BlockSpec((1, 1, tn), lambda l, j: (l, 0, j)),
        out_shape=jax.ShapeDtypeStruct((L, 1, N), F32),
        compiler_params=_cparams(("arbitrary", "arbitrary")),
        name="ada_mod",
    )(c.reshape(D, 1), ada_w, ada_b.reshape(L, 1, N))


def _modulated_norm(x, g, sh, sc):
    ms = jnp.mean(x * x, axis=-1, keepdims=True)
    y = x * lax.rsqrt(ms + EPS) * g
    return y * (1.0 + sc) + sh


def _log_sigmoid(x):
    return jnp.minimum(x, 0.0) - jnp.log1p(jnp.exp(-jnp.abs(x)))


def _inproj_kernel(x_ref, g_ref, sh_ref, sc_ref, w_ref, wff_ref, fb_ref,
                   proj_ref, projt_ref, cum_ref, h_scr, wt_scr, carry_scr,
                   *, q_block, v_block, q_scale):
    i = pl.program_id(0)
    j = pl.program_id(1)
    tm = x_ref.shape[0]

    @pl.when(jnp.logical_and(i == 0, j == 0))
    def _():
        carry_scr[...] = jnp.zeros_like(carry_scr)

    @pl.when(j == 0)
    def _():
        h = _modulated_norm(x_ref[...], g_ref[...], sh_ref[...], sc_ref[...])
        hb = h.astype(BF16)
        h_scr[...] = hb
        ff = jnp.dot(hb, wff_ref[...].astype(BF16), preferred_element_type=F32) + fb_ref[...]
        c = _log_sigmoid(ff)
        row = lax.broadcasted_iota(jnp.int32, c.shape, 0)
        d = 1
        while d < tm:
            c = c + jnp.where(row >= d, pltpu.roll(c, d, axis=0), 0.0)
            d *= 2
        c = c + carry_scr[...]
        cum_ref[...] = c
        carry_scr[...] = c[tm - 1:tm, :]

    @pl.when(jnp.logical_and(j != q_block, j != v_block))
    def _():
        proj_ref[...] = jnp.dot(h_scr[...], w_ref[...].astype(BF16),
                                preferred_element_type=F32).astype(BF16)

    def feature_major(slot, scale):
        @pl.when(i == 0)
        def _():
            wt_scr[slot] = w_ref[...].T.astype(BF16)

        r = lax.dot_general(wt_scr[slot], h_scr[...], (((1,), (1,)), ((), ())),
                            preferred_element_type=F32)
        projt_ref[...] = (r * scale).astype(BF16)

    @pl.when(j == q_block)
    def _():
        feature_major(0, q_scale)

    @pl.when(j == v_block)
    def _():
        feature_major(1, 1.0)


def _inproj(x, g, sh, sc, w_in, layer, w_ff, fb, q_block, v_block, q_scale):
    T, D = x.shape
    tm, tn = min(INPROJ_TM, T), INPROJ_TN
    nj = v_block + 1
    is_q = lambda j: (j == q_block).astype(jnp.int32)
    is_v = lambda j: (j == v_block).astype(jnp.int32)
    w_col = lambda i, j: jnp.where(i == 0, j, j - is_q(j) - is_v(j))
    out_col = lambda j: j - (j >= q_block).astype(jnp.int32) - (j >= v_block).astype(jnp.int32)
    vec = pl.BlockSpec((1, D), lambda i, j: (0, 0))
    return pl.pallas_call(
        functools.partial(_inproj_kernel, q_block=q_block, v_block=v_block, q_scale=q_scale),
        grid=(T // tm, nj),
        in_specs=[
            pl.BlockSpec((tm, D), lambda i, j: (i, 0), pipeline_mode=pl.Buffered(1)),
            vec, vec, vec,
            pl.BlockSpec((None, D, tn), lambda i, j: (layer, 0, w_col(i, j))),
            pl.BlockSpec((D, LANES), lambda i, j: (0, 0)),
            pl.BlockSpec((1, LANES), lambda i, j: (0, 0)),
        ],
        out_specs=[
            pl.BlockSpec((tm, tn), lambda i, j: (i, out_col(j))),
            pl.BlockSpec((tn, tm), lambda i, j: (is_v(j), i)),
            pl.BlockSpec((tm, LANES), lambda i, j: (i, 0)),
        ],
        out_shape=[
            jax.ShapeDtypeStruct((T, (nj - 2) * tn), BF16),
            jax.ShapeDtypeStruct((2 * tn, T), BF16),
            jax.ShapeDtypeStruct((T, LANES), F32),
        ],
        scratch_shapes=[pltpu.VMEM((tm, D), BF16), pltpu.VMEM((2, tn, D), BF16),
                        pltpu.VMEM((1, LANES), F32)],
        compiler_params=_cparams(("arbitrary", "arbitrary")),
        name="inproj",
    )(x, g, sh, sc, w_in, w_ff, fb)


def _conv_kernel(cb_ref, cc_ref, cu_ref, hc_ref, hu_ref, w_ref, o_ref):
    i = pl.program_id(0)
    z = cc_ref[...].astype(F32) * cu_ref[...].astype(F32)
    zh = hc_ref[...].astype(F32) * hu_ref[...].astype(F32)
    zh = jnp.where(i > 0, zh, 0.0)
    hl = zh.shape[0]
    zm1 = zh[hl - 1:hl, :]
    zm2 = zh[hl - 2:hl - 1, :]
    row = lax.broadcasted_iota(jnp.int32, z.shape, 0)
    z1 = jnp.where(row == 0, zm1, pltpu.roll(z, 1, axis=0))
    z2 = jnp.where(row == 0, zm2, jnp.where(row == 1, zm1, pltpu.roll(z, 2, axis=0)))
    w = w_ref[...]
    y = z2 * w[0:1, :] + z1 * w[1:2, :] + z * w[2:3, :]
    o_ref[...] = (cb_ref[...].astype(F32) * y).astype(BF16)


def _conv(proj, conv_w):
    T = proj.shape[0]
    W = conv_w.shape[1]
    tm = min(CONV_TM, T)
    hb = tm // CONV_HALO
    halo = lambda c: pl.BlockSpec((CONV_HALO, W), lambda i: (jnp.maximum(i * hb - 1, 0), c))
    return pl.pallas_call(
        _conv_kernel,
        grid=(T // tm,),
        in_specs=[
            pl.BlockSpec((tm, W), lambda i: (i, 0)),
            pl.BlockSpec((tm, W), lambda i: (i, 1)),
            pl.BlockSpec((tm, W), lambda i: (i, 2)),
            halo(1), halo(2),
            pl.BlockSpec((CONV_K, W), lambda i: (0, 0)),
        ],
        out_specs=pl.BlockSpec((tm, W), lambda i: (i, 0)),
        out_shape=jax.ShapeDtypeStruct((T, W), BF16),
        compiler_params=_cparams(("arbitrary",)),
        name="conv",
    )(proj, proj, proj, proj, proj, conv_w)


def _ret_gammas():
    return [1.0 - 2.0 ** (-5.0 - h) for h in range(RET_HEADS)]


def _ret_tables(R):
    n = np.arange(R, dtype=np.float64)
    chunk = np.arange(R) // CHUNK
    allowed = chunk[None, :] <= chunk[:, None]
    dm, qd, kd = [], [], []
    for g in _ret_gammas():
        lg = math.log(g)
        dm.append(np.where(allowed, np.exp(lg * np.abs(n[:, None] - n[None, :])), 0.0))
        qd.append(np.broadcast_to(np.exp(lg * (n + 1.0))[:, None], (R, LANES)))
        kd.append(np.broadcast_to(np.exp(lg * (R - 1.0 - n))[:, None], (R, LANES)))
    f = lambda a: jnp.asarray(np.stack(a), dtype=F32)
    return f(dm), f(qd), f(kd)


def _rope_tables(T):
    half = RET_DK // 2
    inv = ROPE_BASE ** (-jnp.arange(half, dtype=F32) / half)
    ang = jnp.arange(T, dtype=F32)[:, None] * inv[None, :]
    cos, sin = jnp.cos(ang), jnp.sin(ang)
    reps = LANES // RET_DK
    cos_t = jnp.tile(jnp.concatenate([cos, cos], axis=1), (1, reps))
    sin_t = jnp.tile(jnp.concatenate([-sin, sin], axis=1), (1, reps))
    return cos_t, sin_t


def _ret_kernel(q_ref, k_ref, v_ref, g_ref, cos_ref, sin_ref, dm_ref, qd_ref, kd_ref, gn_ref,
                o_ref, s_scr):
    i = pl.program_id(0)
    R = q_ref.shape[0]

    @pl.when(i == 0)
    def _():
        s_scr[...] = jnp.zeros_like(s_scr)

    lane = lax.broadcasted_iota(jnp.int32, (R, LANES), 1)
    first_half = (lane % RET_DK) < (RET_DK // 2)
    low_head = lane < RET_DK
    cosv = cos_ref[...]
    sinv = sin_ref[...]
    c_dec = [g ** R for g in _ret_gammas()]

    def rot(t):
        swapped = jnp.where(first_half, pltpu.roll(t, LANES - RET_DK // 2, axis=1),
                            pltpu.roll(t, RET_DK // 2, axis=1))
        return t * cosv + swapped * sinv

    heads_per_vreg = LANES // RET_DK
    for p in range(RET_HEADS // heads_per_vreg):
        cols = slice(p * LANES, (p + 1) * LANES)
        qr = rot(q_ref[:, cols].astype(F32))
        kb = (rot(k_ref[:, cols].astype(F32)) * (RET_DK ** -0.5)).astype(BF16)
        for hh in range(heads_per_vreg):
            h = p * heads_per_vreg + hh
            hc = slice(h * RET_DV, (h + 1) * RET_DV)
            mask = low_head if hh == 0 else jnp.logical_not(low_head)
            qm = jnp.where(mask, qr, 0.0).astype(BF16)
            s = lax.dot_general(qm, kb, (((1,), (1,)), ((), ())), preferred_element_type=F32)
            s = s * dm_ref[h]
            v = v_ref[:, hc]
            o = jnp.dot(s.astype(BF16), v, preferred_element_type=F32)
            state = s_scr[h]
            o = o + jnp.dot(qm, state.astype(BF16), preferred_element_type=F32) * qd_ref[h]
            vd = (v.astype(F32) * kd_ref[h]).astype(BF16)
            kv = lax.dot_general(kb, vd, (((0,), (0,)), ((), ())), preferred_element_type=F32)
            s_scr[h] = state * c_dec[h] + kv
            mu = jnp.mean(o, axis=-1, keepdims=True)
            d = o - mu
            var = jnp.mean(d * d, axis=-1, keepdims=True)
            on = d * lax.rsqrt(var + EPS) * gn_ref[:, hc]
            gate = g_ref[:, hc].astype(F32)
            o_ref[:, hc] = (gate * jax.nn.sigmoid(gate) * on).astype(BF16)


def _retention(proj, cos_t, sin_t, gn_g, col0):
    T = proj.shape[0]
    R = min(RET_ROWS, T)
    QK = RET_HEADS * RET_DK
    V = RET_HEADS * RET_DV
    dm, qd, kd = _ret_tables(R)
    q_blk = col0 // QK
    v_blk = (col0 + 2 * QK) // V
    full3 = lambda a: pl.BlockSpec(a.shape, lambda i: (0, 0, 0))
    return pl.pallas_call(
        _ret_kernel,
        grid=(T // R,),
        in_specs=[
            pl.BlockSpec((R, QK), lambda i: (i, q_blk)),
            pl.BlockSpec((R, QK), lambda i: (i, q_blk + 1)),
            pl.BlockSpec((R, V), lambda i: (i, v_blk)),
            pl.BlockSpec((R, V), lambda i: (i, v_blk + 1)),
            pl.BlockSpec((R, LANES), lambda i: (i, 0)),
            pl.BlockSpec((R, LANES), lambda i: (i, 0)),
            full3(dm), full3(qd), full3(kd),
            pl.BlockSpec((1, V), lambda i: (0, 0)),
        ],
        out_specs=pl.BlockSpec((R, V), lambda i: (i, 0)),
        out_shape=jax.ShapeDtypeStruct((T, V), BF16),
        scratch_shapes=[pltpu.VMEM((RET_HEADS, LANES, RET_DV), F32)],
        compiler_params=_cparams(("arbitrary",)),
        name="retention",
    )(proj, proj, proj, proj, cos_t, sin_t, dm, qd, kd, gn_g)


def _fox_kernel(qt_ref, k_ref, vt_ref, cum_ref, o_ref, ka_scr, cend_scr, kn_scr,
                s0_scr, s1_scr, p0_scr, p1_scr, a0_scr, a1_scr, m_scr, l_scr, acc_scr):
    h = pl.program_id(0)
    qi = pl.program_id(1)
    tq = qt_ref.shape[1]
    tk = s0_scr.shape[0]
    T = k_ref.shape[0]
    assert tq == 2 * tk
    s_scr, p_scr, a_scr = (s0_scr, s1_scr), (p0_scr, p1_scr), (a0_scr, a1_scr)

    @pl.when(qi == 0)
    def _():
        cend_scr[...] = jnp.zeros_like(cend_scr)
        ones = jnp.ones((FOX_DH, LANES), BF16)
        kn2 = jnp.zeros((SUBLANES, LANES), F32)
        for r in range(0, T, tq):
            kf = k_ref[r:r + tq, :].astype(F32)
            n2 = jnp.dot((kf * kf).astype(BF16), ones, preferred_element_type=F32)
            kn2 = jnp.maximum(kn2, jnp.max(n2.reshape(tq // SUBLANES, SUBLANES, LANES), axis=0))
        kn_scr[...] = jnp.max(kn2, axis=0, keepdims=True)

    lane = lax.broadcasted_iota(jnp.int32, cum_ref.shape, 1)
    col = jnp.sum(jnp.where(lane == h, cum_ref[...], 0.0), axis=1, keepdims=True) * LOG2E
    q0 = pl.multiple_of(qi * tq, tq)
    ck_rep = jnp.broadcast_to(col, cum_ref.shape)
    bf = lambda v: v.astype(BF16).astype(F32)
    c_hi = bf(col)
    c_mid = bf(col - c_hi)
    c_lo = bf(col - c_hi - c_mid)
    extra = jnp.where(lane == 0, c_hi, jnp.where(lane == 1, c_mid, jnp.where(lane == 2, c_lo, 0.0)))
    ka_scr[pl.ds(q0, tq), 0:FOX_DH] = k_ref[pl.ds(q0, tq), :]
    ka_scr[pl.ds(q0, tq), FOX_DH:2 * FOX_DH] = extra.astype(BF16)
    sub = lax.broadcasted_iota(jnp.int32, (FOX_DH, tq), 0)
    q_ext = jnp.concatenate([qt_ref[...], jnp.where(sub < 3, -1.0, 0.0).astype(BF16)], axis=0)
    cend_scr[pl.ds(2 * qi, 1), :] = ck_rep[tk - 1:tk, :]
    cend_scr[pl.ds(2 * qi + 1, 1), :] = ck_rep[tq - 1:tq, :]

    qf = qt_ref[...].astype(F32)
    qn2 = jnp.max(jnp.sum(qf * qf, axis=0, keepdims=True), axis=1, keepdims=True)
    reach = 2.0 * FOX_NORM_SLACK * jnp.sqrt(qn2 * kn_scr[...]) + FOX_SKIP_LOG2
    decay = cend_scr[...] - ck_rep[0:1, :]
    chunk_id = lax.broadcasted_iota(jnp.int32, decay.shape, 0)
    dead = jnp.logical_and(decay >= reach, chunk_id < 2 * qi)
    n_dead = jnp.max(jnp.sum(dead.astype(jnp.int32), axis=0, keepdims=True))
    first_pair = n_dead // 2

    m_scr[...] = jnp.full_like(m_scr, -jnp.inf)
    l_scr[...] = jnp.zeros_like(l_scr)
    acc_scr[...] = jnp.zeros_like(acc_scr)
    p1_scr[...] = jnp.zeros_like(p1_scr)
    a1_scr[...] = jnp.ones_like(a1_scr)

    def score(c, slot):
        k0 = pl.multiple_of(c * tk, tk)
        s_scr[slot][...] = jnp.dot(ka_scr[pl.ds(k0, tk), :], q_ext, preferred_element_type=F32)

    def softmax(c, slot, masked):
        if masked:
            s = s_scr[slot][...]
            kpos = c * tk + lax.broadcasted_iota(jnp.int32, s.shape, 0)
            qpos = q0 + lax.broadcasted_iota(jnp.int32, s.shape, 1)
            s_scr[slot][...] = jnp.where(kpos <= qpos, s, -jnp.inf)
        sref, pref = s_scr[slot], p_scr[slot]
        mx = sref[0:FOX_SLAB, :]
        for r in range(FOX_SLAB, tk, FOX_SLAB):
            mx = jnp.maximum(mx, sref[r:r + FOX_SLAB, :])
        m_old = m_scr[...]
        m_new = jnp.maximum(m_old, jnp.max(mx, axis=0, keepdims=True))
        alpha = jnp.exp2(m_old - m_new)
        m_rows = jnp.broadcast_to(m_new, (FOX_SLAB, tq))
        psum = jnp.zeros((FOX_SLAB, tq), F32)
        for r in range(0, tk, FOX_SLAB):
            p = jnp.exp2(sref[r:r + FOX_SLAB, :] - m_rows)
            psum = psum + p
            pref[r:r + FOX_SLAB, :] = p.astype(BF16)
        l_scr[...] = alpha * l_scr[...] + jnp.sum(psum, axis=0, keepdims=True)
        a_scr[slot][...] = alpha
        m_scr[...] = m_new

    def accumulate(c, slot):
        k0 = pl.multiple_of(jnp.maximum(c, 0) * tk, tk)
        pv = jnp.dot(vt_ref[:, pl.ds(k0, tk)], p_scr[slot][...], preferred_element_type=F32)
        acc_scr[...] = a_scr[slot][...] * acc_scr[...] + pv

    score(2 * first_pair, 0)

    def pair(i, carry):
        c = 2 * i
        score(c + 1, 1)
        accumulate(c - 1, 1)
        softmax(c, 0, False)
        score(c + 2, 0)
        accumulate(c, 0)
        softmax(c + 1, 1, False)
        return carry

    lax.fori_loop(first_pair, qi, pair, 0)
    c = 2 * qi
    score(c + 1, 1)
    accumulate(c - 1, 1)
    softmax(c, 0, True)
    accumulate(c, 0)
    softmax(c + 1, 1, True)
    accumulate(c + 1, 1)
    o_ref[...] = (acc_scr[...] / l_scr[...]).T.astype(BF16)


def _fox(proj, proj_t, cum, k_col0):
    T = proj.shape[0]
    tq = min(FOX_TQ, T)
    tk = tq // 2
    W = FOX_HEADS * FOX_DH
    kb = k_col0 // FOX_DH
    return pl.pallas_call(
        _fox_kernel,
        grid=(FOX_HEADS, T // tq),
        in_specs=[
            pl.BlockSpec((FOX_DH, tq), lambda h, i: (h, i)),
            pl.BlockSpec((T, FOX_DH), lambda h, i: (0, kb + h)),
            pl.BlockSpec((FOX_DH, T), lambda h, i: (FOX_HEADS + h, 0)),
            pl.BlockSpec((tq, LANES), lambda h, i: (i, 0)),
        ],
        out_specs=pl.BlockSpec((tq, FOX_DH), lambda h, i: (i, h)),
        out_shape=jax.ShapeDtypeStruct((T, W), BF16),
        scratch_shapes=[pltpu.VMEM((T, 2 * FOX_DH), BF16),
                        pltpu.VMEM((-(-(T // tk) // SUBLANES) * SUBLANES, LANES), F32),
                        pltpu.VMEM((1, LANES), F32),
                        pltpu.VMEM((tk, tq), F32), pltpu.VMEM((tk, tq), F32),
                        pltpu.VMEM((tk, tq), BF16), pltpu.VMEM((tk, tq), BF16),
                        pltpu.VMEM((1, tq), F32), pltpu.VMEM((1, tq), F32),
                        pltpu.VMEM((1, tq), F32), pltpu.VMEM((1, tq), F32),
                        pltpu.VMEM((FOX_DH, tq), F32)],
        compiler_params=_cparams(("arbitrary", "arbitrary")),
        name="fox",
    )(proj_t, proj, proj_t, cum)


def _outproj_kernel(yc_ref, yr_ref, yf_ref, wc_ref, wr_ref, wf_ref, x_ref, g1_ref,
                    g_ref, sh_ref, sc_ref, rwh_ref, rwl_ref, xo_ref, h_ref, lg_ref):
    mix = jnp.dot(yc_ref[...], wc_ref[...], preferred_element_type=F32)
    mix = mix + jnp.dot(yr_ref[...], wr_ref[...], preferred_element_type=F32)
    mix = mix + jnp.dot(yf_ref[...], wf_ref[...], preferred_element_type=F32)
    x = x_ref[...] + g1_ref[...] * mix
    xo_ref[...] = x
    h = _modulated_norm(x, g_ref[...], sh_ref[...], sc_ref[...])
    h_ref[...] = _pack_halves(h)
    h_hi = h.astype(BF16)
    h_lo = (h - h_hi.astype(F32)).astype(BF16)
    nt = lambda a, b: lax.dot_general(a, b, (((1,), (1,)), ((), ())), preferred_element_type=F32)
    lg_ref[...] = nt(rwh_ref[...], h_hi) + (nt(rwh_ref[...], h_lo) + nt(rwl_ref[...], h_hi))


def _outproj(yc, yr, yf, wc, wr, wf, x, g1, g, sh, sc, rw_t):
    T, D = x.shape
    tm = min(OUT_TM, T)
    E = rw_t.shape[0]
    rw_hi = rw_t.astype(BF16)
    rw_lo = (rw_t - rw_hi.astype(F32)).astype(BF16)
    vec = pl.BlockSpec((1, D), lambda i: (0, 0))
    rows = lambda a: pl.BlockSpec((tm, a.shape[1]), lambda i: (i, 0))
    whole = lambda a: pl.BlockSpec(a.shape, lambda i: (0, 0))
    return pl.pallas_call(
        _outproj_kernel,
        grid=(T // tm,),
        in_specs=[rows(yc), rows(yr), rows(yf), whole(wc), whole(wr), whole(wf), rows(x),
                  vec, vec, vec, vec, whole(rw_hi), whole(rw_lo)],
        out_specs=[rows(x), pl.BlockSpec((tm, D // 2), lambda i: (i, 0)),
                   pl.BlockSpec((E, tm), lambda i: (0, i))],
        out_shape=[jax.ShapeDtypeStruct((T, D), F32), jax.ShapeDtypeStruct((T, D // 2), U32),
                   jax.ShapeDtypeStruct((E, T), F32)],
        compiler_params=_cparams(("arbitrary",)),
        name="outproj",
    )(yc, yr, yf, wc, wr, wf, x, g1, g, sh, sc, rw_hi, rw_lo)


def _route_kernel(lg_ref, b_ref, tri_ref, e_ref, r_ref, w_ref, cnt_ref, carry_scr):
    i = pl.program_id(0)
    E, tt = lg_ref.shape

    @pl.when(i == 0)
    def _():
        carry_scr[...] = jnp.zeros_like(carry_scr)

    aff = jax.nn.sigmoid(lg_ref[...])
    sel = aff + b_ref[...]
    row8 = lax.broadcasted_iota(jnp.int32, (GROUP_SIZE, tt), 0)
    best = None
    for g in range(N_GROUPS):
        slab = sel[g * GROUP_SIZE:(g + 1) * GROUP_SIZE, :]
        m1 = jnp.max(slab, axis=0, keepdims=True)
        i1 = jnp.min(jnp.where(slab == m1, row8, GROUP_SIZE), axis=0, keepdims=True)
        rest = jnp.where(row8 == i1, -jnp.inf, slab)
        m2 = jnp.max(rest, axis=0, keepdims=True)
        i2 = jnp.min(jnp.where(rest == m2, row8, GROUP_SIZE), axis=0, keepdims=True)
        cand = (m1 + m2, g * GROUP_SIZE + i1, g * GROUP_SIZE + i2)
        if best is None:
            best = cand
        else:
            upd = cand[0] > best[0]
            best = tuple(jnp.where(upd, n, o) for n, o in zip(cand, best))
    _, e0, e1 = best

    row = lax.broadcasted_iota(jnp.int32, (E, tt), 0)
    oh0 = row == e0
    oh1 = row == e1
    a0 = jnp.sum(jnp.where(oh0, aff, 0.0), axis=0, keepdims=True)
    a1 = jnp.sum(jnp.where(oh1, aff, 0.0), axis=0, keepdims=True)
    w_ref[0:1, :] = a0 / (a0 + a1)
    w_ref[1:2, :] = a1 / (a0 + a1)
    e_ref[0:1, :] = e0
    e_ref[1:2, :] = e1

    oh = jnp.logical_or(oh0, oh1)
    ohf = jnp.where(oh, 1.0, 0.0)
    before = jnp.dot(ohf.astype(BF16), tri_ref[...], preferred_element_type=F32) + carry_scr[:, 0:1]
    r_ref[0:1, :] = jnp.sum(jnp.where(oh0, before, 0.0), axis=0, keepdims=True).astype(jnp.int32)
    r_ref[1:2, :] = jnp.sum(jnp.where(oh1, before, 0.0), axis=0, keepdims=True).astype(jnp.int32)
    carry = carry_scr[...] + jnp.sum(ohf, axis=1, keepdims=True)
    carry_scr[...] = carry
    cnt_ref[...] = carry


def _route(logits_t, router_b):
    E, T = logits_t.shape
    tt = min(ROUTE_TT, T)
    tri = jnp.asarray(np.triu(np.ones((tt, tt), np.float32), k=1), dtype=BF16)
    pair = lambda dt: jax.ShapeDtypeStruct((TOP_K, T), dt)
    return pl.pallas_call(
        _route_kernel,
        grid=(T // tt,),
        in_specs=[
            pl.BlockSpec((E, tt), lambda i: (0, i)),
            pl.BlockSpec((E, 1), lambda i: (0, 0)),
            pl.BlockSpec((tt, tt), lambda i: (0, 0)),
        ],
        out_specs=[
            pl.BlockSpec((TOP_K, tt), lambda i: (0, i)),
            pl.BlockSpec((TOP_K, tt), lambda i: (0, i)),
            pl.BlockSpec((TOP_K, tt), lambda i: (0, i)),
            pl.BlockSpec((E, LANES), lambda i: (0, 0)),
        ],
        out_shape=[pair(jnp.int32), pair(jnp.int32), pair(F32), jax.ShapeDtypeStruct((E, LANES), F32)],
        scratch_shapes=[pltpu.VMEM((E, LANES), F32)],
        compiler_params=_cparams(("arbitrary",)),
        name="route",
    )(logits_t, router_b.reshape(E, 1), tri)


def _block_tables(counts, T):
    E = counts.shape[0]
    A = T * TOP_K
    B = MOE_ROWS
    seg = (counts + SUBLANES - 1) // SUBLANES * SUBLANES
    seg_end = jnp.cumsum(seg)
    seg_start = seg_end - seg
    nb = (A + E * (B - 1) + B - 1) // B
    nblk = (counts + B - 1) // B
    blk_end = jnp.cumsum(nblk)
    total = blk_end[-1]
    b = jnp.arange(nb, dtype=jnp.int32)
    bc = jnp.minimum(b, total - 1)
    blk_e = jnp.minimum(jnp.searchsorted(blk_end, bc, side="right"), E - 1).astype(jnp.int32)
    local = bc - (blk_end[blk_e] - nblk[blk_e])
    blk_start = (seg_start[blk_e] + local * B).astype(jnp.int32)
    blk_first = jnp.logical_and(b < total, local == 0).astype(jnp.int32)
    nonempty = counts > 0
    order = jnp.cumsum(nonempty.astype(jnp.int32)) - 1
    ids = jnp.where(nonempty, jnp.arange(E, dtype=jnp.int32), E)
    later = jnp.concatenate([lax.cummin(ids, reverse=True)[1:], jnp.full((1,), E, jnp.int32)])
    next_e = jnp.where(later < E, later, -1)
    blk_wslot = (order[blk_e] % 2).astype(jnp.int32)
    blk_next_e = next_e[blk_e].astype(jnp.int32)
    rows = A + E * (SUBLANES - 1) + B
    rows = (rows + SUBLANES - 1) // SUBLANES * SUBLANES
    blocks = (blk_e, blk_start, blk_first, blk_wslot, blk_next_e, total.astype(jnp.int32).reshape(1))
    return seg_start.astype(jnp.int32), seg_end.astype(jnp.int32), blocks, nb, rows


def _zero_rows_from(zero_ref, hbm_ref, start, sem):
    piece = zero_ref.shape[0]
    rows = hbm_ref.shape[0]

    def body(j, carry):
        at = pl.multiple_of(jnp.minimum(start + j * piece, rows - piece), SUBLANES)
        cp = pltpu.make_async_copy(zero_ref, hbm_ref.at[pl.ds(at, piece), :], sem)
        cp.start()
        cp.wait()
        return carry

    lax.fori_loop(0, (rows - start + piece - 1) // piece, body, 0)


def _dispatch_kernel(seg_end_ref, pos_ref, h_ref, xs_hbm, zero_scr, sem, zsem):
    i = pl.program_id(0)
    td = pos_ref.shape[1]
    E = seg_end_ref.shape[0]

    @pl.when(i == 0)
    def _():
        zero_scr[...] = jnp.zeros_like(zero_scr)

        def tail(e, carry):
            end = seg_end_ref[e]
            at = pl.multiple_of(jnp.maximum(end - SUBLANES, 0), SUBLANES)
            cp = pltpu.make_async_copy(zero_scr.at[pl.ds(0, SUBLANES), :],
                                       xs_hbm.at[pl.ds(at, SUBLANES), :], zsem)
            cp.start()
            cp.wait()
            return carry

        lax.fori_loop(0, E, tail, 0)
        _zero_rows_from(zero_scr, xs_hbm, seg_end_ref[E - 1], zsem)

    def body(r, carry):
        for k in range(TOP_K):
            pltpu.make_async_copy(h_ref.at[pl.ds(r, 1), :],
                                  xs_hbm.at[pl.ds(pos_ref[k, r], 1), :], sem).start(priority=k)
        return carry

    lax.fori_loop(0, td, body, 0, unroll=8)
    for k in range(TOP_K):
        pltpu.make_async_copy(h_ref, xs_hbm.at[pl.ds(0, td), :], sem).wait()


def _dispatch(h2, pos, seg_end, rows):
    T, D = h2.shape
    td = min(DISPATCH_TT, T)
    grid_spec = pltpu.PrefetchScalarGridSpec(
        num_scalar_prefetch=1,
        grid=(T // td,),
        in_specs=[pl.BlockSpec((TOP_K, td), lambda i, se: (0, i), memory_space=pltpu.SMEM),
                  pl.BlockSpec((td, D), lambda i, se: (i, 0))],
        out_specs=pl.BlockSpec(memory_space=pl.ANY),
        scratch_shapes=[pltpu.VMEM((MOE_ROWS, D), h2.dtype), pltpu.SemaphoreType.DMA(()),
                        pltpu.SemaphoreType.DMA(())],
    )
    return pl.pallas_call(
        _dispatch_kernel,
        grid_spec=grid_spec,
        out_shape=jax.ShapeDtypeStruct((rows, D), h2.dtype),
        compiler_params=_cparams(("arbitrary",), disable_bounds_checks=True),
        name="dispatch",
    )(seg_end, pos, h2)


def _experts_kernel(blk_e_ref, blk_start_ref, blk_first_ref, blk_wslot_ref, blk_next_ref, total_ref,
                    tail_ref, xs_hbm, wg_hbm, wu_hbm, wd_hbm, ys_hbm,
                    wg_f, wu_f, wd_f, wg_b, wu_b, wd_b, xbuf, ybuf, isem, osem, wsem, *, layer):
    b = pl.program_id(0)
    total = total_ref[0]
    B = xbuf.shape[1]
    slot = b % 2

    def load(blk, to_slot):
        at = pl.multiple_of(blk_start_ref[blk], SUBLANES)
        return pltpu.make_async_copy(xs_hbm.at[pl.ds(at, B), :], xbuf.at[to_slot], isem.at[to_slot])

    def store(blk, from_slot):
        at = pl.multiple_of(blk_start_ref[blk], SUBLANES)
        return pltpu.make_async_copy(ybuf.at[from_slot], ys_hbm.at[pl.ds(at, B), :], osem)

    def weights(e, ws):
        return [pltpu.make_async_copy(src.at[layer, e], dst.at[ws], wsem.at[ws])
                for src, dst in ((wg_hbm, wg_f), (wu_hbm, wu_f), (wd_hbm, wd_f))]

    @pl.when(b == 0)
    def _():
        load(0, 0).start()
        for cp in weights(blk_e_ref[0], blk_wslot_ref[0]):
            cp.start()
        ybuf[1] = jnp.zeros(ybuf.shape[1:], ybuf.dtype)
        _zero_rows_from(ybuf.at[1], ys_hbm, tail_ref[0], osem)

    @pl.when(b + 1 < total)
    def _():
        load(b + 1, 1 - slot).start()

    @pl.when(b < total)
    def _():
        @pl.when(blk_first_ref[b] == 1)
        def _():
            ws = blk_wslot_ref[b]
            for cp in weights(blk_e_ref[b], ws):
                cp.wait()
            wg_b[...] = wg_f[ws].astype(BF16)
            wu_b[...] = wu_f[ws].astype(BF16)
            wd_b[...] = wd_f[ws].astype(BF16)
            nxt = blk_next_ref[b]

            @pl.when(nxt >= 0)
            def _():
                for cp in weights(nxt, 1 - ws):
                    cp.start()

        load(b, slot).wait()
        lo, hi = _unpack_halves(xbuf[slot])
        lo, hi = lo.astype(BF16), hi.astype(BF16)
        half = lo.shape[1]
        g = (jnp.dot(lo, wg_b[:half, :], preferred_element_type=F32)
             + jnp.dot(hi, wg_b[half:, :], preferred_element_type=F32))
        u = (jnp.dot(lo, wu_b[:half, :], preferred_element_type=F32)
             + jnp.dot(hi, wu_b[half:, :], preferred_element_type=F32))
        a = (g * jax.nn.sigmoid(g) * u).astype(BF16)
        ybuf[slot] = _pack_halves(jnp.dot(a, wd_b[...], preferred_element_type=F32))

        @pl.when(b > 0)
        def _():
            store(b - 1, 1 - slot).wait()

        store(b, slot).start()

        @pl.when(b == total - 1)
        def _():
            store(b, slot).wait()


def _experts(xs, layer, w_gate, w_up, w_down, blocks, tail, nb):
    rows, DP = xs.shape
    D, DE = w_gate.shape[-2:]
    B = MOE_ROWS
    anywhere = pl.BlockSpec(memory_space=pl.ANY)
    grid_spec = pltpu.PrefetchScalarGridSpec(
        num_scalar_prefetch=len(blocks) + 1,
        grid=(nb,),
        in_specs=[anywhere, anywhere, anywhere, anywhere],
        out_specs=anywhere,
        scratch_shapes=[
            pltpu.VMEM((2, D, DE), F32), pltpu.VMEM((2, D, DE), F32), pltpu.VMEM((2, DE, D), F32),
            pltpu.VMEM((D, DE), BF16), pltpu.VMEM((D, DE), BF16), pltpu.VMEM((DE, D), BF16),
            pltpu.VMEM((2, B, DP), U32), pltpu.VMEM((2, B, DP), U32),
            pltpu.SemaphoreType.DMA((2,)), pltpu.SemaphoreType.DMA(()), pltpu.SemaphoreType.DMA((2,)),
        ],
    )
    return pl.pallas_call(
        functools.partial(_experts_kernel, layer=layer),
        grid_spec=grid_spec,
        out_shape=jax.ShapeDtypeStruct((rows, DP), U32),
        compiler_params=_cparams(("arbitrary",)),
        name="experts",
    )(*blocks, tail, xs, w_gate, w_up, w_down)


def _combine_kernel(pos_ref, posn_ref, x_ref, ys_hbm, w_ref, g2_ref, fg_ref, o_ref, gbuf, sem,
                    *, final):
    i = pl.program_id(0)
    n = pl.num_programs(0)
    tm = x_ref.shape[0]
    slot = i % 2

    def gather(p_ref, to_slot):
        def body(r, carry):
            for k in range(TOP_K):
                pltpu.make_async_copy(ys_hbm.at[pl.ds(p_ref[k, r], 1), :],
                                      gbuf.at[to_slot, k, pl.ds(r, 1), :], sem.at[to_slot]).start(priority=k)
            return carry

        lax.fori_loop(0, tm, body, 0, unroll=8)

    @pl.when(i == 0)
    def _():
        gather(pos_ref, 0)

    @pl.when(i + 1 < n)
    def _():
        gather(posn_ref, 1 - slot)

    for k in range(TOP_K):
        pltpu.make_async_copy(ys_hbm.at[pl.ds(0, tm), :], gbuf.at[slot, k], sem.at[slot]).wait()

    w = w_ref[...]
    D = x_ref.shape[1]
    half = D // 2
    lo0, hi0 = _unpack_halves(gbuf[slot, 0])
    lo1, hi1 = _unpack_halves(gbuf[slot, 1])
    xl = x_ref[:, :half] + g2_ref[:, :half] * (lo0 * w[:, 0:1] + lo1 * w[:, 1:2])
    xh = x_ref[:, half:] + g2_ref[:, half:] * (hi0 * w[:, 0:1] + hi1 * w[:, 1:2])
    if final:
        ms = (jnp.sum(xl * xl, axis=-1, keepdims=True) + jnp.sum(xh * xh, axis=-1, keepdims=True)) / D
        r = lax.rsqrt(ms + EPS)
        xl = xl * r * fg_ref[:, :half]
        xh = xh * r * fg_ref[:, half:]
    o_ref[:, :half] = xl
    o_ref[:, half:] = xh


def _combine(x, ys, pos, top_w, g2, final_g, final):
    T, D = x.shape
    tm = min(COMB_TM, T)
    n = T // tm
    vec = pl.BlockSpec((1, D), lambda i: (0, 0))
    return pl.pallas_call(
        functools.partial(_combine_kernel, final=final),
        grid=(n,),
        in_specs=[
            pl.BlockSpec((TOP_K, tm), lambda i: (0, i), memory_space=pltpu.SMEM),
            pl.BlockSpec((TOP_K, tm), lambda i: (0, jnp.minimum(i + 1, n - 1)), memory_space=pltpu.SMEM),
            pl.BlockSpec((tm, D), lambda i: (i, 0)),
            pl.BlockSpec(memory_space=pl.ANY),
            pl.BlockSpec((tm, TOP_K), lambda i: (i, 0)),
            vec, vec,
        ],
        out_specs=pl.BlockSpec((tm, D), lambda i: (i, 0)),
        out_shape=jax.ShapeDtypeStruct((T, D), F32),
        scratch_shapes=[pltpu.VMEM((2, TOP_K, tm, ys.shape[1]), ys.dtype), pltpu.SemaphoreType.DMA((2,))],
        compiler_params=_cparams(("arbitrary",), disable_bounds_checks=True),
        name="combine",
    )(pos, pos, x, ys, top_w, g2, final_g)


def kernel(x, c, ada_w, ada_b, norm1_g, norm2_g, w_in, conv_w, ret_gn_g, fox_fb, w_out,
           router_w, router_b, moe_w_gate, moe_w_up, moe_w_down, final_g):
    Bsz, S, D = x.shape
    assert Bsz == 1, "one sequence per call"
    L = ada_w.shape[0]
    T = S
    conv_wd = conv_w.shape[-1]
    ret_qk = RET_HEADS * RET_DK
    ret_v = RET_HEADS * RET_DV
    fox_w = FOX_HEADS * FOX_DH
    tn = INPROJ_TN
    ret_col0 = 3 * conv_wd
    fox_col0 = ret_col0 + 2 * ret_qk + 2 * ret_v
    n_main = fox_col0 + 3 * fox_w
    assert fox_w == tn and fox_col0 % tn == 0
    q_block = fox_col0 // tn
    v_block = q_block + 2
    k_col0 = fox_col0

    mod = _ada_mod(c, ada_w, ada_b)
    cos_t, sin_t = _rope_tables(T)
    rw_t = router_w.T
    xt = x.reshape(T, D)

    for l in range(L):
        sh1, sc1, g1, sh2, sc2, g2 = [mod[l, :, k * D:(k + 1) * D] for k in range(6)]
        w_ff = jnp.pad(w_in[l, :, n_main:], ((0, 0), (0, LANES - FOX_HEADS)))
        fb = jnp.pad(fox_fb[l], (0, LANES - FOX_HEADS)).reshape(1, LANES)
        proj, proj_t, cum = _inproj(xt, norm1_g[l].reshape(1, D), sh1, sc1, w_in, l, w_ff, fb,
                                    q_block, v_block, LOG2E * FOX_DH ** -0.5)

        y_conv = _conv(proj, conv_w[l])
        y_ret = _retention(proj, cos_t, sin_t, ret_gn_g[l].reshape(1, ret_v), ret_col0)
        y_fox = _fox(proj, proj_t, cum, k_col0)

        wo = w_out[l].astype(BF16)
        xt, h2, logits_t = _outproj(
            y_conv, y_ret, y_fox, wo[:conv_wd], wo[conv_wd:conv_wd + ret_v], wo[conv_wd + ret_v:],
            xt, g1, norm2_g[l].reshape(1, D), sh2, sc2, rw_t)

        top_e, rank, top_w, cnt = _route(logits_t, router_b)
        counts = cnt[:, 0].astype(jnp.int32)
        seg_start, seg_end, blocks, nb, rows = _block_tables(counts, T)
        hit = top_e[None] == jnp.arange(N_EXPERTS, dtype=jnp.int32)[:, None, None]
        pos = rank + jnp.sum(jnp.where(hit, seg_start[:, None, None], 0), axis=0)
        xs = _dispatch(h2, pos, seg_end, rows)
        ys = _experts(xs, l, moe_w_gate, moe_w_up, moe_w_down, blocks, seg_end[-1:], nb)
        xt = _combine(xt, ys, pos, top_w.T, g2, final_g.reshape(1, D), final=(l == L - 1))

    return xt.reshape(Bsz, S, D)
```

```python
import functools
import math

import numpy as np
import jax
import jax.numpy as jnp
from jax import lax
from jax.experimental import pallas as pl
from jax.experimental.pallas import tpu as pltpu

F32 = jnp.float32
BF16 = jnp.bfloat16

CHUNK = 64
CONV_K = 3
RET_HEADS = 6
RET_DK = 64
RET_DV = 128
FOX_HEADS = 6
FOX_DH = 128
N_EXPERTS = 64
N_GROUPS = 8
GROUP_SIZE = N_EXPERTS // N_GROUPS
TOP_K = 2
ROPE_BASE = 10000.0
EPS = 1e-6
LOG2E = 1.4426950408889634

LANES = 128
SUBLANES = 8

MOD_TN = 1024
INPROJ_TM = 1024
INPROJ_TN = 768
CONV_TM = 1024
CONV_HALO = 16
RET_ROWS = 256
FOX_TQ = 1024
FOX_SLAB = 16
FOX_SKIP_LOG2 = 160.0
FOX_NORM_SLACK = 1.02
OUT_TM = 512
MOE_ROWS = 256
MOE_STEP = 64
ROUTE_TT = 1024
DISPATCH_TT = 512
COMB_TM = 512
VMEM_LIMIT = 56 * 1024 * 1024


def _cparams(sem, **kw):
    return pltpu.CompilerParams(dimension_semantics=sem, vmem_limit_bytes=VMEM_LIMIT, **kw)


U32 = jnp.uint32


def _pack_halves(x):
    half = x.shape[1] // 2
    xb = x.astype(BF16).astype(F32)
    lo = lax.bitcast_convert_type(xb[:, :half], U32)
    hi = lax.bitcast_convert_type(xb[:, half:], U32)
    return lax.shift_right_logical(lo, U32(16)) | (hi & U32(0xFFFF0000))


def _unpack_halves(w):
    lo = lax.bitcast_convert_type(lax.shift_left(w, U32(16)), F32)
    hi = lax.bitcast_convert_type(w & U32(0xFFFF0000), F32)
    return lo, hi


def _mod_kernel(c_ref, w_ref, b_ref, o_ref):
    c = c_ref[...]
    ca = c * jax.nn.sigmoid(c)
    o_ref[0] = jnp.sum(w_ref[0] * ca, axis=0, keepdims=True) + b_ref[0]


def _ada_mod(c, ada_w, ada_b):
    L, D, N = ada_w.shape
    tn = MOD_TN
    return pl.pallas_call(
        _mod_kernel,
        grid=(L, N // tn),
        in_specs=[
            pl.BlockSpec((D, 1), lambda l, j: (0, 0)),
            pl.BlockSpec((1, D, tn), lambda l, j: (l, 0, j)),
            pl.BlockSpec((1, 1, tn), lambda l, j: (l, 0, j)),
        ],
        out_specs=pl.BlockSpec((1, 1, tn), lambda l, j: (l, 0, j)),
        out_shape=jax.ShapeDtypeStruct((L, 1, N), F32),
        compiler_params=_cparams(("arbitrary", "arbitrary")),
        name="ada_mod",
    )(c.reshape(D, 1), ada_w, ada_b.reshape(L, 1, N))


def _modulated_norm(x, g, sh, sc):
    ms = jnp.mean(x * x, axis=-1, keepdims=True)
    y = x * lax.rsqrt(ms + EPS) * g
    return y * (1.0 + sc) + sh


def _log_sigmoid(x):
    return jnp.minimum(x, 0.0) - jnp.log1p(jnp.exp(-jnp.abs(x)))


def _inproj_kernel(x_ref, g_ref, sh_ref, sc_ref, w_ref, wff_ref, fb_ref,
                   proj_ref, projt_ref, cum_ref, h_scr, wt_scr, carry_scr,
                   *, q_block, v_block, q_scale):
    i = pl.program_id(0)
    j = pl.program_id(1)
    tm = x_ref.shape[0]

    @pl.when(jnp.logical_and(i == 0, j == 0))
    def _():
        carry_scr[...] = jnp.zeros_like(carry_scr)

    @pl.when(j == 0)
    def _():
        h = _modulated_norm(x_ref[...], g_ref[...], sh_ref[...], sc_ref[...])
        hb = h.astype(BF16)
        h_scr[...] = hb
        ff = jnp.dot(hb, wff_ref[...].astype(BF16), preferred_element_type=F32) + fb_ref[...]
        c = _log_sigmoid(ff)
        row = lax.broadcasted_iota(jnp.int32, c.shape, 0)
        d = 1
        while d < tm:
            c = c + jnp.where(row >= d, pltpu.roll(c, d, axis=0), 0.0)
            d *= 2
        c = c + carry_scr[...]
        cum_ref[...] = c
        carry_scr[...] = c[tm - 1:tm, :]

    @pl.when(jnp.logical_and(j != q_block, j != v_block))
    def _():
        proj_ref[...] = jnp.dot(h_scr[...], w_ref[...].astype(BF16),
                                preferred_element_type=F32).astype(BF16)

    def feature_major(slot, scale):
        @pl.when(i == 0)
        def _():
            wt_scr[slot] = w_ref[...].T.astype(BF16)

        r = lax.dot_general(wt_scr[slot], h_scr[...], (((1,), (1,)), ((), ())),
                            preferred_element_type=F32)
        projt_ref[...] = (r * scale).astype(BF16)

    @pl.when(j == q_block)
    def _():
        feature_major(0, q_scale)

    @pl.when(j == v_block)
    def _():
        feature_major(1, 1.0)


def _inproj(x, g, sh, sc, w_in, layer, w_ff, fb, q_block, v_block, q_scale):
    T, D = x.shape
    tm, tn = min(INPROJ_TM, T), INPROJ_TN
    nj = v_block + 1
    is_q = lambda j: (j == q_block).astype(jnp.int32)
    is_v = lambda j: (j == v_block).astype(jnp.int32)
    w_col = lambda i, j: jnp.where(i == 0, j, j - is_q(j) - is_v(j))
    out_col = lambda j: j - (j >= q_block).astype(jnp.int32) - (j >= v_block).astype(jnp.int32)
    vec = pl.BlockSpec((1, D), lambda i, j: (0, 0))
    return pl.pallas_call(
        functools.partial(_inproj_kernel, q_block=q_block, v_block=v_block, q_scale=q_scale),
        grid=(T // tm, nj),
        in_specs=[
            pl.BlockSpec((tm, D), lambda i, j: (i, 0), pipeline_mode=pl.Buffered(1)),
            vec, vec, vec,
            pl.BlockSpec((None, D, tn), lambda i, j: (layer, 0, w_col(i, j))),
            pl.BlockSpec((D, LANES), lambda i, j: (0, 0)),
            pl.BlockSpec((1, LANES), lambda i, j: (0, 0)),
        ],
        out_specs=[
            pl.BlockSpec((tm, tn), lambda i, j: (i, out_col(j))),
            pl.BlockSpec((tn, tm), lambda i, j: (is_v(j), i)),
            pl.BlockSpec((tm, LANES), lambda i, j: (i, 0)),
        ],
        out_shape=[
            jax.ShapeDtypeStruct((T, (nj - 2) * tn), BF16),
            jax.ShapeDtypeStruct((2 * tn, T), BF16),
            jax.ShapeDtypeStruct((T, LANES), F32),
        ],
        scratch_shapes=[pltpu.VMEM((tm, D), BF16), pltpu.VMEM((2, tn, D), BF16),
                        pltpu.VMEM((1, LANES), F32)],
        compiler_params=_cparams(("arbitrary", "arbitrary")),
        name="inproj",
    )(x, g, sh, sc, w_in, w_ff, fb)


def _conv_kernel(cb_ref, cc_ref, cu_ref, hc_ref, hu_ref, w_ref, o_ref):
    i = pl.program_id(0)
    z = cc_ref[...].astype(F32) * cu_ref[...].astype(F32)
    zh = hc_ref[...].astype(F32) * hu_ref[...].astype(F32)
    zh = jnp.where(i > 0, zh, 0.0)
    hl = zh.shape[0]
    zm1 = zh[hl - 1:hl, :]
    zm2 = zh[hl - 2:hl - 1, :]
    row = lax.broadcasted_iota(jnp.int32, z.shape, 0)
    z1 = jnp.where(row == 0, zm1, pltpu.roll(z, 1, axis=0))
    z2 = jnp.where(row == 0, zm2, jnp.where(row == 1, zm1, pltpu.roll(z, 2, axis=0)))
    w = w_ref[...]
    y = z2 * w[0:1, :] + z1 * w[1:2, :] + z * w[2:3, :]
    o_ref[...] = (cb_ref[...].astype(F32) * y).astype(BF16)


def _conv(proj, conv_w):
    T = proj.shape[0]
    W = conv_w.shape[1]
    tm = min(CONV_TM, T)
    hb = tm // CONV_HALO
    halo = lambda c: pl.BlockSpec((CONV_HALO, W), lambda i: (jnp.maximum(i * hb - 1, 0), c))
    return pl.pallas_call(
        _conv_kernel,
        grid=(T // tm,),
        in_specs=[
            pl.BlockSpec((tm, W), lambda i: (i, 0)),
            pl.BlockSpec((tm, W), lambda i: (i, 1)),
            pl.BlockSpec((tm, W), lambda i: (i, 2)),
            halo(1), halo(2),
            pl.BlockSpec((CONV_K, W), lambda i: (0, 0)),
        ],
        out_specs=pl.BlockSpec((tm, W), lambda i: (i, 0)),
        out_shape=jax.ShapeDtypeStruct((T, W), BF16),
        compiler_params=_cparams(("arbitrary",)),
        name="conv",
    )(proj, proj, proj, proj, proj, conv_w)


def _ret_gammas():
    return [1.0 - 2.0 ** (-5.0 - h) for h in range(RET_HEADS)]


def _ret_tables(R):
    n = np.arange(R, dtype=np.float64)
    chunk = np.arange(R) // CHUNK
    allowed = chunk[None, :] <= chunk[:, None]
    dm, qd, kd = [], [], []
    for g in _ret_gammas():
        lg = math.log(g)
        dm.append(np.where(allowed, np.exp(lg * np.abs(n[:, None] - n[None, :])), 0.0))
        qd.append(np.broadcast_to(np.exp(lg * (n + 1.0))[:, None], (R, LANES)))
        kd.append(np.broadcast_to(np.exp(lg * (R - 1.0 - n))[:, None], (R, LANES)))
    f = lambda a: jnp.asarray(np.stack(a), dtype=F32)
    return f(dm), f(qd), f(kd)


def _rope_tables(T):
    half = RET_DK // 2
    inv = ROPE_BASE ** (-jnp.arange(half, dtype=F32) / half)
    ang = jnp.arange(T, dtype=F32)[:, None] * inv[None, :]
    cos, sin = jnp.cos(ang), jnp.sin(ang)
    reps = LANES // RET_DK
    cos_t = jnp.tile(jnp.concatenate([cos, cos], axis=1), (1, reps))
    sin_t = jnp.tile(jnp.concatenate([-sin, sin], axis=1), (1, reps))
    return cos_t, sin_t


def _ret_kernel(q_ref, k_ref, v_ref, g_ref, cos_ref, sin_ref, dm_ref, qd_ref, kd_ref, gn_ref,
                o_ref, s_scr):
    i = pl.program_id(0)
    R = q_ref.shape[0]

    @pl.when(i == 0)
    def _():
        s_scr[...] = jnp.zeros_like(s_scr)

    lane = lax.broadcasted_iota(jnp.int32, (R, LANES), 1)
    first_half = (lane % RET_DK) < (RET_DK // 2)
    low_head = lane < RET_DK
    cosv = cos_ref[...]
    sinv = sin_ref[...]
    c_dec = [g ** R for g in _ret_gammas()]

    def rot(t):
        swapped = jnp.where(first_half, pltpu.roll(t, LANES - RET_DK // 2, axis=1),
                            pltpu.roll(t, RET_DK // 2, axis=1))
        return t * cosv + swapped * sinv

    heads_per_vreg = LANES // RET_DK
    for p in range(RET_HEADS // heads_per_vreg):
        cols = slice(p * LANES, (p + 1) * LANES)
        qr = rot(q_ref[:, cols].astype(F32))
        kb = (rot(k_ref[:, cols].astype(F32)) * (RET_DK ** -0.5)).astype(BF16)
        for hh in range(heads_per_vreg):
            h = p * heads_per_vreg + hh
            hc = slice(h * RET_DV, (h + 1) * RET_DV)
            mask = low_head if hh == 0 else jnp.logical_not(low_head)
            qm = jnp.where(mask, qr, 0.0).astype(BF16)
            s = lax.dot_general(qm, kb, (((1,), (1,)), ((), ())), preferred_element_type=F32)
            s = s * dm_ref[h]
            v = v_ref[:, hc]
            o = jnp.dot(s.astype(BF16), v, preferred_element_type=F32)
            state = s_scr[h]
            o = o + jnp.dot(qm, state.astype(BF16), preferred_element_type=F32) * qd_ref[h]
            vd = (v.astype(F32) * kd_ref[h]).astype(BF16)
            kv = lax.dot_general(kb, vd, (((0,), (0,)), ((), ())), preferred_element_type=F32)
            s_scr[h] = state * c_dec[h] + kv
            mu = jnp.mean(o, axis=-1, keepdims=True)
            d = o - mu
            var = jnp.mean(d * d, axis=-1, keepdims=True)
            on = d * lax.rsqrt(var + EPS) * gn_ref[:, hc]
            gate = g_ref[:, hc].astype(F32)
            o_ref[:, hc] = (gate * jax.nn.sigmoid(gate) * on).astype(BF16)


def _retention(proj, cos_t, sin_t, gn_g, col0):
    T = proj.shape[0]
    R = min(RET_ROWS, T)
    QK = RET_HEADS * RET_DK
    V = RET_HEADS * RET_DV
    dm, qd, kd = _ret_tables(R)
    q_blk = col0 // QK
    v_blk = (col0 + 2 * QK) // V
    full3 = lambda a: pl.BlockSpec(a.shape, lambda i: (0, 0, 0))
    return pl.pallas_call(
        _ret_kernel,
        grid=(T // R,),
        in_specs=[
            pl.BlockSpec((R, QK), lambda i: (i, q_blk)),
            pl.BlockSpec((R, QK), lambda i: (i, q_blk + 1)),
            pl.BlockSpec((R, V), lambda i: (i, v_blk)),
            pl.BlockSpec((R, V), lambda i: (i, v_blk + 1)),
            pl.BlockSpec((R, LANES), lambda i: (i, 0)),
            pl.BlockSpec((R, LANES), lambda i: (i, 0)),
            full3(dm), full3(qd), full3(kd),
            pl.BlockSpec((1, V), lambda i: (0, 0)),
        ],
        out_specs=pl.BlockSpec((R, V), lambda i: (i, 0)),
        out_shape=jax.ShapeDtypeStruct((T, V), BF16),
        scratch_shapes=[pltpu.VMEM((RET_HEADS, LANES, RET_DV), F32)],
        compiler_params=_cparams(("arbitrary",)),
        name="retention",
    )(proj, proj, proj, proj, cos_t, sin_t, dm, qd, kd, gn_g)


def _fox_kernel(qt_ref, k_ref, vt_ref, cum_ref, o_ref, ka_scr, cend_scr, kn_scr,
                s0_scr, s1_scr, p0_scr, p1_scr, a0_scr, a1_scr, m_scr, l_scr, acc_scr):
    h = pl.program_id(0)
    qi = pl.program_id(1)
    tq = qt_ref.shape[1]
    tk = s0_scr.shape[0]
    T = k_ref.shape[0]
    assert tq == 2 * tk
    s_scr, p_scr, a_scr = (s0_scr, s1_scr), (p0_scr, p1_scr), (a0_scr, a1_scr)

    @pl.when(qi == 0)
    def _():
        cend_scr[...] = jnp.zeros_like(cend_scr)
        ones = jnp.ones((FOX_DH, LANES), BF16)
        kn2 = jnp.zeros((SUBLANES, LANES), F32)
        for r in range(0, T, tq):
            kf = k_ref[r:r + tq, :].astype(F32)
            n2 = jnp.dot((kf * kf).astype(BF16), ones, preferred_element_type=F32)
            kn2 = jnp.maximum(kn2, jnp.max(n2.reshape(tq // SUBLANES, SUBLANES, LANES), axis=0))
        kn_scr[...] = jnp.max(kn2, axis=0, keepdims=True)

    lane = lax.broadcasted_iota(jnp.int32, cum_ref.shape, 1)
    col = jnp.sum(jnp.where(lane == h, cum_ref[...], 0.0), axis=1, keepdims=True) * LOG2E
    q0 = pl.multiple_of(qi * tq, tq)
    ck_rep = jnp.broadcast_to(col, cum_ref.shape)
    bf = lambda v: v.astype(BF16).astype(F32)
    c_hi = bf(col)
    c_mid = bf(col - c_hi)
    c_lo = bf(col - c_hi - c_mid)
    extra = jnp.where(lane == 0, c_hi, jnp.where(lane == 1, c_mid, jnp.where(lane == 2, c_lo, 0.0)))
    ka_scr[pl.ds(q0, tq), 0:FOX_DH] = k_ref[pl.ds(q0, tq), :]
    ka_scr[pl.ds(q0, tq), FOX_DH:2 * FOX_DH] = extra.astype(BF16)
    sub = lax.broadcasted_iota(jnp.int32, (FOX_DH, tq), 0)
    q_ext = jnp.concatenate([qt_ref[...], jnp.where(sub < 3, -1.0, 0.0).astype(BF16)], axis=0)
    cend_scr[pl.ds(2 * qi, 1), :] = ck_rep[tk - 1:tk, :]
    cend_scr[pl.ds(2 * qi + 1, 1), :] = ck_rep[tq - 1:tq, :]

    qf = qt_ref[...].astype(F32)
    qn2 = jnp.max(jnp.sum(qf * qf, axis=0, keepdims=True), axis=1, keepdims=True)
    reach = 2.0 * FOX_NORM_SLACK * jnp.sqrt(qn2 * kn_scr[...]) + FOX_SKIP_LOG2
    decay = cend_scr[...] - ck_rep[0:1, :]
    chunk_id = lax.broadcasted_iota(jnp.int32, decay.shape, 0)
    dead = jnp.logical_and(decay >= reach, chunk_id < 2 * qi)
    n_dead = jnp.max(jnp.sum(dead.astype(jnp.int32), axis=0, keepdims=True))
    first_pair = n_dead // 2

    m_scr[...] = jnp.full_like(m_scr, -jnp.inf)
    l_scr[...] = jnp.zeros_like(l_scr)
    acc_scr[...] = jnp.zeros_like(acc_scr)
    p1_scr[...] = jnp.zeros_like(p1_scr)
    a1_scr[...] = jnp.ones_like(a1_scr)

    everyone = slice(0, tq)

    def score(c, slot, qs=everyone):
        k0 = pl.multiple_of(c * tk, tk)
        s_scr[slot][:, qs] = jnp.dot(ka_scr[pl.ds(k0, tk), :], q_ext[:, qs],
                                     preferred_element_type=F32)

    def softmax(c, slot, masked, qs=everyone):
        nq = qs.stop - qs.start
        sref, pref = s_scr[slot], p_scr[slot]
        if masked:
            s = sref[:, qs]
            kpos = c * tk + lax.broadcasted_iota(jnp.int32, s.shape, 0)
            qpos = q0 + qs.start + lax.broadcasted_iota(jnp.int32, s.shape, 1)
            sref[:, qs] = jnp.where(kpos <= qpos, s, -jnp.inf)
        mx = sref[0:FOX_SLAB, qs]
        for r in range(FOX_SLAB, tk, FOX_SLAB):
            mx = jnp.maximum(mx, sref[r:r + FOX_SLAB, qs])
        m_old = m_scr[:, qs]
        m_new = jnp.maximum(m_old, jnp.max(mx, axis=0, keepdims=True))
        alpha = jnp.exp2(m_old - m_new)
        m_rows = jnp.broadcast_to(m_new, (FOX_SLAB, nq))
        psum = jnp.zeros((FOX_SLAB, nq), F32)
        for r in range(0, tk, FOX_SLAB):
            p = jnp.exp2(sref[r:r + FOX_SLAB, qs] - m_rows)
            psum = psum + p
            pref[r:r + FOX_SLAB, qs] = p.astype(BF16)
        l_scr[:, qs] = alpha * l_scr[:, qs] + jnp.sum(psum, axis=0, keepdims=True)
        a_scr[slot][:, qs] = alpha
        m_scr[:, qs] = m_new

    def accumulate(c, slot, qs=everyone):
        k0 = pl.multiple_of(jnp.maximum(c, 0) * tk, tk)
        pv = jnp.dot(vt_ref[:, pl.ds(k0, tk)], p_scr[slot][:, qs], preferred_element_type=F32)
        acc_scr[:, qs] = a_scr[slot][:, qs] * acc_scr[:, qs] + pv

    score(2 * first_pair, 0)

    def pair(i, carry):
        c = 2 * i
        score(c + 1, 1)
        accumulate(c - 1, 1)
        softmax(c, 0, False)
        score(c + 2, 0)
        accumulate(c, 0)
        softmax(c + 1, 1, False)
        return carry

    lax.fori_loop(first_pair, qi, pair, 0)
    c = 2 * qi
    late = slice(tk, tq)
    score(c + 1, 1, late)
    accumulate(c - 1, 1)
    softmax(c, 0, True)
    accumulate(c, 0)
    softmax(c + 1, 1, True, late)
    accumulate(c + 1, 1, late)
    o_ref[...] = (acc_scr[...] / l_scr[...]).T.astype(BF16)


def _fox(proj, proj_t, cum, k_col0):
    T = proj.shape[0]
    tq = min(FOX_TQ, T)
    tk = tq // 2
    W = FOX_HEADS * FOX_DH
    kb = k_col0 // FOX_DH
    return pl.pallas_call(
        _fox_kernel,
        grid=(FOX_HEADS, T // tq),
        in_specs=[
            pl.BlockSpec((FOX_DH, tq), lambda h, i: (h, i)),
            pl.BlockSpec((T, FOX_DH), lambda h, i: (0, kb + h)),
            pl.BlockSpec((FOX_DH, T), lambda h, i: (FOX_HEADS + h, 0)),
            pl.BlockSpec((tq, LANES), lambda h, i: (i, 0)),
        ],
        out_specs=pl.BlockSpec((tq, FOX_DH), lambda h, i: (i, h)),
        out_shape=jax.ShapeDtypeStruct((T, W), BF16),
        scratch_shapes=[pltpu.VMEM((T, 2 * FOX_DH), BF16),
                        pltpu.VMEM((-(-(T // tk) // SUBLANES) * SUBLANES, LANES), F32),
                        pltpu.VMEM((1, LANES), F32),
                        pltpu.VMEM((tk, tq), F32), pltpu.VMEM((tk, tq), F32),
                        pltpu.VMEM((tk, tq), BF16), pltpu.VMEM((tk, tq), BF16),
                        pltpu.VMEM((1, tq), F32), pltpu.VMEM((1, tq), F32),
                        pltpu.VMEM((1, tq), F32), pltpu.VMEM((1, tq), F32),
                        pltpu.VMEM((FOX_DH, tq), F32)],
        compiler_params=_cparams(("arbitrary", "arbitrary")),
        name="fox",
    )(proj_t, proj, proj_t, cum)


def _outproj_kernel(yc_ref, yr_ref, yf_ref, wc_ref, wr_ref, wf_ref, x_ref, g1_ref,
                    g_ref, sh_ref, sc_ref, rwh_ref, rwl_ref, xo_ref, h_ref, lg_ref):
    mix = jnp.dot(yc_ref[...], wc_ref[...], preferred_element_type=F32)
    mix = mix + jnp.dot(yr_ref[...], wr_ref[...], preferred_element_type=F32)
    mix = mix + jnp.dot(yf_ref[...], wf_ref[...], preferred_element_type=F32)
    x = x_ref[...] + g1_ref[...] * mix
    xo_ref[...] = x
    h = _modulated_norm(x, g_ref[...], sh_ref[...], sc_ref[...])
    h_ref[...] = _pack_halves(h)
    h_hi = h.astype(BF16)
    h_lo = (h - h_hi.astype(F32)).astype(BF16)
    nt = lambda a, b: lax.dot_general(a, b, (((1,), (1,)), ((), ())), preferred_element_type=F32)
    lg_ref[...] = nt(rwh_ref[...], h_hi) + (nt(rwh_ref[...], h_lo) + nt(rwl_ref[...], h_hi))


def _outproj(yc, yr, yf, wc, wr, wf, x, g1, g, sh, sc, rw_t):
    T, D = x.shape
    tm = min(OUT_TM, T)
    E = rw_t.shape[0]
    rw_hi = rw_t.astype(BF16)
    rw_lo = (rw_t - rw_hi.astype(F32)).astype(BF16)
    vec = pl.BlockSpec((1, D), lambda i: (0, 0))
    rows = lambda a: pl.BlockSpec((tm, a.shape[1]), lambda i: (i, 0))
    whole = lambda a: pl.BlockSpec(a.shape, lambda i: (0, 0))
    return pl.pallas_call(
        _outproj_kernel,
        grid=(T // tm,),
        in_specs=[rows(yc), rows(yr), rows(yf), whole(wc), whole(wr), whole(wf), rows(x),
                  vec, vec, vec, vec, whole(rw_hi), whole(rw_lo)],
        out_specs=[rows(x), pl.BlockSpec((tm, D // 2), lambda i: (i, 0)),
                   pl.BlockSpec((E, tm), lambda i: (0, i))],
        out_shape=[jax.ShapeDtypeStruct((T, D), F32), jax.ShapeDtypeStruct((T, D // 2), U32),
                   jax.ShapeDtypeStruct((E, T), F32)],
        compiler_params=_cparams(("arbitrary",)),
        name="outproj",
    )(yc, yr, yf, wc, wr, wf, x, g1, g, sh, sc, rw_hi, rw_lo)


def _route_kernel(lg_ref, b_ref, tri_ref, e_ref, r_ref, w_ref, cnt_ref, carry_scr):
    i = pl.program_id(0)
    E, tt = lg_ref.shape

    @pl.when(i == 0)
    def _():
        carry_scr[...] = jnp.zeros_like(carry_scr)

    aff = jax.nn.sigmoid(lg_ref[...])
    sel = aff + b_ref[...]
    row8 = lax.broadcasted_iota(jnp.int32, (GROUP_SIZE, tt), 0)
    best = None
    for g in range(N_GROUPS):
        slab = sel[g * GROUP_SIZE:(g + 1) * GROUP_SIZE, :]
        m1 = jnp.max(slab, axis=0, keepdims=True)
        i1 = jnp.min(jnp.where(slab == m1, row8, GROUP_SIZE), axis=0, keepdims=True)
        rest = jnp.where(row8 == i1, -jnp.inf, slab)
        m2 = jnp.max(rest, axis=0, keepdims=True)
        i2 = jnp.min(jnp.where(rest == m2, row8, GROUP_SIZE), axis=0, keepdims=True)
        cand = (m1 + m2, g * GROUP_SIZE + i1, g * GROUP_SIZE + i2)
        if best is None:
            best = cand
        else:
            upd = cand[0] > best[0]
            best = tuple(jnp.where(upd, n, o) for n, o in zip(cand, best))
    _, e0, e1 = best

    row = lax.broadcasted_iota(jnp.int32, (E, tt), 0)
    oh0 = row == e0
    oh1 = row == e1
    a0 = jnp.sum(jnp.where(oh0, aff, 0.0), axis=0, keepdims=True)
    a1 = jnp.sum(jnp.where(oh1, aff, 0.0), axis=0, keepdims=True)
    w_ref[0:1, :] = a0 / (a0 + a1)
    w_ref[1:2, :] = a1 / (a0 + a1)
    e_ref[0:1, :] = e0
    e_ref[1:2, :] = e1

    oh = jnp.logical_or(oh0, oh1)
    ohf = jnp.where(oh, 1.0, 0.0)
    before = jnp.dot(ohf.astype(BF16), tri_ref[...], preferred_element_type=F32) + carry_scr[:, 0:1]
    r_ref[0:1, :] = jnp.sum(jnp.where(oh0, before, 0.0), axis=0, keepdims=True).astype(jnp.int32)
    r_ref[1:2, :] = jnp.sum(jnp.where(oh1, before, 0.0), axis=0, keepdims=True).astype(jnp.int32)
    carry = carry_scr[...] + jnp.sum(ohf, axis=1, keepdims=True)
    carry_scr[...] = carry
    cnt_ref[...] = carry


def _route(logits_t, router_b):
    E, T = logits_t.shape
    tt = min(ROUTE_TT, T)
    tri = jnp.asarray(np.triu(np.ones((tt, tt), np.float32), k=1), dtype=BF16)
    pair = lambda dt: jax.ShapeDtypeStruct((TOP_K, T), dt)
    return pl.pallas_call(
        _route_kernel,
        grid=(T // tt,),
        in_specs=[
            pl.BlockSpec((E, tt), lambda i: (0, i)),
            pl.BlockSpec((E, 1), lambda i: (0, 0)),
            pl.BlockSpec((tt, tt), lambda i: (0, 0)),
        ],
        out_specs=[
            pl.BlockSpec((TOP_K, tt), lambda i: (0, i)),
            pl.BlockSpec((TOP_K, tt), lambda i: (0, i)),
            pl.BlockSpec((TOP_K, tt), lambda i: (0, i)),
            pl.BlockSpec((E, LANES), lambda i: (0, 0)),
        ],
        out_shape=[pair(jnp.int32), pair(jnp.int32), pair(F32), jax.ShapeDtypeStruct((E, LANES), F32)],
        scratch_shapes=[pltpu.VMEM((E, LANES), F32)],
        compiler_params=_cparams(("arbitrary",)),
        name="route",
    )(logits_t, router_b.reshape(E, 1), tri)


def _block_tables(counts, T):
    E = counts.shape[0]
    A = T * TOP_K
    B = MOE_ROWS
    seg = (counts + SUBLANES - 1) // SUBLANES * SUBLANES
    seg_end = jnp.cumsum(seg)
    seg_start = seg_end - seg
    nb = (A + E * (B - 1) + B - 1) // B
    nblk = (counts + B - 1) // B
    blk_end = jnp.cumsum(nblk)
    total = blk_end[-1]
    b = jnp.arange(nb, dtype=jnp.int32)
    bc = jnp.minimum(b, total - 1)
    blk_e = jnp.minimum(jnp.searchsorted(blk_end, bc, side="right"), E - 1).astype(jnp.int32)
    local = bc - (blk_end[blk_e] - nblk[blk_e])
    blk_start = (seg_start[blk_e] + local * B).astype(jnp.int32)
    blk_first = jnp.logical_and(b < total, local == 0).astype(jnp.int32)
    blk_n = jnp.clip(counts[blk_e] - local * B, 0, B).astype(jnp.int32)
    nonempty = counts > 0
    order = jnp.cumsum(nonempty.astype(jnp.int32)) - 1
    ids = jnp.where(nonempty, jnp.arange(E, dtype=jnp.int32), E)
    later = jnp.concatenate([lax.cummin(ids, reverse=True)[1:], jnp.full((1,), E, jnp.int32)])
    next_e = jnp.where(later < E, later, -1)
    blk_wslot = (order[blk_e] % 2).astype(jnp.int32)
    blk_next_e = next_e[blk_e].astype(jnp.int32)
    rows = A + E * (SUBLANES - 1) + B
    rows = (rows + SUBLANES - 1) // SUBLANES * SUBLANES
    blocks = (blk_e, blk_start, blk_first, blk_n, blk_wslot, blk_next_e,
              total.astype(jnp.int32).reshape(1))
    return seg_start.astype(jnp.int32), seg_end.astype(jnp.int32), blocks, nb, rows


def _zero_rows_from(zero_ref, hbm_ref, start, sem):
    piece = zero_ref.shape[0]
    rows = hbm_ref.shape[0]

    def body(j, carry):
        at = pl.multiple_of(jnp.minimum(start + j * piece, rows - piece), SUBLANES)
        cp = pltpu.make_async_copy(zero_ref, hbm_ref.at[pl.ds(at, piece), :], sem)
        cp.start()
        cp.wait()
        return carry

    lax.fori_loop(0, (rows - start + piece - 1) // piece, body, 0)


def _dispatch_kernel(seg_end_ref, pos_ref, h_ref, xs_hbm, zero_scr, sem, zsem):
    i = pl.program_id(0)
    td = pos_ref.shape[1]
    E = seg_end_ref.shape[0]

    @pl.when(i == 0)
    def _():
        zero_scr[...] = jnp.zeros_like(zero_scr)

        def tail(e, carry):
            end = seg_end_ref[e]
            at = pl.multiple_of(jnp.maximum(end - SUBLANES, 0), SUBLANES)
            cp = pltpu.make_async_copy(zero_scr.at[pl.ds(0, SUBLANES), :],
                                       xs_hbm.at[pl.ds(at, SUBLANES), :], zsem)
            cp.start()
            cp.wait()
            return carry

        lax.fori_loop(0, E, tail, 0)
        _zero_rows_from(zero_scr, xs_hbm, seg_end_ref[E - 1], zsem)

    def body(r, carry):
        for k in range(TOP_K):
            pltpu.make_async_copy(h_ref.at[pl.ds(r, 1), :],
                                  xs_hbm.at[pl.ds(pos_ref[k, r], 1), :], sem).start(priority=k)
        return carry

    lax.fori_loop(0, td, body, 0, unroll=8)
    for k in range(TOP_K):
        pltpu.make_async_copy(h_ref, xs_hbm.at[pl.ds(0, td), :], sem).wait()


def _dispatch(h2, pos, seg_end, rows):
    T, D = h2.shape
    td = min(DISPATCH_TT, T)
    grid_spec = pltpu.PrefetchScalarGridSpec(
        num_scalar_prefetch=1,
        grid=(T // td,),
        in_specs=[pl.BlockSpec((TOP_K, td), lambda i, se: (0, i), memory_space=pltpu.SMEM),
                  pl.BlockSpec((td, D), lambda i, se: (i, 0))],
        out_specs=pl.BlockSpec(memory_space=pl.ANY),
        scratch_shapes=[pltpu.VMEM((MOE_ROWS, D), h2.dtype), pltpu.SemaphoreType.DMA(()),
                        pltpu.SemaphoreType.DMA(())],
    )
    return pl.pallas_call(
        _dispatch_kernel,
        grid_spec=grid_spec,
        out_shape=jax.ShapeDtypeStruct((rows, D), h2.dtype),
        compiler_params=_cparams(("arbitrary",), disable_bounds_checks=True),
        name="dispatch",
    )(seg_end, pos, h2)


def _experts_kernel(blk_e_ref, blk_start_ref, blk_first_ref, blk_n_ref, blk_wslot_ref, blk_next_ref,
                    total_ref, tail_ref, xs_hbm, wg_hbm, wu_hbm, wd_hbm, ys_hbm,
                    wg_f, wu_f, wd_f, wg_b, wu_b, wd_b, xbuf, ybuf, isem, osem, wsem, *, layer):
    b = pl.program_id(0)
    total = total_ref[0]
    B = xbuf.shape[1]
    slot = b % 2

    def load(blk, to_slot):
        at = pl.multiple_of(blk_start_ref[blk], SUBLANES)
        return pltpu.make_async_copy(xs_hbm.at[pl.ds(at, B), :], xbuf.at[to_slot], isem.at[to_slot])

    def store(blk, from_slot):
        at = pl.multiple_of(blk_start_ref[blk], SUBLANES)
        return pltpu.make_async_copy(ybuf.at[from_slot], ys_hbm.at[pl.ds(at, B), :], osem)

    def weights(e, ws):
        return [pltpu.make_async_copy(src.at[layer, e], dst.at[ws], wsem.at[ws])
                for src, dst in ((wg_hbm, wg_f), (wu_hbm, wu_f), (wd_hbm, wd_f))]

    @pl.when(b == 0)
    def _():
        load(0, 0).start()
        for cp in weights(blk_e_ref[0], blk_wslot_ref[0]):
            cp.start()
        ybuf[...] = jnp.zeros_like(ybuf)
        _zero_rows_from(ybuf.at[1], ys_hbm, tail_ref[0], osem)

    @pl.when(b + 1 < total)
    def _():
        load(b + 1, 1 - slot).start()

    @pl.when(b < total)
    def _():
        @pl.when(blk_first_ref[b] == 1)
        def _():
            ws = blk_wslot_ref[b]
            for cp in weights(blk_e_ref[b], ws):
                cp.wait()
            wg_b[...] = wg_f[ws].astype(BF16)
            wu_b[...] = wu_f[ws].astype(BF16)
            wd_b[...] = wd_f[ws].astype(BF16)
            nxt = blk_next_ref[b]

            @pl.when(nxt >= 0)
            def _():
                for cp in weights(nxt, 1 - ws):
                    cp.start()

        load(b, slot).wait()

        def ffn(rows):
            lo, hi = _unpack_halves(xbuf[slot, 0:rows, :])
            lo, hi = lo.astype(BF16), hi.astype(BF16)
            half = lo.shape[1]
            g = (jnp.dot(lo, wg_b[:half, :], preferred_element_type=F32)
                 + jnp.dot(hi, wg_b[half:, :], preferred_element_type=F32))
            u = (jnp.dot(lo, wu_b[:half, :], preferred_element_type=F32)
                 + jnp.dot(hi, wu_b[half:, :], preferred_element_type=F32))
            a = (g * jax.nn.sigmoid(g) * u).astype(BF16)
            ybuf[slot, 0:rows, :] = _pack_halves(jnp.dot(a, wd_b[...], preferred_element_type=F32))

        needed = (blk_n_ref[b] + MOE_STEP - 1) // MOE_STEP
        for q in range(1, B // MOE_STEP + 1):
            @pl.when(needed == q)
            def _():
                ffn(q * MOE_STEP)

        @pl.when(b > 0)
        def _():
            store(b - 1, 1 - slot).wait()

        store(b, slot).start()

        @pl.when(b == total - 1)
        def _():
            store(b, slot).wait()


def _experts(xs, layer, w_gate, w_up, w_down, blocks, tail, nb):
    rows, DP = xs.shape
    D, DE = w_gate.shape[-2:]
    B = MOE_ROWS
    anywhere = pl.BlockSpec(memory_space=pl.ANY)
    grid_spec = pltpu.PrefetchScalarGridSpec(
        num_scalar_prefetch=len(blocks) + 1,
        grid=(nb,),
        in_specs=[anywhere, anywhere, anywhere, anywhere],
        out_specs=anywhere,
        scratch_shapes=[
            pltpu.VMEM((2, D, DE), F32), pltpu.VMEM((2, D, DE), F32), pltpu.VMEM((2, DE, D), F32),
            pltpu.VMEM((D, DE), BF16), pltpu.VMEM((D, DE), BF16), pltpu.VMEM((DE, D), BF16),
            pltpu.VMEM((2, B, DP), U32), pltpu.VMEM((2, B, DP), U32),
            pltpu.SemaphoreType.DMA((2,)), pltpu.SemaphoreType.DMA(()), pltpu.SemaphoreType.DMA((2,)),
        ],
    )
    return pl.pallas_call(
        functools.partial(_experts_kernel, layer=layer),
        grid_spec=grid_spec,
        out_shape=jax.ShapeDtypeStruct((rows, DP), U32),
        compiler_params=_cparams(("arbitrary",)),
        name="experts",
    )(*blocks, tail, xs, w_gate, w_up, w_down)


def _combine_kernel(pos_ref, posn_ref, x_ref, ys_hbm, w_ref, g2_ref, fg_ref, o_ref, gbuf, sem,
                    *, final):
    i = pl.program_id(0)
    n = pl.num_programs(0)
    tm = x_ref.shape[0]
    slot = i % 2

    def gather(p_ref, to_slot):
        def body(r, carry):
            for k in range(TOP_K):
                pltpu.make_async_copy(ys_hbm.at[pl.ds(p_ref[k, r], 1), :],
                                      gbuf.at[to_slot, k, pl.ds(r, 1), :], sem.at[to_slot]).start(priority=k)
            return carry

        lax.fori_loop(0, tm, body, 0, unroll=8)

    @pl.when(i == 0)
    def _():
        gather(pos_ref, 0)

    @pl.when(i + 1 < n)
    def _():
        gather(posn_ref, 1 - slot)

    for k in range(TOP_K):
        pltpu.make_async_copy(ys_hbm.at[pl.ds(0, tm), :], gbuf.at[slot, k], sem.at[slot]).wait()

    w = w_ref[...]
    D = x_ref.shape[1]
    half = D // 2
    lo0, hi0 = _unpack_halves(gbuf[slot, 0])
    lo1, hi1 = _unpack_halves(gbuf[slot, 1])
    xl = x_ref[:, :half] + g2_ref[:, :half] * (lo0 * w[:, 0:1] + lo1 * w[:, 1:2])
    xh = x_ref[:, half:] + g2_ref[:, half:] * (hi0 * w[:, 0:1] + hi1 * w[:, 1:2])
    if final:
        ms = (jnp.sum(xl * xl, axis=-1, keepdims=True) + jnp.sum(xh * xh, axis=-1, keepdims=True)) / D
        r = lax.rsqrt(ms + EPS)
        xl = xl * r * fg_ref[:, :half]
        xh = xh * r * fg_ref[:, half:]
    o_ref[:, :half] = xl
    o_ref[:, half:] = xh


def _combine(x, ys, pos, top_w, g2, final_g, final):
    T, D = x.shape
    tm = min(COMB_TM, T)
    n = T // tm
    vec = pl.BlockSpec((1, D), lambda i: (0, 0))
    return pl.pallas_call(
        functools.partial(_combine_kernel, final=final),
        grid=(n,),
        in_specs=[
            pl.BlockSpec((TOP_K, tm), lambda i: (0, i), memory_space=pltpu.SMEM),
            pl.BlockSpec((TOP_K, tm), lambda i: (0, jnp.minimum(i + 1, n - 1)), memory_space=pltpu.SMEM),
            pl.BlockSpec((tm, D), lambda i: (i, 0)),
            pl.BlockSpec(memory_space=pl.ANY),
            pl.BlockSpec((tm, TOP_K), lambda i: (i, 0)),
            vec, vec,
        ],
        out_specs=pl.BlockSpec((tm, D), lambda i: (i, 0)),
        out_shape=jax.ShapeDtypeStruct((T, D), F32),
        scratch_shapes=[pltpu.VMEM((2, TOP_K, tm, ys.shape[1]), ys.dtype), pltpu.SemaphoreType.DMA((2,))],
        compiler_params=_cparams(("arbitrary",), disable_bounds_checks=True),
        name="combine",
    )(pos, pos, x, ys, top_w, g2, final_g)


def kernel(x, c, ada_w, ada_b, norm1_g, norm2_g, w_in, conv_w, ret_gn_g, fox_fb, w_out,
           router_w, router_b, moe_w_gate, moe_w_up, moe_w_down, final_g):
    Bsz, S, D = x.shape
    assert Bsz == 1, "one sequence per call"
    L = ada_w.shape[0]
    T = S
    conv_wd = conv_w.shape[-1]
    ret_qk = RET_HEADS * RET_DK
    ret_v = RET_HEADS * RET_DV
    fox_w = FOX_HEADS * FOX_DH
    tn = INPROJ_TN
    ret_col0 = 3 * conv_wd
    fox_col0 = ret_col0 + 2 * ret_qk + 2 * ret_v
    n_main = fox_col0 + 3 * fox_w
    assert fox_w == tn and fox_col0 % tn == 0
    q_block = fox_col0 // tn
    v_block = q_block + 2
    k_col0 = fox_col0

    mod = _ada_mod(c, ada_w, ada_b)
    cos_t, sin_t = _rope_tables(T)
    rw_t = router_w.T
    xt = x.reshape(T, D)

    for l in range(L):
        sh1, sc1, g1, sh2, sc2, g2 = [mod[l, :, k * D:(k + 1) * D] for k in range(6)]
        w_ff = jnp.pad(w_in[l, :, n_main:], ((0, 0), (0, LANES - FOX_HEADS)))
        fb = jnp.pad(fox_fb[l], (0, LANES - FOX_HEADS)).reshape(1, LANES)
        proj, proj_t, cum = _inproj(xt, norm1_g[l].reshape(1, D), sh1, sc1, w_in, l, w_ff, fb,
                                    q_block, v_block, LOG2E * FOX_DH ** -0.5)

        y_conv = _conv(proj, conv_w[l])
        y_ret = _retention(proj, cos_t, sin_t, ret_gn_g[l].reshape(1, ret_v), ret_col0)
        y_fox = _fox(proj, proj_t, cum, k_col0)

        wo = w_out[l].astype(BF16)
        xt, h2, logits_t = _outproj(
            y_conv, y_ret, y_fox, wo[:conv_wd], wo[conv_wd:conv_wd + ret_v], wo[conv_wd + ret_v:],
            xt, g1, norm2_g[l].reshape(1, D), sh2, sc2, rw_t)

        top_e, rank, top_w, cnt = _route(logits_t, router_b)
        counts = cnt[:, 0].astype(jnp.int32)
        seg_start, seg_end, blocks, nb, rows = _block_tables(counts, T)
        hit = top_e[None] == jnp.arange(N_EXPERTS, dtype=jnp.int32)[:, None, None]
        pos = rank + jnp.sum(jnp.where(hit, seg_start[:, None, None], 0), axis=0)
        xs = _dispatch(h2, pos, seg_end, rows)
        ys = _experts(xs, l, moe_w_gate, moe_w_up, moe_w_down, blocks, seg_end[-1:], nb)
        xt = _combine(xt, ys, pos, top_w.T, g2, final_g.reshape(1, D), final=(l == L - 1))

    return xt.reshape(Bsz, S, D)
```

```python
import functools
import math

import numpy as np
import jax
import jax.numpy as jnp
from jax import lax
from jax.experimental import pallas as pl
from jax.experimental.pallas import tpu as pltpu

F32 = jnp.float32
BF16 = jnp.bfloat16

CHUNK = 64
CONV_K = 3
RET_HEADS = 6
RET_DK = 64
RET_DV = 128
FOX_HEADS = 6
FOX_DH = 128
N_EXPERTS = 64
N_GROUPS = 8
GROUP_SIZE = N_EXPERTS // N_GROUPS
TOP_K = 2
ROPE_BASE = 10000.0
EPS = 1e-6
LOG2E = 1.4426950408889634

LANES = 128
SUBLANES = 8

MOD_TN = 1024
INPROJ_TM = 1024
INPROJ_TN = 768
CONV_TM = 1024
CONV_HALO = 16
RET_ROWS = 256
FOX_TQ = 1024
FOX_SLAB = 16
FOX_SKIP_LOG2 = 160.0
FOX_NORM_SLACK = 1.02
OUT_TM = 512
MOE_ROWS = 256
MOE_STEP = 64
ROUTE_TT = 1024
DISPATCH_TT = 512
COMB_TM = 512
VMEM_LIMIT = 56 * 1024 * 1024


def _cparams(sem, **kw):
    return pltpu.CompilerParams(dimension_semantics=sem, vmem_limit_bytes=VMEM_LIMIT, **kw)


U32 = jnp.uint32


def _pack_halves(x):
    half = x.shape[1] // 2
    xb = x.astype(BF16).astype(F32)
    lo = lax.bitcast_convert_type(xb[:, :half], U32)
    hi = lax.bitcast_convert_type(xb[:, half:], U32)
    return lax.shift_right_logical(lo, U32(16)) | (hi & U32(0xFFFF0000))


def _unpack_halves(w):
    lo = lax.bitcast_convert_type(lax.shift_left(w, U32(16)), F32)
    hi = lax.bitcast_convert_type(w & U32(0xFFFF0000), F32)
    return lo, hi


def _mod_kernel(c_ref, w_ref, b_ref, o_ref):
    c = c_ref[...]
    ca = c * jax.nn.sigmoid(c)
    o_ref[0] = jnp.sum(w_ref[0] * ca, axis=0, keepdims=True) + b_ref[0]


def _ada_mod(c, ada_w, ada_b):
    L, D, N = ada_w.shape
    tn = MOD_TN
    return pl.pallas_call(
        _mod_kernel,
        grid=(L, N // tn),
        in_specs=[
            pl.BlockSpec((D, 1), lambda l, j: (0, 0)),
            pl.BlockSpec((1, D, tn), lambda l, j: (l, 0, j)),
            pl.BlockSpec((1, 1, tn), lambda l, j: (l, 0, j)),
        ],
        out_specs=pl.BlockSpec((1, 1, tn), lambda l, j: (l, 0, j)),
        out_shape=jax.ShapeDtypeStruct((L, 1, N), F32),
        compiler_params=_cparams(("arbitrary", "arbitrary")),
        name="ada_mod",
    )(c.reshape(D, 1), ada_w, ada_b.reshape(L, 1, N))


def _modulated_norm(x, g, sh, sc):
    ms = jnp.mean(x * x, axis=-1, keepdims=True)
    y = x * lax.rsqrt(ms + EPS) * g
    return y * (1.0 + sc) + sh


def _log_sigmoid(x):
    return jnp.minimum(x, 0.0) - jnp.log1p(jnp.exp(-jnp.abs(x)))


def _inproj_kernel(x_ref, g_ref, sh_ref, sc_ref, w_ref, wff_ref, fb_ref,
                   proj_ref, projt_ref, cum_ref, h_scr, wt_scr, carry_scr,
                   *, q_block, v_block, q_scale):
    i = pl.program_id(0)
    j = pl.program_id(1)
    tm = x_ref.shape[0]

    @pl.when(jnp.logical_and(i == 0, j == 0))
    def _():
        carry_scr[...] = jnp.zeros_like(carry_scr)

    @pl.when(j == 0)
    def _():
        h = _modulated_norm(x_ref[...], g_ref[...], sh_ref[...], sc_ref[...])
        hb = h.astype(BF16)
        h_scr[...] = hb
        ff = jnp.dot(hb, wff_ref[...].astype(BF16), preferred_element_type=F32) + fb_ref[...]
        c = _log_sigmoid(ff)
        row = lax.broadcasted_iota(jnp.int32, c.shape, 0)
        d = 1
        while d < tm:
            c = c + jnp.where(row >= d, pltpu.roll(c, d, axis=0), 0.0)
            d *= 2
        c = c + carry_scr[...]
        cum_ref[...] = c
        carry_scr[...] = c[tm - 1:tm, :]

    @pl.when(jnp.logical_and(j != q_block, j != v_block))
    def _():
        proj_ref[...] = jnp.dot(h_scr[...], w_ref[...].astype(BF16),
                                preferred_element_type=F32).astype(BF16)

    def feature_major(slot, scale):
        @pl.when(i == 0)
        def _():
            wt_scr[slot] = w_ref[...].T.astype(BF16)

        r = lax.dot_general(wt_scr[slot], h_scr[...], (((1,), (1,)), ((), ())),
                            preferred_element_type=F32)
        projt_ref[...] = (r * scale).astype(BF16)

    @pl.when(j == q_block)
    def _():
        feature_major(0, q_scale)

    @pl.when(j == v_block)
    def _():
        feature_major(1, 1.0)


def _inproj(x, g, sh, sc, w_in, layer, w_ff, fb, q_block, v_block, q_scale):
    T, D = x.shape
    tm, tn = min(INPROJ_TM, T), INPROJ_TN
    nj = v_block + 1
    is_q = lambda j: (j == q_block).astype(jnp.int32)
    is_v = lambda j: (j == v_block).astype(jnp.int32)
    w_col = lambda i, j: jnp.where(i == 0, j, j - is_q(j) - is_v(j))
    out_col = lambda j: j - (j >= q_block).astype(jnp.int32) - (j >= v_block).astype(jnp.int32)
    vec = pl.BlockSpec((1, D), lambda i, j: (0, 0))
    return pl.pallas_call(
        functools.partial(_inproj_kernel, q_block=q_block, v_block=v_block, q_scale=q_scale),
        grid=(T // tm, nj),
        in_specs=[
            pl.BlockSpec((tm, D), lambda i, j: (i, 0)),
            vec, vec, vec,
            pl.BlockSpec((None, D, tn), lambda i, j: (layer, 0, w_col(i, j))),
            pl.BlockSpec((D, LANES), lambda i, j: (0, 0)),
            pl.BlockSpec((1, LANES), lambda i, j: (0, 0)),
        ],
        out_specs=[
            pl.BlockSpec((tm, tn), lambda i, j: (i, out_col(j))),
            pl.BlockSpec((tn, tm), lambda i, j: (is_v(j), i)),
            pl.BlockSpec((tm, LANES), lambda i, j: (i, 0)),
        ],
        out_shape=[
            jax.ShapeDtypeStruct((T, (nj - 2) * tn), BF16),
            jax.ShapeDtypeStruct((2 * tn, T), BF16),
            jax.ShapeDtypeStruct((T, LANES), F32),
        ],
        scratch_shapes=[pltpu.VMEM((tm, D), BF16), pltpu.VMEM((2, tn, D), BF16),
                        pltpu.VMEM((1, LANES), F32)],
        compiler_params=_cparams(("arbitrary", "arbitrary")),
        name="inproj",
    )(x, g, sh, sc, w_in, w_ff, fb)


def _conv_kernel(cb_ref, cc_ref, cu_ref, hc_ref, hu_ref, w_ref, o_ref):
    i = pl.program_id(0)
    z = cc_ref[...].astype(F32) * cu_ref[...].astype(F32)
    zh = hc_ref[...].astype(F32) * hu_ref[...].astype(F32)
    zh = jnp.where(i > 0, zh, 0.0)
    hl = zh.shape[0]
    zm1 = zh[hl - 1:hl, :]
    zm2 = zh[hl - 2:hl - 1, :]
    row = lax.broadcasted_iota(jnp.int32, z.shape, 0)
    z1 = jnp.where(row == 0, zm1, pltpu.roll(z, 1, axis=0))
    z2 = jnp.where(row == 0, zm2, jnp.where(row == 1, zm1, pltpu.roll(z, 2, axis=0)))
    w = w_ref[...]
    y = z2 * w[0:1, :] + z1 * w[1:2, :] + z * w[2:3, :]
    o_ref[...] = (cb_ref[...].astype(F32) * y).astype(BF16)


def _conv(proj, conv_w):
    T = proj.shape[0]
    W = conv_w.shape[1]
    tm = min(CONV_TM, T)
    hb = tm // CONV_HALO
    halo = lambda c: pl.BlockSpec((CONV_HALO, W), lambda i: (jnp.maximum(i * hb - 1, 0), c))
    return pl.pallas_call(
        _conv_kernel,
        grid=(T // tm,),
        in_specs=[
            pl.BlockSpec((tm, W), lambda i: (i, 0)),
            pl.BlockSpec((tm, W), lambda i: (i, 1)),
            pl.BlockSpec((tm, W), lambda i: (i, 2)),
            halo(1), halo(2),
            pl.BlockSpec((CONV_K, W), lambda i: (0, 0)),
        ],
        out_specs=pl.BlockSpec((tm, W), lambda i: (i, 0)),
        out_shape=jax.ShapeDtypeStruct((T, W), BF16),
        compiler_params=_cparams(("arbitrary",)),
        name="conv",
    )(proj, proj, proj, proj, proj, conv_w)


def _ret_gammas():
    return [1.0 - 2.0 ** (-5.0 - h) for h in range(RET_HEADS)]


def _ret_tables(R):
    n = np.arange(R, dtype=np.float64)
    chunk = np.arange(R) // CHUNK
    allowed = chunk[None, :] <= chunk[:, None]
    dm, qd, kd = [], [], []
    for g in _ret_gammas():
        lg = math.log(g)
        dm.append(np.where(allowed, np.exp(lg * np.abs(n[:, None] - n[None, :])), 0.0))
        qd.append(np.broadcast_to(np.exp(lg * (n + 1.0))[:, None], (R, LANES)))
        kd.append(np.broadcast_to(np.exp(lg * (R - 1.0 - n))[:, None], (R, LANES)))
    f = lambda a: jnp.asarray(np.stack(a), dtype=F32)
    return f(dm), f(qd), f(kd)


def _rope_tables(T):
    half = RET_DK // 2
    inv = ROPE_BASE ** (-jnp.arange(half, dtype=F32) / half)
    ang = jnp.arange(T, dtype=F32)[:, None] * inv[None, :]
    cos, sin = jnp.cos(ang), jnp.sin(ang)
    reps = LANES // RET_DK
    cos_t = jnp.tile(jnp.concatenate([cos, cos], axis=1), (1, reps))
    sin_t = jnp.tile(jnp.concatenate([-sin, sin], axis=1), (1, reps))
    return cos_t, sin_t


def _ret_kernel(q_ref, k_ref, v_ref, g_ref, cos_ref, sin_ref, dm_ref, qd_ref, kd_ref, gn_ref,
                o_ref, s_scr):
    i = pl.program_id(0)
    R = q_ref.shape[0]

    @pl.when(i == 0)
    def _():
        s_scr[...] = jnp.zeros_like(s_scr)

    lane = lax.broadcasted_iota(jnp.int32, (R, LANES), 1)
    first_half = (lane % RET_DK) < (RET_DK // 2)
    low_head = lane < RET_DK
    cosv = cos_ref[...]
    sinv = sin_ref[...]
    c_dec = [g ** R for g in _ret_gammas()]

    def rot(t):
        swapped = jnp.where(first_half, pltpu.roll(t, LANES - RET_DK // 2, axis=1),
                            pltpu.roll(t, RET_DK // 2, axis=1))
        return t * cosv + swapped * sinv

    heads_per_vreg = LANES // RET_DK
    for p in range(RET_HEADS // heads_per_vreg):
        cols = slice(p * LANES, (p + 1) * LANES)
        qr = rot(q_ref[:, cols].astype(F32))
        kb = (rot(k_ref[:, cols].astype(F32)) * (RET_DK ** -0.5)).astype(BF16)
        for hh in range(heads_per_vreg):
            h = p * heads_per_vreg + hh
            hc = slice(h * RET_DV, (h + 1) * RET_DV)
            mask = low_head if hh == 0 else jnp.logical_not(low_head)
            qm = jnp.where(mask, qr, 0.0).astype(BF16)
            s = lax.dot_general(qm, kb, (((1,), (1,)), ((), ())), preferred_element_type=F32)
            s = s * dm_ref[h]
            v = v_ref[:, hc]
            o = jnp.dot(s.astype(BF16), v, preferred_element_type=F32)
            state = s_scr[h]
            o = o + jnp.dot(qm, state.astype(BF16), preferred_element_type=F32) * qd_ref[h]
            vd = (v.astype(F32) * kd_ref[h]).astype(BF16)
            kv = lax.dot_general(kb, vd, (((0,), (0,)), ((), ())), preferred_element_type=F32)
            s_scr[h] = state * c_dec[h] + kv
            mu = jnp.mean(o, axis=-1, keepdims=True)
            d = o - mu
            var = jnp.mean(d * d, axis=-1, keepdims=True)
            on = d * lax.rsqrt(var + EPS) * gn_ref[:, hc]
            gate = g_ref[:, hc].astype(F32)
            o_ref[:, hc] = (gate * jax.nn.sigmoid(gate) * on).astype(BF16)


def _retention(proj, cos_t, sin_t, gn_g, col0):
    T = proj.shape[0]
    R = min(RET_ROWS, T)
    QK = RET_HEADS * RET_DK
    V = RET_HEADS * RET_DV
    dm, qd, kd = _ret_tables(R)
    q_blk = col0 // QK
    v_blk = (col0 + 2 * QK) // V
    full3 = lambda a: pl.BlockSpec(a.shape, lambda i: (0, 0, 0))
    return pl.pallas_call(
        _ret_kernel,
        grid=(T // R,),
        in_specs=[
            pl.BlockSpec((R, QK), lambda i: (i, q_blk)),
            pl.BlockSpec((R, QK), lambda i: (i, q_blk + 1)),
            pl.BlockSpec((R, V), lambda i: (i, v_blk)),
            pl.BlockSpec((R, V), lambda i: (i, v_blk + 1)),
            pl.BlockSpec((R, LANES), lambda i: (i, 0)),
            pl.BlockSpec((R, LANES), lambda i: (i, 0)),
            full3(dm), full3(qd), full3(kd),
            pl.BlockSpec((1, V), lambda i: (0, 0)),
        ],
        out_specs=pl.BlockSpec((R, V), lambda i: (i, 0)),
        out_shape=jax.ShapeDtypeStruct((T, V), BF16),
        scratch_shapes=[pltpu.VMEM((RET_HEADS, LANES, RET_DV), F32)],
        compiler_params=_cparams(("arbitrary",)),
        name="retention",
    )(proj, proj, proj, proj, cos_t, sin_t, dm, qd, kd, gn_g)


def _fox_kernel(qt_ref, k_ref, vt_ref, cum_ref, o_ref, ka_scr, cend_scr, kn_scr,
                s0_scr, s1_scr, p0_scr, p1_scr, a0_scr, a1_scr, m_scr, l_scr, acc_scr):
    h = pl.program_id(0)
    qi = pl.program_id(1)
    tq = qt_ref.shape[1]
    tk = s0_scr.shape[0]
    T = k_ref.shape[0]
    assert tq == 2 * tk
    s_scr, p_scr, a_scr = (s0_scr, s1_scr), (p0_scr, p1_scr), (a0_scr, a1_scr)

    @pl.when(qi == 0)
    def _():
        cend_scr[...] = jnp.zeros_like(cend_scr)
        ones = jnp.ones((FOX_DH, LANES), BF16)
        kn2 = jnp.zeros((SUBLANES, LANES), F32)
        for r in range(0, T, tq):
            kf = k_ref[r:r + tq, :].astype(F32)
            n2 = jnp.dot((kf * kf).astype(BF16), ones, preferred_element_type=F32)
            kn2 = jnp.maximum(kn2, jnp.max(n2.reshape(tq // SUBLANES, SUBLANES, LANES), axis=0))
        kn_scr[...] = jnp.max(kn2, axis=0, keepdims=True)

    lane = lax.broadcasted_iota(jnp.int32, cum_ref.shape, 1)
    col = jnp.sum(jnp.where(lane == h, cum_ref[...], 0.0), axis=1, keepdims=True) * LOG2E
    q0 = pl.multiple_of(qi * tq, tq)
    ck_rep = jnp.broadcast_to(col, cum_ref.shape)
    bf = lambda v: v.astype(BF16).astype(F32)
    c_hi = bf(col)
    c_mid = bf(col - c_hi)
    c_lo = bf(col - c_hi - c_mid)
    extra = jnp.where(lane == 0, c_hi, jnp.where(lane == 1, c_mid, jnp.where(lane == 2, c_lo, 0.0)))
    ka_scr[pl.ds(q0, tq), 0:FOX_DH] = k_ref[pl.ds(q0, tq), :]
    ka_scr[pl.ds(q0, tq), FOX_DH:2 * FOX_DH] = extra.astype(BF16)
    sub = lax.broadcasted_iota(jnp.int32, (FOX_DH, tq), 0)
    q_ext = jnp.concatenate([qt_ref[...], jnp.where(sub < 3, -1.0, 0.0).astype(BF16)], axis=0)
    cend_scr[pl.ds(2 * qi, 1), :] = ck_rep[tk - 1:tk, :]
    cend_scr[pl.ds(2 * qi + 1, 1), :] = ck_rep[tq - 1:tq, :]

    qf = qt_ref[...].astype(F32)
    qn2 = jnp.max(jnp.sum(qf * qf, axis=0, keepdims=True), axis=1, keepdims=True)
    reach = 2.0 * FOX_NORM_SLACK * jnp.sqrt(qn2 * kn_scr[...]) + FOX_SKIP_LOG2
    decay = cend_scr[...] - ck_rep[0:1, :]
    chunk_id = lax.broadcasted_iota(jnp.int32, decay.shape, 0)
    dead = jnp.logical_and(decay >= reach, chunk_id < 2 * qi)
    n_dead = jnp.max(jnp.sum(dead.astype(jnp.int32), axis=0, keepdims=True))
    first_pair = n_dead // 2

    m_scr[...] = jnp.full_like(m_scr, -jnp.inf)
    l_scr[...] = jnp.zeros_like(l_scr)
    acc_scr[...] = jnp.zeros_like(acc_scr)
    p1_scr[...] = jnp.zeros_like(p1_scr)
    a1_scr[...] = jnp.ones_like(a1_scr)

    everyone = slice(0, tq)

    def score(c, slot, qs=everyone):
        k0 = pl.multiple_of(c * tk, tk)
        s_scr[slot][:, qs] = jnp.dot(ka_scr[pl.ds(k0, tk), :], q_ext[:, qs],
                                     preferred_element_type=F32)

    def softmax(c, slot, masked, qs=everyone):
        nq = qs.stop - qs.start
        sref, pref = s_scr[slot], p_scr[slot]
        if masked:
            s = sref[:, qs]
            kpos = c * tk + lax.broadcasted_iota(jnp.int32, s.shape, 0)
            qpos = q0 + qs.start + lax.broadcasted_iota(jnp.int32, s.shape, 1)
            sref[:, qs] = jnp.where(kpos <= qpos, s, -jnp.inf)
        mx = sref[0:FOX_SLAB, qs]
        for r in range(FOX_SLAB, tk, FOX_SLAB):
            mx = jnp.maximum(mx, sref[r:r + FOX_SLAB, qs])
        m_old = m_scr[:, qs]
        m_new = jnp.maximum(m_old, jnp.max(mx, axis=0, keepdims=True))
        alpha = jnp.exp2(m_old - m_new)
        m_rows = jnp.broadcast_to(m_new, (FOX_SLAB, nq))
        psum = jnp.zeros((FOX_SLAB, nq), F32)
        for r in range(0, tk, FOX_SLAB):
            p = jnp.exp2(sref[r:r + FOX_SLAB, qs] - m_rows)
            psum = psum + p
            pref[r:r + FOX_SLAB, qs] = p.astype(BF16)
        l_scr[:, qs] = alpha * l_scr[:, qs] + jnp.sum(psum, axis=0, keepdims=True)
        a_scr[slot][:, qs] = alpha
        m_scr[:, qs] = m_new

    def accumulate(c, slot, qs=everyone):
        k0 = pl.multiple_of(jnp.maximum(c, 0) * tk, tk)
        pv = jnp.dot(vt_ref[:, pl.ds(k0, tk)], p_scr[slot][:, qs], preferred_element_type=F32)
        acc_scr[:, qs] = a_scr[slot][:, qs] * acc_scr[:, qs] + pv

    score(2 * first_pair, 0)

    def pair(i, carry):
        c = 2 * i
        score(c + 1, 1)
        accumulate(c - 1, 1)
        softmax(c, 0, False)
        score(c + 2, 0)
        accumulate(c, 0)
        softmax(c + 1, 1, False)
        return carry

    lax.fori_loop(first_pair, qi, pair, 0)
    c = 2 * qi
    late = slice(tk, tq)
    score(c + 1, 1, late)
    accumulate(c - 1, 1)
    softmax(c, 0, True)
    accumulate(c, 0)
    softmax(c + 1, 1, True, late)
    accumulate(c + 1, 1, late)
    o_ref[...] = (acc_scr[...] / l_scr[...]).T.astype(BF16)


def _fox(proj, proj_t, cum, k_col0):
    T = proj.shape[0]
    tq = min(FOX_TQ, T)
    tk = tq // 2
    W = FOX_HEADS * FOX_DH
    kb = k_col0 // FOX_DH
    return pl.pallas_call(
        _fox_kernel,
        grid=(FOX_HEADS, T // tq),
        in_specs=[
            pl.BlockSpec((FOX_DH, tq), lambda h, i: (h, i)),
            pl.BlockSpec((T, FOX_DH), lambda h, i: (0, kb + h)),
            pl.BlockSpec((FOX_DH, T), lambda h, i: (FOX_HEADS + h, 0)),
            pl.BlockSpec((tq, LANES), lambda h, i: (i, 0)),
        ],
        out_specs=pl.BlockSpec((tq, FOX_DH), lambda h, i: (i, h)),
        out_shape=jax.ShapeDtypeStruct((T, W), BF16),
        scratch_shapes=[pltpu.VMEM((T, 2 * FOX_DH), BF16),
                        pltpu.VMEM((-(-(T // tk) // SUBLANES) * SUBLANES, LANES), F32),
                        pltpu.VMEM((1, LANES), F32),
                        pltpu.VMEM((tk, tq), F32), pltpu.VMEM((tk, tq), F32),
                        pltpu.VMEM((tk, tq), BF16), pltpu.VMEM((tk, tq), BF16),
                        pltpu.VMEM((1, tq), F32), pltpu.VMEM((1, tq), F32),
                        pltpu.VMEM((1, tq), F32), pltpu.VMEM((1, tq), F32),
                        pltpu.VMEM((FOX_DH, tq), F32)],
        compiler_params=_cparams(("arbitrary", "arbitrary")),
        name="fox",
    )(proj_t, proj, proj_t, cum)


def _outproj_kernel(yc_ref, yr_ref, yf_ref, wc_ref, wr_ref, wf_ref, x_ref, g1_ref,
                    g_ref, sh_ref, sc_ref, rwh_ref, rwl_ref, xo_ref, h_ref, lg_ref):
    mix = jnp.dot(yc_ref[...], wc_ref[...], preferred_element_type=F32)
    mix = mix + jnp.dot(yr_ref[...], wr_ref[...], preferred_element_type=F32)
    mix = mix + jnp.dot(yf_ref[...], wf_ref[...], preferred_element_type=F32)
    x = x_ref[...] + g1_ref[...] * mix
    xo_ref[...] = x
    h = _modulated_norm(x, g_ref[...], sh_ref[...], sc_ref[...])
    h_ref[...] = _pack_halves(h)
    h_hi = h.astype(BF16)
    h_lo = (h - h_hi.astype(F32)).astype(BF16)
    nt = lambda a, b: lax.dot_general(a, b, (((1,), (1,)), ((), ())), preferred_element_type=F32)
    lg_ref[...] = nt(rwh_ref[...], h_hi) + (nt(rwh_ref[...], h_lo) + nt(rwl_ref[...], h_hi))


def _outproj(yc, yr, yf, wc, wr, wf, x, g1, g, sh, sc, rw_t):
    T, D = x.shape
    tm = min(OUT_TM, T)
    E = rw_t.shape[0]
    rw_hi = rw_t.astype(BF16)
    rw_lo = (rw_t - rw_hi.astype(F32)).astype(BF16)
    vec = pl.BlockSpec((1, D), lambda i: (0, 0))
    rows = lambda a: pl.BlockSpec((tm, a.shape[1]), lambda i: (i, 0))
    whole = lambda a: pl.BlockSpec(a.shape, lambda i: (0, 0))
    return pl.pallas_call(
        _outproj_kernel,
        grid=(T // tm,),
        in_specs=[rows(yc), rows(yr), rows(yf), whole(wc), whole(wr), whole(wf), rows(x),
                  vec, vec, vec, vec, whole(rw_hi), whole(rw_lo)],
        out_specs=[rows(x), pl.BlockSpec((tm, D // 2), lambda i: (i, 0)),
                   pl.BlockSpec((E, tm), lambda i: (0, i))],
        out_shape=[jax.ShapeDtypeStruct((T, D), F32), jax.ShapeDtypeStruct((T, D // 2), U32),
                   jax.ShapeDtypeStruct((E, T), F32)],
        compiler_params=_cparams(("arbitrary",)),
        name="outproj",
    )(yc, yr, yf, wc, wr, wf, x, g1, g, sh, sc, rw_hi, rw_lo)


def _route_kernel(lg_ref, b_ref, tri_ref, e_ref, r_ref, w_ref, cnt_ref, carry_scr):
    i = pl.program_id(0)
    E, tt = lg_ref.shape

    @pl.when(i == 0)
    def _():
        carry_scr[...] = jnp.zeros_like(carry_scr)

    aff = jax.nn.sigmoid(lg_ref[...])
    sel = aff + b_ref[...]
    row8 = lax.broadcasted_iota(jnp.int32, (GROUP_SIZE, tt), 0)
    best = None
    for g in range(N_GROUPS):
        slab = sel[g * GROUP_SIZE:(g + 1) * GROUP_SIZE, :]
        m1 = jnp.max(slab, axis=0, keepdims=True)
        i1 = jnp.min(jnp.where(slab == m1, row8, GROUP_SIZE), axis=0, keepdims=True)
        rest = jnp.where(row8 == i1, -jnp.inf, slab)
        m2 = jnp.max(rest, axis=0, keepdims=True)
        i2 = jnp.min(jnp.where(rest == m2, row8, GROUP_SIZE), axis=0, keepdims=True)
        cand = (m1 + m2, g * GROUP_SIZE + i1, g * GROUP_SIZE + i2)
        if best is None:
            best = cand
        else:
            upd = cand[0] > best[0]
            best = tuple(jnp.where(upd, n, o) for n, o in zip(cand, best))
    _, e0, e1 = best

    row = lax.broadcasted_iota(jnp.int32, (E, tt), 0)
    oh0 = row == e0
    oh1 = row == e1
    a0 = jnp.sum(jnp.where(oh0, aff, 0.0), axis=0, keepdims=True)
    a1 = jnp.sum(jnp.where(oh1, aff, 0.0), axis=0, keepdims=True)
    w_ref[0:1, :] = a0 / (a0 + a1)
    w_ref[1:2, :] = a1 / (a0 + a1)
    e_ref[0:1, :] = e0
    e_ref[1:2, :] = e1

    oh = jnp.logical_or(oh0, oh1)
    ohf = jnp.where(oh, 1.0, 0.0)
    before = jnp.dot(ohf.astype(BF16), tri_ref[...], preferred_element_type=F32) + carry_scr[:, 0:1]
    r_ref[0:1, :] = jnp.sum(jnp.where(oh0, before, 0.0), axis=0, keepdims=True).astype(jnp.int32)
    r_ref[1:2, :] = jnp.sum(jnp.where(oh1, before, 0.0), axis=0, keepdims=True).astype(jnp.int32)
    carry = carry_scr[...] + jnp.sum(ohf, axis=1, keepdims=True)
    carry_scr[...] = carry
    cnt_ref[...] = carry


def _route(logits_t, router_b):
    E, T = logits_t.shape
    tt = min(ROUTE_TT, T)
    tri = jnp.asarray(np.triu(np.ones((tt, tt), np.float32), k=1), dtype=BF16)
    pair = lambda dt: jax.ShapeDtypeStruct((TOP_K, T), dt)
    return pl.pallas_call(
        _route_kernel,
        grid=(T // tt,),
        in_specs=[
            pl.BlockSpec((E, tt), lambda i: (0, i)),
            pl.BlockSpec((E, 1), lambda i: (0, 0)),
            pl.BlockSpec((tt, tt), lambda i: (0, 0)),
        ],
        out_specs=[
            pl.BlockSpec((TOP_K, tt), lambda i: (0, i)),
            pl.BlockSpec((TOP_K, tt), lambda i: (0, i)),
            pl.BlockSpec((TOP_K, tt), lambda i: (0, i)),
            pl.BlockSpec((E, LANES), lambda i: (0, 0)),
        ],
        out_shape=[pair(jnp.int32), pair(jnp.int32), pair(F32), jax.ShapeDtypeStruct((E, LANES), F32)],
        scratch_shapes=[pltpu.VMEM((E, LANES), F32)],
        compiler_params=_cparams(("arbitrary",)),
        name="route",
    )(logits_t, router_b.reshape(E, 1), tri)


def _block_tables(counts, T):
    E = counts.shape[0]
    A = T * TOP_K
    B = MOE_ROWS
    seg = (counts + SUBLANES - 1) // SUBLANES * SUBLANES
    seg_end = jnp.cumsum(seg)
    seg_start = seg_end - seg
    nb = (A + E * (B - 1) + B - 1) // B
    nblk = (counts + B - 1) // B
    blk_end = jnp.cumsum(nblk)
    total = blk_end[-1]
    b = jnp.arange(nb, dtype=jnp.int32)
    bc = jnp.minimum(b, total - 1)
    blk_e = jnp.minimum(jnp.searchsorted(blk_end, bc, side="right"), E - 1).astype(jnp.int32)
    local = bc - (blk_end[blk_e] - nblk[blk_e])
    blk_start = (seg_start[blk_e] + local * B).astype(jnp.int32)
    blk_first = jnp.logical_and(b < total, local == 0).astype(jnp.int32)
    blk_n = jnp.clip(counts[blk_e] - local * B, 0, B).astype(jnp.int32)
    nonempty = counts > 0
    order = jnp.cumsum(nonempty.astype(jnp.int32)) - 1
    ids = jnp.where(nonempty, jnp.arange(E, dtype=jnp.int32), E)
    later = jnp.concatenate([lax.cummin(ids, reverse=True)[1:], jnp.full((1,), E, jnp.int32)])
    next_e = jnp.where(later < E, later, -1)
    blk_wslot = (order[blk_e] % 2).astype(jnp.int32)
    blk_next_e = next_e[blk_e].astype(jnp.int32)
    rows = A + E * (SUBLANES - 1) + B
    rows = (rows + SUBLANES - 1) // SUBLANES * SUBLANES
    blocks = (blk_e, blk_start, blk_first, blk_n, blk_wslot, blk_next_e,
              total.astype(jnp.int32).reshape(1))
    return seg_start.astype(jnp.int32), seg_end.astype(jnp.int32), blocks, nb, rows


def _zero_rows_from(zero_ref, hbm_ref, start, sem):
    piece = zero_ref.shape[0]
    rows = hbm_ref.shape[0]

    def body(j, carry):
        at = pl.multiple_of(jnp.minimum(start + j * piece, rows - piece), SUBLANES)
        cp = pltpu.make_async_copy(zero_ref, hbm_ref.at[pl.ds(at, piece), :], sem)
        cp.start()
        cp.wait()
        return carry

    lax.fori_loop(0, (rows - start + piece - 1) // piece, body, 0)


def _dispatch_kernel(seg_end_ref, pos_ref, h_ref, xs_hbm, zero_scr, sem, zsem):
    i = pl.program_id(0)
    td = pos_ref.shape[1]
    E = seg_end_ref.shape[0]

    @pl.when(i == 0)
    def _():
        zero_scr[...] = jnp.zeros_like(zero_scr)

        def tail(e, carry):
            end = seg_end_ref[e]
            at = pl.multiple_of(jnp.maximum(end - SUBLANES, 0), SUBLANES)
            cp = pltpu.make_async_copy(zero_scr.at[pl.ds(0, SUBLANES), :],
                                       xs_hbm.at[pl.ds(at, SUBLANES), :], zsem)
            cp.start()
            cp.wait()
            return carry

        lax.fori_loop(0, E, tail, 0)
        _zero_rows_from(zero_scr, xs_hbm, seg_end_ref[E - 1], zsem)

    def body(r, carry):
        for k in range(TOP_K):
            pltpu.make_async_copy(h_ref.at[pl.ds(r, 1), :],
                                  xs_hbm.at[pl.ds(pos_ref[k, r], 1), :], sem).start(priority=k)
        return carry

    lax.fori_loop(0, td, body, 0, unroll=8)
    for k in range(TOP_K):
        pltpu.make_async_copy(h_ref, xs_hbm.at[pl.ds(0, td), :], sem).wait()


def _dispatch(h2, pos, seg_end, rows):
    T, D = h2.shape
    td = min(DISPATCH_TT, T)
    grid_spec = pltpu.PrefetchScalarGridSpec(
        num_scalar_prefetch=1,
        grid=(T // td,),
        in_specs=[pl.BlockSpec((TOP_K, td), lambda i, se: (0, i), memory_space=pltpu.SMEM),
                  pl.BlockSpec((td, D), lambda i, se: (i, 0))],
        out_specs=pl.BlockSpec(memory_space=pl.ANY),
        scratch_shapes=[pltpu.VMEM((MOE_ROWS, D), h2.dtype), pltpu.SemaphoreType.DMA(()),
                        pltpu.SemaphoreType.DMA(())],
    )
    return pl.pallas_call(
        _dispatch_kernel,
        grid_spec=grid_spec,
        out_shape=jax.ShapeDtypeStruct((rows, D), h2.dtype),
        compiler_params=_cparams(("arbitrary",), disable_bounds_checks=True),
        name="dispatch",
    )(seg_end, pos, h2)


def _experts_kernel(blk_e_ref, blk_start_ref, blk_first_ref, blk_n_ref, blk_wslot_ref, blk_next_ref,
                    total_ref, tail_ref, xs_hbm, wg_hbm, wu_hbm, wd_hbm, ys_hbm,
                    wg_f, wu_f, wd_f, wg_b, wu_b, wd_b, xbuf, ybuf, isem, osem, wsem, *, layer):
    b = pl.program_id(0)
    total = total_ref[0]
    B = xbuf.shape[1]
    slot = b % 2

    def load(blk, to_slot):
        at = pl.multiple_of(blk_start_ref[blk], SUBLANES)
        return pltpu.make_async_copy(xs_hbm.at[pl.ds(at, B), :], xbuf.at[to_slot], isem.at[to_slot])

    def store(blk, from_slot):
        at = pl.multiple_of(blk_start_ref[blk], SUBLANES)
        return pltpu.make_async_copy(ybuf.at[from_slot], ys_hbm.at[pl.ds(at, B), :], osem)

    def weights(e, ws):
        return [pltpu.make_async_copy(src.at[layer, e], dst.at[ws], wsem.at[ws])
                for src, dst in ((wg_hbm, wg_f), (wu_hbm, wu_f), (wd_hbm, wd_f))]

    @pl.when(b == 0)
    def _():
        load(0, 0).start()
        for cp in weights(blk_e_ref[0], blk_wslot_ref[0]):
            cp.start()
        ybuf[...] = jnp.zeros_like(ybuf)
        _zero_rows_from(ybuf.at[1], ys_hbm, tail_ref[0], osem)

    @pl.when(b + 1 < total)
    def _():
        load(b + 1, 1 - slot).start()

    @pl.when(b < total)
    def _():
        @pl.when(blk_first_ref[b] == 1)
        def _():
            ws = blk_wslot_ref[b]
            for cp in weights(blk_e_ref[b], ws):
                cp.wait()
            wg_b[...] = wg_f[ws].astype(BF16)
            wu_b[...] = wu_f[ws].astype(BF16)
            wd_b[...] = wd_f[ws].astype(BF16)
            nxt = blk_next_ref[b]

            @pl.when(nxt >= 0)
            def _():
                for cp in weights(nxt, 1 - ws):
                    cp.start()

        load(b, slot).wait()

        def ffn(rows):
            lo, hi = _unpack_halves(xbuf[slot, 0:rows, :])
            lo, hi = lo.astype(BF16), hi.astype(BF16)
            half = lo.shape[1]
            g = (jnp.dot(lo, wg_b[:half, :], preferred_element_type=F32)
                 + jnp.dot(hi, wg_b[half:, :], preferred_element_type=F32))
            u = (jnp.dot(lo, wu_b[:half, :], preferred_element_type=F32)
                 + jnp.dot(hi, wu_b[half:, :], preferred_element_type=F32))
            a = (g * jax.nn.sigmoid(g) * u).astype(BF16)
            ybuf[slot, 0:rows, :] = _pack_halves(jnp.dot(a, wd_b[...], preferred_element_type=F32))

        needed = (blk_n_ref[b] + MOE_STEP - 1) // MOE_STEP
        for q in range(1, B // MOE_STEP + 1):
            @pl.when(needed == q)
            def _():
                ffn(q * MOE_STEP)

        @pl.when(b > 0)
        def _():
            store(b - 1, 1 - slot).wait()

        store(b, slot).start()

        @pl.when(b == total - 1)
        def _():
            store(b, slot).wait()


def _experts(xs, layer, w_gate, w_up, w_down, blocks, tail, nb):
    rows, DP = xs.shape
    D, DE = w_gate.shape[-2:]
    B = MOE_ROWS
    anywhere = pl.BlockSpec(memory_space=pl.ANY)
    grid_spec = pltpu.PrefetchScalarGridSpec(
        num_scalar_prefetch=len(blocks) + 1,
        grid=(nb,),
        in_specs=[anywhere, anywhere, anywhere, anywhere],
        out_specs=anywhere,
        scratch_shapes=[
            pltpu.VMEM((2, D, DE), F32), pltpu.VMEM((2, D, DE), F32), pltpu.VMEM((2, DE, D), F32),
            pltpu.VMEM((D, DE), BF16), pltpu.VMEM((D, DE), BF16), pltpu.VMEM((DE, D), BF16),
            pltpu.VMEM((2, B, DP), U32), pltpu.VMEM((2, B, DP), U32),
            pltpu.SemaphoreType.DMA((2,)), pltpu.SemaphoreType.DMA(()), pltpu.SemaphoreType.DMA((2,)),
        ],
    )
    return pl.pallas_call(
        functools.partial(_experts_kernel, layer=layer),
        grid_spec=grid_spec,
        out_shape=jax.ShapeDtypeStruct((rows, DP), U32),
        compiler_params=_cparams(("arbitrary",)),
        name="experts",
    )(*blocks, tail, xs, w_gate, w_up, w_down)


def _combine_kernel(pos_ref, posn_ref, x_ref, ys_hbm, w_ref, g2_ref, fg_ref, o_ref, gbuf, sem,
                    *, final):
    i = pl.program_id(0)
    n = pl.num_programs(0)
    tm = x_ref.shape[0]
    slot = i % 2

    def gather(p_ref, to_slot):
        def body(r, carry):
            for k in range(TOP_K):
                pltpu.make_async_copy(ys_hbm.at[pl.ds(p_ref[k, r], 1), :],
                                      gbuf.at[to_slot, k, pl.ds(r, 1), :], sem.at[to_slot]).start(priority=k)
            return carry

        lax.fori_loop(0, tm, body, 0, unroll=8)

    @pl.when(i == 0)
    def _():
        gather(pos_ref, 0)

    @pl.when(i + 1 < n)
    def _():
        gather(posn_ref, 1 - slot)

    for k in range(TOP_K):
        pltpu.make_async_copy(ys_hbm.at[pl.ds(0, tm), :], gbuf.at[slot, k], sem.at[slot]).wait()

    w = w_ref[...]
    D = x_ref.shape[1]
    half = D // 2
    lo0, hi0 = _unpack_halves(gbuf[slot, 0])
    lo1, hi1 = _unpack_halves(gbuf[slot, 1])
    xl = x_ref[:, :half] + g2_ref[:, :half] * (lo0 * w[:, 0:1] + lo1 * w[:, 1:2])
    xh = x_ref[:, half:] + g2_ref[:, half:] * (hi0 * w[:, 0:1] + hi1 * w[:, 1:2])
    if final:
        ms = (jnp.sum(xl * xl, axis=-1, keepdims=True) + jnp.sum(xh * xh, axis=-1, keepdims=True)) / D
        r = lax.rsqrt(ms + EPS)
        xl = xl * r * fg_ref[:, :half]
        xh = xh * r * fg_ref[:, half:]
    o_ref[:, :half] = xl
    o_ref[:, half:] = xh


def _combine(x, ys, pos, top_w, g2, final_g, final):
    T, D = x.shape
    tm = min(COMB_TM, T)
    n = T // tm
    vec = pl.BlockSpec((1, D), lambda i: (0, 0))
    return pl.pallas_call(
        functools.partial(_combine_kernel, final=final),
        grid=(n,),
        in_specs=[
            pl.BlockSpec((TOP_K, tm), lambda i: (0, i), memory_space=pltpu.SMEM),
            pl.BlockSpec((TOP_K, tm), lambda i: (0, jnp.minimum(i + 1, n - 1)), memory_space=pltpu.SMEM),
            pl.BlockSpec((tm, D), lambda i: (i, 0)),
            pl.BlockSpec(memory_space=pl.ANY),
            pl.BlockSpec((tm, TOP_K), lambda i: (i, 0)),
            vec, vec,
        ],
        out_specs=pl.BlockSpec((tm, D), lambda i: (i, 0)),
        out_shape=jax.ShapeDtypeStruct((T, D), F32),
        scratch_shapes=[pltpu.VMEM((2, TOP_K, tm, ys.shape[1]), ys.dtype), pltpu.SemaphoreType.DMA((2,))],
        compiler_params=_cparams(("arbitrary",), disable_bounds_checks=True),
        name="combine",
    )(pos, pos, x, ys, top_w, g2, final_g)


def kernel(x, c, ada_w, ada_b, norm1_g, norm2_g, w_in, conv_w, ret_gn_g, fox_fb, w_out,
           router_w, router_b, moe_w_gate, moe_w_up, moe_w_down, final_g):
    Bsz, S, D = x.shape
    assert Bsz == 1, "one sequence per call"
    L = ada_w.shape[0]
    T = S
    conv_wd = conv_w.shape[-1]
    ret_qk = RET_HEADS * RET_DK
    ret_v = RET_HEADS * RET_DV
    fox_w = FOX_HEADS * FOX_DH
    tn = INPROJ_TN
    ret_col0 = 3 * conv_wd
    fox_col0 = ret_col0 + 2 * ret_qk + 2 * ret_v
    n_main = fox_col0 + 3 * fox_w
    assert fox_w == tn and fox_col0 % tn == 0
    q_block = fox_col0 // tn
    v_block = q_block + 2
    k_col0 = fox_col0

    mod = _ada_mod(c, ada_w, ada_b)
    cos_t, sin_t = _rope_tables(T)
    rw_t = router_w.T
    xt = x.reshape(T, D)

    for l in range(L):
        sh1, sc1, g1, sh2, sc2, g2 = [mod[l, :, k * D:(k + 1) * D] for k in range(6)]
        w_ff = jnp.pad(w_in[l, :, n_main:], ((0, 0), (0, LANES - FOX_HEADS)))
        fb = jnp.pad(fox_fb[l], (0, LANES - FOX_HEADS)).reshape(1, LANES)
        proj, proj_t, cum = _inproj(xt, norm1_g[l].reshape(1, D), sh1, sc1, w_in, l, w_ff, fb,
                                    q_block, v_block, LOG2E * FOX_DH ** -0.5)

        y_conv = _conv(proj, conv_w[l])
        y_ret = _retention(proj, cos_t, sin_t, ret_gn_g[l].reshape(1, ret_v), ret_col0)
        y_fox = _fox(proj, proj_t, cum, k_col0)

        wo = w_out[l].astype(BF16)
        xt, h2, logits_t = _outproj(
            y_conv, y_ret, y_fox, wo[:conv_wd], wo[conv_wd:conv_wd + ret_v], wo[conv_wd + ret_v:],
            xt, g1, norm2_g[l].reshape(1, D), sh2, sc2, rw_t)

        top_e, rank, top_w, cnt = _route(logits_t, router_b)
        counts = cnt[:, 0].astype(jnp.int32)
        seg_start, seg_end, blocks, nb, rows = _block_tables(counts, T)
        hit = top_e[None] == jnp.arange(N_EXPERTS, dtype=jnp.int32)[:, None, None]
        pos = rank + jnp.sum(jnp.where(hit, seg_start[:, None, None], 0), axis=0)
        xs = _dispatch(h2, pos, seg_end, rows)
        ys = _experts(xs, l, moe_w_gate, moe_w_up, moe_w_down, blocks, seg_end[-1:], nb)
        xt = _combine(xt, ys, pos, top_w.T, g2, final_g.reshape(1, D), final=(l == L - 1))

    return xt.reshape(Bsz, S, D)
```

```python
import functools
import math

import numpy as np
import jax
import jax.numpy as jnp
from jax import lax
from jax.experimental import pallas as pl
from jax.experimental.pallas import tpu as pltpu

F32 = jnp.float32
BF16 = jnp.bfloat16

CHUNK = 64
CONV_K = 3
RET_HEADS = 6
RET_DK = 64
RET_DV = 128
FOX_HEADS = 6
FOX_DH = 128
N_EXPERTS = 64
N_GROUPS = 8
GROUP_SIZE = N_EXPERTS // N_GROUPS
TOP_K = 2
ROPE_BASE = 10000.0
EPS = 1e-6
LOG2E = 1.4426950408889634

LANES = 128
SUBLANES = 8

MOD_TN = 1024
INPROJ_TM = 1024
INPROJ_TN = 768
CONV_TM = 1024
CONV_HALO = 16
RET_ROWS = 256
FOX_TQ = 1024
FOX_SLAB = 16
FOX_SKIP_LOG2 = 160.0
FOX_NORM_SLACK = 1.02
OUT_TM = 512
MOE_ROWS = 256
MOE_STEP = 64
ROUTE_TT = 1024
DISPATCH_TT = 512
COMB_TM = 512
VMEM_LIMIT = 56 * 1024 * 1024


def _cparams(sem, **kw):
    return pltpu.CompilerParams(dimension_semantics=sem, vmem_limit_bytes=VMEM_LIMIT, **kw)


U32 = jnp.uint32


def _pack_halves(x):
    half = x.shape[1] // 2
    xb = x.astype(BF16).astype(F32)
    lo = lax.bitcast_convert_type(xb[:, :half], U32)
    hi = lax.bitcast_convert_type(xb[:, half:], U32)
    return lax.shift_right_logical(lo, U32(16)) | (hi & U32(0xFFFF0000))


def _unpack_halves(w):
    lo = lax.bitcast_convert_type(lax.shift_left(w, U32(16)), F32)
    hi = lax.bitcast_convert_type(w & U32(0xFFFF0000), F32)
    return lo, hi


def _mod_kernel(c_ref, w_ref, b_ref, o_ref):
    c = c_ref[...]
    ca = c * jax.nn.sigmoid(c)
    o_ref[0] = jnp.sum(w_ref[0] * ca, axis=0, keepdims=True) + b_ref[0]


def _ada_mod(c, ada_w, ada_b):
    L, D, N = ada_w.shape
    tn = MOD_TN
    return pl.pallas_call(
        _mod_kernel,
        grid=(L, N // tn),
        in_specs=[
            pl.BlockSpec((D, 1), lambda l, j: (0, 0)),
            pl.BlockSpec((1, D, tn), lambda l, j: (l, 0, j)),
            pl.BlockSpec((1, 1, tn), lambda l, j: (l, 0, j)),
        ],
        out_specs=pl.BlockSpec((1, 1, tn), lambda l, j: (l, 0, j)),
        out_shape=jax.ShapeDtypeStruct((L, 1, N), F32),
        compiler_params=_cparams(("arbitrary", "arbitrary")),
        name="ada_mod",
    )(c.reshape(D, 1), ada_w, ada_b.reshape(L, 1, N))


def _modulated_norm(x, g, sh, sc):
    ms = jnp.mean(x * x, axis=-1, keepdims=True)
    y = x * lax.rsqrt(ms + EPS) * g
    return y * (1.0 + sc) + sh


def _log_sigmoid(x):
    return jnp.minimum(x, 0.0) - jnp.log1p(jnp.exp(-jnp.abs(x)))


def _inproj_kernel(x_ref, g_ref, sh_ref, sc_ref, w_ref, wff_ref, fb_ref,
                   proj_ref, projt_ref, cum_ref, h_scr, wt_scr, carry_scr,
                   *, q_block, v_block, q_scale):
    i = pl.program_id(0)
    j = pl.program_id(1)
    tm = x_ref.shape[0]

    @pl.when(jnp.logical_and(i == 0, j == 0))
    def _():
        carry_scr[...] = jnp.zeros_like(carry_scr)

    @pl.when(j == 0)
    def _():
        h = _modulated_norm(x_ref[...], g_ref[...], sh_ref[...], sc_ref[...])
        hb = h.astype(BF16)
        h_scr[...] = hb
        ff = jnp.dot(hb, wff_ref[...].astype(BF16), preferred_element_type=F32) + fb_ref[...]
        c = _log_sigmoid(ff)
        row = lax.broadcasted_iota(jnp.int32, c.shape, 0)
        d = 1
        while d < tm:
            c = c + jnp.where(row >= d, pltpu.roll(c, d, axis=0), 0.0)
            d *= 2
        c = c + carry_scr[...]
        cum_ref[...] = c
        carry_scr[...] = c[tm - 1:tm, :]

    @pl.when(jnp.logical_and(j != q_block, j != v_block))
    def _():
        proj_ref[...] = jnp.dot(h_scr[...], w_ref[...].astype(BF16),
                                preferred_element_type=F32).astype(BF16)

    def feature_major(slot, scale):
        @pl.when(i == 0)
        def _():
            wt_scr[slot] = w_ref[...].T.astype(BF16)

        r = lax.dot_general(wt_scr[slot], h_scr[...], (((1,), (1,)), ((), ())),
                            preferred_element_type=F32)
        projt_ref[...] = (r * scale).astype(BF16)

    @pl.when(j == q_block)
    def _():
        feature_major(0, q_scale)

    @pl.when(j == v_block)
    def _():
        feature_major(1, 1.0)


def _inproj(x, g, sh, sc, w_in, layer, w_ff, fb, q_block, v_block, q_scale):
    T, D = x.shape
    tm, tn = min(INPROJ_TM, T), INPROJ_TN
    nj = v_block + 1
    is_q = lambda j: (j == q_block).astype(jnp.int32)
    is_v = lambda j: (j == v_block).astype(jnp.int32)
    w_col = lambda i, j: jnp.where(i == 0, j, j - is_q(j) - is_v(j))
    out_col = lambda j: j - (j >= q_block).astype(jnp.int32) - (j >= v_block).astype(jnp.int32)
    vec = pl.BlockSpec((1, D), lambda i, j: (0, 0))
    return pl.pallas_call(
        functools.partial(_inproj_kernel, q_block=q_block, v_block=v_block, q_scale=q_scale),
        grid=(T // tm, nj),
        in_specs=[
            pl.BlockSpec((tm, D), lambda i, j: (i, 0)),
            vec, vec, vec,
            pl.BlockSpec((None, D, tn), lambda i, j: (layer, 0, w_col(i, j))),
            pl.BlockSpec((D, LANES), lambda i, j: (0, 0)),
            pl.BlockSpec((1, LANES), lambda i, j: (0, 0)),
        ],
        out_specs=[
            pl.BlockSpec((tm, tn), lambda i, j: (i, out_col(j))),
            pl.BlockSpec((tn, tm), lambda i, j: (is_v(j), i)),
            pl.BlockSpec((tm, LANES), lambda i, j: (i, 0)),
        ],
        out_shape=[
            jax.ShapeDtypeStruct((T, (nj - 2) * tn), BF16),
            jax.ShapeDtypeStruct((2 * tn, T), BF16),
            jax.ShapeDtypeStruct((T, LANES), F32),
        ],
        scratch_shapes=[pltpu.VMEM((tm, D), BF16), pltpu.VMEM((2, tn, D), BF16),
                        pltpu.VMEM((1, LANES), F32)],
        compiler_params=_cparams(("arbitrary", "arbitrary")),
        name="inproj",
    )(x, g, sh, sc, w_in, w_ff, fb)


def _conv_kernel(cb_ref, cc_ref, cu_ref, hc_ref, hu_ref, w_ref, o_ref):
    i = pl.program_id(0)
    z = cc_ref[...].astype(F32) * cu_ref[...].astype(F32)
    zh = hc_ref[...].astype(F32) * hu_ref[...].astype(F32)
    zh = jnp.where(i > 0, zh, 0.0)
    hl = zh.shape[0]
    zm1 = zh[hl - 1:hl, :]
    zm2 = zh[hl - 2:hl - 1, :]
    row = lax.broadcasted_iota(jnp.int32, z.shape, 0)
    z1 = jnp.where(row == 0, zm1, pltpu.roll(z, 1, axis=0))
    z2 = jnp.where(row == 0, zm2, jnp.where(row == 1, zm1, pltpu.roll(z, 2, axis=0)))
    w = w_ref[...]
    y = z2 * w[0:1, :] + z1 * w[1:2, :] + z * w[2:3, :]
    o_ref[...] = (cb_ref[...].astype(F32) * y).astype(BF16)


def _conv(proj, conv_w):
    T = proj.shape[0]
    W = conv_w.shape[1]
    tm = min(CONV_TM, T)
    hb = tm // CONV_HALO
    halo = lambda c: pl.BlockSpec((CONV_HALO, W), lambda i: (jnp.maximum(i * hb - 1, 0), c))
    return pl.pallas_call(
        _conv_kernel,
        grid=(T // tm,),
        in_specs=[
            pl.BlockSpec((tm, W), lambda i: (i, 0)),
            pl.BlockSpec((tm, W), lambda i: (i, 1)),
            pl.BlockSpec((tm, W), lambda i: (i, 2)),
            halo(1), halo(2),
            pl.BlockSpec((CONV_K, W), lambda i: (0, 0)),
        ],
        out_specs=pl.BlockSpec((tm, W), lambda i: (i, 0)),
        out_shape=jax.ShapeDtypeStruct((T, W), BF16),
        compiler_params=_cparams(("arbitrary",)),
        name="conv",
    )(proj, proj, proj, proj, proj, conv_w)


def _ret_gammas():
    return [1.0 - 2.0 ** (-5.0 - h) for h in range(RET_HEADS)]


def _ret_tables(R):
    n = np.arange(R, dtype=np.float64)
    chunk = np.arange(R) // CHUNK
    allowed = chunk[None, :] <= chunk[:, None]
    dm, qd, kd = [], [], []
    for g in _ret_gammas():
        lg = math.log(g)
        dm.append(np.where(allowed, np.exp(lg * np.abs(n[:, None] - n[None, :])), 0.0))
        qd.append(np.broadcast_to(np.exp(lg * (n + 1.0))[:, None], (R, LANES)))
        kd.append(np.broadcast_to(np.exp(lg * (R - 1.0 - n))[:, None], (R, LANES)))
    f = lambda a: jnp.asarray(np.stack(a), dtype=F32)
    return f(dm), f(qd), f(kd)


def _rope_tables(T):
    half = RET_DK // 2
    inv = ROPE_BASE ** (-jnp.arange(half, dtype=F32) / half)
    ang = jnp.arange(T, dtype=F32)[:, None] * inv[None, :]
    cos, sin = jnp.cos(ang), jnp.sin(ang)
    reps = LANES // RET_DK
    cos_t = jnp.tile(jnp.concatenate([cos, cos], axis=1), (1, reps))
    sin_t = jnp.tile(jnp.concatenate([-sin, sin], axis=1), (1, reps))
    return cos_t, sin_t


def _ret_kernel(q_ref, k_ref, v_ref, g_ref, cos_ref, sin_ref, dm_ref, qd_ref, kd_ref, gn_ref,
                o_ref, s_scr):
    i = pl.program_id(0)
    R = q_ref.shape[0]

    @pl.when(i == 0)
    def _():
        s_scr[...] = jnp.zeros_like(s_scr)

    lane = lax.broadcasted_iota(jnp.int32, (R, LANES), 1)
    first_half = (lane % RET_DK) < (RET_DK // 2)
    low_head = lane < RET_DK
    cosv = cos_ref[...]
    sinv = sin_ref[...]
    c_dec = [g ** R for g in _ret_gammas()]

    def rot(t):
        swapped = jnp.where(first_half, pltpu.roll(t, LANES - RET_DK // 2, axis=1),
                            pltpu.roll(t, RET_DK // 2, axis=1))
        return t * cosv + swapped * sinv

    heads_per_vreg = LANES // RET_DK
    for p in range(RET_HEADS // heads_per_vreg):
        cols = slice(p * LANES, (p + 1) * LANES)
        qr = rot(q_ref[:, cols].astype(F32))
        kb = (rot(k_ref[:, cols].astype(F32)) * (RET_DK ** -0.5)).astype(BF16)
        for hh in range(heads_per_vreg):
            h = p * heads_per_vreg + hh
            hc = slice(h * RET_DV, (h + 1) * RET_DV)
            mask = low_head if hh == 0 else jnp.logical_not(low_head)
            qm = jnp.where(mask, qr, 0.0).astype(BF16)
            s = lax.dot_general(qm, kb, (((1,), (1,)), ((), ())), preferred_element_type=F32)
            s = s * dm_ref[h]
            v = v_ref[:, hc]
            o = jnp.dot(s.astype(BF16), v, preferred_element_type=F32)
            state = s_scr[h]
            o = o + jnp.dot(qm, state.astype(BF16), preferred_element_type=F32) * qd_ref[h]
            vd = (v.astype(F32) * kd_ref[h]).astype(BF16)
            kv = lax.dot_general(kb, vd, (((0,), (0,)), ((), ())), preferred_element_type=F32)
            s_scr[h] = state * c_dec[h] + kv
            mu = jnp.mean(o, axis=-1, keepdims=True)
            d = o - mu
            var = jnp.mean(d * d, axis=-1, keepdims=True)
            on = d * lax.rsqrt(var + EPS) * gn_ref[:, hc]
            gate = g_ref[:, hc].astype(F32)
            o_ref[:, hc] = (gate * jax.nn.sigmoid(gate) * on).astype(BF16)


def _retention(proj, cos_t, sin_t, gn_g, col0):
    T = proj.shape[0]
    R = min(RET_ROWS, T)
    QK = RET_HEADS * RET_DK
    V = RET_HEADS * RET_DV
    dm, qd, kd = _ret_tables(R)
    q_blk = col0 // QK
    v_blk = (col0 + 2 * QK) // V
    full3 = lambda a: pl.BlockSpec(a.shape, lambda i: (0, 0, 0))
    return pl.pallas_call(
        _ret_kernel,
        grid=(T // R,),
        in_specs=[
            pl.BlockSpec((R, QK), lambda i: (i, q_blk)),
            pl.BlockSpec((R, QK), lambda i: (i, q_blk + 1)),
            pl.BlockSpec((R, V), lambda i: (i, v_blk)),
            pl.BlockSpec((R, V), lambda i: (i, v_blk + 1)),
            pl.BlockSpec((R, LANES), lambda i: (i, 0)),
            pl.BlockSpec((R, LANES), lambda i: (i, 0)),
            full3(dm), full3(qd), full3(kd),
            pl.BlockSpec((1, V), lambda i: (0, 0)),
        ],
        out_specs=pl.BlockSpec((R, V), lambda i: (i, 0)),
        out_shape=jax.ShapeDtypeStruct((T, V), BF16),
        scratch_shapes=[pltpu.VMEM((RET_HEADS, LANES, RET_DV), F32)],
        compiler_params=_cparams(("arbitrary",)),
        name="retention",
    )(proj, proj, proj, proj, cos_t, sin_t, dm, qd, kd, gn_g)


def _fox_kernel(qt_ref, k_ref, vt_ref, cum_ref, o_ref, ka_scr, cend_scr, kn_scr,
                s0_scr, s1_scr, p0_scr, p1_scr, a0_scr, a1_scr, m_scr, l_scr, acc_scr):
    h = pl.program_id(0)
    qi = pl.program_id(1)
    tq = qt_ref.shape[1]
    tk = s0_scr.shape[0]
    T = k_ref.shape[0]
    assert tq == 2 * tk
    s_scr, p_scr, a_scr = (s0_scr, s1_scr), (p0_scr, p1_scr), (a0_scr, a1_scr)

    @pl.when(qi == 0)
    def _():
        cend_scr[...] = jnp.zeros_like(cend_scr)
        ones = jnp.ones((FOX_DH, LANES), BF16)
        kn2 = jnp.zeros((SUBLANES, LANES), F32)
        for r in range(0, T, tq):
            kf = k_ref[r:r + tq, :].astype(F32)
            n2 = jnp.dot((kf * kf).astype(BF16), ones, preferred_element_type=F32)
            kn2 = jnp.maximum(kn2, jnp.max(n2.reshape(tq // SUBLANES, SUBLANES, LANES), axis=0))
        kn_scr[...] = jnp.max(kn2, axis=0, keepdims=True)

    lane = lax.broadcasted_iota(jnp.int32, cum_ref.shape, 1)
    col = jnp.sum(jnp.where(lane == h, cum_ref[...], 0.0), axis=1, keepdims=True) * LOG2E
    q0 = pl.multiple_of(qi * tq, tq)
    ck_rep = jnp.broadcast_to(col, cum_ref.shape)
    bf = lambda v: v.astype(BF16).astype(F32)
    c_hi = bf(col)
    c_mid = bf(col - c_hi)
    c_lo = bf(col - c_hi - c_mid)
    extra = jnp.where(lane == 0, c_hi, jnp.where(lane == 1, c_mid, jnp.where(lane == 2, c_lo, 0.0)))
    ka_scr[pl.ds(q0, tq), 0:FOX_DH] = k_ref[pl.ds(q0, tq), :]
    ka_scr[pl.ds(q0, tq), FOX_DH:2 * FOX_DH] = extra.astype(BF16)
    sub = lax.broadcasted_iota(jnp.int32, (FOX_DH, tq), 0)
    q_ext = jnp.concatenate([qt_ref[...], jnp.where(sub < 3, -1.0, 0.0).astype(BF16)], axis=0)
    cend_scr[pl.ds(2 * qi, 1), :] = ck_rep[tk - 1:tk, :]
    cend_scr[pl.ds(2 * qi + 1, 1), :] = ck_rep[tq - 1:tq, :]

    qf = qt_ref[...].astype(F32)
    qn2 = jnp.max(jnp.sum(qf * qf, axis=0, keepdims=True), axis=1, keepdims=True)
    reach = 2.0 * FOX_NORM_SLACK * jnp.sqrt(qn2 * kn_scr[...]) + FOX_SKIP_LOG2
    decay = cend_scr[...] - ck_rep[0:1, :]
    chunk_id = lax.broadcasted_iota(jnp.int32, decay.shape, 0)
    dead = jnp.logical_and(decay >= reach, chunk_id < 2 * qi)
    n_dead = jnp.max(jnp.sum(dead.astype(jnp.int32), axis=0, keepdims=True))
    first_pair = (n_dead + 1) // 2

    m_scr[...] = jnp.full_like(m_scr, -jnp.inf)
    l_scr[...] = jnp.zeros_like(l_scr)
    acc_scr[...] = jnp.zeros_like(acc_scr)
    for p_ref, a_ref in zip(p_scr, a_scr):
        p_ref[...] = jnp.zeros_like(p_ref)
        a_ref[...] = jnp.ones_like(a_ref)

    everyone = slice(0, tq)

    def score(c, slot, qs=everyone):
        k0 = pl.multiple_of(c * tk, tk)
        s_scr[slot][:, qs] = jnp.dot(ka_scr[pl.ds(k0, tk), :], q_ext[:, qs],
                                     preferred_element_type=F32)

    def softmax(c, slot, masked, qs=everyone):
        nq = qs.stop - qs.start
        sref, pref = s_scr[slot], p_scr[slot]
        if masked:
            s = sref[:, qs]
            kpos = c * tk + lax.broadcasted_iota(jnp.int32, s.shape, 0)
            qpos = q0 + qs.start + lax.broadcasted_iota(jnp.int32, s.shape, 1)
            sref[:, qs] = jnp.where(kpos <= qpos, s, -jnp.inf)
        mx = sref[0:FOX_SLAB, qs]
        for r in range(FOX_SLAB, tk, FOX_SLAB):
            mx = jnp.maximum(mx, sref[r:r + FOX_SLAB, qs])
        m_old = m_scr[:, qs]
        m_new = jnp.maximum(m_old, jnp.max(mx, axis=0, keepdims=True))
        alpha = jnp.exp2(m_old - m_new)
        m_rows = jnp.broadcast_to(m_new, (FOX_SLAB, nq))
        psum = jnp.zeros((FOX_SLAB, nq), F32)
        for r in range(0, tk, FOX_SLAB):
            p = jnp.exp2(sref[r:r + FOX_SLAB, qs] - m_rows)
            psum = psum + p
            pref[r:r + FOX_SLAB, qs] = p.astype(BF16)
        l_scr[:, qs] = alpha * l_scr[:, qs] + jnp.sum(psum, axis=0, keepdims=True)
        a_scr[slot][:, qs] = alpha
        m_scr[:, qs] = m_new

    def accumulate(c, slot, qs=everyone):
        k0 = pl.multiple_of(jnp.maximum(c, 0) * tk, tk)
        pv = jnp.dot(vt_ref[:, pl.ds(k0, tk)], p_scr[slot][:, qs], preferred_element_type=F32)
        acc_scr[:, qs] = a_scr[slot][:, qs] * acc_scr[:, qs] + pv

    @pl.when(n_dead % 2 == 0)
    def _():
        score(n_dead, 0)

    @pl.when(n_dead % 2 == 1)
    def _():
        score(n_dead, 1)
        score(n_dead + 1, 0)
        accumulate(n_dead - 1, 0)
        softmax(n_dead, 1, False)

    def pair(i, carry):
        c = 2 * i
        score(c + 1, 1)
        accumulate(c - 1, 1)
        softmax(c, 0, False)
        score(c + 2, 0)
        accumulate(c, 0)
        softmax(c + 1, 1, False)
        return carry

    lax.fori_loop(first_pair, qi, pair, 0)
    c = 2 * qi
    late = slice(tk, tq)
    score(c + 1, 1, late)
    accumulate(c - 1, 1)
    softmax(c, 0, True)
    accumulate(c, 0)
    softmax(c + 1, 1, True, late)
    accumulate(c + 1, 1, late)
    o_ref[...] = (acc_scr[...] / l_scr[...]).T.astype(BF16)


def _fox(proj, proj_t, cum, k_col0):
    T = proj.shape[0]
    tq = min(FOX_TQ, T)
    tk = tq // 2
    W = FOX_HEADS * FOX_DH
    kb = k_col0 // FOX_DH
    return pl.pallas_call(
        _fox_kernel,
        grid=(FOX_HEADS, T // tq),
        in_specs=[
            pl.BlockSpec((FOX_DH, tq), lambda h, i: (h, i)),
            pl.BlockSpec((T, FOX_DH), lambda h, i: (0, kb + h)),
            pl.BlockSpec((FOX_DH, T), lambda h, i: (FOX_HEADS + h, 0)),
            pl.BlockSpec((tq, LANES), lambda h, i: (i, 0)),
        ],
        out_specs=pl.BlockSpec((tq, FOX_DH), lambda h, i: (i, h)),
        out_shape=jax.ShapeDtypeStruct((T, W), BF16),
        scratch_shapes=[pltpu.VMEM((T, 2 * FOX_DH), BF16),
                        pltpu.VMEM((-(-(T // tk) // SUBLANES) * SUBLANES, LANES), F32),
                        pltpu.VMEM((1, LANES), F32),
                        pltpu.VMEM((tk, tq), F32), pltpu.VMEM((tk, tq), F32),
                        pltpu.VMEM((tk, tq), BF16), pltpu.VMEM((tk, tq), BF16),
                        pltpu.VMEM((1, tq), F32), pltpu.VMEM((1, tq), F32),
                        pltpu.VMEM((1, tq), F32), pltpu.VMEM((1, tq), F32),
                        pltpu.VMEM((FOX_DH, tq), F32)],
        compiler_params=_cparams(("arbitrary", "arbitrary")),
        name="fox",
    )(proj_t, proj, proj_t, cum)


def _outproj_kernel(yc_ref, yr_ref, yf_ref, wc_ref, wr_ref, wf_ref, x_ref, g1_ref,
                    g_ref, sh_ref, sc_ref, rwh_ref, rwl_ref, xo_ref, h_ref, lg_ref):
    mix = jnp.dot(yc_ref[...], wc_ref[...], preferred_element_type=F32)
    mix = mix + jnp.dot(yr_ref[...], wr_ref[...], preferred_element_type=F32)
    mix = mix + jnp.dot(yf_ref[...], wf_ref[...], preferred_element_type=F32)
    x = x_ref[...] + g1_ref[...] * mix
    xo_ref[...] = x
    h = _modulated_norm(x, g_ref[...], sh_ref[...], sc_ref[...])
    h_ref[...] = _pack_halves(h)
    h_hi = h.astype(BF16)
    h_lo = (h - h_hi.astype(F32)).astype(BF16)
    nt = lambda a, b: lax.dot_general(a, b, (((1,), (1,)), ((), ())), preferred_element_type=F32)
    lg_ref[...] = nt(rwh_ref[...], h_hi) + (nt(rwh_ref[...], h_lo) + nt(rwl_ref[...], h_hi))


def _outproj(yc, yr, yf, wc, wr, wf, x, g1, g, sh, sc, rw_t):
    T, D = x.shape
    tm = min(OUT_TM, T)
    E = rw_t.shape[0]
    rw_hi = rw_t.astype(BF16)
    rw_lo = (rw_t - rw_hi.astype(F32)).astype(BF16)
    vec = pl.BlockSpec((1, D), lambda i: (0, 0))
    rows = lambda a: pl.BlockSpec((tm, a.shape[1]), lambda i: (i, 0))
    whole = lambda a: pl.BlockSpec(a.shape, lambda i: (0, 0))
    return pl.pallas_call(
        _outproj_kernel,
        grid=(T // tm,),
        in_specs=[rows(yc), rows(yr), rows(yf), whole(wc), whole(wr), whole(wf), rows(x),
                  vec, vec, vec, vec, whole(rw_hi), whole(rw_lo)],
        out_specs=[rows(x), pl.BlockSpec((tm, D // 2), lambda i: (i, 0)),
                   pl.BlockSpec((E, tm), lambda i: (0, i))],
        out_shape=[jax.ShapeDtypeStruct((T, D), F32), jax.ShapeDtypeStruct((T, D // 2), U32),
                   jax.ShapeDtypeStruct((E, T), F32)],
        compiler_params=_cparams(("arbitrary",)),
        name="outproj",
    )(yc, yr, yf, wc, wr, wf, x, g1, g, sh, sc, rw_hi, rw_lo)


def _route_kernel(lg_ref, b_ref, tri_ref, e_ref, r_ref, w_ref, cnt_ref, carry_scr):
    i = pl.program_id(0)
    E, tt = lg_ref.shape

    @pl.when(i == 0)
    def _():
        carry_scr[...] = jnp.zeros_like(carry_scr)

    aff = jax.nn.sigmoid(lg_ref[...])
    sel = aff + b_ref[...]
    row8 = lax.broadcasted_iota(jnp.int32, (GROUP_SIZE, tt), 0)
    best = None
    for g in range(N_GROUPS):
        slab = sel[g * GROUP_SIZE:(g + 1) * GROUP_SIZE, :]
        m1 = jnp.max(slab, axis=0, keepdims=True)
        i1 = jnp.min(jnp.where(slab == m1, row8, GROUP_SIZE), axis=0, keepdims=True)
        rest = jnp.where(row8 == i1, -jnp.inf, slab)
        m2 = jnp.max(rest, axis=0, keepdims=True)
        i2 = jnp.min(jnp.where(rest == m2, row8, GROUP_SIZE), axis=0, keepdims=True)
        cand = (m1 + m2, g * GROUP_SIZE + i1, g * GROUP_SIZE + i2)
        if best is None:
            best = cand
        else:
            upd = cand[0] > best[0]
            best = tuple(jnp.where(upd, n, o) for n, o in zip(cand, best))
    _, e0, e1 = best

    row = lax.broadcasted_iota(jnp.int32, (E, tt), 0)
    oh0 = row == e0
    oh1 = row == e1
    a0 = jnp.sum(jnp.where(oh0, aff, 0.0), axis=0, keepdims=True)
    a1 = jnp.sum(jnp.where(oh1, aff, 0.0), axis=0, keepdims=True)
    w_ref[0:1, :] = a0 / (a0 + a1)
    w_ref[1:2, :] = a1 / (a0 + a1)
    e_ref[0:1, :] = e0
    e_ref[1:2, :] = e1

    oh = jnp.logical_or(oh0, oh1)
    ohf = jnp.where(oh, 1.0, 0.0)
    before = jnp.dot(ohf.astype(BF16), tri_ref[...], preferred_element_type=F32) + carry_scr[:, 0:1]
    r_ref[0:1, :] = jnp.sum(jnp.where(oh0, before, 0.0), axis=0, keepdims=True).astype(jnp.int32)
    r_ref[1:2, :] = jnp.sum(jnp.where(oh1, before, 0.0), axis=0, keepdims=True).astype(jnp.int32)
    carry = carry_scr[...] + jnp.sum(ohf, axis=1, keepdims=True)
    carry_scr[...] = carry
    cnt_ref[...] = carry


def _route(logits_t, router_b):
    E, T = logits_t.shape
    tt = min(ROUTE_TT, T)
    tri = jnp.asarray(np.triu(np.ones((tt, tt), np.float32), k=1), dtype=BF16)
    pair = lambda dt: jax.ShapeDtypeStruct((TOP_K, T), dt)
    return pl.pallas_call(
        _route_kernel,
        grid=(T // tt,),
        in_specs=[
            pl.BlockSpec((E, tt), lambda i: (0, i)),
            pl.BlockSpec((E, 1), lambda i: (0, 0)),
            pl.BlockSpec((tt, tt), lambda i: (0, 0)),
        ],
        out_specs=[
            pl.BlockSpec((TOP_K, tt), lambda i: (0, i)),
            pl.BlockSpec((TOP_K, tt), lambda i: (0, i)),
            pl.BlockSpec((TOP_K, tt), lambda i: (0, i)),
            pl.BlockSpec((E, LANES), lambda i: (0, 0)),
        ],
        out_shape=[pair(jnp.int32), pair(jnp.int32), pair(F32), jax.ShapeDtypeStruct((E, LANES), F32)],
        scratch_shapes=[pltpu.VMEM((E, LANES), F32)],
        compiler_params=_cparams(("arbitrary",)),
        name="route",
    )(logits_t, router_b.reshape(E, 1), tri)


def _block_tables(counts, T):
    E = counts.shape[0]
    A = T * TOP_K
    B = MOE_ROWS
    seg = (counts + SUBLANES - 1) // SUBLANES * SUBLANES
    seg_end = jnp.cumsum(seg)
    seg_start = seg_end - seg
    nb = (A + E * (B - 1) + B - 1) // B
    nblk = (counts + B - 1) // B
    blk_end = jnp.cumsum(nblk)
    total = blk_end[-1]
    b = jnp.arange(nb, dtype=jnp.int32)
    bc = jnp.minimum(b, total - 1)
    blk_e = jnp.minimum(jnp.searchsorted(blk_end, bc, side="right"), E - 1).astype(jnp.int32)
    local = bc - (blk_end[blk_e] - nblk[blk_e])
    blk_start = (seg_start[blk_e] + local * B).astype(jnp.int32)
    blk_first = jnp.logical_and(b < total, local == 0).astype(jnp.int32)
    blk_n = jnp.clip(counts[blk_e] - local * B, 0, B).astype(jnp.int32)
    nonempty = counts > 0
    order = jnp.cumsum(nonempty.astype(jnp.int32)) - 1
    ids = jnp.where(nonempty, jnp.arange(E, dtype=jnp.int32), E)
    later = jnp.concatenate([lax.cummin(ids, reverse=True)[1:], jnp.full((1,), E, jnp.int32)])
    next_e = jnp.where(later < E, later, -1)
    blk_wslot = (order[blk_e] % 2).astype(jnp.int32)
    blk_next_e = next_e[blk_e].astype(jnp.int32)
    rows = A + E * (SUBLANES - 1) + B
    rows = (rows + SUBLANES - 1) // SUBLANES * SUBLANES
    blocks = (blk_e, blk_start, blk_first, blk_n, blk_wslot, blk_next_e,
              total.astype(jnp.int32).reshape(1))
    return seg_start.astype(jnp.int32), seg_end.astype(jnp.int32), blocks, nb, rows


def _zero_rows_from(zero_ref, hbm_ref, start, sem):
    piece = zero_ref.shape[0]
    rows = hbm_ref.shape[0]

    def body(j, carry):
        at = pl.multiple_of(jnp.minimum(start + j * piece, rows - piece), SUBLANES)
        cp = pltpu.make_async_copy(zero_ref, hbm_ref.at[pl.ds(at, piece), :], sem)
        cp.start()
        cp.wait()
        return carry

    lax.fori_loop(0, (rows - start + piece - 1) // piece, body, 0)


def _dispatch_kernel(seg_end_ref, pos_ref, h_ref, xs_hbm, zero_scr, sem, zsem):
    i = pl.program_id(0)
    td = pos_ref.shape[1]
    E = seg_end_ref.shape[0]

    @pl.when(i == 0)
    def _():
        zero_scr[...] = jnp.zeros_like(zero_scr)

        def tail(e, carry):
            end = seg_end_ref[e]
            at = pl.multiple_of(jnp.maximum(end - SUBLANES, 0), SUBLANES)
            cp = pltpu.make_async_copy(zero_scr.at[pl.ds(0, SUBLANES), :],
                                       xs_hbm.at[pl.ds(at, SUBLANES), :], zsem)
            cp.start()
            cp.wait()
            return carry

        lax.fori_loop(0, E, tail, 0)
        _zero_rows_from(zero_scr, xs_hbm, seg_end_ref[E - 1], zsem)

    def body(r, carry):
        for k in range(TOP_K):
            pltpu.make_async_copy(h_ref.at[pl.ds(r, 1), :],
                                  xs_hbm.at[pl.ds(pos_ref[k, r], 1), :], sem).start(priority=k)
        return carry

    lax.fori_loop(0, td, body, 0, unroll=8)
    for k in range(TOP_K):
        pltpu.make_async_copy(h_ref, xs_hbm.at[pl.ds(0, td), :], sem).wait()


def _dispatch(h2, pos, seg_end, rows):
    T, D = h2.shape
    td = min(DISPATCH_TT, T)
    grid_spec = pltpu.PrefetchScalarGridSpec(
        num_scalar_prefetch=1,
        grid=(T // td,),
        in_specs=[pl.BlockSpec((TOP_K, td), lambda i, se: (0, i), memory_space=pltpu.SMEM),
                  pl.BlockSpec((td, D), lambda i, se: (i, 0))],
        out_specs=pl.BlockSpec(memory_space=pl.ANY),
        scratch_shapes=[pltpu.VMEM((MOE_ROWS, D), h2.dtype), pltpu.SemaphoreType.DMA(()),
                        pltpu.SemaphoreType.DMA(())],
    )
    return pl.pallas_call(
        _dispatch_kernel,
        grid_spec=grid_spec,
        out_shape=jax.ShapeDtypeStruct((rows, D), h2.dtype),
        compiler_params=_cparams(("arbitrary",), disable_bounds_checks=True),
        name="dispatch",
    )(seg_end, pos, h2)


def _experts_kernel(blk_e_ref, blk_start_ref, blk_first_ref, blk_n_ref, blk_wslot_ref, blk_next_ref,
                    total_ref, tail_ref, xs_hbm, wg_hbm, wu_hbm, wd_hbm, ys_hbm,
                    wg_f, wu_f, wd_f, wg_b, wu_b, wd_b, xbuf, ybuf, isem, osem, wsem, *, layer):
    b = pl.program_id(0)
    total = total_ref[0]
    B = xbuf.shape[1]
    slot = b % 2

    def load(blk, to_slot):
        at = pl.multiple_of(blk_start_ref[blk], SUBLANES)
        return pltpu.make_async_copy(xs_hbm.at[pl.ds(at, B), :], xbuf.at[to_slot], isem.at[to_slot])

    def store(blk, from_slot):
        at = pl.multiple_of(blk_start_ref[blk], SUBLANES)
        return pltpu.make_async_copy(ybuf.at[from_slot], ys_hbm.at[pl.ds(at, B), :], osem)

    def weights(e, ws):
        return [pltpu.make_async_copy(src.at[layer, e], dst.at[ws], wsem.at[ws])
                for src, dst in ((wg_hbm, wg_f), (wu_hbm, wu_f), (wd_hbm, wd_f))]

    @pl.when(b == 0)
    def _():
        load(0, 0).start()
        for cp in weights(blk_e_ref[0], blk_wslot_ref[0]):
            cp.start()
        ybuf[...] = jnp.zeros_like(ybuf)
        _zero_rows_from(ybuf.at[1], ys_hbm, tail_ref[0], osem)

    @pl.when(b + 1 < total)
    def _():
        load(b + 1, 1 - slot).start()

    @pl.when(b < total)
    def _():
        @pl.when(blk_first_ref[b] == 1)
        def _():
            ws = blk_wslot_ref[b]
            for cp in weights(blk_e_ref[b], ws):
                cp.wait()
            wg_b[...] = wg_f[ws].astype(BF16)
            wu_b[...] = wu_f[ws].astype(BF16)
            wd_b[...] = wd_f[ws].astype(BF16)
            nxt = blk_next_ref[b]

            @pl.when(nxt >= 0)
            def _():
                for cp in weights(nxt, 1 - ws):
                    cp.start()

        load(b, slot).wait()

        def ffn(rows):
            lo, hi = _unpack_halves(xbuf[slot, 0:rows, :])
            lo, hi = lo.astype(BF16), hi.astype(BF16)
            half = lo.shape[1]
            g = (jnp.dot(lo, wg_b[:half, :], preferred_element_type=F32)
                 + jnp.dot(hi, wg_b[half:, :], preferred_element_type=F32))
            u = (jnp.dot(lo, wu_b[:half, :], preferred_element_type=F32)
                 + jnp.dot(hi, wu_b[half:, :], preferred_element_type=F32))
            a = (g * jax.nn.sigmoid(g) * u).astype(BF16)
            ybuf[slot, 0:rows, :] = _pack_halves(jnp.dot(a, wd_b[...], preferred_element_type=F32))

        needed = (blk_n_ref[b] + MOE_STEP - 1) // MOE_STEP
        for q in range(1, B // MOE_STEP + 1):
            @pl.when(needed == q)
            def _():
                ffn(q * MOE_STEP)

        @pl.when(b > 0)
        def _():
            store(b - 1, 1 - slot).wait()

        store(b, slot).start()

        @pl.when(b == total - 1)
        def _():
            store(b, slot).wait()


def _experts(xs, layer, w_gate, w_up, w_down, blocks, tail, nb):
    rows, DP = xs.shape
    D, DE = w_gate.shape[-2:]
    B = MOE_ROWS
    anywhere = pl.BlockSpec(memory_space=pl.ANY)
    grid_spec = pltpu.PrefetchScalarGridSpec(
        num_scalar_prefetch=len(blocks) + 1,
        grid=(nb,),
        in_specs=[anywhere, anywhere, anywhere, anywhere],
        out_specs=anywhere,
        scratch_shapes=[
            pltpu.VMEM((2, D, DE), F32), pltpu.VMEM((2, D, DE), F32), pltpu.VMEM((2, DE, D), F32),
            pltpu.VMEM((D, DE), BF16), pltpu.VMEM((D, DE), BF16), pltpu.VMEM((DE, D), BF16),
            pltpu.VMEM((2, B, DP), U32), pltpu.VMEM((2, B, DP), U32),
            pltpu.SemaphoreType.DMA((2,)), pltpu.SemaphoreType.DMA(()), pltpu.SemaphoreType.DMA((2,)),
        ],
    )
    return pl.pallas_call(
        functools.partial(_experts_kernel, layer=layer),
        grid_spec=grid_spec,
        out_shape=jax.ShapeDtypeStruct((rows, DP), U32),
        compiler_params=_cparams(("arbitrary",)),
        name="experts",
    )(*blocks, tail, xs, w_gate, w_up, w_down)


def _combine_kernel(pos_ref, posn_ref, x_ref, ys_hbm, w_ref, g2_ref, fg_ref, o_ref, gbuf, sem,
                    *, final):
    i = pl.program_id(0)
    n = pl.num_programs(0)
    tm = x_ref.shape[0]
    slot = i % 2

    def gather(p_ref, to_slot):
        def body(r, carry):
            for k in range(TOP_K):
                pltpu.make_async_copy(ys_hbm.at[pl.ds(p_ref[k, r], 1), :],
                                      gbuf.at[to_slot, k, pl.ds(r, 1), :], sem.at[to_slot]).start(priority=k)
            return carry

        lax.fori_loop(0, tm, body, 0, unroll=8)

    @pl.when(i == 0)
    def _():
        gather(pos_ref, 0)

    @pl.when(i + 1 < n)
    def _():
        gather(posn_ref, 1 - slot)

    for k in range(TOP_K):
        pltpu.make_async_copy(ys_hbm.at[pl.ds(0, tm), :], gbuf.at[slot, k], sem.at[slot]).wait()

    w = w_ref[...]
    D = x_ref.shape[1]
    half = D // 2
    lo0, hi0 = _unpack_halves(gbuf[slot, 0])
    lo1, hi1 = _unpack_halves(gbuf[slot, 1])
    xl = x_ref[:, :half] + g2_ref[:, :half] * (lo0 * w[:, 0:1] + lo1 * w[:, 1:2])
    xh = x_ref[:, half:] + g2_ref[:, half:] * (hi0 * w[:, 0:1] + hi1 * w[:, 1:2])
    if final:
        ms = (jnp.sum(xl * xl, axis=-1, keepdims=True) + jnp.sum(xh * xh, axis=-1, keepdims=True)) / D
        r = lax.rsqrt(ms + EPS)
        xl = xl * r * fg_ref[:, :half]
        xh = xh * r * fg_ref[:, half:]
    o_ref[:, :half] = xl
    o_ref[:, half:] = xh


def _combine(x, ys, pos, top_w, g2, final_g, final):
    T, D = x.shape
    tm = min(COMB_TM, T)
    n = T // tm
    vec = pl.BlockSpec((1, D), lambda i: (0, 0))
    return pl.pallas_call(
        functools.partial(_combine_kernel, final=final),
        grid=(n,),
        in_specs=[
            pl.BlockSpec((TOP_K, tm), lambda i: (0, i), memory_space=pltpu.SMEM),
            pl.BlockSpec((TOP_K, tm), lambda i: (0, jnp.minimum(i + 1, n - 1)), memory_space=pltpu.SMEM),
            pl.BlockSpec((tm, D), lambda i: (i, 0)),
            pl.BlockSpec(memory_space=pl.ANY),
            pl.BlockSpec((tm, TOP_K), lambda i: (i, 0)),
            vec, vec,
        ],
        out_specs=pl.BlockSpec((tm, D), lambda i: (i, 0)),
        out_shape=jax.ShapeDtypeStruct((T, D), F32),
        scratch_shapes=[pltpu.VMEM((2, TOP_K, tm, ys.shape[1]), ys.dtype), pltpu.SemaphoreType.DMA((2,))],
        compiler_params=_cparams(("arbitrary",), disable_bounds_checks=True),
        name="combine",
    )(pos, pos, x, ys, top_w, g2, final_g)


def kernel(x, c, ada_w, ada_b, norm1_g, norm2_g, w_in, conv_w, ret_gn_g, fox_fb, w_out,
           router_w, router_b, moe_w_gate, moe_w_up, moe_w_down, final_g):
    Bsz, S, D = x.shape
    assert Bsz == 1, "one sequence per call"
    L = ada_w.shape[0]
    T = S
    conv_wd = conv_w.shape[-1]
    ret_qk = RET_HEADS * RET_DK
    ret_v = RET_HEADS * RET_DV
    fox_w = FOX_HEADS * FOX_DH
    tn = INPROJ_TN
    ret_col0 = 3 * conv_wd
    fox_col0 = ret_col0 + 2 * ret_qk + 2 * ret_v
    n_main = fox_col0 + 3 * fox_w
    assert fox_w == tn and fox_col0 % tn == 0
    q_block = fox_col0 // tn
    v_block = q_block + 2
    k_col0 = fox_col0

    mod = _ada_mod(c, ada_w, ada_b)
    cos_t, sin_t = _rope_tables(T)
    rw_t = router_w.T
    xt = x.reshape(T, D)

    for l in range(L):
        sh1, sc1, g1, sh2, sc2, g2 = [mod[l, :, k * D:(k + 1) * D] for k in range(6)]
        w_ff = jnp.pad(w_in[l, :, n_main:], ((0, 0), (0, LANES - FOX_HEADS)))
        fb = jnp.pad(fox_fb[l], (0, LANES - FOX_HEADS)).reshape(1, LANES)
        proj, proj_t, cum = _inproj(xt, norm1_g[l].reshape(1, D), sh1, sc1, w_in, l, w_ff, fb,
                                    q_block, v_block, LOG2E * FOX_DH ** -0.5)

        y_conv = _conv(proj, conv_w[l])
        y_ret = _retention(proj, cos_t, sin_t, ret_gn_g[l].reshape(1, ret_v), ret_col0)
        y_fox = _fox(proj, proj_t, cum, k_col0)

        wo = w_out[l].astype(BF16)
        xt, h2, logits_t = _outproj(
            y_conv, y_ret, y_fox, wo[:conv_wd], wo[conv_wd:conv_wd + ret_v], wo[conv_wd + ret_v:],
            xt, g1, norm2_g[l].reshape(1, D), sh2, sc2, rw_t)

        top_e, rank, top_w, cnt = _route(logits_t, router_b)
        counts = cnt[:, 0].astype(jnp.int32)
        seg_start, seg_end, blocks, nb, rows = _block_tables(counts, T)
        hit = top_e[None] == jnp.arange(N_EXPERTS, dtype=jnp.int32)[:, None, None]
        pos = rank + jnp.sum(jnp.where(hit, seg_start[:, None, None], 0), axis=0)
        xs = _dispatch(h2, pos, seg_end, rows)
        ys = _experts(xs, l, moe_w_gate, moe_w_up, moe_w_down, blocks, seg_end[-1:], nb)
        xt = _combine(xt, ys, pos, top_w.T, g2, final_g.reshape(1, D), final=(l == L - 1))

    return xt.reshape(Bsz, S, D)
```

```python
import functools
import math

import numpy as np
import jax
import jax.numpy as jnp
from jax import lax
from jax.experimental import pallas as pl
from jax.experimental.pallas import tpu as pltpu

F32 = jnp.float32
BF16 = jnp.bfloat16

CHUNK = 64
CONV_K = 3
RET_HEADS = 6
RET_DK = 64
RET_DV = 128
FOX_HEADS = 6
FOX_DH = 128
N_EXPERTS = 64
N_GROUPS = 8
GROUP_SIZE = N_EXPERTS // N_GROUPS
TOP_K = 2
ROPE_BASE = 10000.0
EPS = 1e-6
LOG2E = 1.4426950408889634

LANES = 128
SUBLANES = 8

MOD_TN = 1024
INPROJ_TM = 1024
INPROJ_TN = 768
CONV_TM = 1024
CONV_HALO = 16
RET_ROWS = 256
FOX_TQ = 1024
FOX_SLAB = 16
FOX_SKIP_LOG2 = 160.0
FOX_NORM_SLACK = 1.02
OUT_TM = 512
MOE_ROWS = 256
MOE_STEP = 64
ROUTE_TT = 1024
DISPATCH_TT = 512
COMB_TM = 512
VMEM_LIMIT = 56 * 1024 * 1024


def _cparams(sem, **kw):
    return pltpu.CompilerParams(dimension_semantics=sem, vmem_limit_bytes=VMEM_LIMIT, **kw)


U32 = jnp.uint32


def _pack_halves(x):
    half = x.shape[1] // 2
    xb = x.astype(BF16).astype(F32)
    lo = lax.bitcast_convert_type(xb[:, :half], U32)
    hi = lax.bitcast_convert_type(xb[:, half:], U32)
    return lax.shift_right_logical(lo, U32(16)) | (hi & U32(0xFFFF0000))


def _unpack_halves(w):
    lo = lax.bitcast_convert_type(lax.shift_left(w, U32(16)), F32)
    hi = lax.bitcast_convert_type(w & U32(0xFFFF0000), F32)
    return lo, hi


def _mod_kernel(c_ref, w_ref, b_ref, o_ref):
    c = c_ref[...]
    ca = c * jax.nn.sigmoid(c)
    o_ref[0] = jnp.sum(w_ref[0] * ca, axis=0, keepdims=True) + b_ref[0]


def _ada_mod(c, ada_w, ada_b):
    L, D, N = ada_w.shape
    tn = MOD_TN
    return pl.pallas_call(
        _mod_kernel,
        grid=(L, N // tn),
        in_specs=[
            pl.BlockSpec((D, 1), lambda l, j: (0, 0)),
            pl.BlockSpec((1, D, tn), lambda l, j: (l, 0, j)),
            pl.BlockSpec((1, 1, tn), lambda l, j: (l, 0, j)),
        ],
        out_specs=pl.BlockSpec((1, 1, tn), lambda l, j: (l, 0, j)),
        out_shape=jax.ShapeDtypeStruct((L, 1, N), F32),
        compiler_params=_cparams(("arbitrary", "arbitrary")),
        name="ada_mod",
    )(c.reshape(D, 1), ada_w, ada_b.reshape(L, 1, N))


def _modulated_norm(x, g, sh, sc):
    ms = jnp.mean(x * x, axis=-1, keepdims=True)
    y = x * lax.rsqrt(ms + EPS) * g
    return y * (1.0 + sc) + sh


def _log_sigmoid(x):
    return jnp.minimum(x, 0.0) - jnp.log1p(jnp.exp(-jnp.abs(x)))


def _inproj_kernel(x_ref, g_ref, sh_ref, sc_ref, w_ref, wff_ref, fb_ref,
                   proj_ref, projt_ref, cum_ref, h_scr, wt_scr, carry_scr,
                   *, q_block, v_block, q_scale):
    i = pl.program_id(0)
    j = pl.program_id(1)
    tm = x_ref.shape[0]

    @pl.when(jnp.logical_and(i == 0, j == 0))
    def _():
        carry_scr[...] = jnp.zeros_like(carry_scr)

    @pl.when(j == 0)
    def _():
        h = _modulated_norm(x_ref[...], g_ref[...], sh_ref[...], sc_ref[...])
        hb = h.astype(BF16)
        h_scr[...] = hb
        ff = jnp.dot(hb, wff_ref[...].astype(BF16), preferred_element_type=F32) + fb_ref[...]
        c = _log_sigmoid(ff)
        row = lax.broadcasted_iota(jnp.int32, c.shape, 0)
        d = 1
        while d < tm:
            c = c + jnp.where(row >= d, pltpu.roll(c, d, axis=0), 0.0)
            d *= 2
        c = c + carry_scr[...]
        cum_ref[...] = c
        carry_scr[...] = c[tm - 1:tm, :]

    @pl.when(jnp.logical_and(j != q_block, j != v_block))
    def _():
        proj_ref[...] = jnp.dot(h_scr[...], w_ref[...].astype(BF16),
                                preferred_element_type=F32).astype(BF16)

    def feature_major(slot, scale):
        @pl.when(i == 0)
        def _():
            wt_scr[slot] = w_ref[...].T.astype(BF16)

        r = lax.dot_general(wt_scr[slot], h_scr[...], (((1,), (1,)), ((), ())),
                            preferred_element_type=F32)
        projt_ref[...] = (r * scale).astype(BF16)

    @pl.when(j == q_block)
    def _():
        feature_major(0, q_scale)

    @pl.when(j == v_block)
    def _():
        feature_major(1, 1.0)


def _inproj(x, g, sh, sc, w_in, layer, w_ff, fb, q_block, v_block, q_scale):
    T, D = x.shape
    tm, tn = min(INPROJ_TM, T), INPROJ_TN
    nj = v_block + 1
    is_q = lambda j: (j == q_block).astype(jnp.int32)
    is_v = lambda j: (j == v_block).astype(jnp.int32)
    w_col = lambda i, j: jnp.where(i == 0, j, j - is_q(j) - is_v(j))
    out_col = lambda j: j - (j >= q_block).astype(jnp.int32) - (j >= v_block).astype(jnp.int32)
    vec = pl.BlockSpec((1, D), lambda i, j: (0, 0))
    return pl.pallas_call(
        functools.partial(_inproj_kernel, q_block=q_block, v_block=v_block, q_scale=q_scale),
        grid=(T // tm, nj),
        in_specs=[
            pl.BlockSpec((tm, D), lambda i, j: (i, 0)),
            vec, vec, vec,
            pl.BlockSpec((None, D, tn), lambda i, j: (layer, 0, w_col(i, j))),
            pl.BlockSpec((D, LANES), lambda i, j: (0, 0)),
            pl.BlockSpec((1, LANES), lambda i, j: (0, 0)),
        ],
        out_specs=[
            pl.BlockSpec((tm, tn), lambda i, j: (i, out_col(j))),
            pl.BlockSpec((tn, tm), lambda i, j: (is_v(j), i)),
            pl.BlockSpec((tm, LANES), lambda i, j: (i, 0)),
        ],
        out_shape=[
            jax.ShapeDtypeStruct((T, (nj - 2) * tn), BF16),
            jax.ShapeDtypeStruct((2 * tn, T), BF16),
            jax.ShapeDtypeStruct((T, LANES), F32),
        ],
        scratch_shapes=[pltpu.VMEM((tm, D), BF16), pltpu.VMEM((2, tn, D), BF16),
                        pltpu.VMEM((1, LANES), F32)],
        compiler_params=_cparams(("arbitrary", "arbitrary")),
        name="inproj",
    )(x, g, sh, sc, w_in, w_ff, fb)


def _conv_kernel(cb_ref, cc_ref, cu_ref, hc_ref, hu_ref, w_ref, o_ref):
    i = pl.program_id(0)
    z = cc_ref[...].astype(F32) * cu_ref[...].astype(F32)
    zh = hc_ref[...].astype(F32) * hu_ref[...].astype(F32)
    zh = jnp.where(i > 0, zh, 0.0)
    hl = zh.shape[0]
    zm1 = zh[hl - 1:hl, :]
    zm2 = zh[hl - 2:hl - 1, :]
    row = lax.broadcasted_iota(jnp.int32, z.shape, 0)
    z1 = jnp.where(row == 0, zm1, pltpu.roll(z, 1, axis=0))
    z2 = jnp.where(row == 0, zm2, jnp.where(row == 1, zm1, pltpu.roll(z, 2, axis=0)))
    w = w_ref[...]
    y = z2 * w[0:1, :] + z1 * w[1:2, :] + z * w[2:3, :]
    o_ref[...] = (cb_ref[...].astype(F32) * y).astype(BF16)


def _conv(proj, conv_w):
    T = proj.shape[0]
    W = conv_w.shape[1]
    tm = min(CONV_TM, T)
    hb = tm // CONV_HALO
    halo = lambda c: pl.BlockSpec((CONV_HALO, W), lambda i: (jnp.maximum(i * hb - 1, 0), c))
    return pl.pallas_call(
        _conv_kernel,
        grid=(T // tm,),
        in_specs=[
            pl.BlockSpec((tm, W), lambda i: (i, 0)),
            pl.BlockSpec((tm, W), lambda i: (i, 1)),
            pl.BlockSpec((tm, W), lambda i: (i, 2)),
            halo(1), halo(2),
            pl.BlockSpec((CONV_K, W), lambda i: (0, 0)),
        ],
        out_specs=pl.BlockSpec((tm, W), lambda i: (i, 0)),
        out_shape=jax.ShapeDtypeStruct((T, W), BF16),
        compiler_params=_cparams(("arbitrary",)),
        name="conv",
    )(proj, proj, proj, proj, proj, conv_w)


def _ret_gammas():
    return [1.0 - 2.0 ** (-5.0 - h) for h in range(RET_HEADS)]


def _ret_tables(R):
    n = np.arange(R, dtype=np.float64)
    chunk = np.arange(R) // CHUNK
    allowed = chunk[None, :] <= chunk[:, None]
    dm, qd, kd = [], [], []
    for g in _ret_gammas():
        lg = math.log(g)
        dm.append(np.where(allowed, np.exp(lg * np.abs(n[:, None] - n[None, :])), 0.0))
        qd.append(np.broadcast_to(np.exp(lg * (n + 1.0))[:, None], (R, LANES)))
        kd.append(np.broadcast_to(np.exp(lg * (R - 1.0 - n))[:, None], (R, LANES)))
    f = lambda a: jnp.asarray(np.stack(a), dtype=F32)
    return f(dm), f(qd), f(kd)


def _rope_tables(T):
    half = RET_DK // 2
    inv = ROPE_BASE ** (-jnp.arange(half, dtype=F32) / half)
    ang = jnp.arange(T, dtype=F32)[:, None] * inv[None, :]
    cos, sin = jnp.cos(ang), jnp.sin(ang)
    reps = LANES // RET_DK
    cos_t = jnp.tile(jnp.concatenate([cos, cos], axis=1), (1, reps))
    sin_t = jnp.tile(jnp.concatenate([-sin, sin], axis=1), (1, reps))
    return cos_t, sin_t


def _ret_kernel(q_ref, k_ref, v_ref, g_ref, cos_ref, sin_ref, dm_ref, qd_ref, kd_ref, gn_ref,
                cb_ref, cc_ref, cu_ref, hc_ref, hu_ref, cw_ref, o_ref, oc_ref, s_scr):
    i = pl.program_id(0)
    R = q_ref.shape[0]
    _conv_kernel(cb_ref, cc_ref, cu_ref, hc_ref, hu_ref, cw_ref, oc_ref)

    @pl.when(i == 0)
    def _():
        s_scr[...] = jnp.zeros_like(s_scr)

    lane = lax.broadcasted_iota(jnp.int32, (R, LANES), 1)
    first_half = (lane % RET_DK) < (RET_DK // 2)
    low_head = lane < RET_DK
    cosv = cos_ref[...]
    sinv = sin_ref[...]
    c_dec = [g ** R for g in _ret_gammas()]

    def rot(t):
        swapped = jnp.where(first_half, pltpu.roll(t, LANES - RET_DK // 2, axis=1),
                            pltpu.roll(t, RET_DK // 2, axis=1))
        return t * cosv + swapped * sinv

    heads_per_vreg = LANES // RET_DK
    for p in range(RET_HEADS // heads_per_vreg):
        cols = slice(p * LANES, (p + 1) * LANES)
        qr = rot(q_ref[:, cols].astype(F32))
        kb = (rot(k_ref[:, cols].astype(F32)) * (RET_DK ** -0.5)).astype(BF16)
        for hh in range(heads_per_vreg):
            h = p * heads_per_vreg + hh
            hc = slice(h * RET_DV, (h + 1) * RET_DV)
            mask = low_head if hh == 0 else jnp.logical_not(low_head)
            qm = jnp.where(mask, qr, 0.0).astype(BF16)
            s = lax.dot_general(qm, kb, (((1,), (1,)), ((), ())), preferred_element_type=F32)
            s = s * dm_ref[h]
            v = v_ref[:, hc]
            o = jnp.dot(s.astype(BF16), v, preferred_element_type=F32)
            state = s_scr[h]
            o = o + jnp.dot(qm, state.astype(BF16), preferred_element_type=F32) * qd_ref[h]
            vd = (v.astype(F32) * kd_ref[h]).astype(BF16)
            kv = lax.dot_general(kb, vd, (((0,), (0,)), ((), ())), preferred_element_type=F32)
            s_scr[h] = state * c_dec[h] + kv
            mu = jnp.mean(o, axis=-1, keepdims=True)
            d = o - mu
            var = jnp.mean(d * d, axis=-1, keepdims=True)
            on = d * lax.rsqrt(var + EPS) * gn_ref[:, hc]
            gate = g_ref[:, hc].astype(F32)
            o_ref[:, hc] = (gate * jax.nn.sigmoid(gate) * on).astype(BF16)


def _retention(proj, cos_t, sin_t, gn_g, col0, conv_w):
    T = proj.shape[0]
    R = min(RET_ROWS, T)
    W = conv_w.shape[1]
    halo = lambda c: pl.BlockSpec((CONV_HALO, W),
                                  lambda i: (jnp.maximum(i * (R // CONV_HALO) - 1, 0), c))
    QK = RET_HEADS * RET_DK
    V = RET_HEADS * RET_DV
    dm, qd, kd = _ret_tables(R)
    q_blk = col0 // QK
    v_blk = (col0 + 2 * QK) // V
    full3 = lambda a: pl.BlockSpec(a.shape, lambda i: (0, 0, 0))
    return pl.pallas_call(
        _ret_kernel,
        grid=(T // R,),
        in_specs=[
            pl.BlockSpec((R, QK), lambda i: (i, q_blk)),
            pl.BlockSpec((R, QK), lambda i: (i, q_blk + 1)),
            pl.BlockSpec((R, V), lambda i: (i, v_blk)),
            pl.BlockSpec((R, V), lambda i: (i, v_blk + 1)),
            pl.BlockSpec((R, LANES), lambda i: (i, 0)),
            pl.BlockSpec((R, LANES), lambda i: (i, 0)),
            full3(dm), full3(qd), full3(kd),
            pl.BlockSpec((1, V), lambda i: (0, 0)),
            pl.BlockSpec((R, W), lambda i: (i, 0)), pl.BlockSpec((R, W), lambda i: (i, 1)),
            pl.BlockSpec((R, W), lambda i: (i, 2)), halo(1), halo(2),
            pl.BlockSpec(conv_w.shape, lambda i: (0, 0)),
        ],
        out_specs=[pl.BlockSpec((R, V), lambda i: (i, 0)), pl.BlockSpec((R, W), lambda i: (i, 0))],
        out_shape=[jax.ShapeDtypeStruct((T, V), BF16), jax.ShapeDtypeStruct((T, W), BF16)],
        scratch_shapes=[pltpu.VMEM((RET_HEADS, LANES, RET_DV), F32)],
        compiler_params=_cparams(("arbitrary",)),
        name="retention",
    )(proj, proj, proj, proj, cos_t, sin_t, dm, qd, kd, gn_g, proj, proj, proj, proj, proj, conv_w)


def _fox_kernel(qt_ref, k_ref, vt_ref, cum_ref, o_ref, ka_scr, cend_scr, kn_scr,
                s0_scr, s1_scr, p0_scr, p1_scr, a0_scr, a1_scr, m_scr, l_scr, acc_scr):
    h = pl.program_id(0)
    qi = pl.program_id(1)
    tq = qt_ref.shape[1]
    tk = s0_scr.shape[0]
    T = k_ref.shape[0]
    assert tq == 2 * tk
    s_scr, p_scr, a_scr = (s0_scr, s1_scr), (p0_scr, p1_scr), (a0_scr, a1_scr)

    @pl.when(qi == 0)
    def _():
        cend_scr[...] = jnp.zeros_like(cend_scr)
        ones = jnp.ones((FOX_DH, LANES), BF16)
        kn2 = jnp.zeros((SUBLANES, LANES), F32)
        for r in range(0, T, tq):
            kf = k_ref[r:r + tq, :].astype(F32)
            n2 = jnp.dot((kf * kf).astype(BF16), ones, preferred_element_type=F32)
            kn2 = jnp.maximum(kn2, jnp.max(n2.reshape(tq // SUBLANES, SUBLANES, LANES), axis=0))
        kn_scr[...] = jnp.max(kn2, axis=0, keepdims=True)

    lane = lax.broadcasted_iota(jnp.int32, cum_ref.shape, 1)
    col = jnp.sum(jnp.where(lane == h, cum_ref[...], 0.0), axis=1, keepdims=True) * LOG2E
    q0 = pl.multiple_of(qi * tq, tq)
    ck_rep = jnp.broadcast_to(col, cum_ref.shape)
    bf = lambda v: v.astype(BF16).astype(F32)
    c_hi = bf(col)
    c_mid = bf(col - c_hi)
    c_lo = bf(col - c_hi - c_mid)
    extra = jnp.where(lane == 0, c_hi, jnp.where(lane == 1, c_mid, jnp.where(lane == 2, c_lo, 0.0)))
    ka_scr[pl.ds(q0, tq), 0:FOX_DH] = k_ref[pl.ds(q0, tq), :]
    ka_scr[pl.ds(q0, tq), FOX_DH:2 * FOX_DH] = extra.astype(BF16)
    sub = lax.broadcasted_iota(jnp.int32, (FOX_DH, tq), 0)
    q_ext = jnp.concatenate([qt_ref[...], jnp.where(sub < 3, -1.0, 0.0).astype(BF16)], axis=0)
    cend_scr[pl.ds(2 * qi, 1), :] = ck_rep[tk - 1:tk, :]
    cend_scr[pl.ds(2 * qi + 1, 1), :] = ck_rep[tq - 1:tq, :]

    qf = qt_ref[...].astype(F32)
    qn2 = jnp.max(jnp.sum(qf * qf, axis=0, keepdims=True), axis=1, keepdims=True)
    reach = 2.0 * FOX_NORM_SLACK * jnp.sqrt(qn2 * kn_scr[...]) + FOX_SKIP_LOG2
    decay = cend_scr[...] - ck_rep[0:1, :]
    chunk_id = lax.broadcasted_iota(jnp.int32, decay.shape, 0)
    dead = jnp.logical_and(decay >= reach, chunk_id < 2 * qi)
    n_dead = jnp.max(jnp.sum(dead.astype(jnp.int32), axis=0, keepdims=True))
    first_pair = (n_dead + 1) // 2

    m_scr[...] = jnp.full_like(m_scr, -jnp.inf)
    l_scr[...] = jnp.zeros_like(l_scr)
    acc_scr[...] = jnp.zeros_like(acc_scr)
    for p_ref, a_ref in zip(p_scr, a_scr):
        p_ref[...] = jnp.zeros_like(p_ref)
        a_ref[...] = jnp.ones_like(a_ref)

    everyone = slice(0, tq)

    def score(c, slot, qs=everyone):
        k0 = pl.multiple_of(c * tk, tk)
        s_scr[slot][:, qs] = jnp.dot(ka_scr[pl.ds(k0, tk), :], q_ext[:, qs],
                                     preferred_element_type=F32)

    def softmax(c, slot, masked, qs=everyone):
        nq = qs.stop - qs.start
        sref, pref = s_scr[slot], p_scr[slot]
        if masked:
            s = sref[:, qs]
            kpos = c * tk + lax.broadcasted_iota(jnp.int32, s.shape, 0)
            qpos = q0 + qs.start + lax.broadcasted_iota(jnp.int32, s.shape, 1)
            sref[:, qs] = jnp.where(kpos <= qpos, s, -jnp.inf)
        mx = sref[0:FOX_SLAB, qs]
        for r in range(FOX_SLAB, tk, FOX_SLAB):
            mx = jnp.maximum(mx, sref[r:r + FOX_SLAB, qs])
        m_old = m_scr[:, qs]
        m_new = jnp.maximum(m_old, jnp.max(mx, axis=0, keepdims=True))
        alpha = jnp.exp2(m_old - m_new)
        m_rows = jnp.broadcast_to(m_new, (FOX_SLAB, nq))
        psum = jnp.zeros((FOX_SLAB, nq), F32)
        for r in range(0, tk, FOX_SLAB):
            p = jnp.exp2(sref[r:r + FOX_SLAB, qs] - m_rows)
            psum = psum + p
            pref[r:r + FOX_SLAB, qs] = p.astype(BF16)
        l_scr[:, qs] = alpha * l_scr[:, qs] + jnp.sum(psum, axis=0, keepdims=True)
        a_scr[slot][:, qs] = alpha
        m_scr[:, qs] = m_new

    def accumulate(c, slot, qs=everyone):
        k0 = pl.multiple_of(jnp.maximum(c, 0) * tk, tk)
        pv = jnp.dot(vt_ref[:, pl.ds(k0, tk)], p_scr[slot][:, qs], preferred_element_type=F32)
        acc_scr[:, qs] = a_scr[slot][:, qs] * acc_scr[:, qs] + pv

    @pl.when(n_dead % 2 == 0)
    def _():
        score(n_dead, 0)

    @pl.when(n_dead % 2 == 1)
    def _():
        score(n_dead, 1)
        score(n_dead + 1, 0)
        accumulate(n_dead - 1, 0)
        softmax(n_dead, 1, False)

    def pair(i, carry):
        c = 2 * i
        score(c + 1, 1)
        accumulate(c - 1, 1)
        softmax(c, 0, False)
        score(c + 2, 0)
        accumulate(c, 0)
        softmax(c + 1, 1, False)
        return carry

    lax.fori_loop(first_pair, qi, pair, 0)
    c = 2 * qi
    late = slice(tk, tq)
    score(c + 1, 1, late)
    accumulate(c - 1, 1)
    softmax(c, 0, True)
    accumulate(c, 0)
    softmax(c + 1, 1, True, late)
    accumulate(c + 1, 1, late)
    o_ref[...] = (acc_scr[...] / l_scr[...]).T.astype(BF16)


def _fox(proj, proj_t, cum, k_col0):
    T = proj.shape[0]
    tq = min(FOX_TQ, T)
    tk = tq // 2
    W = FOX_HEADS * FOX_DH
    kb = k_col0 // FOX_DH
    return pl.pallas_call(
        _fox_kernel,
        grid=(FOX_HEADS, T // tq),
        in_specs=[
            pl.BlockSpec((FOX_DH, tq), lambda h, i: (h, i)),
            pl.BlockSpec((T, FOX_DH), lambda h, i: (0, kb + h)),
            pl.BlockSpec((FOX_DH, T), lambda h, i: (FOX_HEADS + h, 0)),
            pl.BlockSpec((tq, LANES), lambda h, i: (i, 0)),
        ],
        out_specs=pl.BlockSpec((tq, FOX_DH), lambda h, i: (i, h)),
        out_shape=jax.ShapeDtypeStruct((T, W), BF16),
        scratch_shapes=[pltpu.VMEM((T, 2 * FOX_DH), BF16),
                        pltpu.VMEM((-(-(T // tk) // SUBLANES) * SUBLANES, LANES), F32),
                        pltpu.VMEM((1, LANES), F32),
                        pltpu.VMEM((tk, tq), F32), pltpu.VMEM((tk, tq), F32),
                        pltpu.VMEM((tk, tq), BF16), pltpu.VMEM((tk, tq), BF16),
                        pltpu.VMEM((1, tq), F32), pltpu.VMEM((1, tq), F32),
                        pltpu.VMEM((1, tq), F32), pltpu.VMEM((1, tq), F32),
                        pltpu.VMEM((FOX_DH, tq), F32)],
        compiler_params=_cparams(("arbitrary", "arbitrary")),
        name="fox",
    )(proj_t, proj, proj_t, cum)


def _outproj_kernel(yc_ref, yr_ref, yf_ref, wc_ref, wr_ref, wf_ref, x_ref, g1_ref,
                    g_ref, sh_ref, sc_ref, rwh_ref, rwl_ref, xo_ref, h_ref, lg_ref):
    mix = jnp.dot(yc_ref[...], wc_ref[...], preferred_element_type=F32)
    mix = mix + jnp.dot(yr_ref[...], wr_ref[...], preferred_element_type=F32)
    mix = mix + jnp.dot(yf_ref[...], wf_ref[...], preferred_element_type=F32)
    x = x_ref[...] + g1_ref[...] * mix
    xo_ref[...] = x
    h = _modulated_norm(x, g_ref[...], sh_ref[...], sc_ref[...])
    h_ref[...] = _pack_halves(h)
    h_hi = h.astype(BF16)
    h_lo = (h - h_hi.astype(F32)).astype(BF16)
    nt = lambda a, b: lax.dot_general(a, b, (((1,), (1,)), ((), ())), preferred_element_type=F32)
    lg_ref[...] = nt(rwh_ref[...], h_hi) + (nt(rwh_ref[...], h_lo) + nt(rwl_ref[...], h_hi))


def _outproj(yc, yr, yf, wc, wr, wf, x, g1, g, sh, sc, rw_t):
    T, D = x.shape
    tm = min(OUT_TM, T)
    E = rw_t.shape[0]
    rw_hi = rw_t.astype(BF16)
    rw_lo = (rw_t - rw_hi.astype(F32)).astype(BF16)
    vec = pl.BlockSpec((1, D), lambda i: (0, 0))
    rows = lambda a: pl.BlockSpec((tm, a.shape[1]), lambda i: (i, 0))
    whole = lambda a: pl.BlockSpec(a.shape, lambda i: (0, 0))
    return pl.pallas_call(
        _outproj_kernel,
        grid=(T // tm,),
        in_specs=[rows(yc), rows(yr), rows(yf), whole(wc), whole(wr), whole(wf), rows(x),
                  vec, vec, vec, vec, whole(rw_hi), whole(rw_lo)],
        out_specs=[rows(x), pl.BlockSpec((tm, D // 2), lambda i: (i, 0)),
                   pl.BlockSpec((E, tm), lambda i: (0, i))],
        out_shape=[jax.ShapeDtypeStruct((T, D), F32), jax.ShapeDtypeStruct((T, D // 2), U32),
                   jax.ShapeDtypeStruct((E, T), F32)],
        compiler_params=_cparams(("arbitrary",)),
        name="outproj",
    )(yc, yr, yf, wc, wr, wf, x, g1, g, sh, sc, rw_hi, rw_lo)


def _route_kernel(lg_ref, b_ref, tri_ref, e_ref, r_ref, w_ref, cnt_ref, carry_scr):
    i = pl.program_id(0)
    E, tt = lg_ref.shape

    @pl.when(i == 0)
    def _():
        carry_scr[...] = jnp.zeros_like(carry_scr)

    aff = jax.nn.sigmoid(lg_ref[...])
    sel = aff + b_ref[...]
    row8 = lax.broadcasted_iota(jnp.int32, (GROUP_SIZE, tt), 0)
    best = None
    for g in range(N_GROUPS):
        slab = sel[g * GROUP_SIZE:(g + 1) * GROUP_SIZE, :]
        m1 = jnp.max(slab, axis=0, keepdims=True)
        i1 = jnp.min(jnp.where(slab == m1, row8, GROUP_SIZE), axis=0, keepdims=True)
        rest = jnp.where(row8 == i1, -jnp.inf, slab)
        m2 = jnp.max(rest, axis=0, keepdims=True)
        i2 = jnp.min(jnp.where(rest == m2, row8, GROUP_SIZE), axis=0, keepdims=True)
        cand = (m1 + m2, g * GROUP_SIZE + i1, g * GROUP_SIZE + i2)
        if best is None:
            best = cand
        else:
            upd = cand[0] > best[0]
            best = tuple(jnp.where(upd, n, o) for n, o in zip(cand, best))
    _, e0, e1 = best

    row = lax.broadcasted_iota(jnp.int32, (E, tt), 0)
    oh0 = row == e0
    oh1 = row == e1
    a0 = jnp.sum(jnp.where(oh0, aff, 0.0), axis=0, keepdims=True)
    a1 = jnp.sum(jnp.where(oh1, aff, 0.0), axis=0, keepdims=True)
    w_ref[0:1, :] = a0 / (a0 + a1)
    w_ref[1:2, :] = a1 / (a0 + a1)
    e_ref[0:1, :] = e0
    e_ref[1:2, :] = e1

    oh = jnp.logical_or(oh0, oh1)
    ohf = jnp.where(oh, 1.0, 0.0)
    before = jnp.dot(ohf.astype(BF16), tri_ref[...], preferred_element_type=F32) + carry_scr[:, 0:1]
    r_ref[0:1, :] = jnp.sum(jnp.where(oh0, before, 0.0), axis=0, keepdims=True).astype(jnp.int32)
    r_ref[1:2, :] = jnp.sum(jnp.where(oh1, before, 0.0), axis=0, keepdims=True).astype(jnp.int32)
    carry = carry_scr[...] + jnp.sum(ohf, axis=1, keepdims=True)
    carry_scr[...] = carry
    cnt_ref[...] = carry


def _route(logits_t, router_b):
    E, T = logits_t.shape
    tt = min(ROUTE_TT, T)
    tri = jnp.asarray(np.triu(np.ones((tt, tt), np.float32), k=1), dtype=BF16)
    pair = lambda dt: jax.ShapeDtypeStruct((TOP_K, T), dt)
    return pl.pallas_call(
        _route_kernel,
        grid=(T // tt,),
        in_specs=[
            pl.BlockSpec((E, tt), lambda i: (0, i)),
            pl.BlockSpec((E, 1), lambda i: (0, 0)),
            pl.BlockSpec((tt, tt), lambda i: (0, 0)),
        ],
        out_specs=[
            pl.BlockSpec((TOP_K, tt), lambda i: (0, i)),
            pl.BlockSpec((TOP_K, tt), lambda i: (0, i)),
            pl.BlockSpec((TOP_K, tt), lambda i: (0, i)),
            pl.BlockSpec((E, LANES), lambda i: (0, 0)),
        ],
        out_shape=[pair(jnp.int32), pair(jnp.int32), pair(F32), jax.ShapeDtypeStruct((E, LANES), F32)],
        scratch_shapes=[pltpu.VMEM((E, LANES), F32)],
        compiler_params=_cparams(("arbitrary",)),
        name="route",
    )(logits_t, router_b.reshape(E, 1), tri)


def _block_tables(counts, T):
    E = counts.shape[0]
    A = T * TOP_K
    B = MOE_ROWS
    seg = (counts + SUBLANES - 1) // SUBLANES * SUBLANES
    seg_end = jnp.cumsum(seg)
    seg_start = seg_end - seg
    nb = (A + E * (B - 1) + B - 1) // B
    nblk = (counts + B - 1) // B
    blk_end = jnp.cumsum(nblk)
    total = blk_end[-1]
    b = jnp.arange(nb, dtype=jnp.int32)
    bc = jnp.minimum(b, total - 1)
    blk_e = jnp.minimum(jnp.searchsorted(blk_end, bc, side="right"), E - 1).astype(jnp.int32)
    local = bc - (blk_end[blk_e] - nblk[blk_e])
    blk_start = (seg_start[blk_e] + local * B).astype(jnp.int32)
    blk_first = jnp.logical_and(b < total, local == 0).astype(jnp.int32)
    blk_n = jnp.clip(counts[blk_e] - local * B, 0, B).astype(jnp.int32)
    nonempty = counts > 0
    order = jnp.cumsum(nonempty.astype(jnp.int32)) - 1
    ids = jnp.where(nonempty, jnp.arange(E, dtype=jnp.int32), E)
    later = jnp.concatenate([lax.cummin(ids, reverse=True)[1:], jnp.full((1,), E, jnp.int32)])
    next_e = jnp.where(later < E, later, -1)
    blk_wslot = (order[blk_e] % 2).astype(jnp.int32)
    blk_next_e = next_e[blk_e].astype(jnp.int32)
    rows = A + E * (SUBLANES - 1) + B
    rows = (rows + SUBLANES - 1) // SUBLANES * SUBLANES
    blocks = (blk_e, blk_start, blk_first, blk_n, blk_wslot, blk_next_e,
              total.astype(jnp.int32).reshape(1))
    return seg_start.astype(jnp.int32), seg_end.astype(jnp.int32), blocks, nb, rows


def _zero_rows_from(zero_ref, hbm_ref, start, sem):
    piece = zero_ref.shape[0]
    rows = hbm_ref.shape[0]

    def body(j, carry):
        at = pl.multiple_of(jnp.minimum(start + j * piece, rows - piece), SUBLANES)
        cp = pltpu.make_async_copy(zero_ref, hbm_ref.at[pl.ds(at, piece), :], sem)
        cp.start()
        cp.wait()
        return carry

    lax.fori_loop(0, (rows - start + piece - 1) // piece, body, 0)


def _dispatch_kernel(seg_end_ref, pos_ref, h_ref, xs_hbm, zero_scr, sem, zsem):
    i = pl.program_id(0)
    td = pos_ref.shape[1]
    E = seg_end_ref.shape[0]

    @pl.when(i == 0)
    def _():
        zero_scr[...] = jnp.zeros_like(zero_scr)

        def tail(e, carry):
            end = seg_end_ref[e]
            at = pl.multiple_of(jnp.maximum(end - SUBLANES, 0), SUBLANES)
            cp = pltpu.make_async_copy(zero_scr.at[pl.ds(0, SUBLANES), :],
                                       xs_hbm.at[pl.ds(at, SUBLANES), :], zsem)
            cp.start()
            cp.wait()
            return carry

        lax.fori_loop(0, E, tail, 0)
        _zero_rows_from(zero_scr, xs_hbm, seg_end_ref[E - 1], zsem)

    def body(r, carry):
        for k in range(TOP_K):
            pltpu.make_async_copy(h_ref.at[pl.ds(r, 1), :],
                                  xs_hbm.at[pl.ds(pos_ref[k, r], 1), :], sem).start(priority=k)
        return carry

    lax.fori_loop(0, td, body, 0, unroll=8)
    for k in range(TOP_K):
        pltpu.make_async_copy(h_ref, xs_hbm.at[pl.ds(0, td), :], sem).wait()


def _dispatch(h2, pos, seg_end, rows):
    T, D = h2.shape
    td = min(DISPATCH_TT, T)
    grid_spec = pltpu.PrefetchScalarGridSpec(
        num_scalar_prefetch=1,
        grid=(T // td,),
        in_specs=[pl.BlockSpec((TOP_K, td), lambda i, se: (0, i), memory_space=pltpu.SMEM),
                  pl.BlockSpec((td, D), lambda i, se: (i, 0))],
        out_specs=pl.BlockSpec(memory_space=pl.ANY),
        scratch_shapes=[pltpu.VMEM((MOE_ROWS, D), h2.dtype), pltpu.SemaphoreType.DMA(()),
                        pltpu.SemaphoreType.DMA(())],
    )
    return pl.pallas_call(
        _dispatch_kernel,
        grid_spec=grid_spec,
        out_shape=jax.ShapeDtypeStruct((rows, D), h2.dtype),
        compiler_params=_cparams(("arbitrary",), disable_bounds_checks=True),
        name="dispatch",
    )(seg_end, pos, h2)


def _experts_kernel(blk_e_ref, blk_start_ref, blk_first_ref, blk_n_ref, blk_wslot_ref, blk_next_ref,
                    total_ref, tail_ref, xs_hbm, wg_hbm, wu_hbm, wd_hbm, ys_hbm,
                    wg_f, wu_f, wd_f, wg_b, wu_b, wd_b, xbuf, ybuf, isem, osem, wsem, *, layer):
    b = pl.program_id(0)
    total = total_ref[0]
    B = xbuf.shape[1]
    slot = b % 2

    def load(blk, to_slot):
        at = pl.multiple_of(blk_start_ref[blk], SUBLANES)
        return pltpu.make_async_copy(xs_hbm.at[pl.ds(at, B), :], xbuf.at[to_slot], isem.at[to_slot])

    def store(blk, from_slot):
        at = pl.multiple_of(blk_start_ref[blk], SUBLANES)
        return pltpu.make_async_copy(ybuf.at[from_slot], ys_hbm.at[pl.ds(at, B), :], osem)

    def weights(e, ws):
        return [pltpu.make_async_copy(src.at[layer, e], dst.at[ws], wsem.at[ws])
                for src, dst in ((wg_hbm, wg_f), (wu_hbm, wu_f), (wd_hbm, wd_f))]

    @pl.when(b == 0)
    def _():
        load(0, 0).start()
        for cp in weights(blk_e_ref[0], blk_wslot_ref[0]):
            cp.start()
        ybuf[...] = jnp.zeros_like(ybuf)
        _zero_rows_from(ybuf.at[1], ys_hbm, tail_ref[0], osem)

    @pl.when(b + 1 < total)
    def _():
        load(b + 1, 1 - slot).start()

    @pl.when(b < total)
    def _():
        @pl.when(blk_first_ref[b] == 1)
        def _():
            ws = blk_wslot_ref[b]
            for cp in weights(blk_e_ref[b], ws):
                cp.wait()
            wg_b[...] = wg_f[ws].astype(BF16)
            wu_b[...] = wu_f[ws].astype(BF16)
            wd_b[...] = wd_f[ws].astype(BF16)
            nxt = blk_next_ref[b]

            @pl.when(nxt >= 0)
            def _():
                for cp in weights(nxt, 1 - ws):
                    cp.start()

        load(b, slot).wait()

        def ffn(rows):
            lo, hi = _unpack_halves(xbuf[slot, 0:rows, :])
            lo, hi = lo.astype(BF16), hi.astype(BF16)
            half = lo.shape[1]
            g = (jnp.dot(lo, wg_b[:half, :], preferred_element_type=F32)
                 + jnp.dot(hi, wg_b[half:, :], preferred_element_type=F32))
            u = (jnp.dot(lo, wu_b[:half, :], preferred_element_type=F32)
                 + jnp.dot(hi, wu_b[half:, :], preferred_element_type=F32))
            a = (g * jax.nn.sigmoid(g) * u).astype(BF16)
            ybuf[slot, 0:rows, :] = _pack_halves(jnp.dot(a, wd_b[...], preferred_element_type=F32))

        needed = (blk_n_ref[b] + MOE_STEP - 1) // MOE_STEP
        for q in range(1, B // MOE_STEP + 1):
            @pl.when(needed == q)
            def _():
                ffn(q * MOE_STEP)

        @pl.when(b > 0)
        def _():
            store(b - 1, 1 - slot).wait()

        store(b, slot).start()

        @pl.when(b == total - 1)
        def _():
            store(b, slot).wait()


def _experts(xs, layer, w_gate, w_up, w_down, blocks, tail, nb):
    rows, DP = xs.shape
    D, DE = w_gate.shape[-2:]
    B = MOE_ROWS
    anywhere = pl.BlockSpec(memory_space=pl.ANY)
    grid_spec = pltpu.PrefetchScalarGridSpec(
        num_scalar_prefetch=len(blocks) + 1,
        grid=(nb,),
        in_specs=[anywhere, anywhere, anywhere, anywhere],
        out_specs=anywhere,
        scratch_shapes=[
            pltpu.VMEM((2, D, DE), F32), pltpu.VMEM((2, D, DE), F32), pltpu.VMEM((2, DE, D), F32),
            pltpu.VMEM((D, DE), BF16), pltpu.VMEM((D, DE), BF16), pltpu.VMEM((DE, D), BF16),
            pltpu.VMEM((2, B, DP), U32), pltpu.VMEM((2, B, DP), U32),
            pltpu.SemaphoreType.DMA((2,)), pltpu.SemaphoreType.DMA(()), pltpu.SemaphoreType.DMA((2,)),
        ],
    )
    return pl.pallas_call(
        functools.partial(_experts_kernel, layer=layer),
        grid_spec=grid_spec,
        out_shape=jax.ShapeDtypeStruct((rows, DP), U32),
        compiler_params=_cparams(("arbitrary",)),
        name="experts",
    )(*blocks, tail, xs, w_gate, w_up, w_down)


def _combine_kernel(pos_ref, posn_ref, x_ref, ys_hbm, w_ref, g2_ref, fg_ref, o_ref, gbuf, sem,
                    *, final):
    i = pl.program_id(0)
    n = pl.num_programs(0)
    tm = x_ref.shape[0]
    slot = i % 2

    def gather(p_ref, to_slot):
        def body(r, carry):
            for k in range(TOP_K):
                pltpu.make_async_copy(ys_hbm.at[pl.ds(p_ref[k, r], 1), :],
                                      gbuf.at[to_slot, k, pl.ds(r, 1), :], sem.at[to_slot]).start(priority=k)
            return carry

        lax.fori_loop(0, tm, body, 0, unroll=8)

    @pl.when(i == 0)
    def _():
        gather(pos_ref, 0)

    @pl.when(i + 1 < n)
    def _():
        gather(posn_ref, 1 - slot)

    for k in range(TOP_K):
        pltpu.make_async_copy(ys_hbm.at[pl.ds(0, tm), :], gbuf.at[slot, k], sem.at[slot]).wait()

    w = w_ref[...]
    D = x_ref.shape[1]
    half = D // 2
    lo0, hi0 = _unpack_halves(gbuf[slot, 0])
    lo1, hi1 = _unpack_halves(gbuf[slot, 1])
    xl = x_ref[:, :half] + g2_ref[:, :half] * (lo0 * w[:, 0:1] + lo1 * w[:, 1:2])
    xh = x_ref[:, half:] + g2_ref[:, half:] * (hi0 * w[:, 0:1] + hi1 * w[:, 1:2])
    if final:
        ms = (jnp.sum(xl * xl, axis=-1, keepdims=True) + jnp.sum(xh * xh, axis=-1, keepdims=True)) / D
        r = lax.rsqrt(ms + EPS)
        xl = xl * r * fg_ref[:, :half]
        xh = xh * r * fg_ref[:, half:]
    o_ref[:, :half] = xl
    o_ref[:, half:] = xh


def _combine(x, ys, pos, top_w, g2, final_g, final):
    T, D = x.shape
    tm = min(COMB_TM, T)
    n = T // tm
    vec = pl.BlockSpec((1, D), lambda i: (0, 0))
    return pl.pallas_call(
        functools.partial(_combine_kernel, final=final),
        grid=(n,),
        in_specs=[
            pl.BlockSpec((TOP_K, tm), lambda i: (0, i), memory_space=pltpu.SMEM),
            pl.BlockSpec((TOP_K, tm), lambda i: (0, jnp.minimum(i + 1, n - 1)), memory_space=pltpu.SMEM),
            pl.BlockSpec((tm, D), lambda i: (i, 0)),
            pl.BlockSpec(memory_space=pl.ANY),
            pl.BlockSpec((tm, TOP_K), lambda i: (i, 0)),
            vec, vec,
        ],
        out_specs=pl.BlockSpec((tm, D), lambda i: (i, 0)),
        out_shape=jax.ShapeDtypeStruct((T, D), F32),
        scratch_shapes=[pltpu.VMEM((2, TOP_K, tm, ys.shape[1]), ys.dtype), pltpu.SemaphoreType.DMA((2,))],
        compiler_params=_cparams(("arbitrary",), disable_bounds_checks=True),
        name="combine",
    )(pos, pos, x, ys, top_w, g2, final_g)


def kernel(x, c, ada_w, ada_b, norm1_g, norm2_g, w_in, conv_w, ret_gn_g, fox_fb, w_out,
           router_w, router_b, moe_w_gate, moe_w_up, moe_w_down, final_g):
    Bsz, S, D = x.shape
    assert Bsz == 1, "one sequence per call"
    L = ada_w.shape[0]
    T = S
    conv_wd = conv_w.shape[-1]
    ret_qk = RET_HEADS * RET_DK
    ret_v = RET_HEADS * RET_DV
    fox_w = FOX_HEADS * FOX_DH
    tn = INPROJ_TN
    ret_col0 = 3 * conv_wd
    fox_col0 = ret_col0 + 2 * ret_qk + 2 * ret_v
    n_main = fox_col0 + 3 * fox_w
    assert fox_w == tn and fox_col0 % tn == 0
    q_block = fox_col0 // tn
    v_block = q_block + 2
    k_col0 = fox_col0

    mod = _ada_mod(c, ada_w, ada_b)
    cos_t, sin_t = _rope_tables(T)
    rw_t = router_w.T
    xt = x.reshape(T, D)

    for l in range(L):
        sh1, sc1, g1, sh2, sc2, g2 = [mod[l, :, k * D:(k + 1) * D] for k in range(6)]
        w_ff = jnp.pad(w_in[l, :, n_main:], ((0, 0), (0, LANES - FOX_HEADS)))
        fb = jnp.pad(fox_fb[l], (0, LANES - FOX_HEADS)).reshape(1, LANES)
        proj, proj_t, cum = _inproj(xt, norm1_g[l].reshape(1, D), sh1, sc1, w_in, l, w_ff, fb,
                                    q_block, v_block, LOG2E * FOX_DH ** -0.5)

        y_ret, y_conv = _retention(proj, cos_t, sin_t, ret_gn_g[l].reshape(1, ret_v), ret_col0, conv_w[l])
        y_fox = _fox(proj, proj_t, cum, k_col0)

        wo = w_out[l].astype(BF16)
        xt, h2, logits_t = _outproj(
            y_conv, y_ret, y_fox, wo[:conv_wd], wo[conv_wd:conv_wd + ret_v], wo[conv_wd + ret_v:],
            xt, g1, norm2_g[l].reshape(1, D), sh2, sc2, rw_t)

        top_e, rank, top_w, cnt = _route(logits_t, router_b)
        counts = cnt[:, 0].astype(jnp.int32)
        seg_start, seg_end, blocks, nb, rows = _block_tables(counts, T)
        hit = top_e[None] == jnp.arange(N_EXPERTS, dtype=jnp.int32)[:, None, None]
        pos = rank + jnp.sum(jnp.where(hit, seg_start[:, None, None], 0), axis=0)
        xs = _dispatch(h2, pos, seg_end, rows)
        ys = _experts(xs, l, moe_w_gate, moe_w_up, moe_w_down, blocks, seg_end[-1:], nb)
        xt = _combine(xt, ys, pos, top_w.T, g2, final_g.reshape(1, D), final=(l == L - 1))

    return xt.reshape(Bsz, S, D)
```
